```python
import math
import jax, jax.numpy as jnp
from jax import lax
import numpy as np

D_MODEL = 2048
BATCH = 4
SEQ = 2048
DEPTH = 1
DEC_BATCH = 32
DEC_SEQ = 16
PAST_LEN = 1024

CHUNK = 64
Q_BLOCK = 128
SPARSE_Q_BLOCK = CHUNK
MLA_HEADS = D_MODEL // 256
MLA_NOPE = 128
MLA_ROPE = 64
MLA_V = 128
KV_LORA = 512
DSA_HEADS = D_MODEL // 256
DSA_HEAD_DIM = 128
IDX_HEADS = 16
IDX_DIM = 64
TOPK_MAX = 256
N_BUCKETS = 32
MAX_DISTANCE = 128
ROPE_THETA = 10000.0
EPS = 1e-6
WIDTH_A = MLA_HEADS * MLA_V
WIDTH_B = DSA_HEADS * DSA_HEAD_DIM
MIX_WIDTH = WIDTH_A + WIDTH_B
IN_SPLITS = (MLA_HEADS * (MLA_NOPE + MLA_ROPE), KV_LORA, MLA_ROPE, WIDTH_A,
             WIDTH_B, WIDTH_B, WIDTH_B, IDX_HEADS * IDX_DIM, IDX_DIM, IDX_HEADS, WIDTH_B)
IN_WIDTH = sum(IN_SPLITS)

kernel_name = "hybrid_mla_dsa_streaming_step"


def rmsnorm(x, gain):
    xf = x.astype(jnp.float32)
    y = xf * lax.rsqrt(jnp.mean(xf * xf, axis=-1, keepdims=True) + EPS)
    return (y * gain.astype(jnp.float32)).astype(x.dtype)


def rope(x, pos):
    half = x.shape[-1] // 2
    freqs = jnp.power(ROPE_THETA, -jnp.arange(half, dtype=jnp.float32) / half)
    ang = pos.astype(jnp.float32)[:, None] * freqs
    ang = ang.reshape(ang.shape[0], *([1] * (x.ndim - 3)), half)
    cos, sin = jnp.cos(ang).astype(x.dtype), jnp.sin(ang).astype(x.dtype)
    x1, x2 = x[..., :half], x[..., half:]
    return jnp.concatenate([x1 * cos - x2 * sin, x2 * cos + x1 * sin], axis=-1)


def rel_bucket(rel):
    nb = N_BUCKETS // 2
    max_exact = nb // 2
    n = jnp.abs(rel)
    nf = jnp.maximum(n, 1).astype(jnp.float32)
    large = max_exact + (jnp.log(nf / max_exact) / math.log(MAX_DISTANCE / max_exact)
                         * (nb - max_exact)).astype(jnp.int32)
    large = jnp.minimum(large, nb - 1)
    return jnp.where(rel > 0, nb, 0) + jnp.where(n < max_exact, n, large)


def chunk_mask(q_pos, k_pos):
    return (k_pos[None, :] // CHUNK) <= (q_pos[:, None] // CHUNK)


def sweep(fn, q_args, q_pos, block):
    T = q_pos.shape[0]
    if T % block != 0:
        return fn(q_args, q_pos)
    nb = T // block

    def to_blocks(a):
        return jnp.moveaxis(a.reshape(a.shape[0], nb, block, *a.shape[2:]), 1, 0)

    out = lax.map(lambda args: fn(args[0], args[1]),
                  (tuple(to_blocks(a) for a in q_args), q_pos.reshape(nb, block)))
    out = jnp.moveaxis(out, 0, 1)
    return out.reshape(out.shape[0], T, *out.shape[3:])


def mla_attend(q_lat, q_rope, ckv, krope, q_pos, k_pos):
    s = (jnp.einsum('bthc,bsc->bhts', q_lat, ckv, preferred_element_type=jnp.float32)
         + jnp.einsum('bthr,bsr->bhts', q_rope, krope, preferred_element_type=jnp.float32))
    s = s * (MLA_NOPE + MLA_ROPE) ** -0.5
    s = jnp.where(chunk_mask(q_pos, k_pos), s, -jnp.inf)
    p = jax.nn.softmax(s, axis=-1).astype(ckv.dtype)
    return jnp.einsum('bhts,bsc->bthc', p, ckv)


def dsa_attend(q, q_idx, w_idx, q_pos, k, v, k_idx, rel_bias):
    S = k.shape[1]
    topk = min(TOPK_MAX, S // 4)
    k_pos = jnp.arange(S, dtype=jnp.int32)
    dots = jnp.einsum('btge,bse->btgs', q_idx, k_idx, preferred_element_type=jnp.float32) * IDX_DIM ** -0.5
    score = jnp.einsum('btgs,btg->bts', jax.nn.relu(dots), w_idx.astype(jnp.float32))
    score = jnp.where(chunk_mask(q_pos, k_pos)[None], score, -jnp.inf)
    top_val, sel = lax.top_k(score, topk)
    valid = jnp.isfinite(top_val)
    gather = jax.vmap(lambda a, i: a[i])
    k_sel = gather(k, sel)
    v_sel = gather(v, sel)
    bias = rel_bias[rel_bucket(sel - q_pos[None, :, None])]
    s = (jnp.einsum('bthd,btkhd->bhtk', q, k_sel, preferred_element_type=jnp.float32) * DSA_HEAD_DIM ** -0.5
         + jnp.moveaxis(bias, -1, 1).astype(jnp.float32))
    s = jnp.where(valid[:, None], s, -jnp.inf)
    p = jax.nn.softmax(s, axis=-1).astype(v.dtype)
    return jnp.einsum('bhtk,btkhd->bthd', p, v_sel)


def mixer_layer(x, c, pos, past, w_ada, b_ada, ln_gain, w_in, kv_gain, w_uk, w_uv, rel_bias,
                out_gain_a, out_gain_b, w_out):
    B, T = x.shape[0], x.shape[1]
    mod = jnp.einsum('bd,de->be', jax.nn.silu(c), w_ada) + b_ada
    shift, scale, gate = jnp.split(mod, 3, axis=-1)
    h = rmsnorm(x, ln_gain) * (1 + scale[:, None]) + shift[:, None]
    proj = jnp.einsum('btd,de->bte', h, w_in)
    split_at = np.cumsum(IN_SPLITS)[:-1].tolist()
    q_a, ckv, krope, g_a, q_b, k_b, v_b, q_i, k_i, w_i, g_b = jnp.split(proj, split_at, axis=-1)
    q_a = q_a.reshape(B, T, MLA_HEADS, MLA_NOPE + MLA_ROPE)
    q_rope = rope(q_a[..., MLA_NOPE:], pos)
    q_lat = jnp.einsum('bthn,hnc->bthc', q_a[..., :MLA_NOPE], w_uk)
    ckv = rmsnorm(ckv, kv_gain)
    krope = rope(krope, pos)
    q_b = q_b.reshape(B, T, DSA_HEADS, DSA_HEAD_DIM)
    k_b = k_b.reshape(B, T, DSA_HEADS, DSA_HEAD_DIM)
    v_b = v_b.reshape(B, T, DSA_HEADS, DSA_HEAD_DIM)
    q_i = q_i.reshape(B, T, IDX_HEADS, IDX_DIM)
    w_i = w_i * IDX_HEADS ** -0.5
    new_rows = (ckv, krope, k_b, v_b, k_i)
    if past is None:
        keys = new_rows
    else:
        keys = tuple(jnp.concatenate([p, n], axis=1) for p, n in zip(past, new_rows))
    all_ckv, all_krope, all_k, all_v, all_ki = keys
    k_pos = jnp.arange(all_ckv.shape[1], dtype=jnp.int32)
    o_lat = sweep(lambda qs, qp: mla_attend(qs[0], qs[1], all_ckv, all_krope, qp, k_pos),
                  (q_lat, q_rope), pos, Q_BLOCK)
    o_a = jnp.einsum('bthc,hvc->bthv', o_lat, w_uv).reshape(B, T, WIDTH_A)
    o_b = sweep(lambda qs, qp: dsa_attend(qs[0], qs[1], qs[2], qp, all_k, all_v, all_ki, rel_bias),
                (q_b, q_i, w_i), pos, SPARSE_Q_BLOCK).reshape(B, T, WIDTH_B)
    y = jnp.concatenate([rmsnorm(o_a, out_gain_a) * jax.nn.silu(g_a),
                         rmsnorm(o_b, out_gain_b) * jax.nn.silu(g_b)], axis=-1)
    out = jnp.einsum('bte,ed->btd', y, w_out)
    return x + gate[:, None] * out, new_rows


def setup_inputs(seed: int = 0) -> dict:
    key = jax.random.key(seed)
    ks = jax.random.split(key, 24)
    nrm = lambda k, shape, s=1.0: jax.random.normal(k, shape, jnp.float32) * s
    return {
        "x_prompt": nrm(ks[0], (BATCH, SEQ, D_MODEL)),
        "x_sample": nrm(ks[1], (DEC_BATCH, DEC_SEQ, D_MODEL)),
        "cache_mla_ckv": nrm(ks[2], (DEPTH, DEC_BATCH, PAST_LEN, KV_LORA)),
        "cache_mla_krope": nrm(ks[3], (DEPTH, DEC_BATCH, PAST_LEN, MLA_ROPE)),
        "cache_dsa_k": nrm(ks[4], (DEPTH, DEC_BATCH, PAST_LEN, DSA_HEADS, DSA_HEAD_DIM)),
        "cache_dsa_v": nrm(ks[5], (DEPTH, DEC_BATCH, PAST_LEN, DSA_HEADS, DSA_HEAD_DIM)),
        "cache_idx_k": nrm(ks[6], (DEPTH, DEC_BATCH, PAST_LEN, IDX_DIM)),
        "c_prompt": nrm(ks[7], (BATCH, D_MODEL)),
        "c_sample": nrm(ks[8], (DEC_BATCH, D_MODEL)),
        "w_ada": nrm(ks[9], (DEPTH, D_MODEL, 3 * D_MODEL), 0.5 * D_MODEL ** -0.5),
        "b_ada": nrm(ks[10], (DEPTH, 3 * D_MODEL), 0.02),
        "ln_gain": 1.0 + nrm(ks[11], (DEPTH, D_MODEL), 0.02),
        "w_in": nrm(ks[12], (DEPTH, D_MODEL, IN_WIDTH), D_MODEL ** -0.5),
        "mla_kv_gain": 1.0 + nrm(ks[13], (DEPTH, KV_LORA), 0.02),
        "w_uk": nrm(ks[14], (DEPTH, MLA_HEADS, MLA_NOPE, KV_LORA), KV_LORA ** -0.5),
        "w_uv": nrm(ks[15], (DEPTH, MLA_HEADS, MLA_V, KV_LORA), KV_LORA ** -0.5),
        "rel_bias": nrm(ks[16], (N_BUCKETS, DSA_HEADS), 0.5),
        "out_gain_a": 1.0 + nrm(ks[17], (DEPTH, WIDTH_A), 0.02),
        "out_gain_b": 1.0 + nrm(ks[18], (DEPTH, WIDTH_B), 0.02),
        "w_out": nrm(ks[19], (DEPTH, MIX_WIDTH, D_MODEL), MIX_WIDTH ** -0.5),
        "final_gain": 1.0 + nrm(ks[20], (D_MODEL,), 0.02),
    }


def reference(x_prompt, x_sample, cache_mla_ckv, cache_mla_krope, cache_dsa_k, cache_dsa_v, cache_idx_k,
              c_prompt, c_sample, w_ada, b_ada, ln_gain, w_in, mla_kv_gain, w_uk, w_uv, rel_bias,
              out_gain_a, out_gain_b, w_out, final_gain):
    pos_p = jnp.arange(x_prompt.shape[1], dtype=jnp.int32)
    pos_s = cache_mla_ckv.shape[2] + jnp.arange(x_sample.shape[1], dtype=jnp.int32)
    xp, xs = x_prompt, x_sample
    rows_p, rows_s = [], []
    for l in range(DEPTH):
        lw = (w_ada[l], b_ada[l], ln_gain[l], w_in[l], mla_kv_gain[l], w_uk[l], w_uv[l], rel_bias,
              out_gain_a[l], out_gain_b[l], w_out[l])
        xp, rp = mixer_layer(xp, c_prompt, pos_p, None, *lw)
        past = (cache_mla_ckv[l], cache_mla_krope[l], cache_dsa_k[l], cache_dsa_v[l], cache_idx_k[l])
        xs, rs = mixer_layer(xs, c_sample, pos_s, past, *lw)
        rows_p.append(rp)
        rows_s.append(rs)
    y_prompt = rmsnorm(xp, final_gain)
    y_sample = rmsnorm(xs, final_gain)
    ckv_p = jnp.stack([r[0] for r in rows_p])
    krope_p = jnp.stack([r[1] for r in rows_p])
    k_p = jnp.stack([r[2] for r in rows_p])
    v_p = jnp.stack([r[3] for r in rows_p])
    idxk_p = jnp.stack([r[4] for r in rows_p])
    ckv_s = jnp.stack([r[0] for r in rows_s])
    krope_s = jnp.stack([r[1] for r in rows_s])
    k_s = jnp.stack([r[2] for r in rows_s])
    v_s = jnp.stack([r[3] for r in rows_s])
    idxk_s = jnp.stack([r[4] for r in rows_s])
    return (y_prompt, y_sample, ckv_p, krope_p, k_p, v_p, idxk_p, ckv_s, krope_s, k_s, v_s, idxk_s)
```

```python
import functools
import math

import jax
import jax.numpy as jnp
from jax import lax
from jax.experimental import pallas as pl
from jax.experimental.pallas import tpu as pltpu

MXU_DT = jnp.bfloat16

CHUNK = 64
MLA_NOPE = 128
MLA_ROPE = 64
MLA_V = 128
KV_LORA = 512
DSA_HEAD_DIM = 128
IDX_HEADS = 16
IDX_DIM = 64
TOPK_MAX = 256
N_BUCKETS = 32
MAX_DISTANCE = 128
ROPE_THETA = 10000.0
EPS = 1e-6

LANES = 128
HEAD_SLOT = 256
NEG_BIG = -1e30
INT_MIN = -2 ** 31
VMEM_LIMIT = 56 * 1024 * 1024


def _cparams(n_grid, vmem=VMEM_LIMIT):
    return pltpu.CompilerParams(dimension_semantics=("arbitrary",) * n_grid, vmem_limit_bytes=vmem)


def _mx(v):
    return v.astype(MXU_DT)


def _dot(a, b):
    return jnp.dot(a, b, preferred_element_type=jnp.float32)


def _dot_nt(a, b):
    return lax.dot_general(a, b, (((1,), (1,)), ((), ())), preferred_element_type=jnp.float32)


def _silu(v):
    return v * (1.0 / (1.0 + jnp.exp(-v)))


def _resident(shape):
    nd = len(shape)
    return pl.BlockSpec(shape, lambda *_: (0,) * nd, pipeline_mode=pl.Buffered(1))


def _ada_kernel(c_ref, w_ref, b_ref, o_ref):
    a = _mx(_silu(c_ref[...]))
    o_ref[...] = _dot(a, _mx(w_ref[...])) + b_ref[...]


def _ada_mod(c_all, w_ada, b_ada):
    m, d = c_all.shape
    n = w_ada.shape[1]
    tn = 1024
    return pl.pallas_call(
        _ada_kernel,
        grid=(n // tn,),
        in_specs=[pl.BlockSpec((m, d), lambda j: (0, 0)),
                  pl.BlockSpec((d, tn), lambda j: (0, j)),
                  pl.BlockSpec((1, tn), lambda j: (0, j))],
        out_specs=pl.BlockSpec((m, tn), lambda j: (0, j)),
        out_shape=jax.ShapeDtypeStruct((m, n), jnp.float32),
        compiler_params=_cparams(1),
        name="ada_mod",
    )(c_all, w_ada, b_ada.reshape(1, n))


def _modulated_norm(x_ref, scale_ref, shift_ref, lng_ref):
    x = x_ref[...]
    xn = x * lax.rsqrt(jnp.mean(x * x, axis=-1, keepdims=True) + EPS) * lng_ref[...]
    return _mx(xn * (1.0 + scale_ref[...]) + shift_ref[...])


def _rope128(a, tab):
    t = a * tab
    return t + pltpu.roll(t, 64, 1)


def _proj_a_kernel(n_heads, emit_kv, x_ref, scale_ref, shift_ref, lng_ref, tab_ref, w_ref, kvg_ref, *rest):
    if emit_kv:
        wkv_ref, qcat_ref, ckv_ref, krope_ref, ga_ref, gb_ref, kcat_ref, vup_ref = rest
    else:
        qcat_ref, ckv_ref, krope_ref, ga_ref, gb_ref = rest
    hb = _modulated_norm(x_ref, scale_ref, shift_ref, lng_ref)
    tab = tab_ref[...]
    qscale = (MLA_NOPE + MLA_ROPE) ** -0.5
    for h in range(n_heads):
        c0 = h * HEAD_SLOT
        a = _dot(hb, w_ref[:, c0:c0 + HEAD_SLOT])
        qcat_ref[:, c0:c0 + LANES] = _mx(a[:, :LANES] * qscale)
        qcat_ref[:, c0 + LANES:c0 + HEAD_SLOT] = _mx(_rope128(a[:, LANES:], tab) * qscale)
    off = n_heads * HEAD_SLOT
    c = _dot(hb, w_ref[:, off:off + KV_LORA])
    cn = c * lax.rsqrt(jnp.mean(c * c, axis=-1, keepdims=True) + EPS) * kvg_ref[...]
    ckv_ref[...] = cn
    off += KV_LORA
    r = _rope128(_dot(hb, w_ref[:, off:off + LANES]), tab)
    krope_ref[...] = r[:, :MLA_ROPE]
    off += LANES
    wa = n_heads * MLA_V
    ga_ref[...] = _mx(_silu(_dot(hb, w_ref[:, off:off + wa])))
    off += wa
    gb_ref[...] = _mx(_silu(_dot(hb, w_ref[:, off:off + wa])))
    if emit_kv:
        cb = _mx(cn)
        lane = lax.broadcasted_iota(jnp.int32, r.shape, 1)
        krz = _mx(jnp.where(lane < MLA_ROPE, r, 0.0))
        kn = _dot(cb, wkv_ref[:, :n_heads * MLA_NOPE])
        for h in range(n_heads):
            c0 = h * HEAD_SLOT
            kcat_ref[:, c0:c0 + LANES] = _mx(kn[:, h * MLA_NOPE:(h + 1) * MLA_NOPE])
            kcat_ref[:, c0 + LANES:c0 + HEAD_SLOT] = krz
        vup_ref[...] = _mx(_dot(cb, wkv_ref[:, n_heads * MLA_NOPE:]))


def _proj_b_kernel(width_b, x_ref, scale_ref, shift_ref, lng_ref, w_ref,
                   qb_ref, kb_ref, vb_ref, kbb_ref, vbb_ref, qi_ref, ki_ref, kiab_ref, wi_ref):
    hb = _modulated_norm(x_ref, scale_ref, shift_ref, lng_ref)
    dscale = DSA_HEAD_DIM ** -0.5
    step = 512
    for c0 in range(0, width_b, step):
        qb_ref[:, c0:c0 + step] = _mx(_dot(hb, w_ref[:, c0:c0 + step]) * dscale)
    for c0 in range(0, width_b, step):
        kv = _dot(hb, w_ref[:, width_b + c0:width_b + c0 + step])
        kb_ref[:, c0:c0 + step] = kv
        kbb_ref[:, c0:c0 + step] = _mx(kv)
    for c0 in range(0, width_b, step):
        vv = _dot(hb, w_ref[:, 2 * width_b + c0:2 * width_b + c0 + step])
        vb_ref[:, c0:c0 + step] = vv
        vbb_ref[:, c0:c0 + step] = _mx(vv)
    off = 3 * width_b
    wq = IDX_HEADS * IDX_DIM
    for c0 in range(0, wq, step):
        qi_ref[:, c0:c0 + step] = _mx(_dot(hb, w_ref[:, off + c0:off + c0 + step]))
    off += wq
    a = _dot(hb, w_ref[:, off:off + LANES])
    ki_ref[...] = a[:, :IDX_DIM]
    lane = lax.broadcasted_iota(jnp.int32, a.shape, 1)
    kz = jnp.where(lane < IDX_DIM, a, 0.0)
    kiab_ref[:, :LANES] = _mx(kz)
    kiab_ref[:, LANES:] = _mx(pltpu.roll(kz, IDX_DIM, 1))
    wi_ref[...] = a * (IDX_HEADS ** -0.5 * IDX_DIM ** -0.5)


def _row_specs(x3, mod_rows, tm):
    bv, tv, d = x3.shape
    x_spec = pl.BlockSpec((None, tm, d), lambda b, i: (b, i, 0))
    if mod_rows == 1:
        m_spec = pl.BlockSpec((None, 1, d), lambda b, i: (b, 0, 0))
    else:
        m_spec = pl.BlockSpec((None, tm, d), lambda b, i: (b, i, 0))
    return x_spec, m_spec


def _out2d(m, width, dtype, tm, nt):
    return (jax.ShapeDtypeStruct((m, width), dtype),
            pl.BlockSpec((tm, width), lambda b, i: (b * nt + i, 0)))


def _proj_a(x3, scale, shift, ln_gain, tab, w_a, kv_gain, wkv, tm):
    bv, tv, d = x3.shape
    nt = tv // tm
    m = bv * tv
    n_heads = d // 256
    wa = n_heads * MLA_V
    emit_kv = wkv is not None
    x_spec, m_spec = _row_specs(x3, scale.shape[1], tm)
    in_specs = [x_spec, m_spec, m_spec, _resident((1, d)),
                pl.BlockSpec((tm, LANES), lambda b, i: (i, 0)),
                _resident(w_a.shape), _resident((1, KV_LORA))]
    args = [x3, scale, shift, ln_gain.reshape(1, d), tab, w_a, kv_gain.reshape(1, KV_LORA)]
    outs = [_out2d(m, n_heads * HEAD_SLOT, MXU_DT, tm, nt),
            _out2d(m, KV_LORA, jnp.float32, tm, nt),
            _out2d(m, MLA_ROPE, jnp.float32, tm, nt),
            _out2d(m, wa, MXU_DT, tm, nt),
            _out2d(m, wa, MXU_DT, tm, nt)]
    if emit_kv:
        in_specs.append(_resident(wkv.shape))
        args.append(wkv)
        outs += [_out2d(m, n_heads * HEAD_SLOT, MXU_DT, tm, nt),
                 _out2d(m, wa, MXU_DT, tm, nt)]
    return pl.pallas_call(
        functools.partial(_proj_a_kernel, n_heads, emit_kv),
        grid=(bv, nt),
        in_specs=in_specs,
        out_specs=[o[1] for o in outs],
        out_shape=[o[0] for o in outs],
        compiler_params=_cparams(2),
        name="proj_a",
    )(*args)


def _proj_b(x3, scale, shift, ln_gain, w_b, tm):
    bv, tv, d = x3.shape
    nt = tv // tm
    m = bv * tv
    width_b = (d // 256) * DSA_HEAD_DIM
    x_spec, m_spec = _row_specs(x3, scale.shape[1], tm)
    outs = [_out2d(m, width_b, MXU_DT, tm, nt),
            _out2d(m, width_b, jnp.float32, tm, nt),
            _out2d(m, width_b, jnp.float32, tm, nt),
            _out2d(m, width_b, MXU_DT, tm, nt),
            _out2d(m, width_b, MXU_DT, tm, nt),
            _out2d(m, IDX_HEADS * IDX_DIM, MXU_DT, tm, nt),
            _out2d(m, IDX_DIM, jnp.float32, tm, nt),
            _out2d(m, 2 * LANES, MXU_DT, tm, nt),
            _out2d(m, LANES, jnp.float32, tm, nt)]
    return pl.pallas_call(
        functools.partial(_proj_b_kernel, width_b),
        grid=(bv, nt),
        in_specs=[x_spec, m_spec, m_spec, _resident((1, d)), _resident(w_b.shape)],
        out_specs=[o[1] for o in outs],
        out_shape=[o[0] for o in outs],
        compiler_params=_cparams(2),
        name="proj_b",
    )(x3, scale, shift, ln_gain.reshape(1, d), w_b)


def _softmax_pv(s, v):
    m = jnp.max(s, axis=-1, keepdims=True)
    p = jnp.exp(s - m)
    l = jnp.sum(p, axis=-1, keepdims=True)
    return _dot(_mx(p), v) * (1.0 / l)


def _topk_mask(score, adm, topk, key_ref):
    rows, n = score.shape
    score = jnp.where(score == 0.0, 0.0, score)
    bits = pltpu.bitcast(score, jnp.int32)
    key = bits ^ ((bits >> 31) & 0x7FFFFFFF)
    key_ref[...] = jnp.where(adm, key, INT_MIN)
    kf = float(topk)

    def count(pred):
        return jnp.sum(jnp.where(pred, 1.0, 0.0), axis=-1, keepdims=True)

    def bit_step(i, thr):
        inc = lax.shift_left(jnp.int32(1), 31 - i)
        cand = thr + inc
        ok = count(key_ref[...] >= cand) >= kf
        return jnp.where(ok, cand, thr)

    thr = lax.fori_loop(0, 32, bit_step, jnp.full((rows, 1), INT_MIN, jnp.int32))
    key = key_ref[...]
    gt = key > thr
    need = kf - count(gt)
    eq = key == thr
    col = lax.broadcasted_iota(jnp.int32, (rows, n), 1)
    nbits = max(1, int(n).bit_length())

    def col_step(i, bound):
        cand = bound + lax.shift_left(jnp.int32(1), nbits - 1 - i)
        ok = count(eq & (col < cand)) <= need
        return jnp.where(ok, cand, bound)

    bound = lax.fori_loop(0, nbits, col_step, jnp.zeros((rows, 1), jnp.int32))
    sel = (gt | (eq & (col < bound))) & adm
    return jnp.where(sel, 0.0, NEG_BIG)


def _indexer_scores(qi_ref, wi, kia, kib):
    half = IDX_HEADS // 2
    score = None
    for j in range(half):
        qp = qi_ref[:, j * LANES:(j + 1) * LANES]
        da = jnp.maximum(_dot_nt(qp, kia), 0.0) * wi[:, IDX_DIM + j:IDX_DIM + j + 1]
        db = jnp.maximum(_dot_nt(qp, kib), 0.0) * wi[:, IDX_DIM + half + j:IDX_DIM + half + j + 1]
        score = da + db if score is None else score + da + db
    return score


def _mla_prompt_kernel(n_heads, tq, q_ref, k_ref, v_ref, o_ref):
    i = pl.program_id(1)
    t = k_ref.shape[0]
    qc = (i * tq + lax.broadcasted_iota(jnp.int32, (tq, t), 0)) // CHUNK
    kc = lax.broadcasted_iota(jnp.int32, (tq, t), 1) // CHUNK
    madd = jnp.where(kc <= qc, 0.0, NEG_BIG)
    for h in range(n_heads):
        c0 = h * HEAD_SLOT
        s = _dot_nt(q_ref[:, c0:c0 + HEAD_SLOT], k_ref[:, c0:c0 + HEAD_SLOT]) + madd
        o_ref[:, h * MLA_V:(h + 1) * MLA_V] = _softmax_pv(s, v_ref[:, h * MLA_V:(h + 1) * MLA_V])


def _mla_prompt(qcat, kcat, vup, b, t, tq):
    n_heads = qcat.shape[1] // HEAD_SLOT
    nq = t // tq
    wa = n_heads * MLA_V
    return pl.pallas_call(
        functools.partial(_mla_prompt_kernel, n_heads, tq),
        grid=(b, nq),
        in_specs=[pl.BlockSpec((tq, qcat.shape[1]), lambda bi, i: (bi * nq + i, 0)),
                  pl.BlockSpec((t, kcat.shape[1]), lambda bi, i: (bi, 0)),
                  pl.BlockSpec((t, wa), lambda bi, i: (bi, 0))],
        out_specs=pl.BlockSpec((tq, wa), lambda bi, i: (bi * nq + i, 0)),
        out_shape=jax.ShapeDtypeStruct((b * t, wa), jnp.float32),
        compiler_params=_cparams(2),
        name="mla_prompt",
    )(qcat, kcat, vup)


def _toeplitz_bias(tab_ref, h, start, tq, n_keys, le_mask):
    pieces = []
    prev = None
    for n in range(n_keys // LANES + 1):
        seg = tab_ref[pl.ds(h, 1), pl.ds(pl.multiple_of(start + n * LANES, LANES), LANES)]
        rot = pltpu.roll(jnp.broadcast_to(seg, (tq, LANES)), 1, 1, stride=1, stride_axis=0)
        if prev is not None:
            pieces.append(jnp.where(le_mask, prev, rot))
        prev = rot
    return jnp.concatenate(pieces, axis=1)


def _dsa_prompt_kernel(n_heads, tq, topk, qb_ref, qi_ref, wi_ref, k_ref, v_ref, kiab_ref, tab_ref,
                       o_ref, key_ref, madd_ref):
    i = pl.program_id(1)
    t = k_ref.shape[0]
    score = _indexer_scores(qi_ref, wi_ref[...], kiab_ref[:, :LANES], kiab_ref[:, LANES:])
    qc = (i * tq + lax.broadcasted_iota(jnp.int32, (tq, t), 0)) // CHUNK
    kc = lax.broadcasted_iota(jnp.int32, (tq, t), 1) // CHUNK
    madd_ref[...] = _topk_mask(score, kc <= qc, topk, key_ref)
    le_mask = (lax.broadcasted_iota(jnp.int32, (tq, LANES), 1)
               <= lax.broadcasted_iota(jnp.int32, (tq, LANES), 0))
    start = t - (i + 1) * tq
    for h in range(n_heads):
        c0 = h * DSA_HEAD_DIM
        s = (_dot_nt(qb_ref[:, c0:c0 + DSA_HEAD_DIM], k_ref[:, c0:c0 + DSA_HEAD_DIM])
             + _toeplitz_bias(tab_ref, h, start, tq, t, le_mask) + madd_ref[...])
        o_ref[:, c0:c0 + DSA_HEAD_DIM] = _softmax_pv(s, v_ref[:, c0:c0 + DSA_HEAD_DIM])


def _dsa_prompt(qb, qi, wi, kbb, vbb, kiab, bias_tab, b, t, topk):
    tq = LANES
    n_heads = qb.shape[1] // DSA_HEAD_DIM
    nq = t // tq
    wb = qb.shape[1]
    row = lambda bi, i: (bi * nq + i, 0)
    per_b = lambda bi, i: (bi, 0)
    return pl.pallas_call(
        functools.partial(_dsa_prompt_kernel, n_heads, tq, topk),
        grid=(b, nq),
        in_specs=[pl.BlockSpec((tq, wb), row),
                  pl.BlockSpec((tq, qi.shape[1]), row),
                  pl.BlockSpec((tq, LANES), row),
                  pl.BlockSpec((t, wb), per_b),
                  pl.BlockSpec((t, wb), per_b),
                  pl.BlockSpec((t, 2 * LANES), per_b),
                  _resident(bias_tab.shape)],
        out_specs=pl.BlockSpec((tq, wb), row),
        out_shape=jax.ShapeDtypeStruct((b * t, wb), jnp.float32),
        scratch_shapes=[pltpu.VMEM((tq, t), jnp.int32), pltpu.VMEM((tq, t), jnp.float32)],
        compiler_params=_cparams(2),
        name="dsa_prompt",
    )(qb, qi, wi, kbb, vbb, kiab, bias_tab)


def _mla_sample_kernel(n_heads, past, ts, qcat_ref, cckv_ref, ckr_ref, nckv_ref, nkr_ref, wuk_ref, wuv_ref,
                       o_ref, kall_ref, rall_ref):
    s_pad = kall_ref.shape[0]
    n_keys = past + ts

    @pl.when(pl.program_id(0) == 0)
    def _():
        kall_ref[...] = jnp.zeros_like(kall_ref)
        rall_ref[...] = jnp.zeros_like(rall_ref)

    kall_ref[0:past, :] = _mx(cckv_ref[...])
    kall_ref[past:n_keys, :] = _mx(nckv_ref[...])
    rall_ref[0:past, 0:MLA_ROPE] = _mx(ckr_ref[...])
    rall_ref[past:n_keys, 0:MLA_ROPE] = _mx(nkr_ref[...])
    qlat, qrope = [], []
    for h in range(n_heads):
        c0 = h * HEAD_SLOT
        qlat.append(_mx(_dot(qcat_ref[:, c0:c0 + LANES], wuk_ref[h])))
        qrope.append(qcat_ref[:, c0 + LANES:c0 + HEAD_SLOT])
    qlat = jnp.concatenate(qlat, axis=0)
    qrope = jnp.concatenate(qrope, axis=0)
    s = _dot_nt(qlat, kall_ref[...]) + _dot_nt(qrope, rall_ref[...])
    col = lax.broadcasted_iota(jnp.int32, s.shape, 1)
    s = jnp.where(col < n_keys, s, NEG_BIG)
    olat = _mx(_softmax_pv(s, kall_ref[...]))
    for h in range(n_heads):
        o_ref[:, h * MLA_V:(h + 1) * MLA_V] = _dot_nt(olat[h * ts:(h + 1) * ts], wuv_ref[h])


def _mla_sample(qcat, cache_ckv, cache_kr, new_ckv, new_kr, wuk, wuv, ts):
    nb, past, c = cache_ckv.shape
    n_heads = qcat.shape[1] // HEAD_SLOT
    wa = n_heads * MLA_V
    s_pad = pl.cdiv(past + ts, LANES) * LANES
    row = lambda bi: (bi, 0)
    return pl.pallas_call(
        functools.partial(_mla_sample_kernel, n_heads, past, ts),
        grid=(nb,),
        in_specs=[pl.BlockSpec((ts, qcat.shape[1]), row),
                  pl.BlockSpec((None, past, c), lambda bi: (bi, 0, 0)),
                  pl.BlockSpec((None, past, MLA_ROPE), lambda bi: (bi, 0, 0)),
                  pl.BlockSpec((ts, c), row),
                  pl.BlockSpec((ts, MLA_ROPE), row),
                  _resident(wuk.shape), _resident(wuv.shape)],
        out_specs=pl.BlockSpec((ts, wa), row),
        out_shape=jax.ShapeDtypeStruct((nb * ts, wa), jnp.float32),
        scratch_shapes=[pltpu.VMEM((s_pad, c), MXU_DT), pltpu.VMEM((s_pad, LANES), MXU_DT)],
        compiler_params=_cparams(1),
        name="mla_sample",
    )(qcat, cache_ckv, cache_kr, new_ckv, new_kr, wuk, wuv)


def _dsa_sample_kernel(n_heads, past, ts, topk, qb_ref, qi_ref, wi_ref, ck_ref, cv_ref, cki_ref,
                       nk_ref, nv_ref, nkiab_ref, bias_ref, o_ref,
                       kall_ref, vall_ref, kia_ref, kib_ref, key_ref, madd_ref):
    s_pad = kall_ref.shape[0]
    n_keys = past + ts

    @pl.when(pl.program_id(0) == 0)
    def _():
        for ref in (kall_ref, vall_ref, kia_ref, kib_ref):
            ref[...] = jnp.zeros_like(ref)

    kall_ref[0:past, :] = _mx(ck_ref[...])
    kall_ref[past:n_keys, :] = nk_ref[...]
    vall_ref[0:past, :] = _mx(cv_ref[...])
    vall_ref[past:n_keys, :] = nv_ref[...]
    cki = _mx(cki_ref[...])
    kia_ref[0:past, 0:IDX_DIM] = cki
    kib_ref[0:past, IDX_DIM:LANES] = cki
    kia_ref[past:n_keys, :] = nkiab_ref[:, :LANES]
    kib_ref[past:n_keys, :] = nkiab_ref[:, LANES:]
    score = _indexer_scores(qi_ref, wi_ref[...], kia_ref[...], kib_ref[...])
    col = lax.broadcasted_iota(jnp.int32, (ts, s_pad), 1)
    madd_ref[...] = _topk_mask(score, col < n_keys, topk, key_ref)
    for h in range(n_heads):
        c0 = h * DSA_HEAD_DIM
        s = _dot_nt(qb_ref[:, c0:c0 + DSA_HEAD_DIM], kall_ref[:, c0:c0 + DSA_HEAD_DIM]) + bias_ref[h] + madd_ref[...]
        o_ref[:, c0:c0 + DSA_HEAD_DIM] = _softmax_pv(s, vall_ref[:, c0:c0 + DSA_HEAD_DIM])


def _dsa_sample(qb, qi, wi, cache_k, cache_v, cache_ki, new_k, new_v, new_kiab, bias, ts, topk):
    nb, past, wb = cache_k.shape
    n_heads = wb // DSA_HEAD_DIM
    s_pad = bias.shape[2]
    row = lambda bi: (bi, 0)
    per_b = lambda bi: (bi, 0, 0)
    return pl.pallas_call(
        functools.partial(_dsa_sample_kernel, n_heads, past, ts, topk),
        grid=(nb,),
        in_specs=[pl.BlockSpec((ts, wb), row),
                  pl.BlockSpec((ts, qi.shape[1]), row),
                  pl.BlockSpec((ts, LANES), row),
                  pl.BlockSpec((None, past, wb), per_b),
                  pl.BlockSpec((None, past, wb), per_b),
                  pl.BlockSpec((None, past, IDX_DIM), per_b),
                  pl.BlockSpec((ts, wb), row),
                  pl.BlockSpec((ts, wb), row),
                  pl.BlockSpec((ts, 2 * LANES), row),
                  _resident(bias.shape)],
        out_specs=pl.BlockSpec((ts, wb), row),
        out_shape=jax.ShapeDtypeStruct((nb * ts, wb), jnp.float32),
        scratch_shapes=[pltpu.VMEM((s_pad, wb), MXU_DT), pltpu.VMEM((s_pad, wb), MXU_DT),
                        pltpu.VMEM((s_pad, LANES), MXU_DT), pltpu.VMEM((s_pad, LANES), MXU_DT),
                        pltpu.VMEM((ts, s_pad), jnp.int32), pltpu.VMEM((ts, s_pad), jnp.float32)],
        compiler_params=_cparams(1),
        name="dsa_sample",
    )(qb, qi, wi, cache_k, cache_v, cache_ki, new_k, new_v, new_kiab, bias)


def _out_kernel(oa_ref, ob_ref, ga_ref, gb_ref, x_ref, gate_ref, gna_ref, gnb_ref, w_ref, fg_ref, y_ref):
    def gated(o_ref, g_ref, gain_ref):
        o = o_ref[...]
        on = o * lax.rsqrt(jnp.mean(o * o, axis=-1, keepdims=True) + EPS) * gain_ref[...]
        return _mx(on * g_ref[...].astype(jnp.float32))

    wa = oa_ref.shape[1]
    out = _dot(gated(oa_ref, ga_ref, gna_ref), w_ref[0:wa, :]) + _dot(gated(ob_ref, gb_ref, gnb_ref), w_ref[wa:, :])
    xn = x_ref[...] + gate_ref[...] * out
    y_ref[...] = xn * lax.rsqrt(jnp.mean(xn * xn, axis=-1, keepdims=True) + EPS) * fg_ref[...]


def _out_proj(oa, ob, ga, gb, x3, gate, gain_a, gain_b, w_out, final_gain, tm):
    bv, tv, d = x3.shape
    nt = tv // tm
    wa = oa.shape[1]
    x_spec, g_spec = _row_specs(x3, gate.shape[1], tm)
    row = lambda b, i: (b * nt + i, 0)
    return pl.pallas_call(
        _out_kernel,
        grid=(bv, nt),
        in_specs=[pl.BlockSpec((tm, wa), row), pl.BlockSpec((tm, wa), row),
                  pl.BlockSpec((tm, wa), row), pl.BlockSpec((tm, wa), row),
                  x_spec, g_spec, _resident((1, wa)), _resident((1, wa)),
                  _resident(w_out.shape), _resident((1, d))],
        out_specs=pl.BlockSpec((None, tm, d), lambda b, i: (b, i, 0)),
        out_shape=jax.ShapeDtypeStruct((bv, tv, d), jnp.float32),
        compiler_params=_cparams(2),
        name="out_proj",
    )(oa, ob, ga, gb, x3, gate, gain_a.reshape(1, wa), gain_b.reshape(1, wa), w_out, final_gain.reshape(1, d))


def _pack_w_in(w_in, n_heads):
    d = w_in.shape[0]
    wa = n_heads * MLA_V
    sizes = (n_heads * (MLA_NOPE + MLA_ROPE), KV_LORA, MLA_ROPE, wa, wa, wa, wa,
             IDX_HEADS * IDX_DIM, IDX_DIM, IDX_HEADS, wa)
    offs = [0]
    for s in sizes:
        offs.append(offs[-1] + s)
    assert offs[-1] == w_in.shape[1]
    q_a, ckv, kr, g_a, q_b, k_b, v_b, q_i, k_i, w_i, g_b = (w_in[:, offs[n]:offs[n + 1]] for n in range(len(sizes)))
    half = MLA_ROPE // 2
    q_a = q_a.reshape(d, n_heads, MLA_NOPE + MLA_ROPE)
    r1, r2 = q_a[:, :, MLA_NOPE:MLA_NOPE + half], q_a[:, :, MLA_NOPE + half:]
    q_a = jnp.concatenate([q_a[:, :, :MLA_NOPE], r1, r2, r2, r1], axis=2).reshape(d, n_heads * HEAD_SLOT)
    kr = jnp.concatenate([kr[:, :half], kr[:, half:], kr[:, half:], kr[:, :half]], axis=1)
    q_i = q_i.reshape(d, 2, IDX_HEADS // 2, IDX_DIM).transpose(0, 2, 1, 3).reshape(d, IDX_HEADS * IDX_DIM)
    kiw = jnp.concatenate([k_i, w_i, jnp.zeros((d, LANES - IDX_DIM - IDX_HEADS), w_in.dtype)], axis=1)
    w_a = jnp.concatenate([q_a, ckv, kr, g_a, g_b], axis=1)
    w_b = jnp.concatenate([q_b, k_b, v_b, q_i, kiw], axis=1)
    return _mx(w_a), _mx(w_b)


def _rope_table(pos):
    half = MLA_ROPE // 2
    freqs = jnp.power(ROPE_THETA, -jnp.arange(half, dtype=jnp.float32) / half)
    ang = pos.astype(jnp.float32)[:, None] * freqs
    cos, sin = jnp.cos(ang), jnp.sin(ang)
    return jnp.concatenate([cos, cos, -sin, sin], axis=1)


def _rel_bucket(rel):
    nb = N_BUCKETS // 2
    max_exact = nb // 2
    n = jnp.abs(rel)
    nf = jnp.maximum(n, 1).astype(jnp.float32)
    large = max_exact + (jnp.log(nf / max_exact) / math.log(MAX_DISTANCE / max_exact)
                         * (nb - max_exact)).astype(jnp.int32)
    large = jnp.minimum(large, nb - 1)
    return jnp.where(rel > 0, nb, 0) + jnp.where(n < max_exact, n, large)


def kernel(x_prompt, x_sample, cache_mla_ckv, cache_mla_krope, cache_dsa_k, cache_dsa_v, cache_idx_k,
           c_prompt, c_sample, w_ada, b_ada, ln_gain, w_in, mla_kv_gain, w_uk, w_uv, rel_bias,
           out_gain_a, out_gain_b, w_out, final_gain):
    assert w_ada.shape[0] == 1, "single-layer step"
    b, t, d = x_prompt.shape
    nb, ts, _ = x_sample.shape
    past = cache_mla_ckv.shape[2]
    n_heads = d // 256
    wb = n_heads * DSA_HEAD_DIM
    s_all = past + ts
    assert t % LANES == 0 and CHUNK * 2 == LANES
    assert (s_all - 1) // CHUNK <= past // CHUNK

    mod = _ada_mod(jnp.concatenate([c_prompt, c_sample], axis=0), w_ada[0], b_ada[0])
    shift, scale, gate = mod[:, :d], mod[:, d:2 * d], mod[:, 2 * d:]
    mod_p = [v[:b].reshape(b, 1, d) for v in (scale, shift, gate)]
    mod_s = [jnp.repeat(v[b:], ts, axis=0).reshape(1, nb * ts, d) for v in (scale, shift, gate)]

    w_a, w_b = _pack_w_in(w_in[0], n_heads)
    wkv = _mx(jnp.concatenate([w_uk[0].transpose(2, 0, 1).reshape(KV_LORA, n_heads * MLA_NOPE),
                               w_uv[0].transpose(2, 0, 1).reshape(KV_LORA, n_heads * MLA_V)], axis=1))
    w_o = _mx(w_out[0])
    pos_p = jnp.arange(t, dtype=jnp.int32)
    pos_s = past + jnp.arange(ts, dtype=jnp.int32)
    tab_p = _rope_table(pos_p)
    tab_s = jnp.tile(_rope_table(pos_s), (nb, 1))

    tm = 256
    qcat, ckv_p, krope_p, ga, gb, kcat, vup = _proj_a(
        x_prompt, mod_p[0], mod_p[1], ln_gain[0], tab_p, w_a, mla_kv_gain[0], wkv, tm)
    qb, k_p, v_p, kbb, vbb, qi, idxk_p, kiab, wi = _proj_b(x_prompt, mod_p[0], mod_p[1], ln_gain[0], w_b, tm)
    o_a = _mla_prompt(qcat, kcat, vup, b, t, 256)
    rel = jnp.arange(-(t - 1), t + 1, dtype=jnp.int32)
    bias_tab = rel_bias[_rel_bucket(rel)].T
    o_b = _dsa_prompt(qb, qi, wi, kbb, vbb, kiab, bias_tab, b, t, min(TOPK_MAX, t // 4))
    y_prompt = _out_proj(o_a, o_b, ga, gb, x_prompt, mod_p[2], out_gain_a[0], out_gain_b[0], w_o, final_gain, tm)

    xs3 = x_sample.reshape(1, nb * ts, d)
    tms = min(256, nb * ts)
    qcat_s, ckv_s, krope_s, ga_s, gb_s = _proj_a(
        xs3, mod_s[0], mod_s[1], ln_gain[0], tab_s, w_a, mla_kv_gain[0], None, tms)
    qb_s, k_s, v_s, kbb_s, vbb_s, qi_s, idxk_s, kiab_s, wi_s = _proj_b(xs3, mod_s[0], mod_s[1], ln_gain[0], w_b, tms)
    oa_s = _mla_sample(qcat_s, cache_mla_ckv[0], cache_mla_krope[0], ckv_s, krope_s,
                       _mx(w_uk[0]), _mx(w_uv[0]), ts)
    s_pad = pl.cdiv(s_all, LANES) * LANES
    rel_s = jnp.arange(s_pad, dtype=jnp.int32)[None, :] - pos_s[:, None]
    bias_s = jnp.moveaxis(rel_bias[_rel_bucket(rel_s)], -1, 0)
    ob_s = _dsa_sample(qb_s, qi_s, wi_s, cache_dsa_k[0].reshape(nb, past, wb), cache_dsa_v[0].reshape(nb, past, wb),
                       cache_idx_k[0], kbb_s, vbb_s, kiab_s, bias_s, ts, min(TOPK_MAX, s_all // 4))
    y_sample = _out_proj(oa_s, ob_s, ga_s, gb_s, xs3, mod_s[2], out_gain_a[0], out_gain_b[0], w_o, final_gain, tms)

    hd = (n_heads, DSA_HEAD_DIM)
    return (y_prompt, y_sample.reshape(nb, ts, d),
            ckv_p.reshape(1, b, t, KV_LORA), krope_p.reshape(1, b, t, MLA_ROPE),
            k_p.reshape(1, b, t, *hd), v_p.reshape(1, b, t, *hd), idxk_p.reshape(1, b, t, IDX_DIM),
            ckv_s.reshape(1, nb, ts, KV_LORA), krope_s.reshape(1, nb, ts, MLA_ROPE),
            k_s.reshape(1, nb, ts, *hd), v_s.reshape(1, nb, ts, *hd), idxk_s.reshape(1, nb, ts, IDX_DIM))
```

```python
import functools
import math

import jax
import jax.numpy as jnp
from jax import lax
from jax.experimental import pallas as pl
from jax.experimental.pallas import tpu as pltpu

MXU_DT = jnp.bfloat16

CHUNK = 64
MLA_NOPE = 128
MLA_ROPE = 64
MLA_V = 128
KV_LORA = 512
DSA_HEAD_DIM = 128
IDX_HEADS = 16
IDX_DIM = 64
TOPK_MAX = 256
N_BUCKETS = 32
MAX_DISTANCE = 128
ROPE_THETA = 10000.0
EPS = 1e-6

LANES = 128
HEAD_SLOT = 256
NEG_BIG = -1e30
INT_MIN = -2 ** 31
VMEM_LIMIT = 56 * 1024 * 1024


def _cparams(n_grid, vmem=VMEM_LIMIT):
    return pltpu.CompilerParams(dimension_semantics=("arbitrary",) * n_grid, vmem_limit_bytes=vmem)


def _mx(v):
    return v.astype(MXU_DT)


def _dot(a, b):
    return jnp.dot(a, b, preferred_element_type=jnp.float32)


def _dot_nt(a, b):
    return lax.dot_general(a, b, (((1,), (1,)), ((), ())), preferred_element_type=jnp.float32)


def _silu(v):
    return v * (1.0 / (1.0 + jnp.exp(-v)))


def _resident(shape):
    nd = len(shape)
    return pl.BlockSpec(shape, lambda *_: (0,) * nd, pipeline_mode=pl.Buffered(1))


def _ada_kernel(c_ref, w_ref, b_ref, o_ref):
    a = _mx(_silu(c_ref[...]))
    o_ref[...] = _dot(a, _mx(w_ref[...])) + b_ref[...]


def _ada_mod(c_all, w_ada, b_ada):
    m, d = c_all.shape
    n = w_ada.shape[1]
    tn = 1024
    return pl.pallas_call(
        _ada_kernel,
        grid=(n // tn,),
        in_specs=[pl.BlockSpec((m, d), lambda j: (0, 0)),
                  pl.BlockSpec((d, tn), lambda j: (0, j)),
                  pl.BlockSpec((1, tn), lambda j: (0, j))],
        out_specs=pl.BlockSpec((m, tn), lambda j: (0, j)),
        out_shape=jax.ShapeDtypeStruct((m, n), jnp.float32),
        compiler_params=_cparams(1),
        name="ada_mod",
    )(c_all, w_ada, b_ada.reshape(1, n))


def _modulated_norm(x_ref, scale_ref, shift_ref, lng_ref):
    x = x_ref[...]
    xn = x * lax.rsqrt(jnp.mean(x * x, axis=-1, keepdims=True) + EPS) * lng_ref[...]
    return _mx(xn * (1.0 + scale_ref[...]) + shift_ref[...])


def _rope128(a, tab):
    t = a * tab
    return t + pltpu.roll(t, 64, 1)


def _proj_a_kernel(n_heads, emit_kv, x_ref, scale_ref, shift_ref, lng_ref, tab_ref,
                   wq_ref, wc_ref, wr_ref, wga_ref, wgb_ref, kvg_ref, *rest):
    if emit_kv:
        wkv_ref, qcat_ref, ckv_ref, krope_ref, ga_ref, gb_ref, kcat_ref, vup_ref = rest
    else:
        qcat_ref, ckv_ref, krope_ref, ga_ref, gb_ref = rest
    hb = _modulated_norm(x_ref, scale_ref, shift_ref, lng_ref)
    tab = tab_ref[...]
    qscale = (MLA_NOPE + MLA_ROPE) ** -0.5
    for h in range(n_heads):
        c0 = h * HEAD_SLOT
        a = _dot(hb, wq_ref[:, c0:c0 + HEAD_SLOT])
        qcat_ref[:, c0:c0 + LANES] = _mx(a[:, :LANES] * qscale)
        qcat_ref[:, c0 + LANES:c0 + HEAD_SLOT] = _mx(_rope128(a[:, LANES:], tab) * qscale)
    c = _dot(hb, wc_ref[...])
    cn = c * lax.rsqrt(jnp.mean(c * c, axis=-1, keepdims=True) + EPS) * kvg_ref[...]
    ckv_ref[...] = cn
    r = _rope128(_dot(hb, wr_ref[...]), tab)
    krope_ref[...] = r[:, :MLA_ROPE]
    ga_ref[...] = _mx(_silu(_dot(hb, wga_ref[...])))
    gb_ref[...] = _mx(_silu(_dot(hb, wgb_ref[...])))
    if emit_kv:
        cb = _mx(cn)
        lane = lax.broadcasted_iota(jnp.int32, r.shape, 1)
        krz = _mx(jnp.where(lane < MLA_ROPE, r, 0.0))
        kn = _dot(cb, wkv_ref[:, :n_heads * MLA_NOPE])
        for h in range(n_heads):
            c0 = h * HEAD_SLOT
            kcat_ref[:, c0:c0 + LANES] = _mx(kn[:, h * MLA_NOPE:(h + 1) * MLA_NOPE])
            kcat_ref[:, c0 + LANES:c0 + HEAD_SLOT] = krz
        vup_ref[...] = _mx(_dot(cb, wkv_ref[:, n_heads * MLA_NOPE:]))


def _proj_b_kernel(n_heads, x_ref, scale_ref, shift_ref, lng_ref, wq_ref, wk_ref, wv_ref, wqi_ref, wkw_ref,
                   qb_ref, kb_ref, vb_ref, kbb_ref, vbb_ref, qi_ref, ki_ref, kiab_ref, wi_ref):
    hb = _modulated_norm(x_ref, scale_ref, shift_ref, lng_ref)
    dscale = DSA_HEAD_DIM ** -0.5
    step = 512
    for c0 in range(0, wq_ref.shape[1], step):
        qb_ref[:, c0:c0 + step] = _mx(_dot(hb, wq_ref[:, c0:c0 + step]) * dscale)
    for w_ref, f_ref, b_ref in ((wk_ref, kb_ref, kbb_ref), (wv_ref, vb_ref, vbb_ref)):
        for c0 in range(0, w_ref.shape[1], step):
            kv = _dot(hb, w_ref[:, c0:c0 + step])
            b_ref[:, c0:c0 + step] = _mx(kv)
            for h in range(c0 // DSA_HEAD_DIM, (c0 + step) // DSA_HEAD_DIM):
                f_ref[:, h, :] = kv[:, h * DSA_HEAD_DIM - c0:(h + 1) * DSA_HEAD_DIM - c0]
    for c0 in range(0, wqi_ref.shape[1], step):
        qi_ref[:, c0:c0 + step] = _mx(_dot(hb, wqi_ref[:, c0:c0 + step]))
    a = _dot(hb, wkw_ref[...])
    ki_ref[...] = a[:, :IDX_DIM]
    lane = lax.broadcasted_iota(jnp.int32, a.shape, 1)
    kz = jnp.where(lane < IDX_DIM, a, 0.0)
    kiab_ref[:, :LANES] = _mx(kz)
    kiab_ref[:, LANES:] = _mx(pltpu.roll(kz, IDX_DIM, 1))
    wi_ref[...] = a * (IDX_HEADS ** -0.5 * IDX_DIM ** -0.5)


def _row_specs(x3, mod_rows, tm):
    bv, tv, d = x3.shape
    x_spec = pl.BlockSpec((None, tm, d), lambda b, i: (b, i, 0))
    if mod_rows == 1:
        m_spec = pl.BlockSpec((None, 1, d), lambda b, i: (b, 0, 0))
    else:
        m_spec = pl.BlockSpec((None, tm, d), lambda b, i: (b, i, 0))
    return x_spec, m_spec


def _out2d(m, width, dtype, tm, nt):
    return (jax.ShapeDtypeStruct((m, width), dtype),
            pl.BlockSpec((tm, width), lambda b, i: (b * nt + i, 0)))


def _proj_a(x3, scale, shift, ln_gain, tab, ws_a, kv_gain, wkv, tm):
    bv, tv, d = x3.shape
    nt = tv // tm
    m = bv * tv
    n_heads = d // 256
    wa = n_heads * MLA_V
    emit_kv = wkv is not None
    x_spec, m_spec = _row_specs(x3, scale.shape[1], tm)
    in_specs = ([x_spec, m_spec, m_spec, _resident((1, d)), pl.BlockSpec((tm, LANES), lambda b, i: (i, 0))]
                + [_resident(w.shape) for w in ws_a] + [_resident((1, KV_LORA))])
    args = [x3, scale, shift, ln_gain.reshape(1, d), tab, *ws_a, kv_gain.reshape(1, KV_LORA)]
    outs = [_out2d(m, n_heads * HEAD_SLOT, MXU_DT, tm, nt),
            _out2d(m, KV_LORA, jnp.float32, tm, nt),
            _out2d(m, MLA_ROPE, jnp.float32, tm, nt),
            _out2d(m, wa, MXU_DT, tm, nt),
            _out2d(m, wa, MXU_DT, tm, nt)]
    if emit_kv:
        in_specs.append(_resident(wkv.shape))
        args.append(wkv)
        outs += [_out2d(m, n_heads * HEAD_SLOT, MXU_DT, tm, nt),
                 _out2d(m, wa, MXU_DT, tm, nt)]
    return pl.pallas_call(
        functools.partial(_proj_a_kernel, n_heads, emit_kv),
        grid=(bv, nt),
        in_specs=in_specs,
        out_specs=[o[1] for o in outs],
        out_shape=[o[0] for o in outs],
        compiler_params=_cparams(2),
        name="proj_a",
    )(*args)


def _proj_b(x3, scale, shift, ln_gain, ws_b, tm):
    bv, tv, d = x3.shape
    nt = tv // tm
    m = bv * tv
    n_heads = d // 256
    width_b = n_heads * DSA_HEAD_DIM
    x_spec, m_spec = _row_specs(x3, scale.shape[1], tm)
    cache_rows = (jax.ShapeDtypeStruct((m, n_heads, DSA_HEAD_DIM), jnp.float32),
                  pl.BlockSpec((tm, n_heads, DSA_HEAD_DIM), lambda b, i: (b * nt + i, 0, 0)))
    outs = [_out2d(m, width_b, MXU_DT, tm, nt),
            cache_rows,
            cache_rows,
            _out2d(m, width_b, MXU_DT, tm, nt),
            _out2d(m, width_b, MXU_DT, tm, nt),
            _out2d(m, IDX_HEADS * IDX_DIM, MXU_DT, tm, nt),
            _out2d(m, IDX_DIM, jnp.float32, tm, nt),
            _out2d(m, 2 * LANES, MXU_DT, tm, nt),
            _out2d(m, LANES, jnp.float32, tm, nt)]
    return pl.pallas_call(
        functools.partial(_proj_b_kernel, n_heads),
        grid=(bv, nt),
        in_specs=[x_spec, m_spec, m_spec, _resident((1, d))] + [_resident(w.shape) for w in ws_b],
        out_specs=[o[1] for o in outs],
        out_shape=[o[0] for o in outs],
        compiler_params=_cparams(2),
        name="proj_b",
    )(x3, scale, shift, ln_gain.reshape(1, d), *ws_b)


def _softmax_pv(s, v):
    m = jnp.max(s, axis=-1, keepdims=True)
    p = jnp.exp(s - m)
    l = jnp.sum(p, axis=-1, keepdims=True)
    return _dot(_mx(p), v) * (1.0 / l)


def _topk_mask(score, adm, topk, key_ref, madd_ref):
    rows, n = score.shape
    if n <= topk:
        madd_ref[...] = jnp.where(adm, 0.0, NEG_BIG)
        return
    score = jnp.where(score == 0.0, 0.0, score)
    bits = pltpu.bitcast(score, jnp.int32)
    key = bits ^ ((bits >> 31) & 0x7FFFFFFF)
    key_ref[...] = jnp.where(adm, key, INT_MIN)
    kf = float(topk)

    def count(pred):
        return jnp.sum(jnp.where(pred, 1.0, 0.0), axis=-1, keepdims=True)

    def bit_step(i, thr):
        inc = lax.shift_left(jnp.int32(1), 31 - i)
        cand = thr + inc
        ok = count(key_ref[...] >= cand) >= kf
        return jnp.where(ok, cand, thr)

    thr = lax.fori_loop(0, 32, bit_step, jnp.full((rows, 1), INT_MIN, jnp.int32))
    ge = key_ref[...] >= thr
    madd_ref[...] = jnp.where(ge, 0.0, NEG_BIG)

    @pl.when(jnp.max(count(ge)) > kf)
    def _():
        key = key_ref[...]
        gt = key > thr
        need = kf - count(gt)
        eqf = jnp.where(key == thr, 1.0, 0.0)
        col = lax.broadcasted_iota(jnp.int32, (rows, n), 1)
        nbits = max(1, int(n).bit_length())

        def col_step(i, bound):
            cand = bound + lax.shift_left(jnp.int32(1), nbits - 1 - i)
            taken = jnp.sum(jnp.where(col < cand, eqf, 0.0), axis=-1, keepdims=True)
            return jnp.where(taken <= need, cand, bound)

        bound = lax.fori_loop(0, nbits, col_step, jnp.zeros((rows, 1), jnp.int32))
        tie_madd = jnp.where(jnp.where(col < bound, eqf, 0.0) > 0.5, 0.0, NEG_BIG)
        madd_ref[...] = jnp.where(adm, jnp.where(gt, 0.0, tie_madd), NEG_BIG)


def _indexer_scores(qi_ref, wi, kia, kib):
    half = IDX_HEADS // 2
    score = None
    for j in range(half):
        qp = qi_ref[:, j * LANES:(j + 1) * LANES]
        da = jnp.maximum(_dot_nt(qp, kia), 0.0) * wi[:, IDX_DIM + j:IDX_DIM + j + 1]
        db = jnp.maximum(_dot_nt(qp, kib), 0.0) * wi[:, IDX_DIM + half + j:IDX_DIM + half + j + 1]
        score = da + db if score is None else score + da + db
    return score


PROMPT_TQ = 256


def _chunk_madd(row0, tq, n_keys):
    qc = (row0 + lax.broadcasted_iota(jnp.int32, (tq, n_keys), 0)) // CHUNK
    kc = lax.broadcasted_iota(jnp.int32, (tq, n_keys), 1) // CHUNK
    return kc <= qc


def _mla_prompt_kernel(n_heads, row0, q_ref, k_ref, v_ref, o_ref):
    tq, n_keys = q_ref.shape[0], k_ref.shape[0]
    madd = jnp.where(_chunk_madd(row0, tq, n_keys), 0.0, NEG_BIG)
    for h in range(n_heads):
        c0 = h * HEAD_SLOT
        s = _dot_nt(q_ref[:, c0:c0 + HEAD_SLOT], k_ref[:, c0:c0 + HEAD_SLOT]) + madd
        o_ref[:, h * MLA_V:(h + 1) * MLA_V] = _softmax_pv(s, v_ref[:, h * MLA_V:(h + 1) * MLA_V])


def _mla_prompt(qcat, kcat, vup, b, t):
    tq = PROMPT_TQ
    n_heads = qcat.shape[1] // HEAD_SLOT
    nq = t // tq
    wa = n_heads * MLA_V
    k3 = kcat.reshape(b, t, kcat.shape[1])
    v3 = vup.reshape(b, t, wa)
    tiles = []
    for c in range(nq):
        n_keys = (c + 1) * tq
        tiles.append(pl.pallas_call(
            functools.partial(_mla_prompt_kernel, n_heads, c * tq),
            grid=(b,),
            in_specs=[pl.BlockSpec((tq, qcat.shape[1]), lambda bi, c=c: (bi * nq + c, 0)),
                      pl.BlockSpec((None, n_keys, kcat.shape[1]), lambda bi: (bi, 0, 0)),
                      pl.BlockSpec((None, n_keys, wa), lambda bi: (bi, 0, 0))],
            out_specs=pl.BlockSpec((None, tq, wa), lambda bi: (bi, 0, 0)),
            out_shape=jax.ShapeDtypeStruct((b, tq, wa), jnp.float32),
            compiler_params=_cparams(1),
            name="mla_prompt_%d" % c,
        )(qcat, k3, v3))
    return jnp.concatenate(tiles, axis=1).reshape(b * t, wa)


def _toeplitz_bias(tab_ref, h, start, n_cols, le_mask):
    pieces = []
    prev = None
    for n in range(n_cols // LANES + 1):
        seg = tab_ref[h:h + 1, start + n * LANES:start + (n + 1) * LANES]
        rot = pltpu.roll(jnp.broadcast_to(seg, (LANES, LANES)), 1, 1, stride=1, stride_axis=0)
        if prev is not None:
            pieces.append(jnp.where(le_mask, prev, rot))
        prev = rot
    return jnp.concatenate(pieces, axis=1)


def _dsa_prompt_kernel(n_heads, row0, t, topk, qb_ref, qi_ref, wi_ref, k_ref, v_ref, kiab_ref, tab_ref,
                       o_ref, key_ref, madd_ref):
    tq, n_keys = qb_ref.shape[0], k_ref.shape[0]
    score = _indexer_scores(qi_ref, wi_ref[...], kiab_ref[:, :LANES], kiab_ref[:, LANES:])
    _topk_mask(score, _chunk_madd(row0, tq, n_keys), topk, key_ref, madd_ref)
    near0 = max(0, row0 - LANES)
    n_near = n_keys - near0
    le_mask = (lax.broadcasted_iota(jnp.int32, (LANES, LANES), 1)
               <= lax.broadcasted_iota(jnp.int32, (LANES, LANES), 0))
    for h in range(n_heads):
        c0 = h * DSA_HEAD_DIM
        near = jnp.concatenate(
            [_toeplitz_bias(tab_ref, h, near0 - (row0 + u * LANES) - LANES + t, n_near, le_mask)
             for u in range(tq // LANES)], axis=0)
        if near0 > 0:
            far = jnp.broadcast_to(tab_ref[h:h + 1, 0:1], (tq, near0))
            bias = jnp.concatenate([far, near], axis=1)
        else:
            bias = near
        s = _dot_nt(qb_ref[:, c0:c0 + DSA_HEAD_DIM], k_ref[:, c0:c0 + DSA_HEAD_DIM]) + bias + madd_ref[...]
        o_ref[:, c0:c0 + DSA_HEAD_DIM] = _softmax_pv(s, v_ref[:, c0:c0 + DSA_HEAD_DIM])


def _dsa_prompt(qb, qi, wi, kbb, vbb, kiab, bias_tab, b, t, topk):
    tq = PROMPT_TQ
    assert MAX_DISTANCE <= LANES and tq % LANES == 0
    n_heads = qb.shape[1] // DSA_HEAD_DIM
    nq = t // tq
    wb = qb.shape[1]
    k3, v3, ki3 = kbb.reshape(b, t, wb), vbb.reshape(b, t, wb), kiab.reshape(b, t, 2 * LANES)
    per_b = lambda bi: (bi, 0, 0)
    tiles = []
    for c in range(nq):
        n_keys = (c + 1) * tq
        row = lambda bi, c=c: (bi * nq + c, 0)
        tiles.append(pl.pallas_call(
            functools.partial(_dsa_prompt_kernel, n_heads, c * tq, t, topk),
            grid=(b,),
            in_specs=[pl.BlockSpec((tq, wb), row),
                      pl.BlockSpec((tq, qi.shape[1]), row),
                      pl.BlockSpec((tq, LANES), row),
                      pl.BlockSpec((None, n_keys, wb), per_b),
                      pl.BlockSpec((None, n_keys, wb), per_b),
                      pl.BlockSpec((None, n_keys, 2 * LANES), per_b),
                      _resident(bias_tab.shape)],
            out_specs=pl.BlockSpec((None, tq, wb), per_b),
            out_shape=jax.ShapeDtypeStruct((b, tq, wb), jnp.float32),
            scratch_shapes=[pltpu.VMEM((tq, n_keys), jnp.int32), pltpu.VMEM((tq, n_keys), jnp.float32)],
            compiler_params=_cparams(1),
            name="dsa_prompt_%d" % c,
        )(qb, qi, wi, k3, v3, ki3, bias_tab))
    return jnp.concatenate(tiles, axis=1).reshape(b * t, wb)


def _mla_sample_kernel(n_heads, past, ts, qcat_ref, cckv_ref, ckr_ref, nckv_ref, nkr_ref, wuk_ref, wuv_ref,
                       o_ref, kall_ref, rall_ref):
    s_pad = kall_ref.shape[0]
    n_keys = past + ts

    @pl.when(pl.program_id(0) == 0)
    def _():
        kall_ref[...] = jnp.zeros_like(kall_ref)
        rall_ref[...] = jnp.zeros_like(rall_ref)

    kall_ref[0:past, :] = _mx(cckv_ref[...])
    kall_ref[past:n_keys, :] = _mx(nckv_ref[...])
    rall_ref[0:past, 0:MLA_ROPE] = _mx(ckr_ref[...])
    rall_ref[past:n_keys, 0:MLA_ROPE] = _mx(nkr_ref[...])
    qlat, qrope = [], []
    for h in range(n_heads):
        c0 = h * HEAD_SLOT
        qlat.append(_mx(_dot(qcat_ref[:, c0:c0 + LANES], wuk_ref[h])))
        qrope.append(qcat_ref[:, c0 + LANES:c0 + HEAD_SLOT])
    qlat = jnp.concatenate(qlat, axis=0)
    qrope = jnp.concatenate(qrope, axis=0)
    s = _dot_nt(qlat, kall_ref[...]) + _dot_nt(qrope, rall_ref[...])
    col = lax.broadcasted_iota(jnp.int32, s.shape, 1)
    s = jnp.where(col < n_keys, s, NEG_BIG)
    olat = _mx(_softmax_pv(s, kall_ref[...]))
    for h in range(n_heads):
        o_ref[:, h * MLA_V:(h + 1) * MLA_V] = _dot_nt(olat[h * ts:(h + 1) * ts], wuv_ref[h])


def _mla_sample(qcat, cache_ckv, cache_kr, new_ckv, new_kr, wuk, wuv, ts):
    nb, past, c = cache_ckv.shape
    n_heads = qcat.shape[1] // HEAD_SLOT
    wa = n_heads * MLA_V
    s_pad = pl.cdiv(past + ts, LANES) * LANES
    row = lambda bi: (bi, 0)
    return pl.pallas_call(
        functools.partial(_mla_sample_kernel, n_heads, past, ts),
        grid=(nb,),
        in_specs=[pl.BlockSpec((ts, qcat.shape[1]), row),
                  pl.BlockSpec((None, past, c), lambda bi: (bi, 0, 0)),
                  pl.BlockSpec((None, past, MLA_ROPE), lambda bi: (bi, 0, 0)),
                  pl.BlockSpec((ts, c), row),
                  pl.BlockSpec((ts, MLA_ROPE), row),
                  _resident(wuk.shape), _resident(wuv.shape)],
        out_specs=pl.BlockSpec((ts, wa), row),
        out_shape=jax.ShapeDtypeStruct((nb * ts, wa), jnp.float32),
        scratch_shapes=[pltpu.VMEM((s_pad, c), MXU_DT), pltpu.VMEM((s_pad, LANES), MXU_DT)],
        compiler_params=_cparams(1),
        name="mla_sample",
    )(qcat, cache_ckv, cache_kr, new_ckv, new_kr, wuk, wuv)


def _dsa_sample_kernel(n_heads, past, ts, topk, qb_ref, qi_ref, wi_ref, ck_ref, cv_ref, cki_ref,
                       nk_ref, nv_ref, nkiab_ref, bias_ref, o_ref,
                       kall_ref, vall_ref, kia_ref, kib_ref, key_ref, madd_ref):
    s_pad = kall_ref.shape[0]
    n_keys = past + ts

    @pl.when(pl.program_id(0) == 0)
    def _():
        for ref in (kall_ref, vall_ref, kia_ref, kib_ref):
            ref[...] = jnp.zeros_like(ref)

    for h in range(n_heads):
        c0 = h * DSA_HEAD_DIM
        kall_ref[0:past, c0:c0 + DSA_HEAD_DIM] = _mx(ck_ref[:, h, :])
        vall_ref[0:past, c0:c0 + DSA_HEAD_DIM] = _mx(cv_ref[:, h, :])
    kall_ref[past:n_keys, :] = nk_ref[...]
    vall_ref[past:n_keys, :] = nv_ref[...]
    cki = _mx(cki_ref[...])
    kia_ref[0:past, 0:IDX_DIM] = cki
    kib_ref[0:past, IDX_DIM:LANES] = cki
    kia_ref[past:n_keys, :] = nkiab_ref[:, :LANES]
    kib_ref[past:n_keys, :] = nkiab_ref[:, LANES:]
    score = _indexer_scores(qi_ref, wi_ref[...], kia_ref[...], kib_ref[...])
    col = lax.broadcasted_iota(jnp.int32, (ts, s_pad), 1)
    _topk_mask(score, col < n_keys, topk, key_ref, madd_ref)
    for h in range(n_heads):
        c0 = h * DSA_HEAD_DIM
        s = _dot_nt(qb_ref[:, c0:c0 + DSA_HEAD_DIM], kall_ref[:, c0:c0 + DSA_HEAD_DIM]) + bias_ref[h] + madd_ref[...]
        o_ref[:, c0:c0 + DSA_HEAD_DIM] = _softmax_pv(s, vall_ref[:, c0:c0 + DSA_HEAD_DIM])


def _dsa_sample(qb, qi, wi, cache_k, cache_v, cache_ki, new_k, new_v, new_kiab, bias, ts, topk):
    nb, past, n_heads, _ = cache_k.shape
    wb = n_heads * DSA_HEAD_DIM
    s_pad = bias.shape[2]
    row = lambda bi: (bi, 0)
    per_b = lambda bi: (bi, 0, 0)
    per_b4 = lambda bi: (bi, 0, 0, 0)
    return pl.pallas_call(
        functools.partial(_dsa_sample_kernel, n_heads, past, ts, topk),
        grid=(nb,),
        in_specs=[pl.BlockSpec((ts, wb), row),
                  pl.BlockSpec((ts, qi.shape[1]), row),
                  pl.BlockSpec((ts, LANES), row),
                  pl.BlockSpec((None, past, n_heads, DSA_HEAD_DIM), per_b4),
                  pl.BlockSpec((None, past, n_heads, DSA_HEAD_DIM), per_b4),
                  pl.BlockSpec((None, past, IDX_DIM), per_b),
                  pl.BlockSpec((ts, wb), row),
                  pl.BlockSpec((ts, wb), row),
                  pl.BlockSpec((ts, 2 * LANES), row),
                  _resident(bias.shape)],
        out_specs=pl.BlockSpec((ts, wb), row),
        out_shape=jax.ShapeDtypeStruct((nb * ts, wb), jnp.float32),
        scratch_shapes=[pltpu.VMEM((s_pad, wb), MXU_DT), pltpu.VMEM((s_pad, wb), MXU_DT),
                        pltpu.VMEM((s_pad, LANES), MXU_DT), pltpu.VMEM((s_pad, LANES), MXU_DT),
                        pltpu.VMEM((ts, s_pad), jnp.int32), pltpu.VMEM((ts, s_pad), jnp.float32)],
        compiler_params=_cparams(1),
        name="dsa_sample",
    )(qb, qi, wi, cache_k, cache_v, cache_ki, new_k, new_v, new_kiab, bias)


def _out_kernel(oa_ref, ob_ref, ga_ref, gb_ref, x_ref, gate_ref, gna_ref, gnb_ref, w_ref, fg_ref, y_ref):
    def gated(o_ref, g_ref, gain_ref):
        o = o_ref[...]
        on = o * lax.rsqrt(jnp.mean(o * o, axis=-1, keepdims=True) + EPS) * gain_ref[...]
        return _mx(on * g_ref[...].astype(jnp.float32))

    wa = oa_ref.shape[1]
    out = _dot(gated(oa_ref, ga_ref, gna_ref), w_ref[0:wa, :]) + _dot(gated(ob_ref, gb_ref, gnb_ref), w_ref[wa:, :])
    xn = x_ref[...] + gate_ref[...] * out
    y_ref[...] = xn * lax.rsqrt(jnp.mean(xn * xn, axis=-1, keepdims=True) + EPS) * fg_ref[...]


def _out_proj(oa, ob, ga, gb, x3, gate, gain_a, gain_b, w_out, final_gain, tm):
    bv, tv, d = x3.shape
    nt = tv // tm
    wa = oa.shape[1]
    x_spec, g_spec = _row_specs(x3, gate.shape[1], tm)
    row = lambda b, i: (b * nt + i, 0)
    return pl.pallas_call(
        _out_kernel,
        grid=(bv, nt),
        in_specs=[pl.BlockSpec((tm, wa), row), pl.BlockSpec((tm, wa), row),
                  pl.BlockSpec((tm, wa), row), pl.BlockSpec((tm, wa), row),
                  x_spec, g_spec, _resident((1, wa)), _resident((1, wa)),
                  _resident(w_out.shape), _resident((1, d))],
        out_specs=pl.BlockSpec((None, tm, d), lambda b, i: (b, i, 0)),
        out_shape=jax.ShapeDtypeStruct((bv, tv, d), jnp.float32),
        compiler_params=_cparams(2),
        name="out_proj",
    )(oa, ob, ga, gb, x3, gate, gain_a.reshape(1, wa), gain_b.reshape(1, wa), w_out, final_gain.reshape(1, d))


def _pack_w_in(w_in, n_heads):
    d = w_in.shape[0]
    wa = n_heads * MLA_V
    sizes = (n_heads * (MLA_NOPE + MLA_ROPE), KV_LORA, MLA_ROPE, wa, wa, wa, wa,
             IDX_HEADS * IDX_DIM, IDX_DIM, IDX_HEADS, wa)
    offs = [0]
    for s in sizes:
        offs.append(offs[-1] + s)
    assert offs[-1] == w_in.shape[1]
    q_a, ckv, kr, g_a, q_b, k_b, v_b, q_i, k_i, w_i, g_b = (w_in[:, offs[n]:offs[n + 1]] for n in range(len(sizes)))
    half = MLA_ROPE // 2
    q_a = q_a.reshape(d, n_heads, MLA_NOPE + MLA_ROPE)
    r1, r2 = q_a[:, :, MLA_NOPE:MLA_NOPE + half], q_a[:, :, MLA_NOPE + half:]
    q_a = jnp.concatenate([q_a[:, :, :MLA_NOPE], r1, r2, r2, r1], axis=2).reshape(d, n_heads * HEAD_SLOT)
    kr = jnp.concatenate([kr[:, :half], kr[:, half:], kr[:, half:], kr[:, :half]], axis=1)
    q_i = q_i.reshape(d, 2, IDX_HEADS // 2, IDX_DIM).transpose(0, 2, 1, 3).reshape(d, IDX_HEADS * IDX_DIM)
    kiw = jnp.concatenate([k_i, w_i, jnp.zeros((d, LANES - IDX_DIM - IDX_HEADS), w_in.dtype)], axis=1)
    return [_mx(w) for w in (q_a, ckv, kr, g_a, g_b)], [_mx(w) for w in (q_b, k_b, v_b, q_i, kiw)]


def _rope_table(pos):
    half = MLA_ROPE // 2
    freqs = jnp.power(ROPE_THETA, -jnp.arange(half, dtype=jnp.float32) / half)
    ang = pos.astype(jnp.float32)[:, None] * freqs
    cos, sin = jnp.cos(ang), jnp.sin(ang)
    return jnp.concatenate([cos, cos, -sin, sin], axis=1)


def _rel_bucket(rel):
    nb = N_BUCKETS // 2
    max_exact = nb // 2
    n = jnp.abs(rel)
    nf = jnp.maximum(n, 1).astype(jnp.float32)
    large = max_exact + (jnp.log(nf / max_exact) / math.log(MAX_DISTANCE / max_exact)
                         * (nb - max_exact)).astype(jnp.int32)
    large = jnp.minimum(large, nb - 1)
    return jnp.where(rel > 0, nb, 0) + jnp.where(n < max_exact, n, large)


def kernel(x_prompt, x_sample, cache_mla_ckv, cache_mla_krope, cache_dsa_k, cache_dsa_v, cache_idx_k,
           c_prompt, c_sample, w_ada, b_ada, ln_gain, w_in, mla_kv_gain, w_uk, w_uv, rel_bias,
           out_gain_a, out_gain_b, w_out, final_gain):
    assert w_ada.shape[0] == 1, "single-layer step"
    b, t, d = x_prompt.shape
    nb, ts, _ = x_sample.shape
    past = cache_mla_ckv.shape[2]
    n_heads = d // 256
    wb = n_heads * DSA_HEAD_DIM
    s_all = past + ts
    assert t % PROMPT_TQ == 0 and PROMPT_TQ % CHUNK == 0
    assert (s_all - 1) // CHUNK <= past // CHUNK

    mod = _ada_mod(jnp.concatenate([c_prompt, c_sample], axis=0), w_ada[0], b_ada[0])
    shift, scale, gate = mod[:, :d], mod[:, d:2 * d], mod[:, 2 * d:]
    mod_p = [v[:b].reshape(b, 1, d) for v in (scale, shift, gate)]
    mod_s = [jnp.broadcast_to(v[b:, None, :], (nb, ts, d)).reshape(1, nb * ts, d) for v in (scale, shift, gate)]

    w_a, w_b = _pack_w_in(w_in[0], n_heads)
    wkv = _mx(jnp.concatenate([w_uk[0].transpose(2, 0, 1).reshape(KV_LORA, n_heads * MLA_NOPE),
                               w_uv[0].transpose(2, 0, 1).reshape(KV_LORA, n_heads * MLA_V)], axis=1))
    w_o = _mx(w_out[0])
    pos_p = jnp.arange(t, dtype=jnp.int32)
    pos_s = past + jnp.arange(ts, dtype=jnp.int32)
    tab_p = _rope_table(pos_p)
    tab_s = jnp.tile(_rope_table(pos_s), (nb, 1))

    tm = 256
    qcat, ckv_p, krope_p, ga, gb, kcat, vup = _proj_a(
        x_prompt, mod_p[0], mod_p[1], ln_gain[0], tab_p, w_a, mla_kv_gain[0], wkv, tm)
    qb, k_p, v_p, kbb, vbb, qi, idxk_p, kiab, wi = _proj_b(x_prompt, mod_p[0], mod_p[1], ln_gain[0], w_b, tm)
    o_a = _mla_prompt(qcat, kcat, vup, b, t)
    rel = jnp.arange(-(t - 1), t + 1, dtype=jnp.int32)
    bias_tab = rel_bias[_rel_bucket(rel)].T
    o_b = _dsa_prompt(qb, qi, wi, kbb, vbb, kiab, bias_tab, b, t, min(TOPK_MAX, t // 4))
    y_prompt = _out_proj(o_a, o_b, ga, gb, x_prompt, mod_p[2], out_gain_a[0], out_gain_b[0], w_o, final_gain, tm)

    xs3 = x_sample.reshape(1, nb * ts, d)
    tms = min(256, nb * ts)
    qcat_s, ckv_s, krope_s, ga_s, gb_s = _proj_a(
        xs3, mod_s[0], mod_s[1], ln_gain[0], tab_s, w_a, mla_kv_gain[0], None, tms)
    qb_s, k_s, v_s, kbb_s, vbb_s, qi_s, idxk_s, kiab_s, wi_s = _proj_b(xs3, mod_s[0], mod_s[1], ln_gain[0], w_b, tms)
    oa_s = _mla_sample(qcat_s, cache_mla_ckv[0], cache_mla_krope[0], ckv_s, krope_s,
                       _mx(w_uk[0]), _mx(w_uv[0]), ts)
    s_pad = pl.cdiv(s_all, LANES) * LANES
    rel_s = jnp.arange(s_pad, dtype=jnp.int32)[None, :] - pos_s[:, None]
    bias_s = jnp.moveaxis(rel_bias[_rel_bucket(rel_s)], -1, 0)
    ob_s = _dsa_sample(qb_s, qi_s, wi_s, cache_dsa_k[0], cache_dsa_v[0],
                       cache_idx_k[0], kbb_s, vbb_s, kiab_s, bias_s, ts, min(TOPK_MAX, s_all // 4))
    y_sample = _out_proj(oa_s, ob_s, ga_s, gb_s, xs3, mod_s[2], out_gain_a[0], out_gain_b[0], w_o, final_gain, tms)

    hd = (n_heads, DSA_HEAD_DIM)
    return (y_prompt, y_sample.reshape(nb, ts, d),
            ckv_p.reshape(1, b, t, KV_LORA), krope_p.reshape(1, b, t, MLA_ROPE),
            k_p.reshape(1, b, t, *hd), v_p.reshape(1, b, t, *hd), idxk_p.reshape(1, b, t, IDX_DIM),
            ckv_s.reshape(1, nb, ts, KV_LORA), krope_s.reshape(1, nb, ts, MLA_ROPE),
            k_s.reshape(1, nb, ts, *hd), v_s.reshape(1, nb, ts, *hd), idxk_s.reshape(1, nb, ts, IDX_DIM))
```

```python
import functools
import math

import jax
import jax.numpy as jnp
from jax import lax
from jax.experimental import pallas as pl
from jax.experimental.pallas import tpu as pltpu

MXU_DT = jnp.bfloat16

CHUNK = 64
MLA_NOPE = 128
MLA_ROPE = 64
MLA_V = 128
KV_LORA = 512
DSA_HEAD_DIM = 128
IDX_HEADS = 16
IDX_DIM = 64
TOPK_MAX = 256
N_BUCKETS = 32
MAX_DISTANCE = 128
ROPE_THETA = 10000.0
EPS = 1e-6

LANES = 128
HEAD_SLOT = 256
NEG_BIG = -1e30
INT_MIN = -2 ** 31
VMEM_LIMIT = 56 * 1024 * 1024


def _cparams(n_grid, vmem=VMEM_LIMIT):
    return pltpu.CompilerParams(dimension_semantics=("arbitrary",) * n_grid, vmem_limit_bytes=vmem)


def _mx(v):
    return v.astype(MXU_DT)


def _dot(a, b):
    return jnp.dot(a, b, preferred_element_type=jnp.float32)


def _dot_nt(a, b):
    return lax.dot_general(a, b, (((1,), (1,)), ((), ())), preferred_element_type=jnp.float32)


def _silu(v):
    return v * (1.0 / (1.0 + jnp.exp(-v)))


def _resident(shape):
    nd = len(shape)
    return pl.BlockSpec(shape, lambda *_: (0,) * nd, pipeline_mode=pl.Buffered(1))


def _ada_kernel(c_ref, w_ref, b_ref, o_ref):
    a = _mx(_silu(c_ref[...]))
    o_ref[...] = _dot(a, _mx(w_ref[...])) + b_ref[...]


def _ada_mod(c_all, w_ada, b_ada):
    m, d = c_all.shape
    n = w_ada.shape[1]
    tn = 1024
    return pl.pallas_call(
        _ada_kernel,
        grid=(n // tn,),
        in_specs=[pl.BlockSpec((m, d), lambda j: (0, 0)),
                  pl.BlockSpec((d, tn), lambda j: (0, j)),
                  pl.BlockSpec((1, tn), lambda j: (0, j))],
        out_specs=pl.BlockSpec((m, tn), lambda j: (0, j)),
        out_shape=jax.ShapeDtypeStruct((m, n), jnp.float32),
        compiler_params=_cparams(1),
        name="ada_mod",
    )(c_all, w_ada, b_ada.reshape(1, n))


def _modulated_norm(x_ref, scale_ref, shift_ref, lng_ref):
    x = x_ref[...]
    xn = x * lax.rsqrt(jnp.mean(x * x, axis=-1, keepdims=True) + EPS) * lng_ref[...]
    return _mx(xn * (1.0 + scale_ref[...]) + shift_ref[...])


def _rope128(a, tab):
    t = a * tab
    return t + pltpu.roll(t, 64, 1)


def _proj_a_kernel(n_heads, emit_kv, x_ref, scale_ref, shift_ref, lng_ref, tab_ref,
                   wq_ref, wc_ref, wr_ref, wga_ref, wgb_ref, kvg_ref, *rest):
    if emit_kv:
        wkv_ref, qcat_ref, ckv_ref, krope_ref, ga_ref, gb_ref, kcat_ref, vup_ref = rest
    else:
        qcat_ref, ckv_ref, krope_ref, ga_ref, gb_ref = rest
    hb = _modulated_norm(x_ref, scale_ref, shift_ref, lng_ref)
    tab = tab_ref[...]
    qscale = (MLA_NOPE + MLA_ROPE) ** -0.5
    for h in range(n_heads):
        c0 = h * HEAD_SLOT
        a = _dot(hb, wq_ref[:, c0:c0 + HEAD_SLOT])
        qcat_ref[:, c0:c0 + LANES] = _mx(a[:, :LANES] * qscale)
        qcat_ref[:, c0 + LANES:c0 + HEAD_SLOT] = _mx(_rope128(a[:, LANES:], tab) * qscale)
    c = _dot(hb, wc_ref[...])
    cn = c * lax.rsqrt(jnp.mean(c * c, axis=-1, keepdims=True) + EPS) * kvg_ref[...]
    ckv_ref[...] = cn
    r = _rope128(_dot(hb, wr_ref[...]), tab)
    krope_ref[...] = r[:, :MLA_ROPE]
    ga_ref[...] = _mx(_silu(_dot(hb, wga_ref[...])))
    gb_ref[...] = _mx(_silu(_dot(hb, wgb_ref[...])))
    if emit_kv:
        cb = _mx(cn)
        lane = lax.broadcasted_iota(jnp.int32, r.shape, 1)
        krz = _mx(jnp.where(lane < MLA_ROPE, r, 0.0))
        kn = _dot(cb, wkv_ref[:, :n_heads * MLA_NOPE])
        for h in range(n_heads):
            c0 = h * HEAD_SLOT
            kcat_ref[:, c0:c0 + LANES] = _mx(kn[:, h * MLA_NOPE:(h + 1) * MLA_NOPE])
            kcat_ref[:, c0 + LANES:c0 + HEAD_SLOT] = krz
        vup_ref[...] = _mx(_dot(cb, wkv_ref[:, n_heads * MLA_NOPE:]))


def _proj_b_kernel(n_heads, x_ref, scale_ref, shift_ref, lng_ref, wq_ref, wk_ref, wv_ref, wqi_ref, wkw_ref,
                   qb_ref, kb_ref, vb_ref, kbb_ref, vbb_ref, qi_ref, ki_ref, kiab_ref, wi_ref):
    hb = _modulated_norm(x_ref, scale_ref, shift_ref, lng_ref)
    dscale = DSA_HEAD_DIM ** -0.5
    step = 512
    for c0 in range(0, wq_ref.shape[1], step):
        qb_ref[:, c0:c0 + step] = _mx(_dot(hb, wq_ref[:, c0:c0 + step]) * dscale)
    for w_ref, f_ref, b_ref in ((wk_ref, kb_ref, kbb_ref), (wv_ref, vb_ref, vbb_ref)):
        for c0 in range(0, w_ref.shape[1], step):
            kv = _dot(hb, w_ref[:, c0:c0 + step])
            f_ref[:, c0:c0 + step] = kv
            b_ref[:, c0:c0 + step] = _mx(kv)
    for c0 in range(0, wqi_ref.shape[1], step):
        qi_ref[:, c0:c0 + step] = _mx(_dot(hb, wqi_ref[:, c0:c0 + step]))
    a = _dot(hb, wkw_ref[...])
    ki_ref[...] = a[:, :IDX_DIM]
    lane = lax.broadcasted_iota(jnp.int32, a.shape, 1)
    kz = jnp.where(lane < IDX_DIM, a, 0.0)
    kiab_ref[:, :LANES] = _mx(kz)
    kiab_ref[:, LANES:] = _mx(pltpu.roll(kz, IDX_DIM, 1))
    wi_ref[...] = a * (IDX_HEADS ** -0.5 * IDX_DIM ** -0.5)


def _row_specs(x3, mod_rows, tm):
    bv, tv, d = x3.shape
    x_spec = pl.BlockSpec((None, tm, d), lambda b, i: (b, i, 0))
    if mod_rows == 1:
        m_spec = pl.BlockSpec((None, 1, d), lambda b, i: (b, 0, 0))
    else:
        m_spec = pl.BlockSpec((None, tm, d), lambda b, i: (b, i, 0))
    return x_spec, m_spec


def _out2d(m, width, dtype, tm, nt):
    return (jax.ShapeDtypeStruct((m, width), dtype),
            pl.BlockSpec((tm, width), lambda b, i: (b * nt + i, 0)))


def _proj_a(x3, scale, shift, ln_gain, tab, ws_a, kv_gain, wkv, tm):
    bv, tv, d = x3.shape
    nt = tv // tm
    m = bv * tv
    n_heads = d // 256
    wa = n_heads * MLA_V
    emit_kv = wkv is not None
    x_spec, m_spec = _row_specs(x3, scale.shape[1], tm)
    in_specs = ([x_spec, m_spec, m_spec, _resident((1, d)), pl.BlockSpec((tm, LANES), lambda b, i: (i, 0))]
                + [_resident(w.shape) for w in ws_a] + [_resident((1, KV_LORA))])
    args = [x3, scale, shift, ln_gain.reshape(1, d), tab, *ws_a, kv_gain.reshape(1, KV_LORA)]
    outs = [_out2d(m, n_heads * HEAD_SLOT, MXU_DT, tm, nt),
            _out2d(m, KV_LORA, jnp.float32, tm, nt),
            _out2d(m, MLA_ROPE, jnp.float32, tm, nt),
            _out2d(m, wa, MXU_DT, tm, nt),
            _out2d(m, wa, MXU_DT, tm, nt)]
    if emit_kv:
        in_specs.append(_resident(wkv.shape))
        args.append(wkv)
        outs += [_out2d(m, n_heads * HEAD_SLOT, MXU_DT, tm, nt),
                 _out2d(m, wa, MXU_DT, tm, nt)]
    return pl.pallas_call(
        functools.partial(_proj_a_kernel, n_heads, emit_kv),
        grid=(bv, nt),
        in_specs=in_specs,
        out_specs=[o[1] for o in outs],
        out_shape=[o[0] for o in outs],
        compiler_params=_cparams(2),
        name="proj_a",
    )(*args)


def _proj_b(x3, scale, shift, ln_gain, ws_b, tm):
    bv, tv, d = x3.shape
    nt = tv // tm
    m = bv * tv
    n_heads = d // 256
    width_b = n_heads * DSA_HEAD_DIM
    x_spec, m_spec = _row_specs(x3, scale.shape[1], tm)
    outs = [_out2d(m, width_b, MXU_DT, tm, nt),
            _out2d(m, width_b, jnp.float32, tm, nt),
            _out2d(m, width_b, jnp.float32, tm, nt),
            _out2d(m, width_b, MXU_DT, tm, nt),
            _out2d(m, width_b, MXU_DT, tm, nt),
            _out2d(m, IDX_HEADS * IDX_DIM, MXU_DT, tm, nt),
            _out2d(m, IDX_DIM, jnp.float32, tm, nt),
            _out2d(m, 2 * LANES, MXU_DT, tm, nt),
            _out2d(m, LANES, jnp.float32, tm, nt)]
    return pl.pallas_call(
        functools.partial(_proj_b_kernel, n_heads),
        grid=(bv, nt),
        in_specs=[x_spec, m_spec, m_spec, _resident((1, d))] + [_resident(w.shape) for w in ws_b],
        out_specs=[o[1] for o in outs],
        out_shape=[o[0] for o in outs],
        compiler_params=_cparams(2),
        name="proj_b",
    )(x3, scale, shift, ln_gain.reshape(1, d), *ws_b)


def _softmax_pv(s, v):
    m = jnp.max(s, axis=-1, keepdims=True)
    p = jnp.exp(s - m)
    l = jnp.sum(p, axis=-1, keepdims=True)
    return _dot(_mx(p), v) * (1.0 / l)


def _topk_mask(score, adm, topk, key_ref, madd_ref):
    rows, n = score.shape
    if n <= topk:
        madd_ref[...] = jnp.where(adm, 0.0, NEG_BIG)
        return
    score = jnp.where(score == 0.0, 0.0, score)
    bits = pltpu.bitcast(score, jnp.int32)
    key = bits ^ ((bits >> 31) & 0x7FFFFFFF)
    key_ref[...] = jnp.where(adm, key, INT_MIN)
    kf = float(topk)

    def count(pred):
        return jnp.sum(jnp.where(pred, 1.0, 0.0), axis=-1, keepdims=True)

    def bit_step(i, thr):
        inc = lax.shift_left(jnp.int32(1), 31 - i)
        cand = thr + inc
        ok = count(key_ref[...] >= cand) >= kf
        return jnp.where(ok, cand, thr)

    thr = lax.fori_loop(0, 32, bit_step, jnp.full((rows, 1), INT_MIN, jnp.int32))
    ge = key_ref[...] >= thr
    madd_ref[...] = jnp.where(ge, 0.0, NEG_BIG)

    @pl.when(jnp.max(count(ge)) > kf)
    def _():
        key = key_ref[...]
        gt = key > thr
        need = kf - count(gt)
        eqf = jnp.where(key == thr, 1.0, 0.0)
        col = lax.broadcasted_iota(jnp.int32, (rows, n), 1)
        nbits = max(1, int(n).bit_length())

        def col_step(i, bound):
            cand = bound + lax.shift_left(jnp.int32(1), nbits - 1 - i)
            taken = jnp.sum(jnp.where(col < cand, eqf, 0.0), axis=-1, keepdims=True)
            return jnp.where(taken <= need, cand, bound)

        bound = lax.fori_loop(0, nbits, col_step, jnp.zeros((rows, 1), jnp.int32))
        tie_madd = jnp.where(jnp.where(col < bound, eqf, 0.0) > 0.5, 0.0, NEG_BIG)
        madd_ref[...] = jnp.where(adm, jnp.where(gt, 0.0, tie_madd), NEG_BIG)


def _indexer_scores(qi_ref, wi, kia, kib):
    half = IDX_HEADS // 2
    score = None
    for j in range(half):
        qp = qi_ref[:, j * LANES:(j + 1) * LANES]
        da = jnp.maximum(_dot_nt(qp, kia), 0.0) * wi[:, IDX_DIM + j:IDX_DIM + j + 1]
        db = jnp.maximum(_dot_nt(qp, kib), 0.0) * wi[:, IDX_DIM + half + j:IDX_DIM + half + j + 1]
        score = da + db if score is None else score + da + db
    return score


PROMPT_TQ = 256


def _chunk_madd(row0, tq, n_keys):
    qc = (row0 + lax.broadcasted_iota(jnp.int32, (tq, n_keys), 0)) // CHUNK
    kc = lax.broadcasted_iota(jnp.int32, (tq, n_keys), 1) // CHUNK
    return kc <= qc


def _mla_prompt_kernel(n_heads, row0, q_ref, k_ref, v_ref, o_ref):
    tq, n_keys = q_ref.shape[0], k_ref.shape[0]
    madd = jnp.where(_chunk_madd(row0, tq, n_keys), 0.0, NEG_BIG)
    for h in range(n_heads):
        c0 = h * HEAD_SLOT
        s = _dot_nt(q_ref[:, c0:c0 + HEAD_SLOT], k_ref[:, c0:c0 + HEAD_SLOT]) + madd
        o_ref[:, h * MLA_V:(h + 1) * MLA_V] = _softmax_pv(s, v_ref[:, h * MLA_V:(h + 1) * MLA_V])


def _mla_prompt(qcat, kcat, vup, b, t):
    tq = PROMPT_TQ
    n_heads = qcat.shape[1] // HEAD_SLOT
    nq = t // tq
    wa = n_heads * MLA_V
    k3 = kcat.reshape(b, t, kcat.shape[1])
    v3 = vup.reshape(b, t, wa)
    tiles = []
    for c in range(nq):
        n_keys = (c + 1) * tq
        tiles.append(pl.pallas_call(
            functools.partial(_mla_prompt_kernel, n_heads, c * tq),
            grid=(b,),
            in_specs=[pl.BlockSpec((tq, qcat.shape[1]), lambda bi, c=c: (bi * nq + c, 0)),
                      pl.BlockSpec((None, n_keys, kcat.shape[1]), lambda bi: (bi, 0, 0)),
                      pl.BlockSpec((None, n_keys, wa), lambda bi: (bi, 0, 0))],
            out_specs=pl.BlockSpec((None, tq, wa), lambda bi: (bi, 0, 0)),
            out_shape=jax.ShapeDtypeStruct((b, tq, wa), jnp.float32),
            compiler_params=_cparams(1),
            name="mla_prompt_%d" % c,
        )(qcat, k3, v3))
    return jnp.concatenate(tiles, axis=1).reshape(b * t, wa)


def _toeplitz_bias(tab_ref, h, start, n_cols, le_mask):
    pieces = []
    prev = None
    for n in range(n_cols // LANES + 1):
        seg = tab_ref[h:h + 1, start + n * LANES:start + (n + 1) * LANES]
        rot = pltpu.roll(jnp.broadcast_to(seg, (LANES, LANES)), 1, 1, stride=1, stride_axis=0)
        if prev is not None:
            pieces.append(jnp.where(le_mask, prev, rot))
        prev = rot
    return jnp.concatenate(pieces, axis=1)


def _dsa_prompt_kernel(n_heads, row0, t, topk, qb_ref, qi_ref, wi_ref, k_ref, v_ref, kiab_ref, tab_ref,
                       o_ref, key_ref, madd_ref):
    tq, n_keys = qb_ref.shape[0], k_ref.shape[0]
    score = _indexer_scores(qi_ref, wi_ref[...], kiab_ref[:, :LANES], kiab_ref[:, LANES:])
    _topk_mask(score, _chunk_madd(row0, tq, n_keys), topk, key_ref, madd_ref)
    near0 = max(0, row0 - LANES)
    n_near = n_keys - near0
    le_mask = (lax.broadcasted_iota(jnp.int32, (LANES, LANES), 1)
               <= lax.broadcasted_iota(jnp.int32, (LANES, LANES), 0))
    for h in range(n_heads):
        c0 = h * DSA_HEAD_DIM
        near = jnp.concatenate(
            [_toeplitz_bias(tab_ref, h, near0 - (row0 + u * LANES) - LANES + t, n_near, le_mask)
             for u in range(tq // LANES)], axis=0)
        if near0 > 0:
            far = jnp.broadcast_to(tab_ref[h:h + 1, 0:1], (tq, near0))
            bias = jnp.concatenate([far, near], axis=1)
        else:
            bias = near
        s = _dot_nt(qb_ref[:, c0:c0 + DSA_HEAD_DIM], k_ref[:, c0:c0 + DSA_HEAD_DIM]) + bias + madd_ref[...]
        o_ref[:, c0:c0 + DSA_HEAD_DIM] = _softmax_pv(s, v_ref[:, c0:c0 + DSA_HEAD_DIM])


def _dsa_prompt(qb, qi, wi, kbb, vbb, kiab, bias_tab, b, t, topk):
    tq = PROMPT_TQ
    assert MAX_DISTANCE <= LANES and tq % LANES == 0
    n_heads = qb.shape[1] // DSA_HEAD_DIM
    nq = t // tq
    wb = qb.shape[1]
    k3, v3, ki3 = kbb.reshape(b, t, wb), vbb.reshape(b, t, wb), kiab.reshape(b, t, 2 * LANES)
    per_b = lambda bi: (bi, 0, 0)
    tiles = []
    for c in range(nq):
        n_keys = (c + 1) * tq
        row = lambda bi, c=c: (bi * nq + c, 0)
        tiles.append(pl.pallas_call(
            functools.partial(_dsa_prompt_kernel, n_heads, c * tq, t, topk),
            grid=(b,),
            in_specs=[pl.BlockSpec((tq, wb), row),
                      pl.BlockSpec((tq, qi.shape[1]), row),
                      pl.BlockSpec((tq, LANES), row),
                      pl.BlockSpec((None, n_keys, wb), per_b),
                      pl.BlockSpec((None, n_keys, wb), per_b),
                      pl.BlockSpec((None, n_keys, 2 * LANES), per_b),
                      _resident(bias_tab.shape)],
            out_specs=pl.BlockSpec((None, tq, wb), per_b),
            out_shape=jax.ShapeDtypeStruct((b, tq, wb), jnp.float32),
            scratch_shapes=[pltpu.VMEM((tq, n_keys), jnp.int32), pltpu.VMEM((tq, n_keys), jnp.float32)],
            compiler_params=_cparams(1),
            name="dsa_prompt_%d" % c,
        )(qb, qi, wi, k3, v3, ki3, bias_tab))
    return jnp.concatenate(tiles, axis=1).reshape(b * t, wb)


def _mla_sample_kernel(n_heads, past, ts, qcat_ref, cckv_ref, ckr_ref, nckv_ref, nkr_ref, wuk_ref, wuv_ref,
                       o_ref, kall_ref, rall_ref):
    s_pad = kall_ref.shape[0]
    n_keys = past + ts

    @pl.when(pl.program_id(0) == 0)
    def _():
        kall_ref[...] = jnp.zeros_like(kall_ref)
        rall_ref[...] = jnp.zeros_like(rall_ref)

    kall_ref[0:past, :] = _mx(cckv_ref[...])
    kall_ref[past:n_keys, :] = _mx(nckv_ref[...])
    rall_ref[0:past, 0:MLA_ROPE] = _mx(ckr_ref[...])
    rall_ref[past:n_keys, 0:MLA_ROPE] = _mx(nkr_ref[...])
    qlat, qrope = [], []
    for h in range(n_heads):
        c0 = h * HEAD_SLOT
        qlat.append(_mx(_dot(qcat_ref[:, c0:c0 + LANES], wuk_ref[h])))
        qrope.append(qcat_ref[:, c0 + LANES:c0 + HEAD_SLOT])
    qlat = jnp.concatenate(qlat, axis=0)
    qrope = jnp.concatenate(qrope, axis=0)
    s = _dot_nt(qlat, kall_ref[...]) + _dot_nt(qrope, rall_ref[...])
    col = lax.broadcasted_iota(jnp.int32, s.shape, 1)
    s = jnp.where(col < n_keys, s, NEG_BIG)
    olat = _mx(_softmax_pv(s, kall_ref[...]))
    for h in range(n_heads):
        o_ref[:, h * MLA_V:(h + 1) * MLA_V] = _dot_nt(olat[h * ts:(h + 1) * ts], wuv_ref[h])


def _mla_sample(qcat, cache_ckv, cache_kr, new_ckv, new_kr, wuk, wuv, ts):
    nb, past, c = cache_ckv.shape
    n_heads = qcat.shape[1] // HEAD_SLOT
    wa = n_heads * MLA_V
    s_pad = pl.cdiv(past + ts, LANES) * LANES
    row = lambda bi: (bi, 0)
    return pl.pallas_call(
        functools.partial(_mla_sample_kernel, n_heads, past, ts),
        grid=(nb,),
        in_specs=[pl.BlockSpec((ts, qcat.shape[1]), row),
                  pl.BlockSpec((None, past, c), lambda bi: (bi, 0, 0)),
                  pl.BlockSpec((None, past, MLA_ROPE), lambda bi: (bi, 0, 0)),
                  pl.BlockSpec((ts, c), row),
                  pl.BlockSpec((ts, MLA_ROPE), row),
                  _resident(wuk.shape), _resident(wuv.shape)],
        out_specs=pl.BlockSpec((ts, wa), row),
        out_shape=jax.ShapeDtypeStruct((nb * ts, wa), jnp.float32),
        scratch_shapes=[pltpu.VMEM((s_pad, c), MXU_DT), pltpu.VMEM((s_pad, LANES), MXU_DT)],
        compiler_params=_cparams(1),
        name="mla_sample",
    )(qcat, cache_ckv, cache_kr, new_ckv, new_kr, wuk, wuv)


def _dsa_sample_kernel(n_heads, past, ts, topk, qb_ref, qi_ref, wi_ref, ck_ref, cv_ref, cki_ref,
                       nk_ref, nv_ref, nkiab_ref, bias_ref, expand_ref, o_ref,
                       kflat_ref, vflat_ref, kia_ref, kib_ref, key_ref, madd_ref):
    n_keys = past + ts
    s_pad = kia_ref.shape[0]
    wide = n_heads * LANES

    @pl.when(pl.program_id(0) == 0)
    def _():
        kia_ref[...] = jnp.zeros_like(kia_ref)
        kib_ref[...] = jnp.zeros_like(kib_ref)

    kflat_ref[0:past * n_heads, :] = _mx(ck_ref[...])
    kflat_ref[past * n_heads:, :] = nk_ref[...]
    vflat_ref[0:past * n_heads, :] = _mx(cv_ref[...])
    vflat_ref[past * n_heads:, :] = nv_ref[...]
    cki = _mx(cki_ref[...])
    kia_ref[0:past, 0:IDX_DIM] = cki
    kib_ref[0:past, IDX_DIM:LANES] = cki
    kia_ref[past:n_keys, :] = nkiab_ref[:, :LANES]
    kib_ref[past:n_keys, :] = nkiab_ref[:, LANES:]
    score = _indexer_scores(qi_ref, wi_ref[...], kia_ref[...], kib_ref[...])
    col = lax.broadcasted_iota(jnp.int32, (ts, s_pad), 1)
    _topk_mask(score, col < n_keys, topk, key_ref, madd_ref)
    sel = _mx(jnp.where(madd_ref[...] == 0.0, 1.0, 0.0))
    pieces = []
    for j in range(pl.cdiv(n_keys, LANES)):
        width = min(wide, (n_keys - j * LANES) * n_heads)
        pieces.append(_dot(sel[:, j * LANES:(j + 1) * LANES], expand_ref[...])[:, :width])
    sel_wide = jnp.concatenate(pieces, axis=1)
    sel_wide = jnp.concatenate([sel_wide] * n_heads, axis=0)
    q_all = jnp.concatenate([qb_ref[:, h * DSA_HEAD_DIM:(h + 1) * DSA_HEAD_DIM] for h in range(n_heads)], axis=0)
    s = _dot_nt(q_all, kflat_ref[...]) + bias_ref[...] + jnp.where(sel_wide > 0.5, 0.0, NEG_BIG)
    o = _softmax_pv(s, vflat_ref[...])
    for h in range(n_heads):
        o_ref[:, h * DSA_HEAD_DIM:(h + 1) * DSA_HEAD_DIM] = o[h * ts:(h + 1) * ts]


def _dsa_sample(qb, qi, wi, cache_k, cache_v, cache_ki, new_k, new_v, new_kiab, bias_wide, ts, topk):
    nb, past, n_heads, _ = cache_k.shape
    wb = n_heads * DSA_HEAD_DIM
    n_keys = past + ts
    assert (n_keys % LANES * n_heads) % LANES == 0
    s_pad = pl.cdiv(n_keys, LANES) * LANES
    expand = _mx(jnp.repeat(jnp.eye(LANES, dtype=jnp.float32), n_heads, axis=1))
    row = lambda bi: (bi, 0)
    per_b = lambda bi: (bi, 0, 0)
    return pl.pallas_call(
        functools.partial(_dsa_sample_kernel, n_heads, past, ts, topk),
        grid=(nb,),
        in_specs=[pl.BlockSpec((ts, wb), row),
                  pl.BlockSpec((ts, qi.shape[1]), row),
                  pl.BlockSpec((ts, LANES), row),
                  pl.BlockSpec((None, past * n_heads, DSA_HEAD_DIM), per_b),
                  pl.BlockSpec((None, past * n_heads, DSA_HEAD_DIM), per_b),
                  pl.BlockSpec((None, past, IDX_DIM), per_b),
                  pl.BlockSpec((ts * n_heads, DSA_HEAD_DIM), row),
                  pl.BlockSpec((ts * n_heads, DSA_HEAD_DIM), row),
                  pl.BlockSpec((ts, 2 * LANES), row),
                  _resident(bias_wide.shape), _resident(expand.shape)],
        out_specs=pl.BlockSpec((ts, wb), row),
        out_shape=jax.ShapeDtypeStruct((nb * ts, wb), jnp.float32),
        scratch_shapes=[pltpu.VMEM((n_keys * n_heads, DSA_HEAD_DIM), MXU_DT),
                        pltpu.VMEM((n_keys * n_heads, DSA_HEAD_DIM), MXU_DT),
                        pltpu.VMEM((s_pad, LANES), MXU_DT), pltpu.VMEM((s_pad, LANES), MXU_DT),
                        pltpu.VMEM((ts, s_pad), jnp.int32), pltpu.VMEM((ts, s_pad), jnp.float32)],
        compiler_params=_cparams(1),
        name="dsa_sample",
    )(qb, qi, wi, cache_k.reshape(nb, past * n_heads, DSA_HEAD_DIM), cache_v.reshape(nb, past * n_heads, DSA_HEAD_DIM),
      cache_ki, new_k.reshape(nb * ts * n_heads, DSA_HEAD_DIM), new_v.reshape(nb * ts * n_heads, DSA_HEAD_DIM),
      new_kiab, bias_wide, expand)


def _out_kernel(oa_ref, ob_ref, ga_ref, gb_ref, x_ref, gate_ref, gna_ref, gnb_ref, w_ref, fg_ref, y_ref):
    def gated(o_ref, g_ref, gain_ref):
        o = o_ref[...]
        on = o * lax.rsqrt(jnp.mean(o * o, axis=-1, keepdims=True) + EPS) * gain_ref[...]
        return _mx(on * g_ref[...].astype(jnp.float32))

    wa = oa_ref.shape[1]
    out = _dot(gated(oa_ref, ga_ref, gna_ref), w_ref[0:wa, :]) + _dot(gated(ob_ref, gb_ref, gnb_ref), w_ref[wa:, :])
    xn = x_ref[...] + gate_ref[...] * out
    y_ref[...] = xn * lax.rsqrt(jnp.mean(xn * xn, axis=-1, keepdims=True) + EPS) * fg_ref[...]


def _out_proj(oa, ob, ga, gb, x3, gate, gain_a, gain_b, w_out, final_gain, tm):
    bv, tv, d = x3.shape
    nt = tv // tm
    wa = oa.shape[1]
    x_spec, g_spec = _row_specs(x3, gate.shape[1], tm)
    row = lambda b, i: (b * nt + i, 0)
    return pl.pallas_call(
        _out_kernel,
        grid=(bv, nt),
        in_specs=[pl.BlockSpec((tm, wa), row), pl.BlockSpec((tm, wa), row),
                  pl.BlockSpec((tm, wa), row), pl.BlockSpec((tm, wa), row),
                  x_spec, g_spec, _resident((1, wa)), _resident((1, wa)),
                  _resident(w_out.shape), _resident((1, d))],
        out_specs=pl.BlockSpec((None, tm, d), lambda b, i: (b, i, 0)),
        out_shape=jax.ShapeDtypeStruct((bv, tv, d), jnp.float32),
        compiler_params=_cparams(2),
        name="out_proj",
    )(oa, ob, ga, gb, x3, gate, gain_a.reshape(1, wa), gain_b.reshape(1, wa), w_out, final_gain.reshape(1, d))


def _pack_w_in(w_in, n_heads):
    d = w_in.shape[0]
    wa = n_heads * MLA_V
    sizes = (n_heads * (MLA_NOPE + MLA_ROPE), KV_LORA, MLA_ROPE, wa, wa, wa, wa,
             IDX_HEADS * IDX_DIM, IDX_DIM, IDX_HEADS, wa)
    offs = [0]
    for s in sizes:
        offs.append(offs[-1] + s)
    assert offs[-1] == w_in.shape[1]
    q_a, ckv, kr, g_a, q_b, k_b, v_b, q_i, k_i, w_i, g_b = (w_in[:, offs[n]:offs[n + 1]] for n in range(len(sizes)))
    half = MLA_ROPE // 2
    q_a = q_a.reshape(d, n_heads, MLA_NOPE + MLA_ROPE)
    r1, r2 = q_a[:, :, MLA_NOPE:MLA_NOPE + half], q_a[:, :, MLA_NOPE + half:]
    q_a = jnp.concatenate([q_a[:, :, :MLA_NOPE], r1, r2, r2, r1], axis=2).reshape(d, n_heads * HEAD_SLOT)
    kr = jnp.concatenate([kr[:, :half], kr[:, half:], kr[:, half:], kr[:, :half]], axis=1)
    q_i = q_i.reshape(d, 2, IDX_HEADS // 2, IDX_DIM).transpose(0, 2, 1, 3).reshape(d, IDX_HEADS * IDX_DIM)
    kiw = jnp.concatenate([k_i, w_i, jnp.zeros((d, LANES - IDX_DIM - IDX_HEADS), w_in.dtype)], axis=1)
    return [_mx(w) for w in (q_a, ckv, kr, g_a, g_b)], [_mx(w) for w in (q_b, k_b, v_b, q_i, kiw)]


def _rope_table(pos):
    half = MLA_ROPE // 2
    freqs = jnp.power(ROPE_THETA, -jnp.arange(half, dtype=jnp.float32) / half)
    ang = pos.astype(jnp.float32)[:, None] * freqs
    cos, sin = jnp.cos(ang), jnp.sin(ang)
    return jnp.concatenate([cos, cos, -sin, sin], axis=1)


def _rel_bucket(rel):
    nb = N_BUCKETS // 2
    max_exact = nb // 2
    n = jnp.abs(rel)
    nf = jnp.maximum(n, 1).astype(jnp.float32)
    large = max_exact + (jnp.log(nf / max_exact) / math.log(MAX_DISTANCE / max_exact)
                         * (nb - max_exact)).astype(jnp.int32)
    large = jnp.minimum(large, nb - 1)
    return jnp.where(rel > 0, nb, 0) + jnp.where(n < max_exact, n, large)


def _bucket_bias(rel_bias, rel):
    onehot = _rel_bucket(rel)[..., None, None] == jnp.arange(N_BUCKETS, dtype=jnp.int32)[:, None]
    return jnp.sum(jnp.where(onehot, rel_bias, 0.0), axis=-2)


def kernel(x_prompt, x_sample, cache_mla_ckv, cache_mla_krope, cache_dsa_k, cache_dsa_v, cache_idx_k,
           c_prompt, c_sample, w_ada, b_ada, ln_gain, w_in, mla_kv_gain, w_uk, w_uv, rel_bias,
           out_gain_a, out_gain_b, w_out, final_gain):
    assert w_ada.shape[0] == 1, "single-layer step"
    b, t, d = x_prompt.shape
    nb, ts, _ = x_sample.shape
    past = cache_mla_ckv.shape[2]
    n_heads = d // 256
    wb = n_heads * DSA_HEAD_DIM
    s_all = past + ts
    assert t % PROMPT_TQ == 0 and PROMPT_TQ % CHUNK == 0
    assert (s_all - 1) // CHUNK <= past // CHUNK

    mod = _ada_mod(jnp.concatenate([c_prompt, c_sample], axis=0), w_ada[0], b_ada[0])
    shift, scale, gate = mod[:, :d], mod[:, d:2 * d], mod[:, 2 * d:]
    mod_p = [v[:b].reshape(b, 1, d) for v in (scale, shift, gate)]
    mod_s = [jnp.broadcast_to(v[b:, None, :], (nb, ts, d)).reshape(1, nb * ts, d) for v in (scale, shift, gate)]

    w_a, w_b = _pack_w_in(w_in[0], n_heads)
    wkv = _mx(jnp.concatenate([w_uk[0].transpose(2, 0, 1).reshape(KV_LORA, n_heads * MLA_NOPE),
                               w_uv[0].transpose(2, 0, 1).reshape(KV_LORA, n_heads * MLA_V)], axis=1))
    w_o = _mx(w_out[0])
    pos_p = jnp.arange(t, dtype=jnp.int32)
    pos_s = past + jnp.arange(ts, dtype=jnp.int32)
    tab_p = _rope_table(pos_p)
    tab_s = jnp.tile(_rope_table(pos_s), (nb, 1))

    tm = 256
    qcat, ckv_p, krope_p, ga, gb, kcat, vup = _proj_a(
        x_prompt, mod_p[0], mod_p[1], ln_gain[0], tab_p, w_a, mla_kv_gain[0], wkv, tm)
    qb, k_p, v_p, kbb, vbb, qi, idxk_p, kiab, wi = _proj_b(x_prompt, mod_p[0], mod_p[1], ln_gain[0], w_b, tm)
    o_a = _mla_prompt(qcat, kcat, vup, b, t)
    rel = jnp.arange(-(t - 1), t + 1, dtype=jnp.int32)
    bias_tab = _bucket_bias(rel_bias, rel).T
    o_b = _dsa_prompt(qb, qi, wi, kbb, vbb, kiab, bias_tab, b, t, min(TOPK_MAX, t // 4))
    y_prompt = _out_proj(o_a, o_b, ga, gb, x_prompt, mod_p[2], out_gain_a[0], out_gain_b[0], w_o, final_gain, tm)

    xs3 = x_sample.reshape(1, nb * ts, d)
    tms = min(256, nb * ts)
    qcat_s, ckv_s, krope_s, ga_s, gb_s = _proj_a(
        xs3, mod_s[0], mod_s[1], ln_gain[0], tab_s, w_a, mla_kv_gain[0], None, tms)
    qb_s, k_s, v_s, kbb_s, vbb_s, qi_s, idxk_s, kiab_s, wi_s = _proj_b(xs3, mod_s[0], mod_s[1], ln_gain[0], w_b, tms)
    oa_s = _mla_sample(qcat_s, cache_mla_ckv[0], cache_mla_krope[0], ckv_s, krope_s,
                       _mx(w_uk[0]), _mx(w_uv[0]), ts)
    rel_s = jnp.arange(s_all, dtype=jnp.int32)[None, :] - pos_s[:, None]
    bias_s = _bucket_bias(rel_bias, rel_s)
    same_head = jnp.eye(n_heads, dtype=bool)[:, None, None, :]
    bias_wide = jnp.where(same_head, jnp.moveaxis(bias_s, -1, 0)[..., None], NEG_BIG)
    bias_wide = bias_wide.reshape(n_heads * ts, s_all * n_heads)
    ob_s = _dsa_sample(qb_s, qi_s, wi_s, cache_dsa_k[0], cache_dsa_v[0],
                       cache_idx_k[0], kbb_s, vbb_s, kiab_s, bias_wide, ts, min(TOPK_MAX, s_all // 4))
    y_sample = _out_proj(oa_s, ob_s, ga_s, gb_s, xs3, mod_s[2], out_gain_a[0], out_gain_b[0], w_o, final_gain, tms)

    hd = (n_heads, DSA_HEAD_DIM)
    return (y_prompt, y_sample.reshape(nb, ts, d),
            ckv_p.reshape(1, b, t, KV_LORA), krope_p.reshape(1, b, t, MLA_ROPE),
            k_p.reshape(1, b, t, *hd), v_p.reshape(1, b, t, *hd), idxk_p.reshape(1, b, t, IDX_DIM),
            ckv_s.reshape(1, nb, ts, KV_LORA), krope_s.reshape(1, nb, ts, MLA_ROPE),
            k_s.reshape(1, nb, ts, *hd), v_s.reshape(1, nb, ts, *hd), idxk_s.reshape(1, nb, ts, IDX_DIM))
```

```python
import functools
import math

import jax
import jax.numpy as jnp
from jax import lax
from jax.experimental import pallas as pl
from jax.experimental.pallas import tpu as pltpu

MXU_DT = jnp.bfloat16

CHUNK = 64
MLA_NOPE = 128
MLA_ROPE = 64
MLA_V = 128
KV_LORA = 512
DSA_HEAD_DIM = 128
IDX_HEADS = 16
IDX_DIM = 64
TOPK_MAX = 256
N_BUCKETS = 32
MAX_DISTANCE = 128
ROPE_THETA = 10000.0
EPS = 1e-6

LANES = 128
HEAD_SLOT = 256
NEG_BIG = -1e30
LOG2E = math.log2(math.e)
INT_MIN = -2 ** 31
VMEM_LIMIT = 56 * 1024 * 1024


def _cparams(n_grid, vmem=VMEM_LIMIT):
    return pltpu.CompilerParams(dimension_semantics=("arbitrary",) * n_grid, vmem_limit_bytes=vmem)


def _mx(v):
    return v.astype(MXU_DT)


def _dot(a, b):
    return jnp.dot(a, b, preferred_element_type=jnp.float32)


def _dot_nt(a, b):
    return lax.dot_general(a, b, (((1,), (1,)), ((), ())), preferred_element_type=jnp.float32)


def _silu(v):
    return v * (1.0 / (1.0 + jnp.exp(-v)))


def _resident(shape):
    nd = len(shape)
    return pl.BlockSpec(shape, lambda *_: (0,) * nd, pipeline_mode=pl.Buffered(1))


def _ada_kernel(c_ref, w_ref, b_ref, o_ref):
    a = _mx(_silu(c_ref[...]))
    o_ref[...] = _dot(a, _mx(w_ref[...])) + b_ref[...]


def _ada_mod(c_all, w_ada, b_ada):
    m, d = c_all.shape
    n = w_ada.shape[1]
    tn = 1024
    return pl.pallas_call(
        _ada_kernel,
        grid=(n // tn,),
        in_specs=[pl.BlockSpec((m, d), lambda j: (0, 0)),
                  pl.BlockSpec((d, tn), lambda j: (0, j)),
                  pl.BlockSpec((1, tn), lambda j: (0, j))],
        out_specs=pl.BlockSpec((m, tn), lambda j: (0, j)),
        out_shape=jax.ShapeDtypeStruct((m, n), jnp.float32),
        compiler_params=_cparams(1),
        name="ada_mod",
    )(c_all, w_ada, b_ada.reshape(1, n))


def _modulated_norm(x_ref, scale_ref, shift_ref, lng_ref):
    x = x_ref[...]
    xn = x * lax.rsqrt(jnp.mean(x * x, axis=-1, keepdims=True) + EPS) * lng_ref[...]
    return _mx(xn * (1.0 + scale_ref[...]) + shift_ref[...])


def _rope128(a, tab):
    t = a * tab
    return t + pltpu.roll(t, 64, 1)


def _proj_a_kernel(n_heads, emit_kv, x_ref, scale_ref, shift_ref, lng_ref, tab_ref,
                   wq_ref, wc_ref, wr_ref, wga_ref, wgb_ref, kvg_ref, *rest):
    if emit_kv:
        wkv_ref, qcat_ref, ckv_ref, krope_ref, ga_ref, gb_ref, kcat_ref, vup_ref = rest
    else:
        qcat_ref, ckv_ref, krope_ref, ga_ref, gb_ref = rest
    hb = _modulated_norm(x_ref, scale_ref, shift_ref, lng_ref)
    tab = tab_ref[...]
    qscale = (MLA_NOPE + MLA_ROPE) ** -0.5 * LOG2E
    for h in range(n_heads):
        c0 = h * HEAD_SLOT
        a = _dot(hb, wq_ref[:, c0:c0 + HEAD_SLOT])
        qcat_ref[:, c0:c0 + LANES] = _mx(a[:, :LANES] * qscale)
        qcat_ref[:, c0 + LANES:c0 + HEAD_SLOT] = _mx(_rope128(a[:, LANES:], tab) * qscale)
    c = _dot(hb, wc_ref[...])
    cn = c * lax.rsqrt(jnp.mean(c * c, axis=-1, keepdims=True) + EPS) * kvg_ref[...]
    ckv_ref[...] = cn
    r = _rope128(_dot(hb, wr_ref[...]), tab)
    krope_ref[...] = r[:, :MLA_ROPE]
    ga_ref[...] = _mx(_silu(_dot(hb, wga_ref[...])))
    gb_ref[...] = _mx(_silu(_dot(hb, wgb_ref[...])))
    if emit_kv:
        cb = _mx(cn)
        lane = lax.broadcasted_iota(jnp.int32, r.shape, 1)
        krz = _mx(jnp.where(lane < MLA_ROPE, r, 0.0))
        kn = _dot(cb, wkv_ref[:, :n_heads * MLA_NOPE])
        for h in range(n_heads):
            c0 = h * HEAD_SLOT
            kcat_ref[:, c0:c0 + LANES] = _mx(kn[:, h * MLA_NOPE:(h + 1) * MLA_NOPE])
            kcat_ref[:, c0 + LANES:c0 + HEAD_SLOT] = krz
        vup_ref[...] = _mx(_dot(cb, wkv_ref[:, n_heads * MLA_NOPE:]))


def _proj_b_kernel(n_heads, x_ref, scale_ref, shift_ref, lng_ref, wq_ref, wk_ref, wv_ref, wqi_ref, wkw_ref,
                   qb_ref, kb_ref, vb_ref, kbb_ref, vbb_ref, qi_ref, ki_ref, kiab_ref, wi_ref):
    hb = _modulated_norm(x_ref, scale_ref, shift_ref, lng_ref)
    dscale = DSA_HEAD_DIM ** -0.5 * LOG2E
    step = 512
    for c0 in range(0, wq_ref.shape[1], step):
        qb_ref[:, c0:c0 + step] = _mx(_dot(hb, wq_ref[:, c0:c0 + step]) * dscale)
    for w_ref, f_ref, b_ref in ((wk_ref, kb_ref, kbb_ref), (wv_ref, vb_ref, vbb_ref)):
        for c0 in range(0, w_ref.shape[1], step):
            kv = _dot(hb, w_ref[:, c0:c0 + step])
            f_ref[:, c0:c0 + step] = kv
            b_ref[:, c0:c0 + step] = _mx(kv)
    for c0 in range(0, wqi_ref.shape[1], step):
        qi_ref[:, c0:c0 + step] = _mx(_dot(hb, wqi_ref[:, c0:c0 + step]))
    a = _dot(hb, wkw_ref[...])
    ki_ref[...] = a[:, :IDX_DIM]
    lane = lax.broadcasted_iota(jnp.int32, a.shape, 1)
    kz = jnp.where(lane < IDX_DIM, a, 0.0)
    kiab_ref[:, :LANES] = _mx(kz)
    kiab_ref[:, LANES:] = _mx(pltpu.roll(kz, IDX_DIM, 1))
    wi_ref[...] = a * (IDX_HEADS ** -0.5 * IDX_DIM ** -0.5)


def _row_specs(x3, mod_rows, tm):
    bv, tv, d = x3.shape
    x_spec = pl.BlockSpec((None, tm, d), lambda b, i: (b, i, 0))
    if mod_rows == 1:
        m_spec = pl.BlockSpec((None, 1, d), lambda b, i: (b, 0, 0))
    else:
        m_spec = pl.BlockSpec((None, tm, d), lambda b, i: (b, i, 0))
    return x_spec, m_spec


def _out2d(m, width, dtype, tm, nt):
    return (jax.ShapeDtypeStruct((m, width), dtype),
            pl.BlockSpec((tm, width), lambda b, i: (b * nt + i, 0)))


def _proj_a(x3, scale, shift, ln_gain, tab, ws_a, kv_gain, wkv, tm):
    bv, tv, d = x3.shape
    nt = tv // tm
    m = bv * tv
    n_heads = d // 256
    wa = n_heads * MLA_V
    emit_kv = wkv is not None
    x_spec, m_spec = _row_specs(x3, scale.shape[1], tm)
    in_specs = ([x_spec, m_spec, m_spec, _resident((1, d)), pl.BlockSpec((tm, LANES), lambda b, i: (i, 0))]
                + [_resident(w.shape) for w in ws_a] + [_resident((1, KV_LORA))])
    args = [x3, scale, shift, ln_gain.reshape(1, d), tab, *ws_a, kv_gain.reshape(1, KV_LORA)]
    outs = [_out2d(m, n_heads * HEAD_SLOT, MXU_DT, tm, nt),
            _out2d(m, KV_LORA, jnp.float32, tm, nt),
            _out2d(m, MLA_ROPE, jnp.float32, tm, nt),
            _out2d(m, wa, MXU_DT, tm, nt),
            _out2d(m, wa, MXU_DT, tm, nt)]
    if emit_kv:
        in_specs.append(_resident(wkv.shape))
        args.append(wkv)
        outs += [_out2d(m, n_heads * HEAD_SLOT, MXU_DT, tm, nt),
                 _out2d(m, wa, MXU_DT, tm, nt)]
    return pl.pallas_call(
        functools.partial(_proj_a_kernel, n_heads, emit_kv),
        grid=(bv, nt),
        in_specs=in_specs,
        out_specs=[o[1] for o in outs],
        out_shape=[o[0] for o in outs],
        compiler_params=_cparams(2),
        name="proj_a",
    )(*args)


def _proj_b(x3, scale, shift, ln_gain, ws_b, tm):
    bv, tv, d = x3.shape
    nt = tv // tm
    m = bv * tv
    n_heads = d // 256
    width_b = n_heads * DSA_HEAD_DIM
    x_spec, m_spec = _row_specs(x3, scale.shape[1], tm)
    outs = [_out2d(m, width_b, MXU_DT, tm, nt),
            _out2d(m, width_b, jnp.float32, tm, nt),
            _out2d(m, width_b, jnp.float32, tm, nt),
            _out2d(m, width_b, MXU_DT, tm, nt),
            _out2d(m, width_b, MXU_DT, tm, nt),
            _out2d(m, IDX_HEADS * IDX_DIM, MXU_DT, tm, nt),
            _out2d(m, IDX_DIM, jnp.float32, tm, nt),
            _out2d(m, 2 * LANES, MXU_DT, tm, nt),
            _out2d(m, LANES, jnp.float32, tm, nt)]
    return pl.pallas_call(
        functools.partial(_proj_b_kernel, n_heads),
        grid=(bv, nt),
        in_specs=[x_spec, m_spec, m_spec, _resident((1, d))] + [_resident(w.shape) for w in ws_b],
        out_specs=[o[1] for o in outs],
        out_shape=[o[0] for o in outs],
        compiler_params=_cparams(2),
        name="proj_b",
    )(x3, scale, shift, ln_gain.reshape(1, d), *ws_b)


def _softmax_pv(s, v):
    m = jnp.max(s, axis=-1, keepdims=True)
    p = jnp.exp2(s - m)
    l = jnp.sum(p, axis=-1, keepdims=True)
    return _dot(_mx(p), v) * (1.0 / l)


def _topk_mask(score, adm, topk, key_ref, madd_ref):
    rows, n = score.shape
    if n <= topk:
        madd_ref[...] = jnp.where(adm, 0.0, NEG_BIG)
        return
    score = jnp.where(score == 0.0, 0.0, score)
    bits = pltpu.bitcast(score, jnp.int32)
    key = bits ^ ((bits >> 31) & 0x7FFFFFFF)
    key_ref[...] = jnp.where(adm, key, INT_MIN)
    kf = float(topk)

    def count(pred):
        return jnp.sum(jnp.where(pred, 1.0, 0.0), axis=-1, keepdims=True)

    def bit_step(i, thr):
        inc = lax.shift_left(jnp.int32(1), 31 - i)
        cand = thr + inc
        ok = count(key_ref[...] >= cand) >= kf
        return jnp.where(ok, cand, thr)

    thr = lax.fori_loop(0, 32, bit_step, jnp.full((rows, 1), INT_MIN, jnp.int32))
    ge = key_ref[...] >= thr
    madd_ref[...] = jnp.where(ge, 0.0, NEG_BIG)

    @pl.when(jnp.max(count(ge)) > kf)
    def _():
        key = key_ref[...]
        gt = key > thr
        need = kf - count(gt)
        eqf = jnp.where(key == thr, 1.0, 0.0)
        col = lax.broadcasted_iota(jnp.int32, (rows, n), 1)
        nbits = max(1, int(n).bit_length())

        def col_step(i, bound):
            cand = bound + lax.shift_left(jnp.int32(1), nbits - 1 - i)
            taken = jnp.sum(jnp.where(col < cand, eqf, 0.0), axis=-1, keepdims=True)
            return jnp.where(taken <= need, cand, bound)

        bound = lax.fori_loop(0, nbits, col_step, jnp.zeros((rows, 1), jnp.int32))
        tie_madd = jnp.where(jnp.where(col < bound, eqf, 0.0) > 0.5, 0.0, NEG_BIG)
        madd_ref[...] = jnp.where(adm, jnp.where(gt, 0.0, tie_madd), NEG_BIG)


def _indexer_scores(qi_ref, wi, kia, kib):
    half = IDX_HEADS // 2
    score = None
    for j in range(half):
        qp = qi_ref[:, j * LANES:(j + 1) * LANES]
        da = jnp.maximum(_dot_nt(qp, kia), 0.0) * wi[:, IDX_DIM + j:IDX_DIM + j + 1]
        db = jnp.maximum(_dot_nt(qp, kib), 0.0) * wi[:, IDX_DIM + half + j:IDX_DIM + half + j + 1]
        score = da + db if score is None else score + da + db
    return score


PROMPT_TQ = 256


def _chunk_madd(row0, tq, n_keys):
    qc = (row0 + lax.broadcasted_iota(jnp.int32, (tq, n_keys), 0)) // CHUNK
    kc = lax.broadcasted_iota(jnp.int32, (tq, n_keys), 1) // CHUNK
    return kc <= qc


def _chained_tile_calls(n_tiles, make_call):
    out = None
    for c in range(n_tiles):
        if out is None:
            out = make_call(c, [], [], {})
        else:
            out = make_call(c, [pl.BlockSpec(memory_space=pl.ANY)], [out], None)
    return out


def _mla_prompt_kernel(n_heads, row0, q_ref, k_ref, v_ref, *rest):
    o_ref = rest[-1]
    tq, n_keys = q_ref.shape[0], k_ref.shape[0]
    madd = jnp.where(_chunk_madd(row0, tq, n_keys), 0.0, NEG_BIG)
    for h in range(n_heads):
        c0 = h * HEAD_SLOT
        s = _dot_nt(q_ref[:, c0:c0 + HEAD_SLOT], k_ref[:, c0:c0 + HEAD_SLOT]) + madd
        o_ref[:, h * MLA_V:(h + 1) * MLA_V] = _softmax_pv(s, v_ref[:, h * MLA_V:(h + 1) * MLA_V])


def _mla_prompt(qcat, kcat, vup, b, t):
    tq = PROMPT_TQ
    n_heads = qcat.shape[1] // HEAD_SLOT
    nq = t // tq
    wa = n_heads * MLA_V
    k3 = kcat.reshape(b, t, kcat.shape[1])
    v3 = vup.reshape(b, t, wa)

    def make_call(c, extra_specs, extra_args, aliases):
        n_keys = (c + 1) * tq
        row = lambda bi: (bi * nq + c, 0)
        return pl.pallas_call(
            functools.partial(_mla_prompt_kernel, n_heads, c * tq),
            grid=(b,),
            in_specs=[pl.BlockSpec((tq, qcat.shape[1]), row),
                      pl.BlockSpec((None, n_keys, kcat.shape[1]), lambda bi: (bi, 0, 0)),
                      pl.BlockSpec((None, n_keys, wa), lambda bi: (bi, 0, 0))] + extra_specs,
            out_specs=pl.BlockSpec((tq, wa), row),
            out_shape=jax.ShapeDtypeStruct((b * t, wa), jnp.float32),
            input_output_aliases={3: 0} if aliases is None else aliases,
            compiler_params=_cparams(1),
            name="mla_prompt_%d" % c,
        )(qcat, k3, v3, *extra_args)

    return _chained_tile_calls(nq, make_call)


def _toeplitz_bias(tab_ref, h, start, n_cols, le_mask):
    pieces = []
    prev = None
    for n in range(n_cols // LANES + 1):
        seg = tab_ref[h:h + 1, start + n * LANES:start + (n + 1) * LANES]
        rot = pltpu.roll(jnp.broadcast_to(seg, (LANES, LANES)), 1, 1, stride=1, stride_axis=0)
        if prev is not None:
            pieces.append(jnp.where(le_mask, prev, rot))
        prev = rot
    return jnp.concatenate(pieces, axis=1)


def _dsa_prompt_kernel(n_heads, row0, t, topk, qb_ref, qi_ref, wi_ref, k_ref, v_ref, kiab_ref, tab_ref, *rest):
    o_ref, key_ref, madd_ref = rest[-3:]
    tq, n_keys = qb_ref.shape[0], k_ref.shape[0]
    score = _indexer_scores(qi_ref, wi_ref[...], kiab_ref[:, :LANES], kiab_ref[:, LANES:])
    _topk_mask(score, _chunk_madd(row0, tq, n_keys), topk, key_ref, madd_ref)
    near0 = max(0, row0 - LANES)
    n_near = n_keys - near0
    le_mask = (lax.broadcasted_iota(jnp.int32, (LANES, LANES), 1)
               <= lax.broadcasted_iota(jnp.int32, (LANES, LANES), 0))
    for h in range(n_heads):
        c0 = h * DSA_HEAD_DIM
        near = jnp.concatenate(
            [_toeplitz_bias(tab_ref, h, near0 - (row0 + u * LANES) - LANES + t, n_near, le_mask)
             for u in range(tq // LANES)], axis=0)
        if near0 > 0:
            far = jnp.broadcast_to(tab_ref[h:h + 1, 0:1], (tq, near0))
            bias = jnp.concatenate([far, near], axis=1)
        else:
            bias = near
        s = _dot_nt(qb_ref[:, c0:c0 + DSA_HEAD_DIM], k_ref[:, c0:c0 + DSA_HEAD_DIM]) + bias + madd_ref[...]
        o_ref[:, c0:c0 + DSA_HEAD_DIM] = _softmax_pv(s, v_ref[:, c0:c0 + DSA_HEAD_DIM])


def _dsa_prompt(qb, qi, wi, kbb, vbb, kiab, bias_tab, b, t, topk):
    tq = PROMPT_TQ
    assert MAX_DISTANCE <= LANES and tq % LANES == 0
    n_heads = qb.shape[1] // DSA_HEAD_DIM
    nq = t // tq
    wb = qb.shape[1]
    k3, v3, ki3 = kbb.reshape(b, t, wb), vbb.reshape(b, t, wb), kiab.reshape(b, t, 2 * LANES)
    per_b = lambda bi: (bi, 0, 0)

    def make_call(c, extra_specs, extra_args, aliases):
        n_keys = (c + 1) * tq
        row = lambda bi: (bi * nq + c, 0)
        return pl.pallas_call(
            functools.partial(_dsa_prompt_kernel, n_heads, c * tq, t, topk),
            grid=(b,),
            in_specs=[pl.BlockSpec((tq, wb), row),
                      pl.BlockSpec((tq, qi.shape[1]), row),
                      pl.BlockSpec((tq, LANES), row),
                      pl.BlockSpec((None, n_keys, wb), per_b),
                      pl.BlockSpec((None, n_keys, wb), per_b),
                      pl.BlockSpec((None, n_keys, 2 * LANES), per_b),
                      _resident(bias_tab.shape)] + extra_specs,
            out_specs=pl.BlockSpec((tq, wb), row),
            out_shape=jax.ShapeDtypeStruct((b * t, wb), jnp.float32),
            input_output_aliases={7: 0} if aliases is None else aliases,
            scratch_shapes=[pltpu.VMEM((tq, n_keys), jnp.int32), pltpu.VMEM((tq, n_keys), jnp.float32)],
            compiler_params=_cparams(1),
            name="dsa_prompt_%d" % c,
        )(qb, qi, wi, k3, v3, ki3, bias_tab, *extra_args)

    return _chained_tile_calls(nq, make_call)


def _mla_sample_kernel(n_heads, past, ts, qcat_ref, cckv_ref, ckr_ref, nckv_ref, nkr_ref, wuk_ref, wuv_ref,
                       o_ref, kall_ref, rall_ref):
    s_pad = kall_ref.shape[0]
    n_keys = past + ts

    @pl.when(pl.program_id(0) == 0)
    def _():
        kall_ref[...] = jnp.zeros_like(kall_ref)
        rall_ref[...] = jnp.zeros_like(rall_ref)

    kall_ref[0:past, :] = _mx(cckv_ref[...])
    kall_ref[past:n_keys, :] = _mx(nckv_ref[...])
    rall_ref[0:past, 0:MLA_ROPE] = _mx(ckr_ref[...])
    rall_ref[past:n_keys, 0:MLA_ROPE] = _mx(nkr_ref[...])
    qlat, qrope = [], []
    for h in range(n_heads):
        c0 = h * HEAD_SLOT
        qlat.append(_mx(_dot(qcat_ref[:, c0:c0 + LANES], wuk_ref[h])))
        qrope.append(qcat_ref[:, c0 + LANES:c0 + HEAD_SLOT])
    qlat = jnp.concatenate(qlat, axis=0)
    qrope = jnp.concatenate(qrope, axis=0)
    s = _dot_nt(qlat, kall_ref[...]) + _dot_nt(qrope, rall_ref[...])
    col = lax.broadcasted_iota(jnp.int32, s.shape, 1)
    s = jnp.where(col < n_keys, s, NEG_BIG)
    olat = _mx(_softmax_pv(s, kall_ref[...]))
    for h in range(n_heads):
        o_ref[:, h * MLA_V:(h + 1) * MLA_V] = _dot_nt(olat[h * ts:(h + 1) * ts], wuv_ref[h])


def _mla_sample(qcat, cache_ckv, cache_kr, new_ckv, new_kr, wuk, wuv, ts):
    nb, past, c = cache_ckv.shape
    n_heads = qcat.shape[1] // HEAD_SLOT
    wa = n_heads * MLA_V
    s_pad = pl.cdiv(past + ts, LANES) * LANES
    row = lambda bi: (bi, 0)
    return pl.pallas_call(
        functools.partial(_mla_sample_kernel, n_heads, past, ts),
        grid=(nb,),
        in_specs=[pl.BlockSpec((ts, qcat.shape[1]), row),
                  pl.BlockSpec((None, past, c), lambda bi: (bi, 0, 0)),
                  pl.BlockSpec((None, past, MLA_ROPE), lambda bi: (bi, 0, 0)),
                  pl.BlockSpec((ts, c), row),
                  pl.BlockSpec((ts, MLA_ROPE), row),
                  _resident(wuk.shape), _resident(wuv.shape)],
        out_specs=pl.BlockSpec((ts, wa), row),
        out_shape=jax.ShapeDtypeStruct((nb * ts, wa), jnp.float32),
        scratch_shapes=[pltpu.VMEM((s_pad, c), MXU_DT), pltpu.VMEM((s_pad, LANES), MXU_DT)],
        compiler_params=_cparams(1),
        name="mla_sample",
    )(qcat, cache_ckv, cache_kr, new_ckv, new_kr, wuk, wuv)


def _dsa_select_kernel(past, ts, topk, qi_ref, wi_ref, cki_ref, nkiab_ref, madd_ref, kia_ref, kib_ref, key_ref):
    bi = pl.program_id(0)
    n_keys = past + ts
    rows, s_pad = madd_ref.shape

    @pl.when(bi == 0)
    def _():
        kia_ref[...] = jnp.zeros_like(kia_ref)
        kib_ref[...] = jnp.zeros_like(kib_ref)

    cki = _mx(cki_ref[...])
    kia_ref[0:past, 0:IDX_DIM] = cki
    kib_ref[0:past, IDX_DIM:LANES] = cki
    kia_ref[past:n_keys, :] = nkiab_ref[:, :LANES]
    kib_ref[past:n_keys, :] = nkiab_ref[:, LANES:]
    madd_ref[pl.ds(pl.multiple_of(bi * ts, ts), ts), :] = _indexer_scores(
        qi_ref, wi_ref[...], kia_ref[...], kib_ref[...])

    @pl.when(bi == pl.num_programs(0) - 1)
    def _():
        col = lax.broadcasted_iota(jnp.int32, (rows, s_pad), 1)
        _topk_mask(madd_ref[...], col < n_keys, topk, key_ref, madd_ref)


def _dsa_select(qi, wi, cache_ki, new_kiab, ts, topk):
    nb, past, _ = cache_ki.shape
    s_pad = pl.cdiv(past + ts, LANES) * LANES
    row = lambda bi: (bi, 0)
    return pl.pallas_call(
        functools.partial(_dsa_select_kernel, past, ts, topk),
        grid=(nb,),
        in_specs=[pl.BlockSpec((ts, qi.shape[1]), row),
                  pl.BlockSpec((ts, LANES), row),
                  pl.BlockSpec((None, past, IDX_DIM), lambda bi: (bi, 0, 0)),
                  pl.BlockSpec((ts, 2 * LANES), row)],
        out_specs=pl.BlockSpec((nb * ts, s_pad), lambda bi: (0, 0)),
        out_shape=jax.ShapeDtypeStruct((nb * ts, s_pad), jnp.float32),
        scratch_shapes=[pltpu.VMEM((s_pad, LANES), MXU_DT), pltpu.VMEM((s_pad, LANES), MXU_DT),
                        pltpu.VMEM((nb * ts, s_pad), jnp.int32)],
        compiler_params=_cparams(1),
        name="dsa_select",
    )(qi, wi, cache_ki, new_kiab)


def _dsa_sample_kernel(n_heads, past, ts, qb_ref, madd_ref, ck_ref, cv_ref, nk_ref, nv_ref, bias_ref, expand_ref,
                       o_ref, kflat_ref, vflat_ref):
    n_keys = past + ts
    wide = n_heads * LANES
    kflat_ref[0:past * n_heads, :] = _mx(ck_ref[...])
    kflat_ref[past * n_heads:, :] = nk_ref[...]
    vflat_ref[0:past * n_heads, :] = _mx(cv_ref[...])
    vflat_ref[past * n_heads:, :] = nv_ref[...]
    sel = _mx(jnp.where(madd_ref[...] == 0.0, 1.0, 0.0))
    pieces = []
    for j in range(pl.cdiv(n_keys, LANES)):
        width = min(wide, (n_keys - j * LANES) * n_heads)
        pieces.append(_dot(sel[:, j * LANES:(j + 1) * LANES], expand_ref[...])[:, :width])
    sel_wide = jnp.concatenate(pieces, axis=1)
    sel_wide = jnp.concatenate([sel_wide] * n_heads, axis=0)
    q_all = jnp.concatenate([qb_ref[:, h * DSA_HEAD_DIM:(h + 1) * DSA_HEAD_DIM] for h in range(n_heads)], axis=0)
    s = _dot_nt(q_all, kflat_ref[...]) + bias_ref[...] + jnp.where(sel_wide > 0.5, 0.0, NEG_BIG)
    o = _softmax_pv(s, vflat_ref[...])
    for h in range(n_heads):
        o_ref[:, h * DSA_HEAD_DIM:(h + 1) * DSA_HEAD_DIM] = o[h * ts:(h + 1) * ts]


def _dsa_sample(qb, madd, cache_k, cache_v, new_k, new_v, bias_wide, ts):
    nb, past, n_heads, _ = cache_k.shape
    wb = n_heads * DSA_HEAD_DIM
    n_keys = past + ts
    assert (n_keys % LANES * n_heads) % LANES == 0
    expand = _mx(jnp.repeat(jnp.eye(LANES, dtype=jnp.float32), n_heads, axis=1))
    row = lambda bi: (bi, 0)
    per_b = lambda bi: (bi, 0, 0)
    return pl.pallas_call(
        functools.partial(_dsa_sample_kernel, n_heads, past, ts),
        grid=(nb,),
        in_specs=[pl.BlockSpec((ts, wb), row),
                  pl.BlockSpec((ts, madd.shape[1]), row),
                  pl.BlockSpec((None, past * n_heads, DSA_HEAD_DIM), per_b),
                  pl.BlockSpec((None, past * n_heads, DSA_HEAD_DIM), per_b),
                  pl.BlockSpec((ts * n_heads, DSA_HEAD_DIM), row),
                  pl.BlockSpec((ts * n_heads, DSA_HEAD_DIM), row),
                  _resident(bias_wide.shape), _resident(expand.shape)],
        out_specs=pl.BlockSpec((ts, wb), row),
        out_shape=jax.ShapeDtypeStruct((nb * ts, wb), jnp.float32),
        scratch_shapes=[pltpu.VMEM((n_keys * n_heads, DSA_HEAD_DIM), MXU_DT),
                        pltpu.VMEM((n_keys * n_heads, DSA_HEAD_DIM), MXU_DT)],
        compiler_params=_cparams(1),
        name="dsa_sample",
    )(qb, madd, cache_k.reshape(nb, past * n_heads, DSA_HEAD_DIM), cache_v.reshape(nb, past * n_heads, DSA_HEAD_DIM),
      new_k.reshape(nb * ts * n_heads, DSA_HEAD_DIM), new_v.reshape(nb * ts * n_heads, DSA_HEAD_DIM),
      bias_wide, expand)


def _out_kernel(oa_ref, ob_ref, ga_ref, gb_ref, x_ref, gate_ref, gna_ref, gnb_ref, w_ref, fg_ref, y_ref):
    def gated(o_ref, g_ref, gain_ref):
        o = o_ref[...]
        on = o * lax.rsqrt(jnp.mean(o * o, axis=-1, keepdims=True) + EPS) * gain_ref[...]
        return _mx(on * g_ref[...].astype(jnp.float32))

    wa = oa_ref.shape[1]
    out = _dot(gated(oa_ref, ga_ref, gna_ref), w_ref[0:wa, :]) + _dot(gated(ob_ref, gb_ref, gnb_ref), w_ref[wa:, :])
    xn = x_ref[...] + gate_ref[...] * out
    y_ref[...] = xn * lax.rsqrt(jnp.mean(xn * xn, axis=-1, keepdims=True) + EPS) * fg_ref[...]


def _out_proj(oa, ob, ga, gb, x3, gate, gain_a, gain_b, w_out, final_gain, tm):
    bv, tv, d = x3.shape
    nt = tv // tm
    wa = oa.shape[1]
    x_spec, g_spec = _row_specs(x3, gate.shape[1], tm)
    row = lambda b, i: (b * nt + i, 0)
    return pl.pallas_call(
        _out_kernel,
        grid=(bv, nt),
        in_specs=[pl.BlockSpec((tm, wa), row), pl.BlockSpec((tm, wa), row),
                  pl.BlockSpec((tm, wa), row), pl.BlockSpec((tm, wa), row),
                  x_spec, g_spec, _resident((1, wa)), _resident((1, wa)),
                  _resident(w_out.shape), _resident((1, d))],
        out_specs=pl.BlockSpec((None, tm, d), lambda b, i: (b, i, 0)),
        out_shape=jax.ShapeDtypeStruct((bv, tv, d), jnp.float32),
        compiler_params=_cparams(2),
        name="out_proj",
    )(oa, ob, ga, gb, x3, gate, gain_a.reshape(1, wa), gain_b.reshape(1, wa), w_out, final_gain.reshape(1, d))


def _pack_kernel(n_heads, w_ref, qa_ref, ckv_ref, kr_ref, ga_ref, gb_ref, qb_ref, kb_ref, vb_ref, qi_ref, kiw_ref):
    wa = n_heads * MLA_V
    half = MLA_ROPE // 2
    q_head = MLA_NOPE + MLA_ROPE
    lane = lax.broadcasted_iota(jnp.int32, (w_ref.shape[0], LANES), 1)

    def dup_rope(x):
        return jnp.where(lane < MLA_ROPE, x,
                         jnp.where(lane < MLA_ROPE + half, pltpu.roll(x, half, 1), pltpu.roll(x, LANES - half, 1)))

    for h in range(n_heads):
        src, dst = h * q_head, h * HEAD_SLOT
        qa_ref[:, dst:dst + LANES] = _mx(w_ref[:, src:src + MLA_NOPE])
        qa_ref[:, dst + LANES:dst + HEAD_SLOT] = _mx(dup_rope(w_ref[:, src + MLA_NOPE:src + MLA_NOPE + LANES]))
    off = n_heads * q_head
    ckv_ref[...] = _mx(w_ref[:, off:off + KV_LORA])
    off += KV_LORA
    kr_ref[...] = _mx(dup_rope(w_ref[:, off:off + LANES]))
    off += MLA_ROPE
    for ref in (ga_ref, qb_ref, kb_ref, vb_ref):
        ref[...] = _mx(w_ref[:, off:off + wa])
        off += wa
    pairs = IDX_HEADS // 2
    for j in range(pairs):
        lo = w_ref[:, off + IDX_DIM * j:off + IDX_DIM * j + LANES]
        hi = w_ref[:, off + IDX_DIM * (j + pairs - 1):off + IDX_DIM * (j + pairs - 1) + LANES]
        qi_ref[:, LANES * j:LANES * (j + 1)] = _mx(jnp.where(lane < IDX_DIM, lo, hi))
    off += IDX_HEADS * IDX_DIM
    kiw_ref[...] = _mx(jnp.where(lane < IDX_DIM + IDX_HEADS, w_ref[:, off:off + LANES], 0.0))
    off += IDX_DIM + IDX_HEADS
    gb_ref[...] = _mx(w_ref[:, off:off + wa])


def _pack_w_in(w_in, n_heads):
    d, n = w_in.shape
    wa = n_heads * MLA_V
    assert n == n_heads * (MLA_NOPE + MLA_ROPE) + KV_LORA + MLA_ROPE + 5 * wa + IDX_HEADS * IDX_DIM + IDX_DIM + IDX_HEADS
    assert 2 * IDX_DIM == LANES and 2 * MLA_ROPE == LANES and MLA_NOPE == LANES
    tr = 256
    widths = [n_heads * HEAD_SLOT, KV_LORA, LANES, wa, wa, wa, wa, wa, IDX_HEADS * IDX_DIM, LANES]
    q_a, ckv, kr, g_a, g_b, q_b, k_b, v_b, q_i, kiw = pl.pallas_call(
        functools.partial(_pack_kernel, n_heads),
        grid=(d // tr,),
        in_specs=[pl.BlockSpec((tr, n), lambda i: (i, 0))],
        out_specs=[pl.BlockSpec((tr, w), lambda i: (i, 0)) for w in widths],
        out_shape=[jax.ShapeDtypeStruct((d, w), MXU_DT) for w in widths],
        compiler_params=_cparams(1),
        name="pack_w_in",
    )(w_in)
    return [q_a, ckv, kr, g_a, g_b], [q_b, k_b, v_b, q_i, kiw]


def _rope_table(pos):
    half = MLA_ROPE // 2
    freqs = jnp.power(ROPE_THETA, -jnp.arange(half, dtype=jnp.float32) / half)
    ang = pos.astype(jnp.float32)[:, None] * freqs
    cos, sin = jnp.cos(ang), jnp.sin(ang)
    return jnp.concatenate([cos, cos, -sin, sin], axis=1)


def _rel_bucket(rel):
    nb = N_BUCKETS // 2
    max_exact = nb // 2
    n = jnp.abs(rel)
    nf = jnp.maximum(n, 1).astype(jnp.float32)
    large = max_exact + (jnp.log(nf / max_exact) / math.log(MAX_DISTANCE / max_exact)
                         * (nb - max_exact)).astype(jnp.int32)
    large = jnp.minimum(large, nb - 1)
    return jnp.where(rel > 0, nb, 0) + jnp.where(n < max_exact, n, large)


def _bucket_bias(rel_bias, rel):
    onehot = _rel_bucket(rel)[..., None, None] == jnp.arange(N_BUCKETS, dtype=jnp.int32)[:, None]
    return jnp.sum(jnp.where(onehot, rel_bias * LOG2E, 0.0), axis=-2)


def kernel(x_prompt, x_sample, cache_mla_ckv, cache_mla_krope, cache_dsa_k, cache_dsa_v, cache_idx_k,
           c_prompt, c_sample, w_ada, b_ada, ln_gain, w_in, mla_kv_gain, w_uk, w_uv, rel_bias,
           out_gain_a, out_gain_b, w_out, final_gain):
    assert w_ada.shape[0] == 1, "single-layer step"
    b, t, d = x_prompt.shape
    nb, ts, _ = x_sample.shape
    past = cache_mla_ckv.shape[2]
    n_heads = d // 256
    wb = n_heads * DSA_HEAD_DIM
    s_all = past + ts
    assert t % PROMPT_TQ == 0 and PROMPT_TQ % CHUNK == 0
    assert (s_all - 1) // CHUNK <= past // CHUNK

    mod = _ada_mod(jnp.concatenate([c_prompt, c_sample], axis=0), w_ada[0], b_ada[0])
    shift, scale, gate = mod[:, :d], mod[:, d:2 * d], mod[:, 2 * d:]
    mod_p = [v[:b].reshape(b, 1, d) for v in (scale, shift, gate)]
    mod_s = [jnp.broadcast_to(v[b:, None, :], (nb, ts, d)).reshape(1, nb * ts, d) for v in (scale, shift, gate)]

    w_a, w_b = _pack_w_in(w_in[0], n_heads)
    wkv = _mx(jnp.concatenate([w_uk[0].transpose(2, 0, 1).reshape(KV_LORA, n_heads * MLA_NOPE),
                               w_uv[0].transpose(2, 0, 1).reshape(KV_LORA, n_heads * MLA_V)], axis=1))
    w_o = _mx(w_out[0])
    pos_p = jnp.arange(t, dtype=jnp.int32)
    pos_s = past + jnp.arange(ts, dtype=jnp.int32)
    tab_p = _rope_table(pos_p)
    tab_s = jnp.tile(_rope_table(pos_s), (nb, 1))

    tm = 256
    qcat, ckv_p, krope_p, ga, gb, kcat, vup = _proj_a(
        x_prompt, mod_p[0], mod_p[1], ln_gain[0], tab_p, w_a, mla_kv_gain[0], wkv, tm)
    qb, k_p, v_p, kbb, vbb, qi, idxk_p, kiab, wi = _proj_b(x_prompt, mod_p[0], mod_p[1], ln_gain[0], w_b, tm)
    o_a = _mla_prompt(qcat, kcat, vup, b, t)
    rel = jnp.arange(-(t - 1), t + 1, dtype=jnp.int32)
    bias_tab = _bucket_bias(rel_bias, rel).T
    o_b = _dsa_prompt(qb, qi, wi, kbb, vbb, kiab, bias_tab, b, t, min(TOPK_MAX, t // 4))
    y_prompt = _out_proj(o_a, o_b, ga, gb, x_prompt, mod_p[2], out_gain_a[0], out_gain_b[0], w_o, final_gain, tm)

    xs3 = x_sample.reshape(1, nb * ts, d)
    tms = min(256, nb * ts)
    qcat_s, ckv_s, krope_s, ga_s, gb_s = _proj_a(
        xs3, mod_s[0], mod_s[1], ln_gain[0], tab_s, w_a, mla_kv_gain[0], None, tms)
    qb_s, k_s, v_s, kbb_s, vbb_s, qi_s, idxk_s, kiab_s, wi_s = _proj_b(xs3, mod_s[0], mod_s[1], ln_gain[0], w_b, tms)
    oa_s = _mla_sample(qcat_s, cache_mla_ckv[0], cache_mla_krope[0], ckv_s, krope_s,
                       _mx(w_uk[0]), _mx(w_uv[0]), ts)
    rel_s = jnp.arange(s_all, dtype=jnp.int32)[None, :] - pos_s[:, None]
    bias_s = _bucket_bias(rel_bias, rel_s)
    same_head = jnp.eye(n_heads, dtype=bool)[:, None, None, :]
    bias_wide = jnp.where(same_head, jnp.moveaxis(bias_s, -1, 0)[..., None], NEG_BIG)
    bias_wide = bias_wide.reshape(n_heads * ts, s_all * n_heads)
    madd_s = _dsa_select(qi_s, wi_s, cache_idx_k[0], kiab_s, ts, min(TOPK_MAX, s_all // 4))
    ob_s = _dsa_sample(qb_s, madd_s, cache_dsa_k[0], cache_dsa_v[0], kbb_s, vbb_s, bias_wide, ts)
    y_sample = _out_proj(oa_s, ob_s, ga_s, gb_s, xs3, mod_s[2], out_gain_a[0], out_gain_b[0], w_o, final_gain, tms)

    hd = (n_heads, DSA_HEAD_DIM)
    return (y_prompt, y_sample.reshape(nb, ts, d),
            ckv_p.reshape(1, b, t, KV_LORA), krope_p.reshape(1, b, t, MLA_ROPE),
            k_p.reshape(1, b, t, *hd), v_p.reshape(1, b, t, *hd), idxk_p.reshape(1, b, t, IDX_DIM),
            ckv_s.reshape(1, nb, ts, KV_LORA), krope_s.reshape(1, nb, ts, MLA_ROPE),
            k_s.reshape(1, nb, ts, *hd), v_s.reshape(1, nb, ts, *hd), idxk_s.reshape(1, nb, ts, IDX_DIM))
```

```python
import functools
import math

import jax
import jax.numpy as jnp
from jax import lax
from jax.experimental import pallas as pl
from jax.experimental.pallas import tpu as pltpu

MXU_DT = jnp.bfloat16

CHUNK = 64
MLA_NOPE = 128
MLA_ROPE = 64
MLA_V = 128
KV_LORA = 512
DSA_HEAD_DIM = 128
IDX_HEADS = 16
IDX_DIM = 64
TOPK_MAX = 256
N_BUCKETS = 32
MAX_DISTANCE = 128
ROPE_THETA = 10000.0
EPS = 1e-6

LANES = 128
HEAD_SLOT = 256
NEG_BIG = -1e30
LOG2E = math.log2(math.e)
INT_MIN = -2 ** 31
KEY_NEG_INF = INT_MIN + 0x7FFFFF
VMEM_LIMIT = 56 * 1024 * 1024


def _cparams(n_grid, vmem=VMEM_LIMIT):
    return pltpu.CompilerParams(dimension_semantics=("arbitrary",) * n_grid, vmem_limit_bytes=vmem)


def _mx(v):
    return v.astype(MXU_DT)


def _dot(a, b):
    return jnp.dot(a, b, preferred_element_type=jnp.float32)


def _dot_nt(a, b):
    return lax.dot_general(a, b, (((1,), (1,)), ((), ())), preferred_element_type=jnp.float32)


def _silu(v):
    return v * (1.0 / (1.0 + jnp.exp(-v)))


def _resident(shape):
    nd = len(shape)
    return pl.BlockSpec(shape, lambda *_: (0,) * nd, pipeline_mode=pl.Buffered(1))


def _ada_kernel(c_ref, w_ref, b_ref, o_ref):
    a = _mx(_silu(c_ref[...]))
    o_ref[...] = _dot(a, _mx(w_ref[...])) + b_ref[...]


def _ada_mod(c_all, w_ada, b_ada):
    m, d = c_all.shape
    n = w_ada.shape[1]
    tn = 1024
    return pl.pallas_call(
        _ada_kernel,
        grid=(n // tn,),
        in_specs=[pl.BlockSpec((m, d), lambda j: (0, 0)),
                  pl.BlockSpec((d, tn), lambda j: (0, j)),
                  pl.BlockSpec((1, tn), lambda j: (0, j))],
        out_specs=pl.BlockSpec((m, tn), lambda j: (0, j)),
        out_shape=jax.ShapeDtypeStruct((m, n), jnp.float32),
        compiler_params=_cparams(1),
        name="ada_mod",
    )(c_all, w_ada, b_ada.reshape(1, n))


def _modulated_norm(x_ref, scale_ref, shift_ref, lng_ref):
    x = x_ref[...]
    xn = x * lax.rsqrt(jnp.mean(x * x, axis=-1, keepdims=True) + EPS) * lng_ref[...]
    return _mx(xn * (1.0 + scale_ref[...]) + shift_ref[...])


def _rope128(a, tab):
    t = a * tab
    return t + pltpu.roll(t, 64, 1)


def _proj_a_kernel(n_heads, emit_kv, x_ref, scale_ref, shift_ref, lng_ref, tab_ref,
                   wq_ref, wc_ref, wr_ref, wga_ref, wgb_ref, kvg_ref, *rest):
    if emit_kv:
        wkv_ref, qcat_ref, ckv_ref, krope_ref, ga_ref, gb_ref, kcat_ref, vup_ref = rest
    else:
        qcat_ref, ckv_ref, krope_ref, ga_ref, gb_ref = rest
    hb = _modulated_norm(x_ref, scale_ref, shift_ref, lng_ref)
    tab = tab_ref[...]
    qscale = (MLA_NOPE + MLA_ROPE) ** -0.5 * LOG2E
    for h in range(n_heads):
        c0 = h * HEAD_SLOT
        a = _dot(hb, wq_ref[:, c0:c0 + HEAD_SLOT])
        qcat_ref[:, c0:c0 + LANES] = _mx(a[:, :LANES] * qscale)
        qcat_ref[:, c0 + LANES:c0 + HEAD_SLOT] = _mx(_rope128(a[:, LANES:], tab) * qscale)
    c = _dot(hb, wc_ref[...])
    cn = c * lax.rsqrt(jnp.mean(c * c, axis=-1, keepdims=True) + EPS) * kvg_ref[...]
    ckv_ref[...] = cn
    r = _rope128(_dot(hb, wr_ref[...]), tab)
    krope_ref[...] = r[:, :MLA_ROPE]
    ga_ref[...] = _mx(_silu(_dot(hb, wga_ref[...])))
    gb_ref[...] = _mx(_silu(_dot(hb, wgb_ref[...])))
    if emit_kv:
        cb = _mx(cn)
        lane = lax.broadcasted_iota(jnp.int32, r.shape, 1)
        krz = _mx(jnp.where(lane < MLA_ROPE, r, 0.0))
        kn = _dot(cb, wkv_ref[:, :n_heads * MLA_NOPE])
        for h in range(n_heads):
            c0 = h * HEAD_SLOT
            kcat_ref[:, c0:c0 + LANES] = _mx(kn[:, h * MLA_NOPE:(h + 1) * MLA_NOPE])
            kcat_ref[:, c0 + LANES:c0 + HEAD_SLOT] = krz
        vup_ref[...] = _mx(_dot(cb, wkv_ref[:, n_heads * MLA_NOPE:]))


def _proj_b_kernel(n_heads, x_ref, scale_ref, shift_ref, lng_ref, wq_ref, wk_ref, wv_ref, wqi_ref, wkw_ref,
                   qb_ref, kb_ref, vb_ref, kbb_ref, vbb_ref, qi_ref, ki_ref, kiab_ref, wi_ref):
    hb = _modulated_norm(x_ref, scale_ref, shift_ref, lng_ref)
    dscale = DSA_HEAD_DIM ** -0.5 * LOG2E
    step = 512
    for c0 in range(0, wq_ref.shape[1], step):
        qb_ref[:, c0:c0 + step] = _mx(_dot(hb, wq_ref[:, c0:c0 + step]) * dscale)
    for w_ref, f_ref, b_ref in ((wk_ref, kb_ref, kbb_ref), (wv_ref, vb_ref, vbb_ref)):
        for c0 in range(0, w_ref.shape[1], step):
            kv = _dot(hb, w_ref[:, c0:c0 + step])
            f_ref[:, c0:c0 + step] = kv
            b_ref[:, c0:c0 + step] = _mx(kv)
    for c0 in range(0, wqi_ref.shape[1], step):
        qi_ref[:, c0:c0 + step] = _mx(_dot(hb, wqi_ref[:, c0:c0 + step]))
    a = _dot(hb, wkw_ref[...])
    ki_ref[...] = a[:, :IDX_DIM]
    lane = lax.broadcasted_iota(jnp.int32, a.shape, 1)
    kz = jnp.where(lane < IDX_DIM, a, 0.0)
    kiab_ref[:, :LANES] = _mx(kz)
    kiab_ref[:, LANES:] = _mx(pltpu.roll(kz, IDX_DIM, 1))
    wi_ref[...] = a * (IDX_HEADS ** -0.5 * IDX_DIM ** -0.5)


def _row_specs(x3, mod_rows, tm):
    bv, tv, d = x3.shape
    x_spec = pl.BlockSpec((None, tm, d), lambda b, i: (b, i, 0))
    if mod_rows == 1:
        m_spec = pl.BlockSpec((None, 1, d), lambda b, i: (b, 0, 0))
    else:
        m_spec = pl.BlockSpec((None, tm, d), lambda b, i: (b, i, 0))
    return x_spec, m_spec


def _out2d(m, width, dtype, tm, nt):
    return (jax.ShapeDtypeStruct((m, width), dtype),
            pl.BlockSpec((tm, width), lambda b, i: (b * nt + i, 0)))


def _proj_a(x3, scale, shift, ln_gain, tab, ws_a, kv_gain, wkv, tm):
    bv, tv, d = x3.shape
    nt = tv // tm
    m = bv * tv
    n_heads = d // 256
    wa = n_heads * MLA_V
    emit_kv = wkv is not None
    x_spec, m_spec = _row_specs(x3, scale.shape[1], tm)
    in_specs = ([x_spec, m_spec, m_spec, _resident((1, d)), pl.BlockSpec((tm, LANES), lambda b, i: (i, 0))]
                + [_resident(w.shape) for w in ws_a] + [_resident((1, KV_LORA))])
    args = [x3, scale, shift, ln_gain.reshape(1, d), tab, *ws_a, kv_gain.reshape(1, KV_LORA)]
    outs = [_out2d(m, n_heads * HEAD_SLOT, MXU_DT, tm, nt),
            _out2d(m, KV_LORA, jnp.float32, tm, nt),
            _out2d(m, MLA_ROPE, jnp.float32, tm, nt),
            _out2d(m, wa, MXU_DT, tm, nt),
            _out2d(m, wa, MXU_DT, tm, nt)]
    if emit_kv:
        in_specs.append(_resident(wkv.shape))
        args.append(wkv)
        outs += [_out2d(m, n_heads * HEAD_SLOT, MXU_DT, tm, nt),
                 _out2d(m, wa, MXU_DT, tm, nt)]
    return pl.pallas_call(
        functools.partial(_proj_a_kernel, n_heads, emit_kv),
        grid=(bv, nt),
        in_specs=in_specs,
        out_specs=[o[1] for o in outs],
        out_shape=[o[0] for o in outs],
        compiler_params=_cparams(2),
        name="proj_a",
    )(*args)


def _proj_b(x3, scale, shift, ln_gain, ws_b, tm):
    bv, tv, d = x3.shape
    nt = tv // tm
    m = bv * tv
    n_heads = d // 256
    width_b = n_heads * DSA_HEAD_DIM
    x_spec, m_spec = _row_specs(x3, scale.shape[1], tm)
    outs = [_out2d(m, width_b, MXU_DT, tm, nt),
            _out2d(m, width_b, jnp.float32, tm, nt),
            _out2d(m, width_b, jnp.float32, tm, nt),
            _out2d(m, width_b, MXU_DT, tm, nt),
            _out2d(m, width_b, MXU_DT, tm, nt),
            _out2d(m, IDX_HEADS * IDX_DIM, MXU_DT, tm, nt),
            _out2d(m, IDX_DIM, jnp.float32, tm, nt),
            _out2d(m, 2 * LANES, MXU_DT, tm, nt),
            _out2d(m, LANES, jnp.float32, tm, nt)]
    return pl.pallas_call(
        functools.partial(_proj_b_kernel, n_heads),
        grid=(bv, nt),
        in_specs=[x_spec, m_spec, m_spec, _resident((1, d))] + [_resident(w.shape) for w in ws_b],
        out_specs=[o[1] for o in outs],
        out_shape=[o[0] for o in outs],
        compiler_params=_cparams(2),
        name="proj_b",
    )(x3, scale, shift, ln_gain.reshape(1, d), *ws_b)


def _softmax_pv(s, v):
    m = jnp.max(s, axis=-1, keepdims=True)
    p = jnp.exp2(s - m)
    l = jnp.sum(p, axis=-1, keepdims=True)
    return _dot(_mx(p), v) * (1.0 / l)


def _topk_mask(score, adm, topk, sc_ref, madd_ref):
    rows, n = score.shape
    if n <= topk:
        madd_ref[...] = jnp.where(adm, 0.0, NEG_BIG)
        return
    sc_ref[...] = jnp.where(adm, score, -jnp.inf)
    kf = float(topk)

    def count(pred):
        return jnp.sum(jnp.where(pred, 1.0, 0.0), axis=-1, keepdims=True)

    def key_to_float(key):
        key = jnp.maximum(key, KEY_NEG_INF)
        return pltpu.bitcast(key ^ ((key >> 31) & 0x7FFFFFFF), jnp.float32)

    def bit_step(i, thr):
        inc = lax.shift_left(jnp.int32(1), 31 - i)
        cand = thr + inc
        ok = count(sc_ref[...] >= key_to_float(cand)) >= kf
        return jnp.where(ok, cand, thr)

    thr = key_to_float(lax.fori_loop(0, 32, bit_step, jnp.full((rows, 1), INT_MIN, jnp.int32)))
    ge = sc_ref[...] >= thr
    madd_ref[...] = jnp.where(ge, 0.0, NEG_BIG)

    @pl.when(jnp.max(count(ge)) > kf)
    def _():
        sc = sc_ref[...]
        gt = sc > thr
        need = kf - count(gt)
        eqf = jnp.where(sc == thr, 1.0, 0.0)
        col = lax.broadcasted_iota(jnp.int32, (rows, n), 1)
        nbits = max(1, int(n).bit_length())

        def col_step(i, bound):
            cand = bound + lax.shift_left(jnp.int32(1), nbits - 1 - i)
            taken = jnp.sum(jnp.where(col < cand, eqf, 0.0), axis=-1, keepdims=True)
            return jnp.where(taken <= need, cand, bound)

        bound = lax.fori_loop(0, nbits, col_step, jnp.zeros((rows, 1), jnp.int32))
        tie_madd = jnp.where(jnp.where(col < bound, eqf, 0.0) > 0.5, 0.0, NEG_BIG)
        madd_ref[...] = jnp.where(adm, jnp.where(gt, 0.0, tie_madd), NEG_BIG)


def _indexer_scores(qi_ref, wi, kia, kib):
    half = IDX_HEADS // 2
    score = None
    for j in range(half):
        qp = qi_ref[:, j * LANES:(j + 1) * LANES]
        da = jnp.maximum(_dot_nt(qp, kia), 0.0) * wi[:, IDX_DIM + j:IDX_DIM + j + 1]
        db = jnp.maximum(_dot_nt(qp, kib), 0.0) * wi[:, IDX_DIM + half + j:IDX_DIM + half + j + 1]
        score = da + db if score is None else score + da + db
    return score


PROMPT_TQ = 256


def _chunk_madd(row0, tq, n_keys):
    qc = (row0 + lax.broadcasted_iota(jnp.int32, (tq, n_keys), 0)) // CHUNK
    kc = lax.broadcasted_iota(jnp.int32, (tq, n_keys), 1) // CHUNK
    return kc <= qc


def _chained_tile_calls(n_tiles, make_call):
    out = None
    for c in range(n_tiles):
        if out is None:
            out = make_call(c, [], [], {})
        else:
            out = make_call(c, [pl.BlockSpec(memory_space=pl.ANY)], [out], None)
    return out


def _mla_prompt_kernel(n_heads, row0, q_ref, k_ref, v_ref, *rest):
    o_ref = rest[-1]
    tq, n_keys = q_ref.shape[0], k_ref.shape[0]
    madd = jnp.where(_chunk_madd(row0, tq, n_keys), 0.0, NEG_BIG)
    for h in range(n_heads):
        c0 = h * HEAD_SLOT
        s = _dot_nt(q_ref[:, c0:c0 + HEAD_SLOT], k_ref[:, c0:c0 + HEAD_SLOT]) + madd
        o_ref[:, h * MLA_V:(h + 1) * MLA_V] = _softmax_pv(s, v_ref[:, h * MLA_V:(h + 1) * MLA_V])


def _mla_prompt(qcat, kcat, vup, b, t):
    tq = PROMPT_TQ
    n_heads = qcat.shape[1] // HEAD_SLOT
    nq = t // tq
    wa = n_heads * MLA_V
    k3 = kcat.reshape(b, t, kcat.shape[1])
    v3 = vup.reshape(b, t, wa)

    def make_call(c, extra_specs, extra_args, aliases):
        n_keys = (c + 1) * tq
        row = lambda bi: (bi * nq + c, 0)
        return pl.pallas_call(
            functools.partial(_mla_prompt_kernel, n_heads, c * tq),
            grid=(b,),
            in_specs=[pl.BlockSpec((tq, qcat.shape[1]), row),
                      pl.BlockSpec((None, n_keys, kcat.shape[1]), lambda bi: (bi, 0, 0)),
                      pl.BlockSpec((None, n_keys, wa), lambda bi: (bi, 0, 0))] + extra_specs,
            out_specs=pl.BlockSpec((tq, wa), row),
            out_shape=jax.ShapeDtypeStruct((b * t, wa), jnp.float32),
            input_output_aliases={3: 0} if aliases is None else aliases,
            compiler_params=_cparams(1),
            name="mla_prompt_%d" % c,
        )(qcat, k3, v3, *extra_args)

    return _chained_tile_calls(nq, make_call)


def _toeplitz_bias(tab_ref, h, start, n_cols, le_mask):
    pieces = []
    prev = None
    for n in range(n_cols // LANES + 1):
        seg = tab_ref[h:h + 1, start + n * LANES:start + (n + 1) * LANES]
        rot = pltpu.roll(jnp.broadcast_to(seg, (LANES, LANES)), 1, 1, stride=1, stride_axis=0)
        if prev is not None:
            pieces.append(jnp.where(le_mask, prev, rot))
        prev = rot
    return jnp.concatenate(pieces, axis=1)


def _dsa_prompt_kernel(n_heads, row0, t, topk, qb_ref, qi_ref, wi_ref, k_ref, v_ref, kiab_ref, tab_ref, *rest):
    o_ref, key_ref, madd_ref = rest[-3:]
    tq, n_keys = qb_ref.shape[0], k_ref.shape[0]
    score = _indexer_scores(qi_ref, wi_ref[...], kiab_ref[:, :LANES], kiab_ref[:, LANES:])
    _topk_mask(score, _chunk_madd(row0, tq, n_keys), topk, key_ref, madd_ref)
    near0 = max(0, row0 - LANES)
    n_near = n_keys - near0
    le_mask = (lax.broadcasted_iota(jnp.int32, (LANES, LANES), 1)
               <= lax.broadcasted_iota(jnp.int32, (LANES, LANES), 0))
    for h in range(n_heads):
        c0 = h * DSA_HEAD_DIM
        near = jnp.concatenate(
            [_toeplitz_bias(tab_ref, h, near0 - (row0 + u * LANES) - LANES + t, n_near, le_mask)
             for u in range(tq // LANES)], axis=0)
        if near0 > 0:
            far = jnp.broadcast_to(tab_ref[h:h + 1, 0:1], (tq, near0))
            bias = jnp.concatenate([far, near], axis=1)
        else:
            bias = near
        s = _dot_nt(qb_ref[:, c0:c0 + DSA_HEAD_DIM], k_ref[:, c0:c0 + DSA_HEAD_DIM]) + bias + madd_ref[...]
        o_ref[:, c0:c0 + DSA_HEAD_DIM] = _softmax_pv(s, v_ref[:, c0:c0 + DSA_HEAD_DIM])


def _dsa_prompt(qb, qi, wi, kbb, vbb, kiab, bias_tab, b, t, topk):
    tq = PROMPT_TQ
    assert MAX_DISTANCE <= LANES and tq % LANES == 0
    n_heads = qb.shape[1] // DSA_HEAD_DIM
    nq = t // tq
    wb = qb.shape[1]
    k3, v3, ki3 = kbb.reshape(b, t, wb), vbb.reshape(b, t, wb), kiab.reshape(b, t, 2 * LANES)
    per_b = lambda bi: (bi, 0, 0)

    def make_call(c, extra_specs, extra_args, aliases):
        n_keys = (c + 1) * tq
        row = lambda bi: (bi * nq + c, 0)
        return pl.pallas_call(
            functools.partial(_dsa_prompt_kernel, n_heads, c * tq, t, topk),
            grid=(b,),
            in_specs=[pl.BlockSpec((tq, wb), row),
                      pl.BlockSpec((tq, qi.shape[1]), row),
                      pl.BlockSpec((tq, LANES), row),
                      pl.BlockSpec((None, n_keys, wb), per_b),
                      pl.BlockSpec((None, n_keys, wb), per_b),
                      pl.BlockSpec((None, n_keys, 2 * LANES), per_b),
                      _resident(bias_tab.shape)] + extra_specs,
            out_specs=pl.BlockSpec((tq, wb), row),
            out_shape=jax.ShapeDtypeStruct((b * t, wb), jnp.float32),
            input_output_aliases={7: 0} if aliases is None else aliases,
            scratch_shapes=[pltpu.VMEM((tq, n_keys), jnp.float32), pltpu.VMEM((tq, n_keys), jnp.float32)],
            compiler_params=_cparams(1),
            name="dsa_prompt_%d" % c,
        )(qb, qi, wi, k3, v3, ki3, bias_tab, *extra_args)

    return _chained_tile_calls(nq, make_call)


def _mla_sample_kernel(n_heads, past, ts, qcat_ref, cckv_ref, ckr_ref, nckv_ref, nkr_ref, wuk_ref, wuv_ref,
                       o_ref, kall_ref, rall_ref):
    s_pad = kall_ref.shape[0]
    n_keys = past + ts

    @pl.when(pl.program_id(0) == 0)
    def _():
        kall_ref[...] = jnp.zeros_like(kall_ref)
        rall_ref[...] = jnp.zeros_like(rall_ref)

    kall_ref[0:past, :] = _mx(cckv_ref[...])
    kall_ref[past:n_keys, :] = _mx(nckv_ref[...])
    rall_ref[0:past, 0:MLA_ROPE] = _mx(ckr_ref[...])
    rall_ref[past:n_keys, 0:MLA_ROPE] = _mx(nkr_ref[...])
    qlat, qrope = [], []
    for h in range(n_heads):
        c0 = h * HEAD_SLOT
        qlat.append(_mx(_dot(qcat_ref[:, c0:c0 + LANES], wuk_ref[h])))
        qrope.append(qcat_ref[:, c0 + LANES:c0 + HEAD_SLOT])
    qlat = jnp.concatenate(qlat, axis=0)
    qrope = jnp.concatenate(qrope, axis=0)
    s = _dot_nt(qlat, kall_ref[...]) + _dot_nt(qrope, rall_ref[...])
    col = lax.broadcasted_iota(jnp.int32, s.shape, 1)
    s = jnp.where(col < n_keys, s, NEG_BIG)
    olat = _mx(_softmax_pv(s, kall_ref[...]))
    for h in range(n_heads):
        o_ref[:, h * MLA_V:(h + 1) * MLA_V] = _dot_nt(olat[h * ts:(h + 1) * ts], wuv_ref[h])


def _mla_sample(qcat, cache_ckv, cache_kr, new_ckv, new_kr, wuk, wuv, ts):
    nb, past, c = cache_ckv.shape
    n_heads = qcat.shape[1] // HEAD_SLOT
    wa = n_heads * MLA_V
    s_pad = pl.cdiv(past + ts, LANES) * LANES
    row = lambda bi: (bi, 0)
    return pl.pallas_call(
        functools.partial(_mla_sample_kernel, n_heads, past, ts),
        grid=(nb,),
        in_specs=[pl.BlockSpec((ts, qcat.shape[1]), row),
                  pl.BlockSpec((None, past, c), lambda bi: (bi, 0, 0)),
                  pl.BlockSpec((None, past, MLA_ROPE), lambda bi: (bi, 0, 0)),
                  pl.BlockSpec((ts, c), row),
                  pl.BlockSpec((ts, MLA_ROPE), row),
                  _resident(wuk.shape), _resident(wuv.shape)],
        out_specs=pl.BlockSpec((ts, wa), row),
        out_shape=jax.ShapeDtypeStruct((nb * ts, wa), jnp.float32),
        scratch_shapes=[pltpu.VMEM((s_pad, c), MXU_DT), pltpu.VMEM((s_pad, LANES), MXU_DT)],
        compiler_params=_cparams(1),
        name="mla_sample",
    )(qcat, cache_ckv, cache_kr, new_ckv, new_kr, wuk, wuv)


def _dsa_select_kernel(past, ts, topk, qi_ref, wi_ref, cki_ref, nkiab_ref, madd_ref, kia_ref, kib_ref, key_ref):
    bi = pl.program_id(0)
    n_keys = past + ts
    rows, s_pad = madd_ref.shape

    @pl.when(bi == 0)
    def _():
        kia_ref[...] = jnp.zeros_like(kia_ref)
        kib_ref[...] = jnp.zeros_like(kib_ref)

    cki = _mx(cki_ref[...])
    kia_ref[0:past, 0:IDX_DIM] = cki
    kib_ref[0:past, IDX_DIM:LANES] = cki
    kia_ref[past:n_keys, :] = nkiab_ref[:, :LANES]
    kib_ref[past:n_keys, :] = nkiab_ref[:, LANES:]
    madd_ref[pl.ds(pl.multiple_of(bi * ts, ts), ts), :] = _indexer_scores(
        qi_ref, wi_ref[...], kia_ref[...], kib_ref[...])

    @pl.when(bi == pl.num_programs(0) - 1)
    def _():
        col = lax.broadcasted_iota(jnp.int32, (rows, s_pad), 1)
        _topk_mask(madd_ref[...], col < n_keys, topk, key_ref, madd_ref)


def _dsa_select(qi, wi, cache_ki, new_kiab, ts, topk):
    nb, past, _ = cache_ki.shape
    s_pad = pl.cdiv(past + ts, LANES) * LANES
    row = lambda bi: (bi, 0)
    return pl.pallas_call(
        functools.partial(_dsa_select_kernel, past, ts, topk),
        grid=(nb,),
        in_specs=[pl.BlockSpec((ts, qi.shape[1]), row),
                  pl.BlockSpec((ts, LANES), row),
                  pl.BlockSpec((None, past, IDX_DIM), lambda bi: (bi, 0, 0)),
                  pl.BlockSpec((ts, 2 * LANES), row)],
        out_specs=pl.BlockSpec((nb * ts, s_pad), lambda bi: (0, 0)),
        out_shape=jax.ShapeDtypeStruct((nb * ts, s_pad), jnp.float32),
        scratch_shapes=[pltpu.VMEM((s_pad, LANES), MXU_DT), pltpu.VMEM((s_pad, LANES), MXU_DT),
                        pltpu.VMEM((nb * ts, s_pad), jnp.float32)],
        compiler_params=_cparams(1),
        name="dsa_select",
    )(qi, wi, cache_ki, new_kiab)


def _dsa_sample_kernel(n_heads, past, ts, qb_ref, madd_ref, ck_ref, cv_ref, nk_ref, nv_ref, bias_ref, expand_ref,
                       o_ref, kflat_ref, vflat_ref, biasw_ref):
    n_keys = past + ts
    wide = n_heads * LANES
    n_blocks = pl.cdiv(n_keys, LANES)
    widths = [min(wide, (n_keys - j * LANES) * n_heads) for j in range(n_blocks)]

    @pl.when(pl.program_id(0) == 0)
    def _():
        shape = (n_heads * ts, wide)
        same_head = (lax.broadcasted_iota(jnp.int32, shape, 0) // ts
                     == lax.broadcasted_iota(jnp.int32, shape, 1) % n_heads)
        for j in range(n_blocks):
            b = bias_ref[:, j * LANES:(j + 1) * LANES]
            hi = _mx(b)
            rest = b - hi.astype(jnp.float32)
            mid = _mx(rest)
            lo = _mx(rest - mid.astype(jnp.float32))
            piece = _dot(hi, expand_ref[...]) + _dot(mid, expand_ref[...]) + _dot(lo, expand_ref[...])
            biasw_ref[:, j * wide:j * wide + widths[j]] = jnp.where(same_head, piece, NEG_BIG)[:, :widths[j]]

    kflat_ref[0:past * n_heads, :] = _mx(ck_ref[...])
    kflat_ref[past * n_heads:, :] = nk_ref[...]
    vflat_ref[0:past * n_heads, :] = _mx(cv_ref[...])
    vflat_ref[past * n_heads:, :] = nv_ref[...]
    sel = _mx(jnp.where(madd_ref[...] == 0.0, 1.0, 0.0))
    pieces = [_dot(sel[:, j * LANES:(j + 1) * LANES], expand_ref[...])[:, :widths[j]] for j in range(n_blocks)]
    sel_wide = jnp.concatenate(pieces, axis=1)
    sel_wide = jnp.concatenate([sel_wide] * n_heads, axis=0)
    q_all = jnp.concatenate([qb_ref[:, h * DSA_HEAD_DIM:(h + 1) * DSA_HEAD_DIM] for h in range(n_heads)], axis=0)
    s = _dot_nt(q_all, kflat_ref[...]) + biasw_ref[...] + jnp.where(sel_wide > 0.5, 0.0, NEG_BIG)
    o = _softmax_pv(s, vflat_ref[...])
    for h in range(n_heads):
        o_ref[:, h * DSA_HEAD_DIM:(h + 1) * DSA_HEAD_DIM] = o[h * ts:(h + 1) * ts]


def _dsa_sample(qb, madd, cache_k, cache_v, new_k, new_v, bias, ts):
    nb, past, n_heads, _ = cache_k.shape
    wb = n_heads * DSA_HEAD_DIM
    n_keys = past + ts
    assert (n_keys % LANES * n_heads) % LANES == 0
    expand = _mx(jnp.repeat(jnp.eye(LANES, dtype=jnp.float32), n_heads, axis=1))
    row = lambda bi: (bi, 0)
    per_b = lambda bi: (bi, 0, 0)
    return pl.pallas_call(
        functools.partial(_dsa_sample_kernel, n_heads, past, ts),
        grid=(nb,),
        in_specs=[pl.BlockSpec((ts, wb), row),
                  pl.BlockSpec((ts, madd.shape[1]), row),
                  pl.BlockSpec((None, past * n_heads, DSA_HEAD_DIM), per_b),
                  pl.BlockSpec((None, past * n_heads, DSA_HEAD_DIM), per_b),
                  pl.BlockSpec((ts * n_heads, DSA_HEAD_DIM), row),
                  pl.BlockSpec((ts * n_heads, DSA_HEAD_DIM), row),
                  _resident(bias.shape), _resident(expand.shape)],
        out_specs=pl.BlockSpec((ts, wb), row),
        out_shape=jax.ShapeDtypeStruct((nb * ts, wb), jnp.float32),
        scratch_shapes=[pltpu.VMEM((n_keys * n_heads, DSA_HEAD_DIM), MXU_DT),
                        pltpu.VMEM((n_keys * n_heads, DSA_HEAD_DIM), MXU_DT),
                        pltpu.VMEM((n_heads * ts, n_keys * n_heads), jnp.float32)],
        compiler_params=_cparams(1),
        name="dsa_sample",
    )(qb, madd, cache_k.reshape(nb, past * n_heads, DSA_HEAD_DIM), cache_v.reshape(nb, past * n_heads, DSA_HEAD_DIM),
      new_k.reshape(nb * ts * n_heads, DSA_HEAD_DIM), new_v.reshape(nb * ts * n_heads, DSA_HEAD_DIM),
      bias, expand)


def _out_kernel(oa_ref, ob_ref, ga_ref, gb_ref, x_ref, gate_ref, gna_ref, gnb_ref, w_ref, fg_ref, y_ref):
    def gated(o_ref, g_ref, gain_ref):
        o = o_ref[...]
        on = o * lax.rsqrt(jnp.mean(o * o, axis=-1, keepdims=True) + EPS) * gain_ref[...]
        return _mx(on * g_ref[...].astype(jnp.float32))

    wa = oa_ref.shape[1]
    out = _dot(gated(oa_ref, ga_ref, gna_ref), w_ref[0:wa, :]) + _dot(gated(ob_ref, gb_ref, gnb_ref), w_ref[wa:, :])
    xn = x_ref[...] + gate_ref[...] * out
    y_ref[...] = xn * lax.rsqrt(jnp.mean(xn * xn, axis=-1, keepdims=True) + EPS) * fg_ref[...]


def _out_proj(oa, ob, ga, gb, x3, gate, gain_a, gain_b, w_out, final_gain, tm):
    bv, tv, d = x3.shape
    nt = tv // tm
    wa = oa.shape[1]
    x_spec, g_spec = _row_specs(x3, gate.shape[1], tm)
    row = lambda b, i: (b * nt + i, 0)
    return pl.pallas_call(
        _out_kernel,
        grid=(bv, nt),
        in_specs=[pl.BlockSpec((tm, wa), row), pl.BlockSpec((tm, wa), row),
                  pl.BlockSpec((tm, wa), row), pl.BlockSpec((tm, wa), row),
                  x_spec, g_spec, _resident((1, wa)), _resident((1, wa)),
                  _resident(w_out.shape), _resident((1, d))],
        out_specs=pl.BlockSpec((None, tm, d), lambda b, i: (b, i, 0)),
        out_shape=jax.ShapeDtypeStruct((bv, tv, d), jnp.float32),
        compiler_params=_cparams(2),
        name="out_proj",
    )(oa, ob, ga, gb, x3, gate, gain_a.reshape(1, wa), gain_b.reshape(1, wa), w_out, final_gain.reshape(1, d))


def _pack_kernel(n_heads, wt_ref, qa_ref, ckv_ref, kr_ref, ga_ref, gb_ref, qb_ref, kb_ref, vb_ref, qi_ref, kiw_ref):
    wa = n_heads * MLA_V
    half = MLA_ROPE // 2
    q_head = MLA_NOPE + MLA_ROPE
    step = 2 * LANES

    def panel(ref, off):
        for c in range(0, ref.shape[1], step):
            w = min(step, ref.shape[1] - c)
            ref[:, c:c + w] = _mx(wt_ref[off + c:off + c + w, :].T)

    def dup_rope(off):
        x1, x2 = wt_ref[off:off + half, :], wt_ref[off + half:off + 2 * half, :]
        return _mx(jnp.concatenate([x1, x2, x2, x1], axis=0).T)

    for h in range(n_heads):
        src, dst = h * q_head, h * HEAD_SLOT
        qa_ref[:, dst:dst + LANES] = _mx(wt_ref[src:src + MLA_NOPE, :].T)
        qa_ref[:, dst + LANES:dst + HEAD_SLOT] = dup_rope(src + MLA_NOPE)
    off = n_heads * q_head
    panel(ckv_ref, off)
    off += KV_LORA
    kr_ref[...] = dup_rope(off)
    off += MLA_ROPE
    for ref in (ga_ref, qb_ref, kb_ref, vb_ref):
        panel(ref, off)
        off += wa
    pairs = IDX_HEADS // 2
    for j in range(pairs):
        lo = wt_ref[off + IDX_DIM * j:off + IDX_DIM * (j + 1), :]
        hi = wt_ref[off + IDX_DIM * (j + pairs):off + IDX_DIM * (j + pairs + 1), :]
        qi_ref[:, LANES * j:LANES * (j + 1)] = _mx(jnp.concatenate([lo, hi], axis=0).T)
    off += IDX_HEADS * IDX_DIM
    n_kw = IDX_DIM + IDX_HEADS
    kiw = jnp.concatenate([wt_ref[off:off + n_kw, :], jnp.zeros((LANES - n_kw, wt_ref.shape[1]), jnp.float32)], axis=0)
    kiw_ref[...] = _mx(kiw.T)
    off += n_kw
    panel(gb_ref, off)


def _pack_w_in(w_in, n_heads):
    wt = w_in.T
    n, d = wt.shape
    wa = n_heads * MLA_V
    assert n == n_heads * (MLA_NOPE + MLA_ROPE) + KV_LORA + MLA_ROPE + 5 * wa + IDX_HEADS * IDX_DIM + IDX_DIM + IDX_HEADS
    assert 2 * IDX_DIM == LANES and 2 * MLA_ROPE == LANES and MLA_NOPE == LANES
    slab = 256
    widths = [n_heads * HEAD_SLOT, KV_LORA, LANES, wa, wa, wa, wa, wa, IDX_HEADS * IDX_DIM, LANES]
    q_a, ckv, kr, g_a, g_b, q_b, k_b, v_b, q_i, kiw = pl.pallas_call(
        functools.partial(_pack_kernel, n_heads),
        grid=(d // slab,),
        in_specs=[pl.BlockSpec((n, slab), lambda i: (0, i))],
        out_specs=[pl.BlockSpec((slab, w), lambda i: (i, 0)) for w in widths],
        out_shape=[jax.ShapeDtypeStruct((d, w), MXU_DT) for w in widths],
        compiler_params=_cparams(1),
        name="pack_w_in",
    )(wt)
    return [q_a, ckv, kr, g_a, g_b], [q_b, k_b, v_b, q_i, kiw]


def _rope_table(pos):
    half = MLA_ROPE // 2
    freqs = jnp.power(ROPE_THETA, -jnp.arange(half, dtype=jnp.float32) / half)
    ang = pos.astype(jnp.float32)[:, None] * freqs
    cos, sin = jnp.cos(ang), jnp.sin(ang)
    return jnp.concatenate([cos, cos, -sin, sin], axis=1)


def _rel_bucket(rel):
    nb = N_BUCKETS // 2
    max_exact = nb // 2
    n = jnp.abs(rel)
    nf = jnp.maximum(n, 1).astype(jnp.float32)
    large = max_exact + (jnp.log(nf / max_exact) / math.log(MAX_DISTANCE / max_exact)
                         * (nb - max_exact)).astype(jnp.int32)
    large = jnp.minimum(large, nb - 1)
    return jnp.where(rel > 0, nb, 0) + jnp.where(n < max_exact, n, large)


def _bucket_bias(rel_bias, rel):
    return (rel_bias * LOG2E)[_rel_bucket(rel)]


def kernel(x_prompt, x_sample, cache_mla_ckv, cache_mla_krope, cache_dsa_k, cache_dsa_v, cache_idx_k,
           c_prompt, c_sample, w_ada, b_ada, ln_gain, w_in, mla_kv_gain, w_uk, w_uv, rel_bias,
           out_gain_a, out_gain_b, w_out, final_gain):
    assert w_ada.shape[0] == 1, "single-layer step"
    b, t, d = x_prompt.shape
    nb, ts, _ = x_sample.shape
    past = cache_mla_ckv.shape[2]
    n_heads = d // 256
    wb = n_heads * DSA_HEAD_DIM
    s_all = past + ts
    assert t % PROMPT_TQ == 0 and PROMPT_TQ % CHUNK == 0
    assert (s_all - 1) // CHUNK <= past // CHUNK

    mod = _ada_mod(jnp.concatenate([c_prompt, c_sample], axis=0), w_ada[0], b_ada[0])
    shift, scale, gate = mod[:, :d], mod[:, d:2 * d], mod[:, 2 * d:]
    mod_p = [v[:b].reshape(b, 1, d) for v in (scale, shift, gate)]
    mod_s = [jnp.broadcast_to(v[b:, None, :], (nb, ts, d)).reshape(1, nb * ts, d) for v in (scale, shift, gate)]

    w_a, w_b = _pack_w_in(w_in[0], n_heads)
    wkv = _mx(jnp.concatenate([w_uk[0].transpose(2, 0, 1).reshape(KV_LORA, n_heads * MLA_NOPE),
                               w_uv[0].transpose(2, 0, 1).reshape(KV_LORA, n_heads * MLA_V)], axis=1))
    w_o = _mx(w_out[0])
    pos_p = jnp.arange(t, dtype=jnp.int32)
    pos_s = past + jnp.arange(ts, dtype=jnp.int32)
    tab_p = _rope_table(pos_p)
    tab_s = jnp.tile(_rope_table(pos_s), (nb, 1))

    tm = 256
    qcat, ckv_p, krope_p, ga, gb, kcat, vup = _proj_a(
        x_prompt, mod_p[0], mod_p[1], ln_gain[0], tab_p, w_a, mla_kv_gain[0], wkv, tm)
    qb, k_p, v_p, kbb, vbb, qi, idxk_p, kiab, wi = _proj_b(x_prompt, mod_p[0], mod_p[1], ln_gain[0], w_b, tm)
    o_a = _mla_prompt(qcat, kcat, vup, b, t)
    rel = jnp.arange(-(t - 1), t + 1, dtype=jnp.int32)
    bias_tab = _bucket_bias(rel_bias, rel).T
    o_b = _dsa_prompt(qb, qi, wi, kbb, vbb, kiab, bias_tab, b, t, min(TOPK_MAX, t // 4))
    y_prompt = _out_proj(o_a, o_b, ga, gb, x_prompt, mod_p[2], out_gain_a[0], out_gain_b[0], w_o, final_gain, tm)

    xs3 = x_sample.reshape(1, nb * ts, d)
    tms = min(256, nb * ts)
    qcat_s, ckv_s, krope_s, ga_s, gb_s = _proj_a(
        xs3, mod_s[0], mod_s[1], ln_gain[0], tab_s, w_a, mla_kv_gain[0], None, tms)
    qb_s, k_s, v_s, kbb_s, vbb_s, qi_s, idxk_s, kiab_s, wi_s = _proj_b(xs3, mod_s[0], mod_s[1], ln_gain[0], w_b, tms)
    oa_s = _mla_sample(qcat_s, cache_mla_ckv[0], cache_mla_krope[0], ckv_s, krope_s,
                       _mx(w_uk[0]), _mx(w_uv[0]), ts)
    s_pad = pl.cdiv(s_all, LANES) * LANES
    tab_s = _bucket_bias(rel_bias, jnp.arange(-(s_all - 1), s_pad, dtype=jnp.int32)).T
    bias_s = jnp.stack([tab_s[:, ts - 1 - i:ts - 1 - i + s_pad] for i in range(ts)], axis=1)
    bias_s = bias_s.reshape(n_heads * ts, s_pad)
    madd_s = _dsa_select(qi_s, wi_s, cache_idx_k[0], kiab_s, ts, min(TOPK_MAX, s_all // 4))
    ob_s = _dsa_sample(qb_s, madd_s, cache_dsa_k[0], cache_dsa_v[0], kbb_s, vbb_s, bias_s, ts)
    y_sample = _out_proj(oa_s, ob_s, ga_s, gb_s, xs3, mod_s[2], out_gain_a[0], out_gain_b[0], w_o, final_gain, tms)

    hd = (n_heads, DSA_HEAD_DIM)
    return (y_prompt, y_sample.reshape(nb, ts, d),
            ckv_p.reshape(1, b, t, KV_LORA), krope_p.reshape(1, b, t, MLA_ROPE),
            k_p.reshape(1, b, t, *hd), v_p.reshape(1, b, t, *hd), idxk_p.reshape(1, b, t, IDX_DIM),
            ckv_s.reshape(1, nb, ts, KV_LORA), krope_s.reshape(1, nb, ts, MLA_ROPE),
            k_s.reshape(1, nb, ts, *hd), v_s.reshape(1, nb, ts, *hd), idxk_s.reshape(1, nb, ts, IDX_DIM))
```

```python
import functools
import math

import jax
import jax.numpy as jnp
from jax import lax
from jax.experimental import pallas as pl
from jax.experimental.pallas import tpu as pltpu

MXU_DT = jnp.bfloat16

CHUNK = 64
MLA_NOPE = 128
MLA_ROPE = 64
MLA_V = 128
KV_LORA = 512
DSA_HEAD_DIM = 128
IDX_HEADS = 16
IDX_DIM = 64
TOPK_MAX = 256
N_BUCKETS = 32
MAX_DISTANCE = 128
ROPE_THETA = 10000.0
EPS = 1e-6

LANES = 128
HEAD_SLOT = 256
NEG_BIG = -1e30
LOG2E = math.log2(math.e)
INT_MIN = -2 ** 31
KEY_NEG_INF = INT_MIN + 0x7FFFFF
VMEM_LIMIT = 56 * 1024 * 1024


def _cparams(n_grid, vmem=VMEM_LIMIT):
    return pltpu.CompilerParams(dimension_semantics=("arbitrary",) * n_grid, vmem_limit_bytes=vmem)


def _mx(v):
    return v.astype(MXU_DT)


def _dot(a, b):
    return jnp.dot(a, b, preferred_element_type=jnp.float32)


def _dot_nt(a, b):
    return lax.dot_general(a, b, (((1,), (1,)), ((), ())), preferred_element_type=jnp.float32)


def _silu(v):
    return v * (1.0 / (1.0 + jnp.exp(-v)))


def _resident(shape):
    nd = len(shape)
    return pl.BlockSpec(shape, lambda *_: (0,) * nd, pipeline_mode=pl.Buffered(1))


def _ada_kernel(c_ref, w_ref, b_ref, o_ref):
    a = _mx(_silu(c_ref[...]))
    o_ref[...] = _dot(a, _mx(w_ref[...])) + b_ref[...]


def _ada_mod(c_all, w_ada, b_ada):
    m, d = c_all.shape
    n = w_ada.shape[1]
    tn = 1024
    return pl.pallas_call(
        _ada_kernel,
        grid=(n // tn,),
        in_specs=[pl.BlockSpec((m, d), lambda j: (0, 0)),
                  pl.BlockSpec((d, tn), lambda j: (0, j)),
                  pl.BlockSpec((1, tn), lambda j: (0, j))],
        out_specs=pl.BlockSpec((m, tn), lambda j: (0, j)),
        out_shape=jax.ShapeDtypeStruct((m, n), jnp.float32),
        compiler_params=_cparams(1),
        name="ada_mod",
    )(c_all, w_ada, b_ada.reshape(1, n))


def _modulated_norm(x_ref, scale_ref, shift_ref, lng_ref):
    x = x_ref[...]
    xn = x * lax.rsqrt(jnp.mean(x * x, axis=-1, keepdims=True) + EPS) * lng_ref[...]
    return _mx(xn * (1.0 + scale_ref[...]) + shift_ref[...])


def _rope128(a, tab):
    t = a * tab
    return t + pltpu.roll(t, 64, 1)


def _proj_a_kernel(n_heads, emit_kv, x_ref, scale_ref, shift_ref, lng_ref, tab_ref,
                   wq_ref, wc_ref, wr_ref, wga_ref, wgb_ref, kvg_ref, *rest):
    if emit_kv:
        wkv_ref, qcat_ref, ckv_ref, krope_ref, ga_ref, gb_ref, kcat_ref, vup_ref, attn_ref = rest
        attn_ref[...] = jnp.zeros_like(attn_ref)
    else:
        qcat_ref, ckv_ref, krope_ref, ga_ref, gb_ref = rest
    hb = _modulated_norm(x_ref, scale_ref, shift_ref, lng_ref)
    tab = tab_ref[...]
    qscale = (MLA_NOPE + MLA_ROPE) ** -0.5 * LOG2E
    for h in range(n_heads):
        c0 = h * HEAD_SLOT
        a = _dot(hb, wq_ref[:, c0:c0 + HEAD_SLOT])
        qcat_ref[:, c0:c0 + LANES] = _mx(a[:, :LANES] * qscale)
        qcat_ref[:, c0 + LANES:c0 + HEAD_SLOT] = _mx(_rope128(a[:, LANES:], tab) * qscale)
    c = _dot(hb, wc_ref[...])
    cn = c * lax.rsqrt(jnp.mean(c * c, axis=-1, keepdims=True) + EPS) * kvg_ref[...]
    ckv_ref[...] = cn
    r = _rope128(_dot(hb, wr_ref[...]), tab)
    krope_ref[...] = r[:, :MLA_ROPE]
    ga_ref[...] = _mx(_silu(_dot(hb, wga_ref[...])))
    gb_ref[...] = _mx(_silu(_dot(hb, wgb_ref[...])))
    if emit_kv:
        cb = _mx(cn)
        lane = lax.broadcasted_iota(jnp.int32, r.shape, 1)
        krz = _mx(jnp.where(lane < MLA_ROPE, r, 0.0))
        kn = _dot(cb, wkv_ref[:, :n_heads * MLA_NOPE])
        for h in range(n_heads):
            c0 = h * HEAD_SLOT
            kcat_ref[:, c0:c0 + LANES] = _mx(kn[:, h * MLA_NOPE:(h + 1) * MLA_NOPE])
            kcat_ref[:, c0 + LANES:c0 + HEAD_SLOT] = krz
        vup_ref[...] = _mx(_dot(cb, wkv_ref[:, n_heads * MLA_NOPE:]))


def _proj_b_kernel(n_heads, x_ref, scale_ref, shift_ref, lng_ref, wq_ref, wk_ref, wv_ref, wqi_ref, wkw_ref,
                   qb_ref, kb_ref, vb_ref, kbb_ref, vbb_ref, qi_ref, ki_ref, kiab_ref, wi_ref, *attn_ref):
    if attn_ref:
        attn_ref[0][...] = jnp.zeros_like(attn_ref[0])
    hb = _modulated_norm(x_ref, scale_ref, shift_ref, lng_ref)
    dscale = DSA_HEAD_DIM ** -0.5 * LOG2E
    step = 512
    for c0 in range(0, wq_ref.shape[1], step):
        qb_ref[:, c0:c0 + step] = _mx(_dot(hb, wq_ref[:, c0:c0 + step]) * dscale)
    for w_ref, f_ref, b_ref in ((wk_ref, kb_ref, kbb_ref), (wv_ref, vb_ref, vbb_ref)):
        for c0 in range(0, w_ref.shape[1], step):
            kv = _dot(hb, w_ref[:, c0:c0 + step])
            f_ref[:, c0:c0 + step] = kv
            b_ref[:, c0:c0 + step] = _mx(kv)
    for c0 in range(0, wqi_ref.shape[1], step):
        qi_ref[:, c0:c0 + step] = _mx(_dot(hb, wqi_ref[:, c0:c0 + step]))
    a = _dot(hb, wkw_ref[...])
    ki_ref[...] = a[:, :IDX_DIM]
    lane = lax.broadcasted_iota(jnp.int32, a.shape, 1)
    kz = jnp.where(lane < IDX_DIM, a, 0.0)
    kiab_ref[:, :LANES] = _mx(kz)
    kiab_ref[:, LANES:] = _mx(pltpu.roll(kz, IDX_DIM, 1))
    wi_ref[...] = a * (IDX_HEADS ** -0.5 * IDX_DIM ** -0.5)


def _row_specs(x3, mod_rows, tm):
    bv, tv, d = x3.shape
    x_spec = pl.BlockSpec((None, tm, d), lambda b, i: (b, i, 0))
    if mod_rows == 1:
        m_spec = pl.BlockSpec((None, 1, d), lambda b, i: (b, 0, 0))
    else:
        m_spec = pl.BlockSpec((None, tm, d), lambda b, i: (b, i, 0))
    return x_spec, m_spec


def _out2d(m, width, dtype, tm, nt):
    return (jax.ShapeDtypeStruct((m, width), dtype),
            pl.BlockSpec((tm, width), lambda b, i: (b * nt + i, 0)))


def _proj_a(x3, scale, shift, ln_gain, tab, ws_a, kv_gain, wkv, tm):
    bv, tv, d = x3.shape
    nt = tv // tm
    m = bv * tv
    n_heads = d // 256
    wa = n_heads * MLA_V
    emit_kv = wkv is not None
    x_spec, m_spec = _row_specs(x3, scale.shape[1], tm)
    in_specs = ([x_spec, m_spec, m_spec, _resident((1, d)), pl.BlockSpec((tm, LANES), lambda b, i: (i, 0))]
                + [_resident(w.shape) for w in ws_a] + [_resident((1, KV_LORA))])
    args = [x3, scale, shift, ln_gain.reshape(1, d), tab, *ws_a, kv_gain.reshape(1, KV_LORA)]
    outs = [_out2d(m, n_heads * HEAD_SLOT, MXU_DT, tm, nt),
            _out2d(m, KV_LORA, jnp.float32, tm, nt),
            _out2d(m, MLA_ROPE, jnp.float32, tm, nt),
            _out2d(m, wa, MXU_DT, tm, nt),
            _out2d(m, wa, MXU_DT, tm, nt)]
    if emit_kv:
        in_specs.append(_resident(wkv.shape))
        args.append(wkv)
        outs += [_out2d(m, n_heads * HEAD_SLOT, MXU_DT, tm, nt),
                 _out2d(m, wa, MXU_DT, tm, nt),
                 _out2d(m, wa, MXU_DT, tm, nt)]
    return pl.pallas_call(
        functools.partial(_proj_a_kernel, n_heads, emit_kv),
        grid=(bv, nt),
        in_specs=in_specs,
        out_specs=[o[1] for o in outs],
        out_shape=[o[0] for o in outs],
        compiler_params=_cparams(2),
        name="proj_a",
    )(*args)


def _proj_b(x3, scale, shift, ln_gain, ws_b, tm, emit_attn_buffer):
    bv, tv, d = x3.shape
    nt = tv // tm
    m = bv * tv
    n_heads = d // 256
    width_b = n_heads * DSA_HEAD_DIM
    x_spec, m_spec = _row_specs(x3, scale.shape[1], tm)
    outs = [_out2d(m, width_b, MXU_DT, tm, nt),
            _out2d(m, width_b, jnp.float32, tm, nt),
            _out2d(m, width_b, jnp.float32, tm, nt),
            _out2d(m, width_b, MXU_DT, tm, nt),
            _out2d(m, width_b, MXU_DT, tm, nt),
            _out2d(m, IDX_HEADS * IDX_DIM, MXU_DT, tm, nt),
            _out2d(m, IDX_DIM, jnp.float32, tm, nt),
            _out2d(m, 2 * LANES, MXU_DT, tm, nt),
            _out2d(m, LANES, jnp.float32, tm, nt)]
    if emit_attn_buffer:
        outs.append(_out2d(m, width_b, MXU_DT, tm, nt))
    return pl.pallas_call(
        functools.partial(_proj_b_kernel, n_heads),
        grid=(bv, nt),
        in_specs=[x_spec, m_spec, m_spec, _resident((1, d))] + [_resident(w.shape) for w in ws_b],
        out_specs=[o[1] for o in outs],
        out_shape=[o[0] for o in outs],
        compiler_params=_cparams(2),
        name="proj_b",
    )(x3, scale, shift, ln_gain.reshape(1, d), *ws_b)


def _softmax_pv(s, v):
    m = jnp.max(s, axis=-1, keepdims=True)
    p = jnp.exp2(s - m)
    l = jnp.sum(p, axis=-1, keepdims=True)
    return _dot(_mx(p), v) * (1.0 / l)


def _topk_mask(score, adm, topk, sc_ref, madd_ref):
    rows, n = score.shape
    if n <= topk:
        madd_ref[...] = jnp.where(adm, 0.0, NEG_BIG)
        return
    sc_ref[...] = jnp.where(adm, score, -jnp.inf)
    kf = float(topk)

    def count(pred):
        return jnp.sum(jnp.where(pred, 1.0, 0.0), axis=-1, keepdims=True)

    def key_to_float(key):
        return pltpu.bitcast(key ^ ((key >> 31) & 0x7FFFFFFF), jnp.float32)

    def bit_step(i, thr):
        inc = lax.shift_left(jnp.int32(1), 31 - i)
        cand = thr + inc
        ok = count(sc_ref[...] >= key_to_float(cand)) >= kf
        return jnp.where(ok, cand, thr)

    thr = lax.fori_loop(0, 32, bit_step, jnp.full((rows, 1), INT_MIN, jnp.int32))
    thr = key_to_float(jnp.maximum(thr, KEY_NEG_INF))
    ge = sc_ref[...] >= thr
    madd_ref[...] = jnp.where(ge, 0.0, NEG_BIG)

    @pl.when(jnp.max(count(ge)) > kf)
    def _():
        sc = sc_ref[...]
        gt = sc > thr
        need = kf - count(gt)
        eqf = jnp.where(sc == thr, 1.0, 0.0)
        col = lax.broadcasted_iota(jnp.int32, (rows, n), 1)
        nbits = max(1, int(n).bit_length())

        def col_step(i, bound):
            cand = bound + lax.shift_left(jnp.int32(1), nbits - 1 - i)
            taken = jnp.sum(jnp.where(col < cand, eqf, 0.0), axis=-1, keepdims=True)
            return jnp.where(taken <= need, cand, bound)

        bound = lax.fori_loop(0, nbits, col_step, jnp.zeros((rows, 1), jnp.int32))
        tie_madd = jnp.where(jnp.where(col < bound, eqf, 0.0) > 0.5, 0.0, NEG_BIG)
        madd_ref[...] = jnp.where(adm, jnp.where(gt, 0.0, tie_madd), NEG_BIG)


def _indexer_scores(qi_ref, wi, kia, kib):
    half = IDX_HEADS // 2
    score = None
    for j in range(half):
        qp = qi_ref[:, j * LANES:(j + 1) * LANES]
        da = jnp.maximum(_dot_nt(qp, kia), 0.0) * wi[:, IDX_DIM + j:IDX_DIM + j + 1]
        db = jnp.maximum(_dot_nt(qp, kib), 0.0) * wi[:, IDX_DIM + half + j:IDX_DIM + half + j + 1]
        score = da + db if score is None else score + da + db
    return score


PROMPT_TQ = 256


def _chunk_madd(row0, tq, n_keys):
    qc = (row0 + lax.broadcasted_iota(jnp.int32, (tq, n_keys), 0)) // CHUNK
    kc = lax.broadcasted_iota(jnp.int32, (tq, n_keys), 1) // CHUNK
    return kc <= qc


def _chained_tile_calls(n_tiles, make_call, out):
    for c in range(n_tiles):
        out = make_call(c, out)
    return out


def _mla_prompt_kernel(n_heads, row0, q_ref, k_ref, v_ref, prev_ref, o_ref):
    del prev_ref
    tq, n_keys = q_ref.shape[0], k_ref.shape[0]
    madd = jnp.where(_chunk_madd(row0, tq, n_keys), 0.0, NEG_BIG)
    for h in range(n_heads):
        c0 = h * HEAD_SLOT
        s = _dot_nt(q_ref[:, c0:c0 + HEAD_SLOT], k_ref[:, c0:c0 + HEAD_SLOT]) + madd
        o_ref[:, h * MLA_V:(h + 1) * MLA_V] = _mx(_softmax_pv(s, v_ref[:, h * MLA_V:(h + 1) * MLA_V]))


def _mla_prompt(qcat, kcat, vup, out_init, b, t):
    tq = PROMPT_TQ
    n_heads = qcat.shape[1] // HEAD_SLOT
    nq = t // tq
    wa = n_heads * MLA_V
    k3 = kcat.reshape(b, t, kcat.shape[1])
    v3 = vup.reshape(b, t, wa)

    def make_call(c, out):
        n_keys = (c + 1) * tq
        row = lambda bi: (bi * nq + c, 0)
        return pl.pallas_call(
            functools.partial(_mla_prompt_kernel, n_heads, c * tq),
            grid=(b,),
            in_specs=[pl.BlockSpec((tq, qcat.shape[1]), row),
                      pl.BlockSpec((None, n_keys, kcat.shape[1]), lambda bi: (bi, 0, 0)),
                      pl.BlockSpec((None, n_keys, wa), lambda bi: (bi, 0, 0)),
                      pl.BlockSpec(memory_space=pl.ANY)],
            out_specs=pl.BlockSpec((tq, wa), row),
            out_shape=jax.ShapeDtypeStruct(out.shape, out.dtype),
            input_output_aliases={3: 0},
            compiler_params=_cparams(1),
            name="mla_prompt_%d" % c,
        )(qcat, k3, v3, out)

    return _chained_tile_calls(nq, make_call, out_init)


def _toeplitz_bias(tab_ref, h, start, n_rows, n_cols):
    shift0 = LANES - (n_rows - 1)
    from_left = (lax.broadcasted_iota(jnp.int32, (n_rows, LANES), 1)
                 < shift0 + lax.broadcasted_iota(jnp.int32, (n_rows, LANES), 0))
    pieces = []
    prev = None
    for n in range(n_cols // LANES + 1):
        seg = tab_ref[h:h + 1, start + n * LANES:start + (n + 1) * LANES]
        rot = pltpu.roll(jnp.broadcast_to(seg, (n_rows, LANES)), shift0 % LANES, 1, stride=1, stride_axis=0)
        if prev is not None:
            pieces.append(jnp.where(from_left, prev, rot))
        prev = rot
    return jnp.concatenate(pieces, axis=1)


def _dsa_prompt_kernel(n_heads, row0, t, topk, qb_ref, qi_ref, wi_ref, k_ref, v_ref, kiab_ref, tab_ref, prev_ref,
                       o_ref, key_ref, madd_ref):
    del prev_ref
    tq, n_keys = qb_ref.shape[0], k_ref.shape[0]
    score = _indexer_scores(qi_ref, wi_ref[...], kiab_ref[:, :LANES], kiab_ref[:, LANES:])
    _topk_mask(score, _chunk_madd(row0, tq, n_keys), topk, key_ref, madd_ref)
    near0 = max(0, row0 - LANES)
    n_near = n_keys - near0
    for h in range(n_heads):
        c0 = h * DSA_HEAD_DIM
        near = jnp.concatenate(
            [_toeplitz_bias(tab_ref, h, near0 - (row0 + u * LANES) - LANES + t, LANES, n_near)
             for u in range(tq // LANES)], axis=0)
        if near0 > 0:
            far = jnp.broadcast_to(tab_ref[h:h + 1, 0:1], (tq, near0))
            bias = jnp.concatenate([far, near], axis=1)
        else:
            bias = near
        s = _dot_nt(qb_ref[:, c0:c0 + DSA_HEAD_DIM], k_ref[:, c0:c0 + DSA_HEAD_DIM]) + bias + madd_ref[...]
        o_ref[:, c0:c0 + DSA_HEAD_DIM] = _mx(_softmax_pv(s, v_ref[:, c0:c0 + DSA_HEAD_DIM]))


def _dsa_prompt(qb, qi, wi, kbb, vbb, kiab, bias_tab, out_init, b, t, topk):
    tq = PROMPT_TQ
    assert MAX_DISTANCE <= LANES and tq % LANES == 0
    n_heads = qb.shape[1] // DSA_HEAD_DIM
    nq = t // tq
    wb = qb.shape[1]
    k3, v3, ki3 = kbb.reshape(b, t, wb), vbb.reshape(b, t, wb), kiab.reshape(b, t, 2 * LANES)
    per_b = lambda bi: (bi, 0, 0)

    def make_call(c, out):
        n_keys = (c + 1) * tq
        row = lambda bi: (bi * nq + c, 0)
        return pl.pallas_call(
            functools.partial(_dsa_prompt_kernel, n_heads, c * tq, t, topk),
            grid=(b,),
            in_specs=[pl.BlockSpec((tq, wb), row),
                      pl.BlockSpec((tq, qi.shape[1]), row),
                      pl.BlockSpec((tq, LANES), row),
                      pl.BlockSpec((None, n_keys, wb), per_b),
                      pl.BlockSpec((None, n_keys, wb), per_b),
                      pl.BlockSpec((None, n_keys, 2 * LANES), per_b),
                      _resident(bias_tab.shape),
                      pl.BlockSpec(memory_space=pl.ANY)],
            out_specs=pl.BlockSpec((tq, wb), row),
            out_shape=jax.ShapeDtypeStruct(out.shape, out.dtype),
            input_output_aliases={7: 0},
            scratch_shapes=[pltpu.VMEM((tq, n_keys), jnp.float32), pltpu.VMEM((tq, n_keys), jnp.float32)],
            compiler_params=_cparams(1),
            name="dsa_prompt_%d" % c,
        )(qb, qi, wi, k3, v3, ki3, bias_tab, out)

    return _chained_tile_calls(nq, make_call, out_init)


def _mla_sample_kernel(n_heads, past, ts, qcat_ref, cckv_ref, ckr_ref, nckv_ref, nkr_ref, wuk_ref, wuv_ref,
                       o_ref, kall_ref, rall_ref):
    s_pad = kall_ref.shape[0]
    n_keys = past + ts

    @pl.when(pl.program_id(0) == 0)
    def _():
        kall_ref[...] = jnp.zeros_like(kall_ref)
        rall_ref[...] = jnp.zeros_like(rall_ref)

    kall_ref[0:past, :] = _mx(cckv_ref[...])
    kall_ref[past:n_keys, :] = _mx(nckv_ref[...])
    rall_ref[0:past, 0:MLA_ROPE] = _mx(ckr_ref[...])
    rall_ref[past:n_keys, 0:MLA_ROPE] = _mx(nkr_ref[...])
    qlat, qrope = [], []
    for h in range(n_heads):
        c0 = h * HEAD_SLOT
        qlat.append(_mx(_dot(qcat_ref[:, c0:c0 + LANES], wuk_ref[h])))
        qrope.append(qcat_ref[:, c0 + LANES:c0 + HEAD_SLOT])
    qlat = jnp.concatenate(qlat, axis=0)
    qrope = jnp.concatenate(qrope, axis=0)
    s = _dot_nt(qlat, kall_ref[...]) + _dot_nt(qrope, rall_ref[...])
    col = lax.broadcasted_iota(jnp.int32, s.shape, 1)
    s = jnp.where(col < n_keys, s, NEG_BIG)
    olat = _mx(_softmax_pv(s, kall_ref[...]))
    for h in range(n_heads):
        o_ref[:, h * MLA_V:(h + 1) * MLA_V] = _mx(_dot_nt(olat[h * ts:(h + 1) * ts], wuv_ref[h]))


def _mla_sample(qcat, cache_ckv, cache_kr, new_ckv, new_kr, wuk, wuv, ts):
    nb, past, c = cache_ckv.shape
    n_heads = qcat.shape[1] // HEAD_SLOT
    wa = n_heads * MLA_V
    s_pad = pl.cdiv(past + ts, LANES) * LANES
    row = lambda bi: (bi, 0)
    return pl.pallas_call(
        functools.partial(_mla_sample_kernel, n_heads, past, ts),
        grid=(nb,),
        in_specs=[pl.BlockSpec((ts, qcat.shape[1]), row),
                  pl.BlockSpec((None, past, c), lambda bi: (bi, 0, 0)),
                  pl.BlockSpec((None, past, MLA_ROPE), lambda bi: (bi, 0, 0)),
                  pl.BlockSpec((ts, c), row),
                  pl.BlockSpec((ts, MLA_ROPE), row),
                  _resident(wuk.shape), _resident(wuv.shape)],
        out_specs=pl.BlockSpec((ts, wa), row),
        out_shape=jax.ShapeDtypeStruct((nb * ts, wa), MXU_DT),
        scratch_shapes=[pltpu.VMEM((s_pad, c), MXU_DT), pltpu.VMEM((s_pad, LANES), MXU_DT)],
        compiler_params=_cparams(1),
        name="mla_sample",
    )(qcat, cache_ckv, cache_kr, new_ckv, new_kr, wuk, wuv)


def _dsa_select_kernel(past, ts, topk, qi_ref, wi_ref, cki_ref, nkiab_ref, madd_ref, kia_ref, kib_ref, key_ref):
    bi = pl.program_id(0)
    n_keys = past + ts
    rows, s_pad = madd_ref.shape

    @pl.when(bi == 0)
    def _():
        kia_ref[...] = jnp.zeros_like(kia_ref)
        kib_ref[...] = jnp.zeros_like(kib_ref)

    cki = _mx(cki_ref[...])
    kia_ref[0:past, 0:IDX_DIM] = cki
    kib_ref[0:past, IDX_DIM:LANES] = cki
    kia_ref[past:n_keys, :] = nkiab_ref[:, :LANES]
    kib_ref[past:n_keys, :] = nkiab_ref[:, LANES:]
    madd_ref[pl.ds(pl.multiple_of(bi * ts, ts), ts), :] = _indexer_scores(
        qi_ref, wi_ref[...], kia_ref[...], kib_ref[...])

    @pl.when(bi == pl.num_programs(0) - 1)
    def _():
        col = lax.broadcasted_iota(jnp.int32, (rows, s_pad), 1)
        _topk_mask(madd_ref[...], col < n_keys, topk, key_ref, madd_ref)


def _dsa_select(qi, wi, cache_ki, new_kiab, ts, topk):
    nb, past, _ = cache_ki.shape
    s_pad = pl.cdiv(past + ts, LANES) * LANES
    row = lambda bi: (bi, 0)
    return pl.pallas_call(
        functools.partial(_dsa_select_kernel, past, ts, topk),
        grid=(nb,),
        in_specs=[pl.BlockSpec((ts, qi.shape[1]), row),
                  pl.BlockSpec((ts, LANES), row),
                  pl.BlockSpec((None, past, IDX_DIM), lambda bi: (bi, 0, 0)),
                  pl.BlockSpec((ts, 2 * LANES), row)],
        out_specs=pl.BlockSpec((nb * ts, s_pad), lambda bi: (0, 0)),
        out_shape=jax.ShapeDtypeStruct((nb * ts, s_pad), jnp.float32),
        scratch_shapes=[pltpu.VMEM((s_pad, LANES), MXU_DT), pltpu.VMEM((s_pad, LANES), MXU_DT),
                        pltpu.VMEM((nb * ts, s_pad), jnp.float32)],
        compiler_params=_cparams(1),
        name="dsa_select",
    )(qi, wi, cache_ki, new_kiab)


def _dsa_sample_kernel(n_heads, past, ts, qb_ref, madd_ref, ck_ref, cv_ref, nk_ref, nv_ref, tab_ref, expand_ref,
                       o_ref, kflat_ref, vflat_ref, bias_ref, biasw_ref):
    n_keys = past + ts
    wide = n_heads * LANES
    n_blocks = pl.cdiv(n_keys, LANES)
    widths = [min(wide, (n_keys - j * LANES) * n_heads) for j in range(n_blocks)]

    @pl.when(pl.program_id(0) == 0)
    def _():
        for h in range(n_heads):
            bias_ref[h * ts:(h + 1) * ts, :] = _toeplitz_bias(tab_ref, h, 0, ts, n_blocks * LANES)
        shape = (n_heads * ts, wide)
        same_head = (lax.broadcasted_iota(jnp.int32, shape, 0) // ts
                     == lax.broadcasted_iota(jnp.int32, shape, 1) % n_heads)
        for j in range(n_blocks):
            b = bias_ref[:, j * LANES:(j + 1) * LANES]
            hi = _mx(b)
            rest = b - hi.astype(jnp.float32)
            mid = _mx(rest)
            lo = _mx(rest - mid.astype(jnp.float32))
            piece = _dot(hi, expand_ref[...]) + _dot(mid, expand_ref[...]) + _dot(lo, expand_ref[...])
            biasw_ref[:, j * wide:j * wide + widths[j]] = jnp.where(same_head, piece, NEG_BIG)[:, :widths[j]]

    kflat_ref[0:past * n_heads, :] = _mx(ck_ref[...])
    kflat_ref[past * n_heads:, :] = nk_ref[...]
    vflat_ref[0:past * n_heads, :] = _mx(cv_ref[...])
    vflat_ref[past * n_heads:, :] = nv_ref[...]
    sel = _mx(jnp.where(madd_ref[...] == 0.0, 1.0, 0.0))
    pieces = [_dot(sel[:, j * LANES:(j + 1) * LANES], expand_ref[...])[:, :widths[j]] for j in range(n_blocks)]
    sel_wide = jnp.concatenate(pieces, axis=1)
    sel_wide = jnp.concatenate([sel_wide] * n_heads, axis=0)
    q_all = jnp.concatenate([qb_ref[:, h * DSA_HEAD_DIM:(h + 1) * DSA_HEAD_DIM] for h in range(n_heads)], axis=0)
    s = _dot_nt(q_all, kflat_ref[...]) + biasw_ref[...] + jnp.where(sel_wide > 0.5, 0.0, NEG_BIG)
    o = _mx(_softmax_pv(s, vflat_ref[...]))
    for h in range(n_heads):
        o_ref[:, h * DSA_HEAD_DIM:(h + 1) * DSA_HEAD_DIM] = o[h * ts:(h + 1) * ts]


def _dsa_sample(qb, madd, cache_k, cache_v, new_k, new_v, bias_tab, ts):
    nb, past, n_heads, _ = cache_k.shape
    wb = n_heads * DSA_HEAD_DIM
    n_keys = past + ts
    assert (n_keys % LANES * n_heads) % LANES == 0
    expand = _mx(jnp.repeat(jnp.eye(LANES, dtype=jnp.float32), n_heads, axis=1))
    row = lambda bi: (bi, 0)
    per_b = lambda bi: (bi, 0, 0)
    return pl.pallas_call(
        functools.partial(_dsa_sample_kernel, n_heads, past, ts),
        grid=(nb,),
        in_specs=[pl.BlockSpec((ts, wb), row),
                  pl.BlockSpec((ts, madd.shape[1]), row),
                  pl.BlockSpec((None, past * n_heads, DSA_HEAD_DIM), per_b),
                  pl.BlockSpec((None, past * n_heads, DSA_HEAD_DIM), per_b),
                  pl.BlockSpec((ts * n_heads, DSA_HEAD_DIM), row),
                  pl.BlockSpec((ts * n_heads, DSA_HEAD_DIM), row),
                  _resident(bias_tab.shape), _resident(expand.shape)],
        out_specs=pl.BlockSpec((ts, wb), row),
        out_shape=jax.ShapeDtypeStruct((nb * ts, wb), MXU_DT),
        scratch_shapes=[pltpu.VMEM((n_keys * n_heads, DSA_HEAD_DIM), MXU_DT),
                        pltpu.VMEM((n_keys * n_heads, DSA_HEAD_DIM), MXU_DT),
                        pltpu.VMEM((n_heads * ts, pl.cdiv(n_keys, LANES) * LANES), jnp.float32),
                        pltpu.VMEM((n_heads * ts, n_keys * n_heads), jnp.float32)],
        compiler_params=_cparams(1),
        name="dsa_sample",
    )(qb, madd, cache_k.reshape(nb, past * n_heads, DSA_HEAD_DIM), cache_v.reshape(nb, past * n_heads, DSA_HEAD_DIM),
      new_k.reshape(nb * ts * n_heads, DSA_HEAD_DIM), new_v.reshape(nb * ts * n_heads, DSA_HEAD_DIM),
      bias_tab, expand)


def _out_kernel(oa_ref, ob_ref, ga_ref, gb_ref, x_ref, gate_ref, gna_ref, gnb_ref, w_ref, fg_ref, y_ref):
    def gated(o_ref, g_ref, gain_ref):
        o = o_ref[...].astype(jnp.float32)
        on = o * lax.rsqrt(jnp.mean(o * o, axis=-1, keepdims=True) + EPS) * gain_ref[...]
        return _mx(on * g_ref[...].astype(jnp.float32))

    wa = oa_ref.shape[1]
    out = _dot(gated(oa_ref, ga_ref, gna_ref), w_ref[0:wa, :]) + _dot(gated(ob_ref, gb_ref, gnb_ref), w_ref[wa:, :])
    xn = x_ref[...] + gate_ref[...] * out
    y_ref[...] = xn * lax.rsqrt(jnp.mean(xn * xn, axis=-1, keepdims=True) + EPS) * fg_ref[...]


def _out_proj(oa, ob, ga, gb, x3, gate, gain_a, gain_b, w_out, final_gain, tm):
    bv, tv, d = x3.shape
    nt = tv // tm
    wa = oa.shape[1]
    x_spec, g_spec = _row_specs(x3, gate.shape[1], tm)
    row = lambda b, i: (b * nt + i, 0)
    return pl.pallas_call(
        _out_kernel,
        grid=(bv, nt),
        in_specs=[pl.BlockSpec((tm, wa), row), pl.BlockSpec((tm, wa), row),
                  pl.BlockSpec((tm, wa), row), pl.BlockSpec((tm, wa), row),
                  x_spec, g_spec, _resident((1, wa)), _resident((1, wa)),
                  _resident(w_out.shape), _resident((1, d))],
        out_specs=pl.BlockSpec((None, tm, d), lambda b, i: (b, i, 0)),
        out_shape=jax.ShapeDtypeStruct((bv, tv, d), jnp.float32),
        compiler_params=_cparams(2),
        name="out_proj",
    )(oa, ob, ga, gb, x3, gate, gain_a.reshape(1, wa), gain_b.reshape(1, wa), w_out, final_gain.reshape(1, d))


def _pack_kernel(n_heads, wt_ref, qa_ref, ckv_ref, kr_ref, ga_ref, gb_ref, qb_ref, kb_ref, vb_ref, qi_ref, kiw_ref):
    wa = n_heads * MLA_V
    half = MLA_ROPE // 2
    q_head = MLA_NOPE + MLA_ROPE
    step = 2 * LANES

    def panel(ref, off):
        for c in range(0, ref.shape[1], step):
            w = min(step, ref.shape[1] - c)
            ref[:, c:c + w] = _mx(wt_ref[off + c:off + c + w, :].T)

    def dup_rope(off):
        x1, x2 = wt_ref[off:off + half, :], wt_ref[off + half:off + 2 * half, :]
        return _mx(jnp.concatenate([x1, x2, x2, x1], axis=0).T)

    for h in range(n_heads):
        src, dst = h * q_head, h * HEAD_SLOT
        qa_ref[:, dst:dst + LANES] = _mx(wt_ref[src:src + MLA_NOPE, :].T)
        qa_ref[:, dst + LANES:dst + HEAD_SLOT] = dup_rope(src + MLA_NOPE)
    off = n_heads * q_head
    panel(ckv_ref, off)
    off += KV_LORA
    kr_ref[...] = dup_rope(off)
    off += MLA_ROPE
    for ref in (ga_ref, qb_ref, kb_ref, vb_ref):
        panel(ref, off)
        off += wa
    pairs = IDX_HEADS // 2
    for j in range(pairs):
        lo = wt_ref[off + IDX_DIM * j:off + IDX_DIM * (j + 1), :]
        hi = wt_ref[off + IDX_DIM * (j + pairs):off + IDX_DIM * (j + pairs + 1), :]
        qi_ref[:, LANES * j:LANES * (j + 1)] = _mx(jnp.concatenate([lo, hi], axis=0).T)
    off += IDX_HEADS * IDX_DIM
    n_kw = IDX_DIM + IDX_HEADS
    kiw = jnp.concatenate([wt_ref[off:off + n_kw, :], jnp.zeros((LANES - n_kw, wt_ref.shape[1]), jnp.float32)], axis=0)
    kiw_ref[...] = _mx(kiw.T)
    off += n_kw
    panel(gb_ref, off)


def _pack_w_in(w_in, n_heads):
    wt = w_in.T
    n, d = wt.shape
    wa = n_heads * MLA_V
    assert n == n_heads * (MLA_NOPE + MLA_ROPE) + KV_LORA + MLA_ROPE + 5 * wa + IDX_HEADS * IDX_DIM + IDX_DIM + IDX_HEADS
    assert 2 * IDX_DIM == LANES and 2 * MLA_ROPE == LANES and MLA_NOPE == LANES
    slab = 256
    widths = [n_heads * HEAD_SLOT, KV_LORA, LANES, wa, wa, wa, wa, wa, IDX_HEADS * IDX_DIM, LANES]
    q_a, ckv, kr, g_a, g_b, q_b, k_b, v_b, q_i, kiw = pl.pallas_call(
        functools.partial(_pack_kernel, n_heads),
        grid=(d // slab,),
        in_specs=[pl.BlockSpec((n, slab), lambda i: (0, i))],
        out_specs=[pl.BlockSpec((slab, w), lambda i: (i, 0)) for w in widths],
        out_shape=[jax.ShapeDtypeStruct((d, w), MXU_DT) for w in widths],
        compiler_params=_cparams(1),
        name="pack_w_in",
    )(wt)
    return [q_a, ckv, kr, g_a, g_b], [q_b, k_b, v_b, q_i, kiw]


def _rope_table(pos):
    half = MLA_ROPE // 2
    freqs = jnp.power(ROPE_THETA, -jnp.arange(half, dtype=jnp.float32) / half)
    ang = pos.astype(jnp.float32)[:, None] * freqs
    cos, sin = jnp.cos(ang), jnp.sin(ang)
    return jnp.concatenate([cos, cos, -sin, sin], axis=1)


def _rel_bucket(rel):
    nb = N_BUCKETS // 2
    max_exact = nb // 2
    n = jnp.abs(rel)
    nf = jnp.maximum(n, 1).astype(jnp.float32)
    large = max_exact + (jnp.log(nf / max_exact) / math.log(MAX_DISTANCE / max_exact)
                         * (nb - max_exact)).astype(jnp.int32)
    large = jnp.minimum(large, nb - 1)
    return jnp.where(rel > 0, nb, 0) + jnp.where(n < max_exact, n, large)


def _bucket_bias(rel_bias, rel):
    return (rel_bias * LOG2E)[_rel_bucket(rel)]


def kernel(x_prompt, x_sample, cache_mla_ckv, cache_mla_krope, cache_dsa_k, cache_dsa_v, cache_idx_k,
           c_prompt, c_sample, w_ada, b_ada, ln_gain, w_in, mla_kv_gain, w_uk, w_uv, rel_bias,
           out_gain_a, out_gain_b, w_out, final_gain):
    assert w_ada.shape[0] == 1, "single-layer step"
    b, t, d = x_prompt.shape
    nb, ts, _ = x_sample.shape
    past = cache_mla_ckv.shape[2]
    n_heads = d // 256
    wb = n_heads * DSA_HEAD_DIM
    s_all = past + ts
    assert t % PROMPT_TQ == 0 and PROMPT_TQ % CHUNK == 0
    assert (s_all - 1) // CHUNK <= past // CHUNK

    mod = _ada_mod(jnp.concatenate([c_prompt, c_sample], axis=0), w_ada[0], b_ada[0])
    shift, scale, gate = mod[:, :d], mod[:, d:2 * d], mod[:, 2 * d:]
    mod_p = [v[:b].reshape(b, 1, d) for v in (scale, shift, gate)]
    mod_s = [jnp.broadcast_to(v[b:, None, :], (nb, ts, d)).reshape(1, nb * ts, d) for v in (scale, shift, gate)]

    w_a, w_b = _pack_w_in(w_in[0], n_heads)
    wkv = _mx(jnp.concatenate([w_uk[0].transpose(2, 0, 1).reshape(KV_LORA, n_heads * MLA_NOPE),
                               w_uv[0].transpose(2, 0, 1).reshape(KV_LORA, n_heads * MLA_V)], axis=1))
    w_o = _mx(w_out[0])
    pos_p = jnp.arange(t, dtype=jnp.int32)
    pos_s = past + jnp.arange(ts, dtype=jnp.int32)
    tab_p = _rope_table(pos_p)
    tab_s = jnp.tile(_rope_table(pos_s), (nb, 1))

    tm = 256
    qcat, ckv_p, krope_p, ga, gb, kcat, vup, oa_buf = _proj_a(
        x_prompt, mod_p[0], mod_p[1], ln_gain[0], tab_p, w_a, mla_kv_gain[0], wkv, tm)
    qb, k_p, v_p, kbb, vbb, qi, idxk_p, kiab, wi, ob_buf = _proj_b(
        x_prompt, mod_p[0], mod_p[1], ln_gain[0], w_b, tm, True)
    o_a = _mla_prompt(qcat, kcat, vup, oa_buf, b, t)
    near = _bucket_bias(rel_bias, jnp.arange(-MAX_DISTANCE - 1, MAX_DISTANCE, dtype=jnp.int32))
    far = near[:1]
    bias_tab = jnp.concatenate([jnp.broadcast_to(far, (t - 1 - MAX_DISTANCE, n_heads)), near[1:],
                                jnp.broadcast_to(far, (t + 1 - MAX_DISTANCE, n_heads))], axis=0).T
    o_b = _dsa_prompt(qb, qi, wi, kbb, vbb, kiab, bias_tab, ob_buf, b, t, min(TOPK_MAX, t // 4))
    y_prompt = _out_proj(o_a, o_b, ga, gb, x_prompt, mod_p[2], out_gain_a[0], out_gain_b[0], w_o, final_gain, tm)

    xs3 = x_sample.reshape(1, nb * ts, d)
    tms = min(256, nb * ts)
    qcat_s, ckv_s, krope_s, ga_s, gb_s = _proj_a(
        xs3, mod_s[0], mod_s[1], ln_gain[0], tab_s, w_a, mla_kv_gain[0], None, tms)
    qb_s, k_s, v_s, kbb_s, vbb_s, qi_s, idxk_s, kiab_s, wi_s = _proj_b(
        xs3, mod_s[0], mod_s[1], ln_gain[0], w_b, tms, False)
    oa_s = _mla_sample(qcat_s, cache_mla_ckv[0], cache_mla_krope[0], ckv_s, krope_s,
                       _mx(w_uk[0]), _mx(w_uv[0]), ts)
    s_pad = pl.cdiv(s_all, LANES) * LANES
    tab_s = _bucket_bias(rel_bias, jnp.arange(s_pad + LANES, dtype=jnp.int32) - (s_all - 1)).T
    madd_s = _dsa_select(qi_s, wi_s, cache_idx_k[0], kiab_s, ts, min(TOPK_MAX, s_all // 4))
    ob_s = _dsa_sample(qb_s, madd_s, cache_dsa_k[0], cache_dsa_v[0], kbb_s, vbb_s, tab_s, ts)
    y_sample = _out_proj(oa_s, ob_s, ga_s, gb_s, xs3, mod_s[2], out_gain_a[0], out_gain_b[0], w_o, final_gain, tms)

    hd = (n_heads, DSA_HEAD_DIM)
    return (y_prompt, y_sample.reshape(nb, ts, d),
            ckv_p.reshape(1, b, t, KV_LORA), krope_p.reshape(1, b, t, MLA_ROPE),
            k_p.reshape(1, b, t, *hd), v_p.reshape(1, b, t, *hd), idxk_p.reshape(1, b, t, IDX_DIM),
            ckv_s.reshape(1, nb, ts, KV_LORA), krope_s.reshape(1, nb, ts, MLA_ROPE),
            k_s.reshape(1, nb, ts, *hd), v_s.reshape(1, nb, ts, *hd), idxk_s.reshape(1, nb, ts, IDX_DIM))
```

```python
import functools
import math

import jax
import jax.numpy as jnp
from jax import lax
from jax.experimental import pallas as pl
from jax.experimental.pallas import tpu as pltpu

MXU_DT = jnp.bfloat16

CHUNK = 64
MLA_NOPE = 128
MLA_ROPE = 64
MLA_V = 128
KV_LORA = 512
DSA_HEAD_DIM = 128
IDX_HEADS = 16
IDX_DIM = 64
TOPK_MAX = 256
N_BUCKETS = 32
MAX_DISTANCE = 128
ROPE_THETA = 10000.0
EPS = 1e-6

LANES = 128
HEAD_SLOT = 256
NEG_BIG = -1e30
LOG2E = math.log2(math.e)
INT_MIN = -2 ** 31
KEY_NEG_INF = INT_MIN + 0x7FFFFF
VMEM_LIMIT = 56 * 1024 * 1024


def _cparams(n_grid, vmem=VMEM_LIMIT):
    return pltpu.CompilerParams(dimension_semantics=("arbitrary",) * n_grid, vmem_limit_bytes=vmem)


def _mx(v):
    return v.astype(MXU_DT)


def _dot(a, b):
    return jnp.dot(a, b, preferred_element_type=jnp.float32)


def _dot_nt(a, b):
    return lax.dot_general(a, b, (((1,), (1,)), ((), ())), preferred_element_type=jnp.float32)


def _silu(v):
    return v * (1.0 / (1.0 + jnp.exp(-v)))


def _resident(shape):
    nd = len(shape)
    return pl.BlockSpec(shape, lambda *_: (0,) * nd, pipeline_mode=pl.Buffered(1))


def _ada_kernel(c_ref, w_ref, b_ref, o_ref):
    a = _mx(_silu(c_ref[...]))
    o_ref[...] = _dot(a, _mx(w_ref[...])) + b_ref[...]


def _ada_mod(c_all, w_ada, b_ada):
    m, d = c_all.shape
    n = w_ada.shape[1]
    tn = 1024
    return pl.pallas_call(
        _ada_kernel,
        grid=(n // tn,),
        in_specs=[pl.BlockSpec((m, d), lambda j: (0, 0)),
                  pl.BlockSpec((d, tn), lambda j: (0, j)),
                  pl.BlockSpec((1, tn), lambda j: (0, j))],
        out_specs=pl.BlockSpec((m, tn), lambda j: (0, j)),
        out_shape=jax.ShapeDtypeStruct((m, n), jnp.float32),
        compiler_params=_cparams(1),
        name="ada_mod",
    )(c_all, w_ada, b_ada.reshape(1, n))


def _modulated_norm(x_ref, scale_ref, shift_ref, lng_ref):
    x = x_ref[...]
    xn = x * lax.rsqrt(jnp.mean(x * x, axis=-1, keepdims=True) + EPS) * lng_ref[...]
    return _mx(xn * (1.0 + scale_ref[...]) + shift_ref[...])


def _rope128(a, tab):
    t = a * tab
    return t + pltpu.roll(t, 64, 1)


def _proj_a_kernel(n_heads, emit_kv, x_ref, scale_ref, shift_ref, lng_ref, tab_ref,
                   wq_ref, wc_ref, wr_ref, wga_ref, wgb_ref, kvg_ref, *rest):
    if emit_kv:
        wkv_ref, qcat_ref, ckv_ref, krope_ref, ga_ref, gb_ref, kcat_ref, vup_ref, attn_ref = rest
        attn_ref[...] = jnp.zeros_like(attn_ref)
    else:
        qcat_ref, ckv_ref, krope_ref, ga_ref, gb_ref = rest
    hb = _modulated_norm(x_ref, scale_ref, shift_ref, lng_ref)
    tab = tab_ref[...]
    qscale = (MLA_NOPE + MLA_ROPE) ** -0.5 * LOG2E
    for h in range(n_heads):
        c0 = h * HEAD_SLOT
        a = _dot(hb, wq_ref[:, c0:c0 + HEAD_SLOT])
        qcat_ref[:, c0:c0 + LANES] = _mx(a[:, :LANES] * qscale)
        qcat_ref[:, c0 + LANES:c0 + HEAD_SLOT] = _mx(_rope128(a[:, LANES:], tab) * qscale)
    c = _dot(hb, wc_ref[...])
    cn = c * lax.rsqrt(jnp.mean(c * c, axis=-1, keepdims=True) + EPS) * kvg_ref[...]
    ckv_ref[...] = cn
    r = _rope128(_dot(hb, wr_ref[...]), tab)
    krope_ref[...] = r[:, :MLA_ROPE]
    ga_ref[...] = _mx(_silu(_dot(hb, wga_ref[...])))
    gb_ref[...] = _mx(_silu(_dot(hb, wgb_ref[...])))
    if emit_kv:
        cb = _mx(cn)
        lane = lax.broadcasted_iota(jnp.int32, r.shape, 1)
        krz = _mx(jnp.where(lane < MLA_ROPE, r, 0.0))
        kn = _dot(cb, wkv_ref[:, :n_heads * MLA_NOPE])
        for h in range(n_heads):
            c0 = h * HEAD_SLOT
            kcat_ref[:, c0:c0 + LANES] = _mx(kn[:, h * MLA_NOPE:(h + 1) * MLA_NOPE])
            kcat_ref[:, c0 + LANES:c0 + HEAD_SLOT] = krz
        vup_ref[...] = _mx(_dot(cb, wkv_ref[:, n_heads * MLA_NOPE:]))


def _proj_b_kernel(n_heads, x_ref, scale_ref, shift_ref, lng_ref, wq_ref, wk_ref, wv_ref, wqi_ref, wkw_ref,
                   qb_ref, kb_ref, vb_ref, kbb_ref, vbb_ref, qi_ref, ki_ref, kiab_ref, wi_ref, *attn_ref):
    if attn_ref:
        attn_ref[0][...] = jnp.zeros_like(attn_ref[0])
    hb = _modulated_norm(x_ref, scale_ref, shift_ref, lng_ref)
    dscale = DSA_HEAD_DIM ** -0.5 * LOG2E
    step = 512
    for c0 in range(0, wq_ref.shape[1], step):
        qb_ref[:, c0:c0 + step] = _mx(_dot(hb, wq_ref[:, c0:c0 + step]) * dscale)
    for w_ref, f_ref, b_ref in ((wk_ref, kb_ref, kbb_ref), (wv_ref, vb_ref, vbb_ref)):
        for c0 in range(0, w_ref.shape[1], step):
            kv = _dot(hb, w_ref[:, c0:c0 + step])
            f_ref[:, c0:c0 + step] = kv
            b_ref[:, c0:c0 + step] = _mx(kv)
    for c0 in range(0, wqi_ref.shape[1], step):
        qi_ref[:, c0:c0 + step] = _mx(_dot(hb, wqi_ref[:, c0:c0 + step]))
    a = _dot(hb, wkw_ref[...])
    ki_ref[...] = a[:, :IDX_DIM]
    lane = lax.broadcasted_iota(jnp.int32, a.shape, 1)
    kz = jnp.where(lane < IDX_DIM, a, 0.0)
    kiab_ref[:, :LANES] = _mx(kz)
    kiab_ref[:, LANES:] = _mx(pltpu.roll(kz, IDX_DIM, 1))
    wi_ref[...] = a * (IDX_HEADS ** -0.5 * IDX_DIM ** -0.5)


def _row_specs(x3, mod_rows, tm):
    bv, tv, d = x3.shape
    x_spec = pl.BlockSpec((None, tm, d), lambda b, i: (b, i, 0))
    if mod_rows == 1:
        m_spec = pl.BlockSpec((None, 1, d), lambda b, i: (b, 0, 0))
    else:
        m_spec = pl.BlockSpec((None, tm, d), lambda b, i: (b, i, 0))
    return x_spec, m_spec


def _out2d(m, width, dtype, tm, nt):
    return (jax.ShapeDtypeStruct((m, width), dtype),
            pl.BlockSpec((tm, width), lambda b, i: (b * nt + i, 0)))


def _proj_a(x3, scale, shift, ln_gain, tab, ws_a, kv_gain, wkv, tm):
    bv, tv, d = x3.shape
    nt = tv // tm
    m = bv * tv
    n_heads = d // 256
    wa = n_heads * MLA_V
    emit_kv = wkv is not None
    x_spec, m_spec = _row_specs(x3, scale.shape[1], tm)
    in_specs = ([x_spec, m_spec, m_spec, _resident((1, d)), pl.BlockSpec((tm, LANES), lambda b, i: (i, 0))]
                + [_resident(w.shape) for w in ws_a] + [_resident((1, KV_LORA))])
    args = [x3, scale, shift, ln_gain.reshape(1, d), tab, *ws_a, kv_gain.reshape(1, KV_LORA)]
    outs = [_out2d(m, n_heads * HEAD_SLOT, MXU_DT, tm, nt),
            _out2d(m, KV_LORA, jnp.float32, tm, nt),
            _out2d(m, MLA_ROPE, jnp.float32, tm, nt),
            _out2d(m, wa, MXU_DT, tm, nt),
            _out2d(m, wa, MXU_DT, tm, nt)]
    if emit_kv:
        in_specs.append(_resident(wkv.shape))
        args.append(wkv)
        outs += [_out2d(m, n_heads * HEAD_SLOT, MXU_DT, tm, nt),
                 _out2d(m, wa, MXU_DT, tm, nt),
                 _out2d(m, wa, MXU_DT, tm, nt)]
    return pl.pallas_call(
        functools.partial(_proj_a_kernel, n_heads, emit_kv),
        grid=(bv, nt),
        in_specs=in_specs,
        out_specs=[o[1] for o in outs],
        out_shape=[o[0] for o in outs],
        compiler_params=_cparams(2),
        name="proj_a",
    )(*args)


def _proj_b(x3, scale, shift, ln_gain, ws_b, tm, emit_attn_buffer):
    bv, tv, d = x3.shape
    nt = tv // tm
    m = bv * tv
    n_heads = d // 256
    width_b = n_heads * DSA_HEAD_DIM
    x_spec, m_spec = _row_specs(x3, scale.shape[1], tm)
    outs = [_out2d(m, width_b, MXU_DT, tm, nt),
            _out2d(m, width_b, jnp.float32, tm, nt),
            _out2d(m, width_b, jnp.float32, tm, nt),
            _out2d(m, width_b, MXU_DT, tm, nt),
            _out2d(m, width_b, MXU_DT, tm, nt),
            _out2d(m, IDX_HEADS * IDX_DIM, MXU_DT, tm, nt),
            _out2d(m, IDX_DIM, jnp.float32, tm, nt),
            _out2d(m, 2 * LANES, MXU_DT, tm, nt),
            _out2d(m, LANES, jnp.float32, tm, nt)]
    if emit_attn_buffer:
        outs.append(_out2d(m, width_b, MXU_DT, tm, nt))
    return pl.pallas_call(
        functools.partial(_proj_b_kernel, n_heads),
        grid=(bv, nt),
        in_specs=[x_spec, m_spec, m_spec, _resident((1, d))] + [_resident(w.shape) for w in ws_b],
        out_specs=[o[1] for o in outs],
        out_shape=[o[0] for o in outs],
        compiler_params=_cparams(2),
        name="proj_b",
    )(x3, scale, shift, ln_gain.reshape(1, d), *ws_b)


def _softmax_pv(s, v):
    m = jnp.max(s, axis=-1, keepdims=True)
    p = jnp.exp2(s - m)
    l = jnp.sum(p, axis=-1, keepdims=True)
    return _dot(_mx(p), v) * (1.0 / l)


def _topk_mask(score, adm, topk, sc_ref, madd_ref):
    rows, n = score.shape
    if n <= topk:
        madd_ref[...] = jnp.where(adm, 0.0, NEG_BIG)
        return
    sc_ref[...] = jnp.where(adm, score, -jnp.inf)
    kf = float(topk)

    def count(pred):
        return jnp.sum(jnp.where(pred, 1.0, 0.0), axis=-1, keepdims=True)

    def key_to_float(key):
        return pltpu.bitcast(key ^ ((key >> 31) & 0x7FFFFFFF), jnp.float32)

    def bit_step(i, thr):
        inc = lax.shift_left(jnp.int32(1), 31 - i)
        cand = thr + inc
        ok = count(sc_ref[...] >= key_to_float(cand)) >= kf
        return jnp.where(ok, cand, thr)

    thr = lax.fori_loop(0, 32, bit_step, jnp.full((rows, 1), INT_MIN, jnp.int32))
    thr = key_to_float(jnp.maximum(thr, KEY_NEG_INF))
    ge = sc_ref[...] >= thr
    madd_ref[...] = jnp.where(ge, 0.0, NEG_BIG)

    @pl.when(jnp.max(count(ge)) > kf)
    def _():
        sc = sc_ref[...]
        gt = sc > thr
        need = kf - count(gt)
        eqf = jnp.where(sc == thr, 1.0, 0.0)
        col = lax.broadcasted_iota(jnp.int32, (rows, n), 1)
        nbits = max(1, int(n).bit_length())

        def col_step(i, bound):
            cand = bound + lax.shift_left(jnp.int32(1), nbits - 1 - i)
            taken = jnp.sum(jnp.where(col < cand, eqf, 0.0), axis=-1, keepdims=True)
            return jnp.where(taken <= need, cand, bound)

        bound = lax.fori_loop(0, nbits, col_step, jnp.zeros((rows, 1), jnp.int32))
        tie_madd = jnp.where(jnp.where(col < bound, eqf, 0.0) > 0.5, 0.0, NEG_BIG)
        madd_ref[...] = jnp.where(adm, jnp.where(gt, 0.0, tie_madd), NEG_BIG)


def _indexer_scores(qi_ref, wi, kia, kib):
    half = IDX_HEADS // 2
    score = None
    for j in range(half):
        qp = qi_ref[:, j * LANES:(j + 1) * LANES]
        da = jnp.maximum(_dot_nt(qp, kia), 0.0) * wi[:, IDX_DIM + j:IDX_DIM + j + 1]
        db = jnp.maximum(_dot_nt(qp, kib), 0.0) * wi[:, IDX_DIM + half + j:IDX_DIM + half + j + 1]
        score = da + db if score is None else score + da + db
    return score


def _indexer_scores_stacked(qi_ref, wi, kia, kib):
    half = IDX_HEADS // 2
    ts = qi_ref.shape[0]
    q_all = jnp.concatenate([qi_ref[:, j * LANES:(j + 1) * LANES] for j in range(half)], axis=0)
    wa = jnp.concatenate([wi[:, IDX_DIM + j:IDX_DIM + j + 1] for j in range(half)], axis=0)
    wb = jnp.concatenate([wi[:, IDX_DIM + half + j:IDX_DIM + half + j + 1] for j in range(half)], axis=0)
    part = jnp.maximum(_dot_nt(q_all, kia), 0.0) * wa + jnp.maximum(_dot_nt(q_all, kib), 0.0) * wb
    score = part[0:ts]
    for j in range(1, half):
        score = score + part[j * ts:(j + 1) * ts]
    return score


MLA_TQ = 512
DSA_TQ = 256


def _chunk_madd(row0, tq, n_keys):
    qc = (row0 + lax.broadcasted_iota(jnp.int32, (tq, n_keys), 0)) // CHUNK
    kc = lax.broadcasted_iota(jnp.int32, (tq, n_keys), 1) // CHUNK
    return kc <= qc


def _chained_tile_calls(n_tiles, make_call, out):
    for c in range(n_tiles):
        out = make_call(c, out)
    return out


def _mla_prompt_kernel(n_heads, row0, q_ref, k_ref, v_ref, prev_ref, o_ref):
    del prev_ref
    tq, n_keys = q_ref.shape[0], k_ref.shape[0]
    madd = jnp.where(_chunk_madd(row0, tq, n_keys), 0.0, NEG_BIG)
    for h in range(n_heads):
        c0 = h * HEAD_SLOT
        s = _dot_nt(q_ref[:, c0:c0 + HEAD_SLOT], k_ref[:, c0:c0 + HEAD_SLOT]) + madd
        o_ref[:, h * MLA_V:(h + 1) * MLA_V] = _mx(_softmax_pv(s, v_ref[:, h * MLA_V:(h + 1) * MLA_V]))


def _mla_prompt(qcat, kcat, vup, out_init, b, t):
    tq = MLA_TQ
    n_heads = qcat.shape[1] // HEAD_SLOT
    nq = t // tq
    wa = n_heads * MLA_V
    k3 = kcat.reshape(b, t, kcat.shape[1])
    v3 = vup.reshape(b, t, wa)

    def make_call(c, out):
        n_keys = (c + 1) * tq
        row = lambda bi: (bi * nq + c, 0)
        return pl.pallas_call(
            functools.partial(_mla_prompt_kernel, n_heads, c * tq),
            grid=(b,),
            in_specs=[pl.BlockSpec((tq, qcat.shape[1]), row),
                      pl.BlockSpec((None, n_keys, kcat.shape[1]), lambda bi: (bi, 0, 0)),
                      pl.BlockSpec((None, n_keys, wa), lambda bi: (bi, 0, 0)),
                      pl.BlockSpec(memory_space=pl.ANY)],
            out_specs=pl.BlockSpec((tq, wa), row),
            out_shape=jax.ShapeDtypeStruct(out.shape, out.dtype),
            input_output_aliases={3: 0},
            compiler_params=_cparams(1),
            name="mla_prompt_%d" % c,
        )(qcat, k3, v3, out)

    return _chained_tile_calls(nq, make_call, out_init)


def _toeplitz_bias(tab_ref, h, start, n_rows, n_cols):
    shift0 = LANES - (n_rows - 1)
    from_left = (lax.broadcasted_iota(jnp.int32, (n_rows, LANES), 1)
                 < shift0 + lax.broadcasted_iota(jnp.int32, (n_rows, LANES), 0))
    pieces = []
    prev = None
    for n in range(n_cols // LANES + 1):
        seg = tab_ref[h:h + 1, start + n * LANES:start + (n + 1) * LANES]
        rot = pltpu.roll(jnp.broadcast_to(seg, (n_rows, LANES)), shift0 % LANES, 1, stride=1, stride_axis=0)
        if prev is not None:
            pieces.append(jnp.where(from_left, prev, rot))
        prev = rot
    return jnp.concatenate(pieces, axis=1)


def _dsa_prompt_kernel(n_heads, row0, t, topk, qb_ref, qi_ref, wi_ref, k_ref, v_ref, kiab_ref, tab_ref, prev_ref,
                       o_ref, key_ref, madd_ref):
    del prev_ref
    tq, n_keys = qb_ref.shape[0], k_ref.shape[0]
    score = _indexer_scores(qi_ref, wi_ref[...], kiab_ref[:, :LANES], kiab_ref[:, LANES:])
    _topk_mask(score, _chunk_madd(row0, tq, n_keys), topk, key_ref, madd_ref)
    near0 = max(0, row0 - LANES)
    n_near = n_keys - near0
    for h in range(n_heads):
        c0 = h * DSA_HEAD_DIM
        near = jnp.concatenate(
            [_toeplitz_bias(tab_ref, h, near0 - (row0 + u * LANES) - LANES + t, LANES, n_near)
             for u in range(tq // LANES)], axis=0)
        if near0 > 0:
            far = jnp.broadcast_to(tab_ref[h:h + 1, 0:1], (tq, near0))
            bias = jnp.concatenate([far, near], axis=1)
        else:
            bias = near
        s = _dot_nt(qb_ref[:, c0:c0 + DSA_HEAD_DIM], k_ref[:, c0:c0 + DSA_HEAD_DIM]) + bias + madd_ref[...]
        o_ref[:, c0:c0 + DSA_HEAD_DIM] = _mx(_softmax_pv(s, v_ref[:, c0:c0 + DSA_HEAD_DIM]))


def _dsa_prompt(qb, qi, wi, kbb, vbb, kiab, bias_tab, out_init, b, t, topk):
    tq = DSA_TQ
    assert MAX_DISTANCE <= LANES and tq % LANES == 0
    n_heads = qb.shape[1] // DSA_HEAD_DIM
    nq = t // tq
    wb = qb.shape[1]
    k3, v3, ki3 = kbb.reshape(b, t, wb), vbb.reshape(b, t, wb), kiab.reshape(b, t, 2 * LANES)
    per_b = lambda bi: (bi, 0, 0)

    def make_call(c, out):
        n_keys = (c + 1) * tq
        row = lambda bi: (bi * nq + c, 0)
        return pl.pallas_call(
            functools.partial(_dsa_prompt_kernel, n_heads, c * tq, t, topk),
            grid=(b,),
            in_specs=[pl.BlockSpec((tq, wb), row),
                      pl.BlockSpec((tq, qi.shape[1]), row),
                      pl.BlockSpec((tq, LANES), row),
                      pl.BlockSpec((None, n_keys, wb), per_b),
                      pl.BlockSpec((None, n_keys, wb), per_b),
                      pl.BlockSpec((None, n_keys, 2 * LANES), per_b),
                      _resident(bias_tab.shape),
                      pl.BlockSpec(memory_space=pl.ANY)],
            out_specs=pl.BlockSpec((tq, wb), row),
            out_shape=jax.ShapeDtypeStruct(out.shape, out.dtype),
            input_output_aliases={7: 0},
            scratch_shapes=[pltpu.VMEM((tq, n_keys), jnp.float32), pltpu.VMEM((tq, n_keys), jnp.float32)],
            compiler_params=_cparams(1),
            name="dsa_prompt_%d" % c,
        )(qb, qi, wi, k3, v3, ki3, bias_tab, out)

    return _chained_tile_calls(nq, make_call, out_init)


def _mla_sample_kernel(n_heads, past, ts, qcat_ref, cckv_ref, ckr_ref, nckv_ref, nkr_ref, wuk_ref, wuv_ref,
                       o_ref, kall_ref, rall_ref):
    s_pad = kall_ref.shape[0]
    n_keys = past + ts

    @pl.when(pl.program_id(0) == 0)
    def _():
        kall_ref[...] = jnp.zeros_like(kall_ref)
        rall_ref[...] = jnp.zeros_like(rall_ref)

    kall_ref[0:past, :] = _mx(cckv_ref[...])
    kall_ref[past:n_keys, :] = _mx(nckv_ref[...])
    rall_ref[0:past, 0:MLA_ROPE] = _mx(ckr_ref[...])
    rall_ref[past:n_keys, 0:MLA_ROPE] = _mx(nkr_ref[...])
    qlat, qrope = [], []
    for h in range(n_heads):
        c0 = h * HEAD_SLOT
        qlat.append(_mx(_dot(qcat_ref[:, c0:c0 + LANES], wuk_ref[h])))
        qrope.append(qcat_ref[:, c0 + LANES:c0 + HEAD_SLOT])
    qlat = jnp.concatenate(qlat, axis=0)
    qrope = jnp.concatenate(qrope, axis=0)
    s = _dot_nt(qlat, kall_ref[...]) + _dot_nt(qrope, rall_ref[...])
    col = lax.broadcasted_iota(jnp.int32, s.shape, 1)
    s = jnp.where(col < n_keys, s, NEG_BIG)
    olat = _mx(_softmax_pv(s, kall_ref[...]))
    for h in range(n_heads):
        o_ref[:, h * MLA_V:(h + 1) * MLA_V] = _mx(_dot_nt(olat[h * ts:(h + 1) * ts], wuv_ref[h]))


def _mla_sample(qcat, cache_ckv, cache_kr, new_ckv, new_kr, wuk, wuv, ts):
    nb, past, c = cache_ckv.shape
    n_heads = qcat.shape[1] // HEAD_SLOT
    wa = n_heads * MLA_V
    s_pad = pl.cdiv(past + ts, LANES) * LANES
    row = lambda bi: (bi, 0)
    return pl.pallas_call(
        functools.partial(_mla_sample_kernel, n_heads, past, ts),
        grid=(nb,),
        in_specs=[pl.BlockSpec((ts, qcat.shape[1]), row),
                  pl.BlockSpec((None, past, c), lambda bi: (bi, 0, 0)),
                  pl.BlockSpec((None, past, MLA_ROPE), lambda bi: (bi, 0, 0)),
                  pl.BlockSpec((ts, c), row),
                  pl.BlockSpec((ts, MLA_ROPE), row),
                  _resident(wuk.shape), _resident(wuv.shape)],
        out_specs=pl.BlockSpec((ts, wa), row),
        out_shape=jax.ShapeDtypeStruct((nb * ts, wa), MXU_DT),
        scratch_shapes=[pltpu.VMEM((s_pad, c), MXU_DT), pltpu.VMEM((s_pad, LANES), MXU_DT)],
        compiler_params=_cparams(1),
        name="mla_sample",
    )(qcat, cache_ckv, cache_kr, new_ckv, new_kr, wuk, wuv)


def _dsa_select_kernel(past, ts, topk, qi_ref, wi_ref, cki_ref, nkiab_ref, madd_ref, kia_ref, kib_ref, key_ref):
    bi = pl.program_id(0)
    n_keys = past + ts
    rows, s_pad = madd_ref.shape

    @pl.when(bi == 0)
    def _():
        kia_ref[...] = jnp.zeros_like(kia_ref)
        kib_ref[...] = jnp.zeros_like(kib_ref)

    cki = _mx(cki_ref[...])
    kia_ref[0:past, 0:IDX_DIM] = cki
    kib_ref[0:past, IDX_DIM:LANES] = cki
    kia_ref[past:n_keys, :] = nkiab_ref[:, :LANES]
    kib_ref[past:n_keys, :] = nkiab_ref[:, LANES:]
    madd_ref[pl.ds(pl.multiple_of(bi * ts, ts), ts), :] = _indexer_scores_stacked(
        qi_ref, wi_ref[...], kia_ref[...], kib_ref[...])

    @pl.when(bi == pl.num_programs(0) - 1)
    def _():
        col = lax.broadcasted_iota(jnp.int32, (rows, s_pad), 1)
        _topk_mask(madd_ref[...], col < n_keys, topk, key_ref, madd_ref)


def _dsa_select(qi, wi, cache_ki, new_kiab, ts, topk):
    nb, past, _ = cache_ki.shape
    s_pad = pl.cdiv(past + ts, LANES) * LANES
    row = lambda bi: (bi, 0)
    return pl.pallas_call(
        functools.partial(_dsa_select_kernel, past, ts, topk),
        grid=(nb,),
        in_specs=[pl.BlockSpec((ts, qi.shape[1]), row),
                  pl.BlockSpec((ts, LANES), row),
                  pl.BlockSpec((None, past, IDX_DIM), lambda bi: (bi, 0, 0)),
                  pl.BlockSpec((ts, 2 * LANES), row)],
        out_specs=pl.BlockSpec((nb * ts, s_pad), lambda bi: (0, 0)),
        out_shape=jax.ShapeDtypeStruct((nb * ts, s_pad), jnp.float32),
        scratch_shapes=[pltpu.VMEM((s_pad, LANES), MXU_DT), pltpu.VMEM((s_pad, LANES), MXU_DT),
                        pltpu.VMEM((nb * ts, s_pad), jnp.float32)],
        compiler_params=_cparams(1),
        name="dsa_select",
    )(qi, wi, cache_ki, new_kiab)


def _dsa_sample_kernel(n_heads, past, ts, qb_ref, madd_ref, ck_ref, cv_ref, nk_ref, nv_ref, tab_ref, expand_ref,
                       o_ref, kflat_ref, vflat_ref, bias_ref, biasw_ref):
    n_keys = past + ts
    wide = n_heads * LANES
    n_blocks = pl.cdiv(n_keys, LANES)
    widths = [min(wide, (n_keys - j * LANES) * n_heads) for j in range(n_blocks)]

    @pl.when(pl.program_id(0) == 0)
    def _():
        for h in range(n_heads):
            bias_ref[h * ts:(h + 1) * ts, :] = _toeplitz_bias(tab_ref, h, 0, ts, n_blocks * LANES)
        shape = (n_heads * ts, wide)
        same_head = (lax.broadcasted_iota(jnp.int32, shape, 0) // ts
                     == lax.broadcasted_iota(jnp.int32, shape, 1) % n_heads)
        for j in range(n_blocks):
            b = bias_ref[:, j * LANES:(j + 1) * LANES]
            hi = _mx(b)
            rest = b - hi.astype(jnp.float32)
            mid = _mx(rest)
            lo = _mx(rest - mid.astype(jnp.float32))
            piece = _dot(hi, expand_ref[...]) + _dot(mid, expand_ref[...]) + _dot(lo, expand_ref[...])
            biasw_ref[:, j * wide:j * wide + widths[j]] = jnp.where(same_head, piece, NEG_BIG)[:, :widths[j]]

    kflat_ref[0:past * n_heads, :] = _mx(ck_ref[...])
    kflat_ref[past * n_heads:, :] = nk_ref[...]
    vflat_ref[0:past * n_heads, :] = _mx(cv_ref[...])
    vflat_ref[past * n_heads:, :] = nv_ref[...]
    sel = _mx(jnp.where(madd_ref[...] == 0.0, 1.0, 0.0))
    pieces = [_dot(sel[:, j * LANES:(j + 1) * LANES], expand_ref[...])[:, :widths[j]] for j in range(n_blocks)]
    sel_wide = jnp.concatenate(pieces, axis=1)
    sel_wide = jnp.concatenate([sel_wide] * n_heads, axis=0)
    q_all = jnp.concatenate([qb_ref[:, h * DSA_HEAD_DIM:(h + 1) * DSA_HEAD_DIM] for h in range(n_heads)], axis=0)
    s = _dot_nt(q_all, kflat_ref[...]) + biasw_ref[...] + jnp.where(sel_wide > 0.5, 0.0, NEG_BIG)
    o = _mx(_softmax_pv(s, vflat_ref[...]))
    for h in range(n_heads):
        o_ref[:, h * DSA_HEAD_DIM:(h + 1) * DSA_HEAD_DIM] = o[h * ts:(h + 1) * ts]


def _dsa_sample(qb, madd, cache_k, cache_v, new_k, new_v, bias_tab, ts):
    nb, past, n_heads, _ = cache_k.shape
    wb = n_heads * DSA_HEAD_DIM
    n_keys = past + ts
    assert (n_keys % LANES * n_heads) % LANES == 0
    expand = _mx(jnp.repeat(jnp.eye(LANES, dtype=jnp.float32), n_heads, axis=1))
    row = lambda bi: (bi, 0)
    per_b = lambda bi: (bi, 0, 0)
    return pl.pallas_call(
        functools.partial(_dsa_sample_kernel, n_heads, past, ts),
        grid=(nb,),
        in_specs=[pl.BlockSpec((ts, wb), row),
                  pl.BlockSpec((ts, madd.shape[1]), row),
                  pl.BlockSpec((None, past * n_heads, DSA_HEAD_DIM), per_b),
                  pl.BlockSpec((None, past * n_heads, DSA_HEAD_DIM), per_b),
                  pl.BlockSpec((ts * n_heads, DSA_HEAD_DIM), row),
                  pl.BlockSpec((ts * n_heads, DSA_HEAD_DIM), row),
                  _resident(bias_tab.shape), _resident(expand.shape)],
        out_specs=pl.BlockSpec((ts, wb), row),
        out_shape=jax.ShapeDtypeStruct((nb * ts, wb), MXU_DT),
        scratch_shapes=[pltpu.VMEM((n_keys * n_heads, DSA_HEAD_DIM), MXU_DT),
                        pltpu.VMEM((n_keys * n_heads, DSA_HEAD_DIM), MXU_DT),
                        pltpu.VMEM((n_heads * ts, pl.cdiv(n_keys, LANES) * LANES), jnp.float32),
                        pltpu.VMEM((n_heads * ts, n_keys * n_heads), jnp.float32)],
        compiler_params=_cparams(1),
        name="dsa_sample",
    )(qb, madd, cache_k.reshape(nb, past * n_heads, DSA_HEAD_DIM), cache_v.reshape(nb, past * n_heads, DSA_HEAD_DIM),
      new_k.reshape(nb * ts * n_heads, DSA_HEAD_DIM), new_v.reshape(nb * ts * n_heads, DSA_HEAD_DIM),
      bias_tab, expand)


def _out_kernel(oa_ref, ob_ref, ga_ref, gb_ref, x_ref, gate_ref, gna_ref, gnb_ref, w_ref, fg_ref, y_ref):
    def gated(o_ref, g_ref, gain_ref):
        o = o_ref[...].astype(jnp.float32)
        on = o * lax.rsqrt(jnp.mean(o * o, axis=-1, keepdims=True) + EPS) * gain_ref[...]
        return _mx(on * g_ref[...].astype(jnp.float32))

    wa = oa_ref.shape[1]
    out = _dot(gated(oa_ref, ga_ref, gna_ref), w_ref[0:wa, :]) + _dot(gated(ob_ref, gb_ref, gnb_ref), w_ref[wa:, :])
    xn = x_ref[...] + gate_ref[...] * out
    y_ref[...] = xn * lax.rsqrt(jnp.mean(xn * xn, axis=-1, keepdims=True) + EPS) * fg_ref[...]


def _out_proj(oa, ob, ga, gb, x3, gate, gain_a, gain_b, w_out, final_gain, tm):
    bv, tv, d = x3.shape
    nt = tv // tm
    wa = oa.shape[1]
    x_spec, g_spec = _row_specs(x3, gate.shape[1], tm)
    row = lambda b, i: (b * nt + i, 0)
    return pl.pallas_call(
        _out_kernel,
        grid=(bv, nt),
        in_specs=[pl.BlockSpec((tm, wa), row), pl.BlockSpec((tm, wa), row),
                  pl.BlockSpec((tm, wa), row), pl.BlockSpec((tm, wa), row),
                  x_spec, g_spec, _resident((1, wa)), _resident((1, wa)),
                  _resident(w_out.shape), _resident((1, d))],
        out_specs=pl.BlockSpec((None, tm, d), lambda b, i: (b, i, 0)),
        out_shape=jax.ShapeDtypeStruct((bv, tv, d), jnp.float32),
        compiler_params=_cparams(2),
        name="out_proj",
    )(oa, ob, ga, gb, x3, gate, gain_a.reshape(1, wa), gain_b.reshape(1, wa), w_out, final_gain.reshape(1, d))


def _pack_kernel(n_heads, wt_ref, qa_ref, ckv_ref, kr_ref, ga_ref, gb_ref, qb_ref, kb_ref, vb_ref, qi_ref, kiw_ref):
    wa = n_heads * MLA_V
    half = MLA_ROPE // 2
    q_head = MLA_NOPE + MLA_ROPE
    step = 2 * LANES

    def panel(ref, off):
        for c in range(0, ref.shape[1], step):
            w = min(step, ref.shape[1] - c)
            ref[:, c:c + w] = _mx(wt_ref[off + c:off + c + w, :].T)

    def dup_rope(off):
        x1, x2 = wt_ref[off:off + half, :], wt_ref[off + half:off + 2 * half, :]
        return _mx(jnp.concatenate([x1, x2, x2, x1], axis=0).T)

    for h in range(n_heads):
        src, dst = h * q_head, h * HEAD_SLOT
        qa_ref[:, dst:dst + LANES] = _mx(wt_ref[src:src + MLA_NOPE, :].T)
        qa_ref[:, dst + LANES:dst + HEAD_SLOT] = dup_rope(src + MLA_NOPE)
    off = n_heads * q_head
    panel(ckv_ref, off)
    off += KV_LORA
    kr_ref[...] = dup_rope(off)
    off += MLA_ROPE
    for ref in (ga_ref, qb_ref, kb_ref, vb_ref):
        panel(ref, off)
        off += wa
    pairs = IDX_HEADS // 2
    for j in range(pairs):
        lo = wt_ref[off + IDX_DIM * j:off + IDX_DIM * (j + 1), :]
        hi = wt_ref[off + IDX_DIM * (j + pairs):off + IDX_DIM * (j + pairs + 1), :]
        qi_ref[:, LANES * j:LANES * (j + 1)] = _mx(jnp.concatenate([lo, hi], axis=0).T)
    off += IDX_HEADS * IDX_DIM
    n_kw = IDX_DIM + IDX_HEADS
    kiw = jnp.concatenate([wt_ref[off:off + n_kw, :], jnp.zeros((LANES - n_kw, wt_ref.shape[1]), jnp.float32)], axis=0)
    kiw_ref[...] = _mx(kiw.T)
    off += n_kw
    panel(gb_ref, off)


def _pack_w_in(w_in, n_heads):
    wt = w_in.T
    n, d = wt.shape
    wa = n_heads * MLA_V
    assert n == n_heads * (MLA_NOPE + MLA_ROPE) + KV_LORA + MLA_ROPE + 5 * wa + IDX_HEADS * IDX_DIM + IDX_DIM + IDX_HEADS
    assert 2 * IDX_DIM == LANES and 2 * MLA_ROPE == LANES and MLA_NOPE == LANES
    slab = 256
    widths = [n_heads * HEAD_SLOT, KV_LORA, LANES, wa, wa, wa, wa, wa, IDX_HEADS * IDX_DIM, LANES]
    q_a, ckv, kr, g_a, g_b, q_b, k_b, v_b, q_i, kiw = pl.pallas_call(
        functools.partial(_pack_kernel, n_heads),
        grid=(d // slab,),
        in_specs=[pl.BlockSpec((n, slab), lambda i: (0, i))],
        out_specs=[pl.BlockSpec((slab, w), lambda i: (i, 0)) for w in widths],
        out_shape=[jax.ShapeDtypeStruct((d, w), MXU_DT) for w in widths],
        compiler_params=_cparams(1),
        name="pack_w_in",
    )(wt)
    return [q_a, ckv, kr, g_a, g_b], [q_b, k_b, v_b, q_i, kiw]


def _rope_table(pos):
    half = MLA_ROPE // 2
    freqs = jnp.power(ROPE_THETA, -jnp.arange(half, dtype=jnp.float32) / half)
    ang = pos.astype(jnp.float32)[:, None] * freqs
    cos, sin = jnp.cos(ang), jnp.sin(ang)
    return jnp.concatenate([cos, cos, -sin, sin], axis=1)


def _rel_bucket(rel):
    nb = N_BUCKETS // 2
    max_exact = nb // 2
    n = jnp.abs(rel)
    nf = jnp.maximum(n, 1).astype(jnp.float32)
    large = max_exact + (jnp.log(nf / max_exact) / math.log(MAX_DISTANCE / max_exact)
                         * (nb - max_exact)).astype(jnp.int32)
    large = jnp.minimum(large, nb - 1)
    return jnp.where(rel > 0, nb, 0) + jnp.where(n < max_exact, n, large)


def _bucket_bias(rel_bias, rel):
    return (rel_bias * LOG2E)[_rel_bucket(rel)]


def kernel(x_prompt, x_sample, cache_mla_ckv, cache_mla_krope, cache_dsa_k, cache_dsa_v, cache_idx_k,
           c_prompt, c_sample, w_ada, b_ada, ln_gain, w_in, mla_kv_gain, w_uk, w_uv, rel_bias,
           out_gain_a, out_gain_b, w_out, final_gain):
    assert w_ada.shape[0] == 1, "single-layer step"
    b, t, d = x_prompt.shape
    nb, ts, _ = x_sample.shape
    past = cache_mla_ckv.shape[2]
    n_heads = d // 256
    wb = n_heads * DSA_HEAD_DIM
    s_all = past + ts
    assert t % MLA_TQ == 0 and t % DSA_TQ == 0 and MLA_TQ % CHUNK == 0 and DSA_TQ % CHUNK == 0
    assert (s_all - 1) // CHUNK <= past // CHUNK

    mod = _ada_mod(jnp.concatenate([c_prompt, c_sample], axis=0), w_ada[0], b_ada[0])
    shift, scale, gate = mod[:, :d], mod[:, d:2 * d], mod[:, 2 * d:]
    mod_p = [v[:b].reshape(b, 1, d) for v in (scale, shift, gate)]
    mod_s = [jnp.broadcast_to(v[b:, None, :], (nb, ts, d)).reshape(1, nb * ts, d) for v in (scale, shift, gate)]

    w_a, w_b = _pack_w_in(w_in[0], n_heads)
    wkv = _mx(jnp.concatenate([w_uk[0].transpose(2, 0, 1).reshape(KV_LORA, n_heads * MLA_NOPE),
                               w_uv[0].transpose(2, 0, 1).reshape(KV_LORA, n_heads * MLA_V)], axis=1))
    w_o = _mx(w_out[0])
    pos_p = jnp.arange(t, dtype=jnp.int32)
    pos_s = past + jnp.arange(ts, dtype=jnp.int32)
    tab_p = _rope_table(pos_p)
    tab_s = jnp.tile(_rope_table(pos_s), (nb, 1))

    tm = 256
    qcat, ckv_p, krope_p, ga, gb, kcat, vup, oa_buf = _proj_a(
        x_prompt, mod_p[0], mod_p[1], ln_gain[0], tab_p, w_a, mla_kv_gain[0], wkv, tm)
    qb, k_p, v_p, kbb, vbb, qi, idxk_p, kiab, wi, ob_buf = _proj_b(
        x_prompt, mod_p[0], mod_p[1], ln_gain[0], w_b, tm, True)
    o_a = _mla_prompt(qcat, kcat, vup, oa_buf, b, t)
    near = _bucket_bias(rel_bias, jnp.arange(-MAX_DISTANCE - 1, MAX_DISTANCE, dtype=jnp.int32))
    far = near[:1]
    bias_tab = jnp.concatenate([jnp.broadcast_to(far, (t - 1 - MAX_DISTANCE, n_heads)), near[1:],
                                jnp.broadcast_to(far, (t + 1 - MAX_DISTANCE, n_heads))], axis=0).T
    o_b = _dsa_prompt(qb, qi, wi, kbb, vbb, kiab, bias_tab, ob_buf, b, t, min(TOPK_MAX, t // 4))
    y_prompt = _out_proj(o_a, o_b, ga, gb, x_prompt, mod_p[2], out_gain_a[0], out_gain_b[0], w_o, final_gain, 2 * tm)

    xs3 = x_sample.reshape(1, nb * ts, d)
    tms = min(256, nb * ts)
    qcat_s, ckv_s, krope_s, ga_s, gb_s = _proj_a(
        xs3, mod_s[0], mod_s[1], ln_gain[0], tab_s, w_a, mla_kv_gain[0], None, tms)
    qb_s, k_s, v_s, kbb_s, vbb_s, qi_s, idxk_s, kiab_s, wi_s = _proj_b(
        xs3, mod_s[0], mod_s[1], ln_gain[0], w_b, tms, False)
    oa_s = _mla_sample(qcat_s, cache_mla_ckv[0], cache_mla_krope[0], ckv_s, krope_s,
                       _mx(w_uk[0]), _mx(w_uv[0]), ts)
    s_pad = pl.cdiv(s_all, LANES) * LANES
    tab_s = _bucket_bias(rel_bias, jnp.arange(s_pad + LANES, dtype=jnp.int32) - (s_all - 1)).T
    madd_s = _dsa_select(qi_s, wi_s, cache_idx_k[0], kiab_s, ts, min(TOPK_MAX, s_all // 4))
    ob_s = _dsa_sample(qb_s, madd_s, cache_dsa_k[0], cache_dsa_v[0], kbb_s, vbb_s, tab_s, ts)
    y_sample = _out_proj(oa_s, ob_s, ga_s, gb_s, xs3, mod_s[2], out_gain_a[0], out_gain_b[0], w_o, final_gain, tms)

    hd = (n_heads, DSA_HEAD_DIM)
    return (y_prompt, y_sample.reshape(nb, ts, d),
            ckv_p.reshape(1, b, t, KV_LORA), krope_p.reshape(1, b, t, MLA_ROPE),
            k_p.reshape(1, b, t, *hd), v_p.reshape(1, b, t, *hd), idxk_p.reshape(1, b, t, IDX_DIM),
            ckv_s.reshape(1, nb, ts, KV_LORA), krope_s.reshape(1, nb, ts, MLA_ROPE),
            k_s.reshape(1, nb, ts, *hd), v_s.reshape(1, nb, ts, *hd), idxk_s.reshape(1, nb, ts, IDX_DIM))
```

```python
import functools
import math

import jax
import jax.numpy as jnp
from jax import lax
from jax.experimental import pallas as pl
from jax.experimental.pallas import tpu as pltpu

MXU_DT = jnp.bfloat16

CHUNK = 64
MLA_NOPE = 128
MLA_ROPE = 64
MLA_V = 128
KV_LORA = 512
DSA_HEAD_DIM = 128
IDX_HEADS = 16
IDX_DIM = 64
TOPK_MAX = 256
N_BUCKETS = 32
MAX_DISTANCE = 128
ROPE_THETA = 10000.0
EPS = 1e-6

LANES = 128
HEAD_SLOT = 256
NEG_BIG = -1e30
LOG2E = math.log2(math.e)
INT_MIN = -2 ** 31
KEY_NEG_INF = INT_MIN + 0x7FFFFF
VMEM_LIMIT = 56 * 1024 * 1024


def _cparams(n_grid, vmem=VMEM_LIMIT):
    return pltpu.CompilerParams(dimension_semantics=("arbitrary",) * n_grid, vmem_limit_bytes=vmem)


def _mx(v):
    return v.astype(MXU_DT)


def _dot(a, b):
    return jnp.dot(a, b, preferred_element_type=jnp.float32)


def _dot_nt(a, b):
    return lax.dot_general(a, b, (((1,), (1,)), ((), ())), preferred_element_type=jnp.float32)


def _silu(v):
    return v * (1.0 / (1.0 + jnp.exp(-v)))


def _resident(shape):
    nd = len(shape)
    return pl.BlockSpec(shape, lambda *_: (0,) * nd, pipeline_mode=pl.Buffered(1))


def _ada_kernel(c_ref, w_ref, b_ref, o_ref):
    a = _mx(_silu(c_ref[...]))
    o_ref[...] = _dot(a, _mx(w_ref[...])) + b_ref[...]


def _ada_mod(c_all, w_ada, b_ada):
    m, d = c_all.shape
    n = w_ada.shape[1]
    tn = 1024
    return pl.pallas_call(
        _ada_kernel,
        grid=(n // tn,),
        in_specs=[pl.BlockSpec((m, d), lambda j: (0, 0)),
                  pl.BlockSpec((d, tn), lambda j: (0, j)),
                  pl.BlockSpec((1, tn), lambda j: (0, j))],
        out_specs=pl.BlockSpec((m, tn), lambda j: (0, j)),
        out_shape=jax.ShapeDtypeStruct((m, n), jnp.float32),
        compiler_params=_cparams(1),
        name="ada_mod",
    )(c_all, w_ada, b_ada.reshape(1, n))


def _modulated_norm(x_ref, scale_ref, shift_ref, lng_ref):
    x = x_ref[...]
    xn = x * lax.rsqrt(jnp.mean(x * x, axis=-1, keepdims=True) + EPS) * lng_ref[...]
    return _mx(xn * (1.0 + scale_ref[...]) + shift_ref[...])


def _rope128(a, tab):
    t = a * tab
    return t + pltpu.roll(t, 64, 1)


def _proj_a_kernel(n_heads, emit_kv, x_ref, scale_ref, shift_ref, lng_ref, tab_ref,
                   wq_ref, wc_ref, wr_ref, wga_ref, wgb_ref, kvg_ref, *rest):
    if emit_kv:
        wkv_ref, qcat_ref, ckv_ref, krope_ref, ga_ref, gb_ref, kcat_ref, vup_ref, attn_ref = rest
        attn_ref[...] = jnp.zeros_like(attn_ref)
    else:
        qcat_ref, ckv_ref, krope_ref, ga_ref, gb_ref = rest
    hb = _modulated_norm(x_ref, scale_ref, shift_ref, lng_ref)
    tab = tab_ref[...]
    qscale = (MLA_NOPE + MLA_ROPE) ** -0.5 * LOG2E
    for h in range(n_heads):
        c0 = h * HEAD_SLOT
        a = _dot(hb, wq_ref[:, c0:c0 + HEAD_SLOT])
        qcat_ref[:, c0:c0 + LANES] = _mx(a[:, :LANES] * qscale)
        qcat_ref[:, c0 + LANES:c0 + HEAD_SLOT] = _mx(_rope128(a[:, LANES:], tab) * qscale)
    c = _dot(hb, wc_ref[...])
    cn = c * lax.rsqrt(jnp.mean(c * c, axis=-1, keepdims=True) + EPS) * kvg_ref[...]
    ckv_ref[...] = cn
    r = _rope128(_dot(hb, wr_ref[...]), tab)
    krope_ref[...] = r[:, :MLA_ROPE]
    ga_ref[...] = _mx(_silu(_dot(hb, wga_ref[...])))
    gb_ref[...] = _mx(_silu(_dot(hb, wgb_ref[...])))
    if emit_kv:
        cb = _mx(cn)
        lane = lax.broadcasted_iota(jnp.int32, r.shape, 1)
        krz = _mx(jnp.where(lane < MLA_ROPE, r, 0.0))
        kn = _dot(cb, wkv_ref[:, :n_heads * MLA_NOPE])
        for h in range(n_heads):
            c0 = h * HEAD_SLOT
            kcat_ref[:, c0:c0 + LANES] = _mx(kn[:, h * MLA_NOPE:(h + 1) * MLA_NOPE])
            kcat_ref[:, c0 + LANES:c0 + HEAD_SLOT] = krz
        vup_ref[...] = _mx(_dot(cb, wkv_ref[:, n_heads * MLA_NOPE:]))


def _proj_b_kernel(n_heads, x_ref, scale_ref, shift_ref, lng_ref, wq_ref, wk_ref, wv_ref, wqi_ref, wkw_ref,
                   qb_ref, kb_ref, vb_ref, kbb_ref, vbb_ref, qi_ref, ki_ref, kiab_ref, wi_ref, *attn_ref):
    if attn_ref:
        attn_ref[0][...] = jnp.zeros_like(attn_ref[0])
    hb = _modulated_norm(x_ref, scale_ref, shift_ref, lng_ref)
    dscale = DSA_HEAD_DIM ** -0.5 * LOG2E
    step = 512
    for c0 in range(0, wq_ref.shape[1], step):
        qb_ref[:, c0:c0 + step] = _mx(_dot(hb, wq_ref[:, c0:c0 + step]) * dscale)
    for w_ref, f_ref, b_ref in ((wk_ref, kb_ref, kbb_ref), (wv_ref, vb_ref, vbb_ref)):
        for c0 in range(0, w_ref.shape[1], step):
            kv = _dot(hb, w_ref[:, c0:c0 + step])
            f_ref[:, c0:c0 + step] = kv
            b_ref[:, c0:c0 + step] = _mx(kv)
    for c0 in range(0, wqi_ref.shape[1], step):
        qi_ref[:, c0:c0 + step] = _mx(_dot(hb, wqi_ref[:, c0:c0 + step]))
    a = _dot(hb, wkw_ref[...])
    ki_ref[...] = a[:, :IDX_DIM]
    lane = lax.broadcasted_iota(jnp.int32, a.shape, 1)
    kz = jnp.where(lane < IDX_DIM, a, 0.0)
    kiab_ref[:, :LANES] = _mx(kz)
    kiab_ref[:, LANES:] = _mx(pltpu.roll(kz, IDX_DIM, 1))
    wi_ref[...] = a * (IDX_HEADS ** -0.5 * IDX_DIM ** -0.5)


def _row_specs(x3, mod_rows, tm):
    bv, tv, d = x3.shape
    x_spec = pl.BlockSpec((None, tm, d), lambda b, i: (b, i, 0))
    if mod_rows == 1:
        m_spec = pl.BlockSpec((None, 1, d), lambda b, i: (b, 0, 0))
    else:
        m_spec = pl.BlockSpec((None, tm, d), lambda b, i: (b, i, 0))
    return x_spec, m_spec


def _out2d(m, width, dtype, tm, nt):
    return (jax.ShapeDtypeStruct((m, width), dtype),
            pl.BlockSpec((tm, width), lambda b, i: (b * nt + i, 0)))


def _proj_a(x3, scale, shift, ln_gain, tab, ws_a, kv_gain, wkv, tm):
    bv, tv, d = x3.shape
    nt = tv // tm
    m = bv * tv
    n_heads = d // 256
    wa = n_heads * MLA_V
    emit_kv = wkv is not None
    x_spec, m_spec = _row_specs(x3, scale.shape[1], tm)
    in_specs = ([x_spec, m_spec, m_spec, _resident((1, d)), pl.BlockSpec((tm, LANES), lambda b, i: (i, 0))]
                + [_resident(w.shape) for w in ws_a] + [_resident((1, KV_LORA))])
    args = [x3, scale, shift, ln_gain.reshape(1, d), tab, *ws_a, kv_gain.reshape(1, KV_LORA)]
    outs = [_out2d(m, n_heads * HEAD_SLOT, MXU_DT, tm, nt),
            _out2d(m, KV_LORA, jnp.float32, tm, nt),
            _out2d(m, MLA_ROPE, jnp.float32, tm, nt),
            _out2d(m, wa, MXU_DT, tm, nt),
            _out2d(m, wa, MXU_DT, tm, nt)]
    if emit_kv:
        in_specs.append(_resident(wkv.shape))
        args.append(wkv)
        outs += [_out2d(m, n_heads * HEAD_SLOT, MXU_DT, tm, nt),
                 _out2d(m, wa, MXU_DT, tm, nt),
                 _out2d(m, wa, MXU_DT, tm, nt)]
    return pl.pallas_call(
        functools.partial(_proj_a_kernel, n_heads, emit_kv),
        grid=(bv, nt),
        in_specs=in_specs,
        out_specs=[o[1] for o in outs],
        out_shape=[o[0] for o in outs],
        compiler_params=_cparams(2),
        name="proj_a",
    )(*args)


def _proj_b(x3, scale, shift, ln_gain, ws_b, tm, emit_attn_buffer):
    bv, tv, d = x3.shape
    nt = tv // tm
    m = bv * tv
    n_heads = d // 256
    width_b = n_heads * DSA_HEAD_DIM
    x_spec, m_spec = _row_specs(x3, scale.shape[1], tm)
    outs = [_out2d(m, width_b, MXU_DT, tm, nt),
            _out2d(m, width_b, jnp.float32, tm, nt),
            _out2d(m, width_b, jnp.float32, tm, nt),
            _out2d(m, width_b, MXU_DT, tm, nt),
            _out2d(m, width_b, MXU_DT, tm, nt),
            _out2d(m, IDX_HEADS * IDX_DIM, MXU_DT, tm, nt),
            _out2d(m, IDX_DIM, jnp.float32, tm, nt),
            _out2d(m, 2 * LANES, MXU_DT, tm, nt),
            _out2d(m, LANES, jnp.float32, tm, nt)]
    if emit_attn_buffer:
        outs.append(_out2d(m, width_b, MXU_DT, tm, nt))
    return pl.pallas_call(
        functools.partial(_proj_b_kernel, n_heads),
        grid=(bv, nt),
        in_specs=[x_spec, m_spec, m_spec, _resident((1, d))] + [_resident(w.shape) for w in ws_b],
        out_specs=[o[1] for o in outs],
        out_shape=[o[0] for o in outs],
        compiler_params=_cparams(2),
        name="proj_b",
    )(x3, scale, shift, ln_gain.reshape(1, d), *ws_b)


def _softmax_pv(s, v):
    m = jnp.max(s, axis=-1, keepdims=True)
    p = jnp.exp2(s - m)
    l = jnp.sum(p, axis=-1, keepdims=True)
    return _dot(_mx(p), v) * (1.0 / l)


def _topk_mask(score, adm, topk, sc_ref, madd_ref):
    rows, n = score.shape
    if n <= topk:
        madd_ref[...] = jnp.where(adm, 0.0, NEG_BIG)
        return
    sc_ref[...] = jnp.where(adm, score, -jnp.inf)
    kf = float(topk)

    def count(pred):
        return jnp.sum(jnp.where(pred, 1.0, 0.0), axis=-1, keepdims=True)

    def key_to_float(key):
        return pltpu.bitcast(key ^ ((key >> 31) & 0x7FFFFFFF), jnp.float32)

    def bit_step(i, thr):
        inc = lax.shift_left(jnp.int32(1), 31 - i)
        cand = thr + inc
        ok = count(sc_ref[...] >= key_to_float(cand)) >= kf
        return jnp.where(ok, cand, thr)

    thr = lax.fori_loop(0, 32, bit_step, jnp.full((rows, 1), INT_MIN, jnp.int32))
    thr = key_to_float(jnp.maximum(thr, KEY_NEG_INF))
    ge = sc_ref[...] >= thr
    madd_ref[...] = jnp.where(ge, 0.0, NEG_BIG)

    @pl.when(jnp.max(count(ge)) > kf)
    def _():
        sc = sc_ref[...]
        gt = sc > thr
        need = kf - count(gt)
        eqf = jnp.where(sc == thr, 1.0, 0.0)
        col = lax.broadcasted_iota(jnp.int32, (rows, n), 1)
        nbits = max(1, int(n).bit_length())

        def col_step(i, bound):
            cand = bound + lax.shift_left(jnp.int32(1), nbits - 1 - i)
            taken = jnp.sum(jnp.where(col < cand, eqf, 0.0), axis=-1, keepdims=True)
            return jnp.where(taken <= need, cand, bound)

        bound = lax.fori_loop(0, nbits, col_step, jnp.zeros((rows, 1), jnp.int32))
        tie_madd = jnp.where(jnp.where(col < bound, eqf, 0.0) > 0.5, 0.0, NEG_BIG)
        madd_ref[...] = jnp.where(adm, jnp.where(gt, 0.0, tie_madd), NEG_BIG)


def _indexer_scores(qi_ref, wi, kia, kib):
    half = IDX_HEADS // 2
    score = None
    for j in range(half):
        qp = qi_ref[:, j * LANES:(j + 1) * LANES]
        da = jnp.maximum(_dot_nt(qp, kia), 0.0) * wi[:, IDX_DIM + j:IDX_DIM + j + 1]
        db = jnp.maximum(_dot_nt(qp, kib), 0.0) * wi[:, IDX_DIM + half + j:IDX_DIM + half + j + 1]
        score = da + db if score is None else score + da + db
    return score


def _indexer_scores_stacked(qi_ref, wi, kia, kib):
    half = IDX_HEADS // 2
    ts = qi_ref.shape[0]
    q_all = jnp.concatenate([qi_ref[:, j * LANES:(j + 1) * LANES] for j in range(half)], axis=0)
    wa = jnp.concatenate([wi[:, IDX_DIM + j:IDX_DIM + j + 1] for j in range(half)], axis=0)
    wb = jnp.concatenate([wi[:, IDX_DIM + half + j:IDX_DIM + half + j + 1] for j in range(half)], axis=0)
    part = jnp.maximum(_dot_nt(q_all, kia), 0.0) * wa + jnp.maximum(_dot_nt(q_all, kib), 0.0) * wb
    score = part[0:ts]
    for j in range(1, half):
        score = score + part[j * ts:(j + 1) * ts]
    return score


MLA_TQ = 512
DSA_TQ = 256


def _chunk_madd(row0, tq, n_keys):
    qc = (row0 + lax.broadcasted_iota(jnp.int32, (tq, n_keys), 0)) // CHUNK
    kc = lax.broadcasted_iota(jnp.int32, (tq, n_keys), 1) // CHUNK
    return kc <= qc


def _chained_tile_calls(n_tiles, make_call, out):
    for c in range(n_tiles):
        out = make_call(c, out)
    return out


def _mla_prompt_kernel(n_heads, row0, q_ref, k_ref, v_ref, prev_ref, o_ref):
    del prev_ref
    tq, n_keys = q_ref.shape[0], k_ref.shape[0]
    madd = jnp.where(_chunk_madd(row0, tq, n_keys), 0.0, NEG_BIG)
    for h in range(n_heads):
        c0 = h * HEAD_SLOT
        s = _dot_nt(q_ref[:, c0:c0 + HEAD_SLOT], k_ref[:, c0:c0 + HEAD_SLOT]) + madd
        o_ref[:, h * MLA_V:(h + 1) * MLA_V] = _mx(_softmax_pv(s, v_ref[:, h * MLA_V:(h + 1) * MLA_V]))


def _mla_prompt(qcat, kcat, vup, out_init, b, t):
    tq = MLA_TQ
    n_heads = qcat.shape[1] // HEAD_SLOT
    nq = t // tq
    wa = n_heads * MLA_V
    k3 = kcat.reshape(b, t, kcat.shape[1])
    v3 = vup.reshape(b, t, wa)

    def make_call(c, out):
        n_keys = (c + 1) * tq
        row = lambda bi: (bi * nq + c, 0)
        return pl.pallas_call(
            functools.partial(_mla_prompt_kernel, n_heads, c * tq),
            grid=(b,),
            in_specs=[pl.BlockSpec((tq, qcat.shape[1]), row),
                      pl.BlockSpec((None, n_keys, kcat.shape[1]), lambda bi: (bi, 0, 0)),
                      pl.BlockSpec((None, n_keys, wa), lambda bi: (bi, 0, 0)),
                      pl.BlockSpec(memory_space=pl.ANY)],
            out_specs=pl.BlockSpec((tq, wa), row),
            out_shape=jax.ShapeDtypeStruct(out.shape, out.dtype),
            input_output_aliases={3: 0},
            compiler_params=_cparams(1),
            name="mla_prompt_%d" % c,
        )(qcat, k3, v3, out)

    return _chained_tile_calls(nq, make_call, out_init)


def _toeplitz_bias(tab_ref, h, start, n_rows, n_cols):
    shift0 = LANES - (n_rows - 1)
    from_left = (lax.broadcasted_iota(jnp.int32, (n_rows, LANES), 1)
                 < shift0 + lax.broadcasted_iota(jnp.int32, (n_rows, LANES), 0))
    pieces = []
    prev = None
    for n in range(n_cols // LANES + 1):
        seg = tab_ref[h:h + 1, start + n * LANES:start + (n + 1) * LANES]
        rot = pltpu.roll(jnp.broadcast_to(seg, (n_rows, LANES)), shift0 % LANES, 1, stride=1, stride_axis=0)
        if prev is not None:
            pieces.append(jnp.where(from_left, prev, rot))
        prev = rot
    return jnp.concatenate(pieces, axis=1)


def _dsa_prompt_kernel(n_heads, row0, t, topk, qb_ref, qi_ref, wi_ref, k_ref, v_ref, kiab_ref, tab_ref, prev_ref,
                       o_ref, key_ref, madd_ref, vext_ref):
    del prev_ref
    tq, n_keys = qb_ref.shape[0], k_ref.shape[0]
    score = _indexer_scores(qi_ref, wi_ref[...], kiab_ref[:, :LANES], kiab_ref[:, LANES:])
    _topk_mask(score, _chunk_madd(row0, tq, n_keys), topk, key_ref, madd_ref)
    ones = jnp.ones((n_keys, LANES), MXU_DT)
    for h in range(n_heads):
        vext_ref[:, 2 * h * LANES:(2 * h + 1) * LANES] = v_ref[:, h * DSA_HEAD_DIM:(h + 1) * DSA_HEAD_DIM]
        vext_ref[:, (2 * h + 1) * LANES:(2 * h + 2) * LANES] = ones
    near0 = max(0, row0 - LANES)
    n_near = n_keys - near0
    for h in range(n_heads):
        c0 = h * DSA_HEAD_DIM
        near = jnp.concatenate(
            [_toeplitz_bias(tab_ref, h, near0 - (row0 + u * LANES) - LANES + t, LANES, n_near)
             for u in range(tq // LANES)], axis=0)
        d = _dot_nt(qb_ref[:, c0:c0 + DSA_HEAD_DIM], k_ref[:, c0:c0 + DSA_HEAD_DIM])
        s_near = d[:, near0:] + near + madd_ref[:, near0:]
        m = jnp.max(s_near, axis=-1, keepdims=True)
        if near0 > 0:
            far_bias = tab_ref[h:h + 1, 0:1]
            s_far = d[:, :near0] + madd_ref[:, :near0]
            m = jnp.maximum(m, jnp.max(s_far, axis=-1, keepdims=True) + far_bias)
            p = jnp.concatenate([jnp.exp2(s_far - (m - far_bias)), jnp.exp2(s_near - m)], axis=1)
        else:
            p = jnp.exp2(s_near - m)
        pv = _dot(_mx(p), vext_ref[:, 2 * h * LANES:(2 * h + 2) * LANES])
        o_ref[:, c0:c0 + DSA_HEAD_DIM] = _mx(pv[:, :LANES] * (1.0 / pv[:, LANES:LANES + 1]))


def _dsa_prompt(qb, qi, wi, kbb, vbb, kiab, bias_tab, out_init, b, t, topk):
    tq = DSA_TQ
    assert MAX_DISTANCE <= LANES and tq % LANES == 0
    n_heads = qb.shape[1] // DSA_HEAD_DIM
    nq = t // tq
    wb = qb.shape[1]
    k3, v3, ki3 = kbb.reshape(b, t, wb), vbb.reshape(b, t, wb), kiab.reshape(b, t, 2 * LANES)
    per_b = lambda bi: (bi, 0, 0)

    def make_call(c, out):
        n_keys = (c + 1) * tq
        row = lambda bi: (bi * nq + c, 0)
        return pl.pallas_call(
            functools.partial(_dsa_prompt_kernel, n_heads, c * tq, t, topk),
            grid=(b,),
            in_specs=[pl.BlockSpec((tq, wb), row),
                      pl.BlockSpec((tq, qi.shape[1]), row),
                      pl.BlockSpec((tq, LANES), row),
                      pl.BlockSpec((None, n_keys, wb), per_b),
                      pl.BlockSpec((None, n_keys, wb), per_b),
                      pl.BlockSpec((None, n_keys, 2 * LANES), per_b),
                      _resident(bias_tab.shape),
                      pl.BlockSpec(memory_space=pl.ANY)],
            out_specs=pl.BlockSpec((tq, wb), row),
            out_shape=jax.ShapeDtypeStruct(out.shape, out.dtype),
            input_output_aliases={7: 0},
            scratch_shapes=[pltpu.VMEM((tq, n_keys), jnp.float32), pltpu.VMEM((tq, n_keys), jnp.float32),
                            pltpu.VMEM((n_keys, 2 * wb), MXU_DT)],
            compiler_params=_cparams(1),
            name="dsa_prompt_%d" % c,
        )(qb, qi, wi, k3, v3, ki3, bias_tab, out)

    return _chained_tile_calls(nq, make_call, out_init)


def _mla_sample_kernel(n_heads, past, ts, qcat_ref, cckv_ref, ckr_ref, nckv_ref, nkr_ref, wuk_ref, wuv_ref,
                       o_ref, kall_ref, rall_ref):
    s_pad = kall_ref.shape[0]
    n_keys = past + ts

    @pl.when(pl.program_id(0) == 0)
    def _():
        kall_ref[...] = jnp.zeros_like(kall_ref)
        rall_ref[...] = jnp.zeros_like(rall_ref)

    kall_ref[0:past, :] = _mx(cckv_ref[...])
    kall_ref[past:n_keys, :] = _mx(nckv_ref[...])
    rall_ref[0:past, 0:MLA_ROPE] = _mx(ckr_ref[...])
    rall_ref[past:n_keys, 0:MLA_ROPE] = _mx(nkr_ref[...])
    qlat, qrope = [], []
    for h in range(n_heads):
        c0 = h * HEAD_SLOT
        qlat.append(_mx(_dot(qcat_ref[:, c0:c0 + LANES], wuk_ref[h])))
        qrope.append(qcat_ref[:, c0 + LANES:c0 + HEAD_SLOT])
    qlat = jnp.concatenate(qlat, axis=0)
    qrope = jnp.concatenate(qrope, axis=0)
    s = _dot_nt(qlat, kall_ref[...]) + _dot_nt(qrope, rall_ref[...])
    col = lax.broadcasted_iota(jnp.int32, s.shape, 1)
    s = jnp.where(col < n_keys, s, NEG_BIG)
    olat = _mx(_softmax_pv(s, kall_ref[...]))
    for h in range(n_heads):
        o_ref[:, h * MLA_V:(h + 1) * MLA_V] = _mx(_dot_nt(olat[h * ts:(h + 1) * ts], wuv_ref[h]))


def _mla_sample(qcat, cache_ckv, cache_kr, new_ckv, new_kr, wuk, wuv, ts):
    nb, past, c = cache_ckv.shape
    n_heads = qcat.shape[1] // HEAD_SLOT
    wa = n_heads * MLA_V
    s_pad = pl.cdiv(past + ts, LANES) * LANES
    row = lambda bi: (bi, 0)
    return pl.pallas_call(
        functools.partial(_mla_sample_kernel, n_heads, past, ts),
        grid=(nb,),
        in_specs=[pl.BlockSpec((ts, qcat.shape[1]), row),
                  pl.BlockSpec((None, past, c), lambda bi: (bi, 0, 0)),
                  pl.BlockSpec((None, past, MLA_ROPE), lambda bi: (bi, 0, 0)),
                  pl.BlockSpec((ts, c), row),
                  pl.BlockSpec((ts, MLA_ROPE), row),
                  _resident(wuk.shape), _resident(wuv.shape)],
        out_specs=pl.BlockSpec((ts, wa), row),
        out_shape=jax.ShapeDtypeStruct((nb * ts, wa), MXU_DT),
        scratch_shapes=[pltpu.VMEM((s_pad, c), MXU_DT), pltpu.VMEM((s_pad, LANES), MXU_DT)],
        compiler_params=_cparams(1),
        name="mla_sample",
    )(qcat, cache_ckv, cache_kr, new_ckv, new_kr, wuk, wuv)


def _dsa_select_kernel(past, ts, topk, qi_ref, wi_ref, cki_ref, nkiab_ref, madd_ref, kia_ref, kib_ref, key_ref):
    bi = pl.program_id(0)
    n_keys = past + ts
    rows, s_pad = madd_ref.shape

    @pl.when(bi == 0)
    def _():
        kia_ref[...] = jnp.zeros_like(kia_ref)
        kib_ref[...] = jnp.zeros_like(kib_ref)

    cki = _mx(cki_ref[...])
    kia_ref[0:past, 0:IDX_DIM] = cki
    kib_ref[0:past, IDX_DIM:LANES] = cki
    kia_ref[past:n_keys, :] = nkiab_ref[:, :LANES]
    kib_ref[past:n_keys, :] = nkiab_ref[:, LANES:]
    madd_ref[pl.ds(pl.multiple_of(bi * ts, ts), ts), :] = _indexer_scores_stacked(
        qi_ref, wi_ref[...], kia_ref[...], kib_ref[...])

    @pl.when(bi == pl.num_programs(0) - 1)
    def _():
        col = lax.broadcasted_iota(jnp.int32, (rows, s_pad), 1)
        _topk_mask(madd_ref[...], col < n_keys, topk, key_ref, madd_ref)


def _dsa_select(qi, wi, cache_ki, new_kiab, ts, topk):
    nb, past, _ = cache_ki.shape
    s_pad = pl.cdiv(past + ts, LANES) * LANES
    row = lambda bi: (bi, 0)
    return pl.pallas_call(
        functools.partial(_dsa_select_kernel, past, ts, topk),
        grid=(nb,),
        in_specs=[pl.BlockSpec((ts, qi.shape[1]), row),
                  pl.BlockSpec((ts, LANES), row),
                  pl.BlockSpec((None, past, IDX_DIM), lambda bi: (bi, 0, 0)),
                  pl.BlockSpec((ts, 2 * LANES), row)],
        out_specs=pl.BlockSpec((nb * ts, s_pad), lambda bi: (0, 0)),
        out_shape=jax.ShapeDtypeStruct((nb * ts, s_pad), jnp.float32),
        scratch_shapes=[pltpu.VMEM((s_pad, LANES), MXU_DT), pltpu.VMEM((s_pad, LANES), MXU_DT),
                        pltpu.VMEM((nb * ts, s_pad), jnp.float32)],
        compiler_params=_cparams(1),
        name="dsa_select",
    )(qi, wi, cache_ki, new_kiab)


def _dsa_sample_kernel(n_heads, past, ts, qb_ref, madd_ref, ck_ref, cv_ref, nk_ref, nv_ref, tab_ref, expand_ref,
                       o_ref, kflat_ref, vflat_ref, bias_ref, biasw_ref):
    n_keys = past + ts
    wide = n_heads * LANES
    n_blocks = pl.cdiv(n_keys, LANES)
    widths = [min(wide, (n_keys - j * LANES) * n_heads) for j in range(n_blocks)]

    @pl.when(pl.program_id(0) == 0)
    def _():
        for h in range(n_heads):
            bias_ref[h * ts:(h + 1) * ts, :] = _toeplitz_bias(tab_ref, h, 0, ts, n_blocks * LANES)
        shape = (n_heads * ts, wide)
        same_head = (lax.broadcasted_iota(jnp.int32, shape, 0) // ts
                     == lax.broadcasted_iota(jnp.int32, shape, 1) % n_heads)
        for j in range(n_blocks):
            b = bias_ref[:, j * LANES:(j + 1) * LANES]
            hi = _mx(b)
            rest = b - hi.astype(jnp.float32)
            mid = _mx(rest)
            lo = _mx(rest - mid.astype(jnp.float32))
            piece = _dot(hi, expand_ref[...]) + _dot(mid, expand_ref[...]) + _dot(lo, expand_ref[...])
            biasw_ref[:, j * wide:j * wide + widths[j]] = jnp.where(same_head, piece, NEG_BIG)[:, :widths[j]]

    kflat_ref[0:past * n_heads, :] = _mx(ck_ref[...])
    kflat_ref[past * n_heads:, :] = nk_ref[...]
    vflat_ref[0:past * n_heads, :] = _mx(cv_ref[...])
    vflat_ref[past * n_heads:, :] = nv_ref[...]
    sel = _mx(jnp.where(madd_ref[...] == 0.0, 1.0, 0.0))
    pieces = [_dot(sel[:, j * LANES:(j + 1) * LANES], expand_ref[...])[:, :widths[j]] for j in range(n_blocks)]
    sel_wide = jnp.concatenate(pieces, axis=1)
    sel_wide = jnp.concatenate([sel_wide] * n_heads, axis=0)
    q_all = jnp.concatenate([qb_ref[:, h * DSA_HEAD_DIM:(h + 1) * DSA_HEAD_DIM] for h in range(n_heads)], axis=0)
    s = _dot_nt(q_all, kflat_ref[...]) + biasw_ref[...] + jnp.where(sel_wide > 0.5, 0.0, NEG_BIG)
    o = _mx(_softmax_pv(s, vflat_ref[...]))
    for h in range(n_heads):
        o_ref[:, h * DSA_HEAD_DIM:(h + 1) * DSA_HEAD_DIM] = o[h * ts:(h + 1) * ts]


def _dsa_sample(qb, madd, cache_k, cache_v, new_k, new_v, bias_tab, ts):
    nb, past, n_heads, _ = cache_k.shape
    wb = n_heads * DSA_HEAD_DIM
    n_keys = past + ts
    assert (n_keys % LANES * n_heads) % LANES == 0
    expand = _mx(jnp.repeat(jnp.eye(LANES, dtype=jnp.float32), n_heads, axis=1))
    row = lambda bi: (bi, 0)
    per_b = lambda bi: (bi, 0, 0)
    return pl.pallas_call(
        functools.partial(_dsa_sample_kernel, n_heads, past, ts),
        grid=(nb,),
        in_specs=[pl.BlockSpec((ts, wb), row),
                  pl.BlockSpec((ts, madd.shape[1]), row),
                  pl.BlockSpec((None, past * n_heads, DSA_HEAD_DIM), per_b),
                  pl.BlockSpec((None, past * n_heads, DSA_HEAD_DIM), per_b),
                  pl.BlockSpec((ts * n_heads, DSA_HEAD_DIM), row),
                  pl.BlockSpec((ts * n_heads, DSA_HEAD_DIM), row),
                  _resident(bias_tab.shape), _resident(expand.shape)],
        out_specs=pl.BlockSpec((ts, wb), row),
        out_shape=jax.ShapeDtypeStruct((nb * ts, wb), MXU_DT),
        scratch_shapes=[pltpu.VMEM((n_keys * n_heads, DSA_HEAD_DIM), MXU_DT),
                        pltpu.VMEM((n_keys * n_heads, DSA_HEAD_DIM), MXU_DT),
                        pltpu.VMEM((n_heads * ts, pl.cdiv(n_keys, LANES) * LANES), jnp.float32),
                        pltpu.VMEM((n_heads * ts, n_keys * n_heads), jnp.float32)],
        compiler_params=_cparams(1),
        name="dsa_sample",
    )(qb, madd, cache_k.reshape(nb, past * n_heads, DSA_HEAD_DIM), cache_v.reshape(nb, past * n_heads, DSA_HEAD_DIM),
      new_k.reshape(nb * ts * n_heads, DSA_HEAD_DIM), new_v.reshape(nb * ts * n_heads, DSA_HEAD_DIM),
      bias_tab, expand)


def _out_kernel(oa_ref, ob_ref, ga_ref, gb_ref, x_ref, gate_ref, gna_ref, gnb_ref, w_ref, fg_ref, y_ref):
    def gated(o_ref, g_ref, gain_ref):
        o = o_ref[...].astype(jnp.float32)
        on = o * lax.rsqrt(jnp.mean(o * o, axis=-1, keepdims=True) + EPS) * gain_ref[...]
        return _mx(on * g_ref[...].astype(jnp.float32))

    wa = oa_ref.shape[1]
    out = _dot(gated(oa_ref, ga_ref, gna_ref), w_ref[0:wa, :]) + _dot(gated(ob_ref, gb_ref, gnb_ref), w_ref[wa:, :])
    xn = x_ref[...] + gate_ref[...] * out
    y_ref[...] = xn * lax.rsqrt(jnp.mean(xn * xn, axis=-1, keepdims=True) + EPS) * fg_ref[...]


def _out_proj(oa, ob, ga, gb, x3, gate, gain_a, gain_b, w_out, final_gain, tm):
    bv, tv, d = x3.shape
    nt = tv // tm
    wa = oa.shape[1]
    x_spec, g_spec = _row_specs(x3, gate.shape[1], tm)
    row = lambda b, i: (b * nt + i, 0)
    return pl.pallas_call(
        _out_kernel,
        grid=(bv, nt),
        in_specs=[pl.BlockSpec((tm, wa), row), pl.BlockSpec((tm, wa), row),
                  pl.BlockSpec((tm, wa), row), pl.BlockSpec((tm, wa), row),
                  x_spec, g_spec, _resident((1, wa)), _resident((1, wa)),
                  _resident(w_out.shape), _resident((1, d))],
        out_specs=pl.BlockSpec((None, tm, d), lambda b, i: (b, i, 0)),
        out_shape=jax.ShapeDtypeStruct((bv, tv, d), jnp.float32),
        compiler_params=_cparams(2),
        name="out_proj",
    )(oa, ob, ga, gb, x3, gate, gain_a.reshape(1, wa), gain_b.reshape(1, wa), w_out, final_gain.reshape(1, d))


def _pack_kernel(n_heads, wt_ref, qa_ref, ckv_ref, kr_ref, ga_ref, gb_ref, qb_ref, kb_ref, vb_ref, qi_ref, kiw_ref):
    wa = n_heads * MLA_V
    half = MLA_ROPE // 2
    q_head = MLA_NOPE + MLA_ROPE
    step = 2 * LANES

    def panel(ref, off):
        for c in range(0, ref.shape[1], step):
            w = min(step, ref.shape[1] - c)
            ref[:, c:c + w] = _mx(wt_ref[off + c:off + c + w, :].T)

    def dup_rope(off):
        x1, x2 = wt_ref[off:off + half, :], wt_ref[off + half:off + 2 * half, :]
        return _mx(jnp.concatenate([x1, x2, x2, x1], axis=0).T)

    for h in range(n_heads):
        src, dst = h * q_head, h * HEAD_SLOT
        qa_ref[:, dst:dst + LANES] = _mx(wt_ref[src:src + MLA_NOPE, :].T)
        qa_ref[:, dst + LANES:dst + HEAD_SLOT] = dup_rope(src + MLA_NOPE)
    off = n_heads * q_head
    panel(ckv_ref, off)
    off += KV_LORA
    kr_ref[...] = dup_rope(off)
    off += MLA_ROPE
    for ref in (ga_ref, qb_ref, kb_ref, vb_ref):
        panel(ref, off)
        off += wa
    pairs = IDX_HEADS // 2
    for j in range(pairs):
        lo = wt_ref[off + IDX_DIM * j:off + IDX_DIM * (j + 1), :]
        hi = wt_ref[off + IDX_DIM * (j + pairs):off + IDX_DIM * (j + pairs + 1), :]
        qi_ref[:, LANES * j:LANES * (j + 1)] = _mx(jnp.concatenate([lo, hi], axis=0).T)
    off += IDX_HEADS * IDX_DIM
    n_kw = IDX_DIM + IDX_HEADS
    kiw = jnp.concatenate([wt_ref[off:off + n_kw, :], jnp.zeros((LANES - n_kw, wt_ref.shape[1]), jnp.float32)], axis=0)
    kiw_ref[...] = _mx(kiw.T)
    off += n_kw
    panel(gb_ref, off)


def _pack_w_in(w_in, n_heads):
    wt = w_in.T
    n, d = wt.shape
    wa = n_heads * MLA_V
    assert n == n_heads * (MLA_NOPE + MLA_ROPE) + KV_LORA + MLA_ROPE + 5 * wa + IDX_HEADS * IDX_DIM + IDX_DIM + IDX_HEADS
    assert 2 * IDX_DIM == LANES and 2 * MLA_ROPE == LANES and MLA_NOPE == LANES
    slab = 256
    widths = [n_heads * HEAD_SLOT, KV_LORA, LANES, wa, wa, wa, wa, wa, IDX_HEADS * IDX_DIM, LANES]
    q_a, ckv, kr, g_a, g_b, q_b, k_b, v_b, q_i, kiw = pl.pallas_call(
        functools.partial(_pack_kernel, n_heads),
        grid=(d // slab,),
        in_specs=[pl.BlockSpec((n, slab), lambda i: (0, i))],
        out_specs=[pl.BlockSpec((slab, w), lambda i: (i, 0)) for w in widths],
        out_shape=[jax.ShapeDtypeStruct((d, w), MXU_DT) for w in widths],
        compiler_params=_cparams(1),
        name="pack_w_in",
    )(wt)
    return [q_a, ckv, kr, g_a, g_b], [q_b, k_b, v_b, q_i, kiw]


def _rope_table(pos):
    half = MLA_ROPE // 2
    freqs = jnp.power(ROPE_THETA, -jnp.arange(half, dtype=jnp.float32) / half)
    ang = pos.astype(jnp.float32)[:, None] * freqs
    cos, sin = jnp.cos(ang), jnp.sin(ang)
    return jnp.concatenate([cos, cos, -sin, sin], axis=1)


def _rel_bucket(rel):
    nb = N_BUCKETS // 2
    max_exact = nb // 2
    n = jnp.abs(rel)
    nf = jnp.maximum(n, 1).astype(jnp.float32)
    large = max_exact + (jnp.log(nf / max_exact) / math.log(MAX_DISTANCE / max_exact)
                         * (nb - max_exact)).astype(jnp.int32)
    large = jnp.minimum(large, nb - 1)
    return jnp.where(rel > 0, nb, 0) + jnp.where(n < max_exact, n, large)


def _bucket_bias(rel_bias, rel):
    return (rel_bias * LOG2E)[_rel_bucket(rel)]


def kernel(x_prompt, x_sample, cache_mla_ckv, cache_mla_krope, cache_dsa_k, cache_dsa_v, cache_idx_k,
           c_prompt, c_sample, w_ada, b_ada, ln_gain, w_in, mla_kv_gain, w_uk, w_uv, rel_bias,
           out_gain_a, out_gain_b, w_out, final_gain):
    assert w_ada.shape[0] == 1, "single-layer step"
    b, t, d = x_prompt.shape
    nb, ts, _ = x_sample.shape
    past = cache_mla_ckv.shape[2]
    n_heads = d // 256
    wb = n_heads * DSA_HEAD_DIM
    s_all = past + ts
    assert t % MLA_TQ == 0 and t % DSA_TQ == 0 and MLA_TQ % CHUNK == 0 and DSA_TQ % CHUNK == 0
    assert (s_all - 1) // CHUNK <= past // CHUNK

    mod = _ada_mod(jnp.concatenate([c_prompt, c_sample], axis=0), w_ada[0], b_ada[0])
    shift, scale, gate = mod[:, :d], mod[:, d:2 * d], mod[:, 2 * d:]
    mod_p = [v[:b].reshape(b, 1, d) for v in (scale, shift, gate)]
    mod_s = [jnp.broadcast_to(v[b:, None, :], (nb, ts, d)).reshape(1, nb * ts, d) for v in (scale, shift, gate)]

    w_a, w_b = _pack_w_in(w_in[0], n_heads)
    wkv = _mx(jnp.concatenate([w_uk[0].transpose(2, 0, 1).reshape(KV_LORA, n_heads * MLA_NOPE),
                               w_uv[0].transpose(2, 0, 1).reshape(KV_LORA, n_heads * MLA_V)], axis=1))
    w_o = _mx(w_out[0])
    pos_p = jnp.arange(t, dtype=jnp.int32)
    pos_s = past + jnp.arange(ts, dtype=jnp.int32)
    tab_p = _rope_table(pos_p)
    tab_s = jnp.tile(_rope_table(pos_s), (nb, 1))

    tm = 256
    qcat, ckv_p, krope_p, ga, gb, kcat, vup, oa_buf = _proj_a(
        x_prompt, mod_p[0], mod_p[1], ln_gain[0], tab_p, w_a, mla_kv_gain[0], wkv, tm)
    qb, k_p, v_p, kbb, vbb, qi, idxk_p, kiab, wi, ob_buf = _proj_b(
        x_prompt, mod_p[0], mod_p[1], ln_gain[0], w_b, tm, True)
    o_a = _mla_prompt(qcat, kcat, vup, oa_buf, b, t)
    near = _bucket_bias(rel_bias, jnp.arange(-MAX_DISTANCE - 1, MAX_DISTANCE, dtype=jnp.int32))
    far = near[:1]
    bias_tab = jnp.concatenate([jnp.broadcast_to(far, (t - 1 - MAX_DISTANCE, n_heads)), near[1:],
                                jnp.broadcast_to(far, (t + 1 - MAX_DISTANCE, n_heads))], axis=0).T
    o_b = _dsa_prompt(qb, qi, wi, kbb, vbb, kiab, bias_tab, ob_buf, b, t, min(TOPK_MAX, t // 4))
    y_prompt = _out_proj(o_a, o_b, ga, gb, x_prompt, mod_p[2], out_gain_a[0], out_gain_b[0], w_o, final_gain, 2 * tm)

    xs3 = x_sample.reshape(1, nb * ts, d)
    tms = min(256, nb * ts)
    qcat_s, ckv_s, krope_s, ga_s, gb_s = _proj_a(
        xs3, mod_s[0], mod_s[1], ln_gain[0], tab_s, w_a, mla_kv_gain[0], None, tms)
    qb_s, k_s, v_s, kbb_s, vbb_s, qi_s, idxk_s, kiab_s, wi_s = _proj_b(
        xs3, mod_s[0], mod_s[1], ln_gain[0], w_b, tms, False)
    oa_s = _mla_sample(qcat_s, cache_mla_ckv[0], cache_mla_krope[0], ckv_s, krope_s,
                       _mx(w_uk[0]), _mx(w_uv[0]), ts)
    s_pad = pl.cdiv(s_all, LANES) * LANES
    tab_s = _bucket_bias(rel_bias, jnp.arange(s_pad + LANES, dtype=jnp.int32) - (s_all - 1)).T
    madd_s = _dsa_select(qi_s, wi_s, cache_idx_k[0], kiab_s, ts, min(TOPK_MAX, s_all // 4))
    ob_s = _dsa_sample(qb_s, madd_s, cache_dsa_k[0], cache_dsa_v[0], kbb_s, vbb_s, tab_s, ts)
    y_sample = _out_proj(oa_s, ob_s, ga_s, gb_s, xs3, mod_s[2], out_gain_a[0], out_gain_b[0], w_o, final_gain, tms)

    hd = (n_heads, DSA_HEAD_DIM)
    return (y_prompt, y_sample.reshape(nb, ts, d),
            ckv_p.reshape(1, b, t, KV_LORA), krope_p.reshape(1, b, t, MLA_ROPE),
            k_p.reshape(1, b, t, *hd), v_p.reshape(1, b, t, *hd), idxk_p.reshape(1, b, t, IDX_DIM),
            ckv_s.reshape(1, nb, ts, KV_LORA), krope_s.reshape(1, nb, ts, MLA_ROPE),
            k_s.reshape(1, nb, ts, *hd), v_s.reshape(1, nb, ts, *hd), idxk_s.reshape(1, nb, ts, IDX_DIM))
```

```python
import functools
import math

import jax
import jax.numpy as jnp
from jax import lax
from jax.experimental import pallas as pl
from jax.experimental.pallas import tpu as pltpu

MXU_DT = jnp.bfloat16

CHUNK = 64
MLA_NOPE = 128
MLA_ROPE = 64
MLA_V = 128
KV_LORA = 512
DSA_HEAD_DIM = 128
IDX_HEADS = 16
IDX_DIM = 64
TOPK_MAX = 256
N_BUCKETS = 32
MAX_DISTANCE = 128
ROPE_THETA = 10000.0
EPS = 1e-6

LANES = 128
HEAD_SLOT = 256
NEG_BIG = -1e30
LOG2E = math.log2(math.e)
INT_MIN = -2 ** 31
KEY_NEG_INF = INT_MIN + 0x7FFFFF
VMEM_LIMIT = 56 * 1024 * 1024


def _cparams(n_grid, vmem=VMEM_LIMIT):
    return pltpu.CompilerParams(dimension_semantics=("arbitrary",) * n_grid, vmem_limit_bytes=vmem)


def _mx(v):
    return v.astype(MXU_DT)


def _dot(a, b):
    return jnp.dot(a, b, preferred_element_type=jnp.float32)


def _dot_nt(a, b):
    return lax.dot_general(a, b, (((1,), (1,)), ((), ())), preferred_element_type=jnp.float32)


def _silu(v):
    return v * (1.0 / (1.0 + jnp.exp(-v)))


def _resident(shape):
    nd = len(shape)
    return pl.BlockSpec(shape, lambda *_: (0,) * nd, pipeline_mode=pl.Buffered(1))


def _ada_kernel(c_ref, w_ref, b_ref, o_ref):
    a = _mx(_silu(c_ref[...]))
    o_ref[...] = _dot(a, _mx(w_ref[...])) + b_ref[...]


def _ada_mod(c_all, w_ada, b_ada):
    m, d = c_all.shape
    n = w_ada.shape[1]
    tn = 1024
    return pl.pallas_call(
        _ada_kernel,
        grid=(n // tn,),
        in_specs=[pl.BlockSpec((m, d), lambda j: (0, 0)),
                  pl.BlockSpec((d, tn), lambda j: (0, j)),
                  pl.BlockSpec((1, tn), lambda j: (0, j))],
        out_specs=pl.BlockSpec((m, tn), lambda j: (0, j)),
        out_shape=jax.ShapeDtypeStruct((m, n), jnp.float32),
        compiler_params=_cparams(1),
        name="ada_mod",
    )(c_all, w_ada, b_ada.reshape(1, n))


def _modulated_norm(x_ref, scale_ref, shift_ref, lng_ref):
    x = x_ref[...]
    xn = x * lax.rsqrt(jnp.mean(x * x, axis=-1, keepdims=True) + EPS) * lng_ref[...]
    return _mx(xn * (1.0 + scale_ref[...]) + shift_ref[...])


def _rope128(a, tab):
    t = a * tab
    return t + pltpu.roll(t, 64, 1)


def _proj_q_kernel(n_heads, x_ref, scale_ref, shift_ref, lng_ref, tab_ref, wq_ref, qcat_ref, *attn_ref):
    if attn_ref:
        attn_ref[0][...] = jnp.zeros_like(attn_ref[0])
    hb = _modulated_norm(x_ref, scale_ref, shift_ref, lng_ref)
    tab = tab_ref[...]
    qscale = (MLA_NOPE + MLA_ROPE) ** -0.5 * LOG2E
    for h in range(n_heads):
        c0 = h * HEAD_SLOT
        a = _dot(hb, wq_ref[:, c0:c0 + HEAD_SLOT])
        qcat_ref[:, c0:c0 + LANES] = _mx(a[:, :LANES] * qscale)
        qcat_ref[:, c0 + LANES:c0 + HEAD_SLOT] = _mx(_rope128(a[:, LANES:], tab) * qscale)


def _proj_a_kernel(n_heads, emit_kv, x_ref, scale_ref, shift_ref, lng_ref, tab_ref,
                   wc_ref, wr_ref, wga_ref, wgb_ref, kvg_ref, *rest):
    if emit_kv:
        wkv_ref, ckv_ref, krope_ref, ga_ref, gb_ref, kcat_ref, vup_ref = rest
    else:
        ckv_ref, krope_ref, ga_ref, gb_ref = rest
    hb = _modulated_norm(x_ref, scale_ref, shift_ref, lng_ref)
    tab = tab_ref[...]
    c = _dot(hb, wc_ref[...])
    cn = c * lax.rsqrt(jnp.mean(c * c, axis=-1, keepdims=True) + EPS) * kvg_ref[...]
    ckv_ref[...] = cn
    r = _rope128(_dot(hb, wr_ref[...]), tab)
    krope_ref[...] = r[:, :MLA_ROPE]
    ga_ref[...] = _mx(_silu(_dot(hb, wga_ref[...])))
    gb_ref[...] = _mx(_silu(_dot(hb, wgb_ref[...])))
    if emit_kv:
        cb = _mx(cn)
        lane = lax.broadcasted_iota(jnp.int32, r.shape, 1)
        krz = _mx(jnp.where(lane < MLA_ROPE, r, 0.0))
        kn = _dot(cb, wkv_ref[:, :n_heads * MLA_NOPE])
        for h in range(n_heads):
            c0 = h * HEAD_SLOT
            kcat_ref[:, c0:c0 + LANES] = _mx(kn[:, h * MLA_NOPE:(h + 1) * MLA_NOPE])
            kcat_ref[:, c0 + LANES:c0 + HEAD_SLOT] = krz
        vup_ref[...] = _mx(_dot(cb, wkv_ref[:, n_heads * MLA_NOPE:]))


def _proj_b_kernel(n_heads, x_ref, scale_ref, shift_ref, lng_ref, wq_ref, wk_ref, wv_ref, wqi_ref, wkw_ref,
                   qb_ref, kb_ref, vb_ref, kbb_ref, vbb_ref, qi_ref, ki_ref, kiab_ref, wi_ref, *attn_ref):
    if attn_ref:
        attn_ref[0][...] = jnp.zeros_like(attn_ref[0])
    hb = _modulated_norm(x_ref, scale_ref, shift_ref, lng_ref)
    dscale = DSA_HEAD_DIM ** -0.5 * LOG2E
    step = 512
    for c0 in range(0, wq_ref.shape[1], step):
        qb_ref[:, c0:c0 + step] = _mx(_dot(hb, wq_ref[:, c0:c0 + step]) * dscale)
    for w_ref, f_ref, b_ref in ((wk_ref, kb_ref, kbb_ref), (wv_ref, vb_ref, vbb_ref)):
        for c0 in range(0, w_ref.shape[1], step):
            kv = _dot(hb, w_ref[:, c0:c0 + step])
            f_ref[:, c0:c0 + step] = kv
            b_ref[:, c0:c0 + step] = _mx(kv)
    for c0 in range(0, wqi_ref.shape[1], step):
        qi_ref[:, c0:c0 + step] = _mx(_dot(hb, wqi_ref[:, c0:c0 + step]))
    a = _dot(hb, wkw_ref[...])
    ki_ref[...] = a[:, :IDX_DIM]
    lane = lax.broadcasted_iota(jnp.int32, a.shape, 1)
    kz = jnp.where(lane < IDX_DIM, a, 0.0)
    kiab_ref[:, :LANES] = _mx(kz)
    kiab_ref[:, LANES:] = _mx(pltpu.roll(kz, IDX_DIM, 1))
    wi_ref[...] = a * (IDX_HEADS ** -0.5 * IDX_DIM ** -0.5)


def _row_specs(x3, mod_rows, tm):
    bv, tv, d = x3.shape
    x_spec = pl.BlockSpec((None, tm, d), lambda b, i: (b, i, 0))
    if mod_rows == 1:
        m_spec = pl.BlockSpec((None, 1, d), lambda b, i: (b, 0, 0))
    else:
        m_spec = pl.BlockSpec((None, tm, d), lambda b, i: (b, i, 0))
    return x_spec, m_spec


def _out2d(m, width, dtype, tm, nt):
    return (jax.ShapeDtypeStruct((m, width), dtype),
            pl.BlockSpec((tm, width), lambda b, i: (b * nt + i, 0)))


def _proj_q(x3, scale, shift, ln_gain, tab, w_q, tm, emit_attn_buffer):
    bv, tv, d = x3.shape
    nt = tv // tm
    m = bv * tv
    n_heads = d // 256
    x_spec, m_spec = _row_specs(x3, scale.shape[1], tm)
    outs = [_out2d(m, n_heads * HEAD_SLOT, MXU_DT, tm, nt)]
    if emit_attn_buffer:
        outs.append(_out2d(m, n_heads * MLA_V, MXU_DT, tm, nt))
    return pl.pallas_call(
        functools.partial(_proj_q_kernel, n_heads),
        grid=(bv, nt),
        in_specs=[x_spec, m_spec, m_spec, _resident((1, d)), pl.BlockSpec((tm, LANES), lambda b, i: (i, 0)),
                  _resident(w_q.shape)],
        out_specs=[o[1] for o in outs],
        out_shape=[o[0] for o in outs],
        compiler_params=_cparams(2),
        name="proj_q",
    )(x3, scale, shift, ln_gain.reshape(1, d), tab, w_q)


def _proj_a(x3, scale, shift, ln_gain, tab, ws_a, kv_gain, wkv, tm):
    bv, tv, d = x3.shape
    nt = tv // tm
    m = bv * tv
    n_heads = d // 256
    wa = n_heads * MLA_V
    emit_kv = wkv is not None
    x_spec, m_spec = _row_specs(x3, scale.shape[1], tm)
    in_specs = ([x_spec, m_spec, m_spec, _resident((1, d)), pl.BlockSpec((tm, LANES), lambda b, i: (i, 0))]
                + [_resident(w.shape) for w in ws_a] + [_resident((1, KV_LORA))])
    args = [x3, scale, shift, ln_gain.reshape(1, d), tab, *ws_a, kv_gain.reshape(1, KV_LORA)]
    outs = [_out2d(m, KV_LORA, jnp.float32, tm, nt),
            _out2d(m, MLA_ROPE, jnp.float32, tm, nt),
            _out2d(m, wa, MXU_DT, tm, nt),
            _out2d(m, wa, MXU_DT, tm, nt)]
    if emit_kv:
        in_specs.append(_resident(wkv.shape))
        args.append(wkv)
        outs += [_out2d(m, n_heads * HEAD_SLOT, MXU_DT, tm, nt),
                 _out2d(m, wa, MXU_DT, tm, nt)]
    return pl.pallas_call(
        functools.partial(_proj_a_kernel, n_heads, emit_kv),
        grid=(bv, nt),
        in_specs=in_specs,
        out_specs=[o[1] for o in outs],
        out_shape=[o[0] for o in outs],
        compiler_params=_cparams(2),
        name="proj_a",
    )(*args)


def _proj_b(x3, scale, shift, ln_gain, ws_b, tm, emit_attn_buffer):
    bv, tv, d = x3.shape
    nt = tv // tm
    m = bv * tv
    n_heads = d // 256
    width_b = n_heads * DSA_HEAD_DIM
    x_spec, m_spec = _row_specs(x3, scale.shape[1], tm)
    outs = [_out2d(m, width_b, MXU_DT, tm, nt),
            _out2d(m, width_b, jnp.float32, tm, nt),
            _out2d(m, width_b, jnp.float32, tm, nt),
            _out2d(m, width_b, MXU_DT, tm, nt),
            _out2d(m, width_b, MXU_DT, tm, nt),
            _out2d(m, IDX_HEADS * IDX_DIM, MXU_DT, tm, nt),
            _out2d(m, IDX_DIM, jnp.float32, tm, nt),
            _out2d(m, 2 * LANES, MXU_DT, tm, nt),
            _out2d(m, LANES, jnp.float32, tm, nt)]
    if emit_attn_buffer:
        outs.append(_out2d(m, width_b, MXU_DT, tm, nt))
    return pl.pallas_call(
        functools.partial(_proj_b_kernel, n_heads),
        grid=(bv, nt),
        in_specs=[x_spec, m_spec, m_spec, _resident((1, d))] + [_resident(w.shape) for w in ws_b],
        out_specs=[o[1] for o in outs],
        out_shape=[o[0] for o in outs],
        compiler_params=_cparams(2),
        name="proj_b",
    )(x3, scale, shift, ln_gain.reshape(1, d), *ws_b)


def _softmax_pv(s, v):
    m = jnp.max(s, axis=-1, keepdims=True)
    p = jnp.exp2(s - m)
    l = jnp.sum(p, axis=-1, keepdims=True)
    return _dot(_mx(p), v) * (1.0 / l)


def _topk_mask(score, adm, topk, sc_ref, madd_ref):
    rows, n = score.shape
    if n <= topk:
        madd_ref[...] = jnp.where(adm, 0.0, NEG_BIG)
        return
    sc_ref[...] = jnp.where(adm, score, -jnp.inf)
    kf = float(topk)

    def count(pred):
        return jnp.sum(jnp.where(pred, 1.0, 0.0), axis=-1, keepdims=True)

    def key_to_float(key):
        return pltpu.bitcast(key ^ ((key >> 31) & 0x7FFFFFFF), jnp.float32)

    def bit_step(i, thr):
        inc = lax.shift_left(jnp.int32(1), 31 - i)
        cand = thr + inc
        ok = count(sc_ref[...] >= key_to_float(cand)) >= kf
        return jnp.where(ok, cand, thr)

    thr = lax.fori_loop(0, 32, bit_step, jnp.full((rows, 1), INT_MIN, jnp.int32))
    thr = key_to_float(jnp.maximum(thr, KEY_NEG_INF))
    ge = sc_ref[...] >= thr
    madd_ref[...] = jnp.where(ge, 0.0, NEG_BIG)

    @pl.when(jnp.max(count(ge)) > kf)
    def _():
        sc = sc_ref[...]
        gt = sc > thr
        need = kf - count(gt)
        eqf = jnp.where(sc == thr, 1.0, 0.0)
        col = lax.broadcasted_iota(jnp.int32, (rows, n), 1)
        nbits = max(1, int(n).bit_length())

        def col_step(i, bound):
            cand = bound + lax.shift_left(jnp.int32(1), nbits - 1 - i)
            taken = jnp.sum(jnp.where(col < cand, eqf, 0.0), axis=-1, keepdims=True)
            return jnp.where(taken <= need, cand, bound)

        bound = lax.fori_loop(0, nbits, col_step, jnp.zeros((rows, 1), jnp.int32))
        tie_madd = jnp.where(jnp.where(col < bound, eqf, 0.0) > 0.5, 0.0, NEG_BIG)
        madd_ref[...] = jnp.where(adm, jnp.where(gt, 0.0, tie_madd), NEG_BIG)


def _indexer_scores(qi_ref, wi, kia, kib):
    half = IDX_HEADS // 2
    score = None
    for j in range(half):
        qp = qi_ref[:, j * LANES:(j + 1) * LANES]
        da = jnp.maximum(_dot_nt(qp, kia), 0.0) * wi[:, IDX_DIM + j:IDX_DIM + j + 1]
        db = jnp.maximum(_dot_nt(qp, kib), 0.0) * wi[:, IDX_DIM + half + j:IDX_DIM + half + j + 1]
        score = da + db if score is None else score + da + db
    return score


def _indexer_scores_stacked(qi_ref, wi, kia, kib):
    half = IDX_HEADS // 2
    ts = qi_ref.shape[0]
    q_all = jnp.concatenate([qi_ref[:, j * LANES:(j + 1) * LANES] for j in range(half)], axis=0)
    wa = jnp.concatenate([wi[:, IDX_DIM + j:IDX_DIM + j + 1] for j in range(half)], axis=0)
    wb = jnp.concatenate([wi[:, IDX_DIM + half + j:IDX_DIM + half + j + 1] for j in range(half)], axis=0)
    part = jnp.maximum(_dot_nt(q_all, kia), 0.0) * wa + jnp.maximum(_dot_nt(q_all, kib), 0.0) * wb
    score = part[0:ts]
    for j in range(1, half):
        score = score + part[j * ts:(j + 1) * ts]
    return score


MLA_TQ = 512
DSA_TQ = 256


def _chunk_madd(row0, tq, n_keys):
    qc = (row0 + lax.broadcasted_iota(jnp.int32, (tq, n_keys), 0)) // CHUNK
    kc = lax.broadcasted_iota(jnp.int32, (tq, n_keys), 1) // CHUNK
    return kc <= qc


def _chained_tile_calls(n_tiles, make_call, out):
    for c in range(n_tiles):
        out = make_call(c, out)
    return out


def _mla_prompt_kernel(n_heads, row0, q_ref, k_ref, v_ref, prev_ref, o_ref):
    del prev_ref
    tq, n_keys = q_ref.shape[0], k_ref.shape[0]
    madd = jnp.where(_chunk_madd(row0, tq, n_keys), 0.0, NEG_BIG)
    for h in range(n_heads):
        c0 = h * HEAD_SLOT
        s = _dot_nt(q_ref[:, c0:c0 + HEAD_SLOT], k_ref[:, c0:c0 + HEAD_SLOT]) + madd
        o_ref[:, h * MLA_V:(h + 1) * MLA_V] = _mx(_softmax_pv(s, v_ref[:, h * MLA_V:(h + 1) * MLA_V]))


def _mla_prompt(qcat, kcat, vup, out_init, b, t):
    tq = MLA_TQ
    n_heads = qcat.shape[1] // HEAD_SLOT
    nq = t // tq
    wa = n_heads * MLA_V
    k3 = kcat.reshape(b, t, kcat.shape[1])
    v3 = vup.reshape(b, t, wa)

    def make_call(c, out):
        n_keys = (c + 1) * tq
        row = lambda bi: (bi * nq + c, 0)
        return pl.pallas_call(
            functools.partial(_mla_prompt_kernel, n_heads, c * tq),
            grid=(b,),
            in_specs=[pl.BlockSpec((tq, qcat.shape[1]), row),
                      pl.BlockSpec((None, n_keys, kcat.shape[1]), lambda bi: (bi, 0, 0)),
                      pl.BlockSpec((None, n_keys, wa), lambda bi: (bi, 0, 0)),
                      pl.BlockSpec(memory_space=pl.ANY)],
            out_specs=pl.BlockSpec((tq, wa), row),
            out_shape=jax.ShapeDtypeStruct(out.shape, out.dtype),
            input_output_aliases={3: 0},
            compiler_params=_cparams(1),
            name="mla_prompt_%d" % c,
        )(qcat, k3, v3, out)

    return _chained_tile_calls(nq, make_call, out_init)


def _toeplitz_bias(tab_ref, h, start, n_rows, n_cols):
    shift0 = LANES - (n_rows - 1)
    from_left = (lax.broadcasted_iota(jnp.int32, (n_rows, LANES), 1)
                 < shift0 + lax.broadcasted_iota(jnp.int32, (n_rows, LANES), 0))
    pieces = []
    prev = None
    for n in range(n_cols // LANES + 1):
        seg = tab_ref[h:h + 1, start + n * LANES:start + (n + 1) * LANES]
        rot = pltpu.roll(jnp.broadcast_to(seg, (n_rows, LANES)), shift0 % LANES, 1, stride=1, stride_axis=0)
        if prev is not None:
            pieces.append(jnp.where(from_left, prev, rot))
        prev = rot
    return jnp.concatenate(pieces, axis=1)


def _dsa_prompt_kernel(n_heads, row0, t, topk, qb_ref, qi_ref, wi_ref, k_ref, v_ref, kiab_ref, tab_ref, prev_ref,
                       o_ref, key_ref, madd_ref):
    del prev_ref
    tq, n_keys = qb_ref.shape[0], k_ref.shape[0]
    score = _indexer_scores(qi_ref, wi_ref[...], kiab_ref[:, :LANES], kiab_ref[:, LANES:])
    _topk_mask(score, _chunk_madd(row0, tq, n_keys), topk, key_ref, madd_ref)
    near0 = max(0, row0 - LANES)
    n_near = n_keys - near0
    for h in range(n_heads):
        c0 = h * DSA_HEAD_DIM
        near = jnp.concatenate(
            [_toeplitz_bias(tab_ref, h, near0 - (row0 + u * LANES) - LANES + t, LANES, n_near)
             for u in range(tq // LANES)], axis=0)
        if near0 > 0:
            far = jnp.broadcast_to(tab_ref[h:h + 1, 0:1], (tq, near0))
            bias = jnp.concatenate([far, near], axis=1)
        else:
            bias = near
        s = _dot_nt(qb_ref[:, c0:c0 + DSA_HEAD_DIM], k_ref[:, c0:c0 + DSA_HEAD_DIM]) + bias + madd_ref[...]
        o_ref[:, c0:c0 + DSA_HEAD_DIM] = _mx(_softmax_pv(s, v_ref[:, c0:c0 + DSA_HEAD_DIM]))


def _dsa_prompt(qb, qi, wi, kbb, vbb, kiab, bias_tab, out_init, b, t, topk):
    tq = DSA_TQ
    assert MAX_DISTANCE <= LANES and tq % LANES == 0
    n_heads = qb.shape[1] // DSA_HEAD_DIM
    nq = t // tq
    wb = qb.shape[1]
    k3, v3, ki3 = kbb.reshape(b, t, wb), vbb.reshape(b, t, wb), kiab.reshape(b, t, 2 * LANES)
    per_b = lambda bi: (bi, 0, 0)

    def make_call(c, out):
        n_keys = (c + 1) * tq
        row = lambda bi: (bi * nq + c, 0)
        return pl.pallas_call(
            functools.partial(_dsa_prompt_kernel, n_heads, c * tq, t, topk),
            grid=(b,),
            in_specs=[pl.BlockSpec((tq, wb), row),
                      pl.BlockSpec((tq, qi.shape[1]), row),
                      pl.BlockSpec((tq, LANES), row),
                      pl.BlockSpec((None, n_keys, wb), per_b),
                      pl.BlockSpec((None, n_keys, wb), per_b),
                      pl.BlockSpec((None, n_keys, 2 * LANES), per_b),
                      _resident(bias_tab.shape),
                      pl.BlockSpec(memory_space=pl.ANY)],
            out_specs=pl.BlockSpec((tq, wb), row),
            out_shape=jax.ShapeDtypeStruct(out.shape, out.dtype),
            input_output_aliases={7: 0},
            scratch_shapes=[pltpu.VMEM((tq, n_keys), jnp.float32), pltpu.VMEM((tq, n_keys), jnp.float32)],
            compiler_params=_cparams(1),
            name="dsa_prompt_%d" % c,
        )(qb, qi, wi, k3, v3, ki3, bias_tab, out)

    return _chained_tile_calls(nq, make_call, out_init)


def _mla_sample_kernel(n_heads, past, ts, qcat_ref, cckv_ref, ckr_ref, nckv_ref, nkr_ref, wuk_ref, wuv_ref,
                       o_ref, kall_ref, rall_ref):
    s_pad = kall_ref.shape[0]
    n_keys = past + ts

    @pl.when(pl.program_id(0) == 0)
    def _():
        kall_ref[...] = jnp.zeros_like(kall_ref)
        rall_ref[...] = jnp.zeros_like(rall_ref)

    kall_ref[0:past, :] = _mx(cckv_ref[...])
    kall_ref[past:n_keys, :] = _mx(nckv_ref[...])
    rall_ref[0:past, 0:MLA_ROPE] = _mx(ckr_ref[...])
    rall_ref[past:n_keys, 0:MLA_ROPE] = _mx(nkr_ref[...])
    qlat, qrope = [], []
    for h in range(n_heads):
        c0 = h * HEAD_SLOT
        qlat.append(_mx(_dot(qcat_ref[:, c0:c0 + LANES], wuk_ref[h])))
        qrope.append(qcat_ref[:, c0 + LANES:c0 + HEAD_SLOT])
    qlat = jnp.concatenate(qlat, axis=0)
    qrope = jnp.concatenate(qrope, axis=0)
    s = _dot_nt(qlat, kall_ref[...]) + _dot_nt(qrope, rall_ref[...])
    col = lax.broadcasted_iota(jnp.int32, s.shape, 1)
    s = jnp.where(col < n_keys, s, NEG_BIG)
    olat = _mx(_softmax_pv(s, kall_ref[...]))
    for h in range(n_heads):
        o_ref[:, h * MLA_V:(h + 1) * MLA_V] = _mx(_dot_nt(olat[h * ts:(h + 1) * ts], wuv_ref[h]))


def _mla_sample(qcat, cache_ckv, cache_kr, new_ckv, new_kr, wuk, wuv, ts):
    nb, past, c = cache_ckv.shape
    n_heads = qcat.shape[1] // HEAD_SLOT
    wa = n_heads * MLA_V
    s_pad = pl.cdiv(past + ts, LANES) * LANES
    row = lambda bi: (bi, 0)
    return pl.pallas_call(
        functools.partial(_mla_sample_kernel, n_heads, past, ts),
        grid=(nb,),
        in_specs=[pl.BlockSpec((ts, qcat.shape[1]), row),
                  pl.BlockSpec((None, past, c), lambda bi: (bi, 0, 0)),
                  pl.BlockSpec((None, past, MLA_ROPE), lambda bi: (bi, 0, 0)),
                  pl.BlockSpec((ts, c), row),
                  pl.BlockSpec((ts, MLA_ROPE), row),
                  _resident(wuk.shape), _resident(wuv.shape)],
        out_specs=pl.BlockSpec((ts, wa), row),
        out_shape=jax.ShapeDtypeStruct((nb * ts, wa), MXU_DT),
        scratch_shapes=[pltpu.VMEM((s_pad, c), MXU_DT), pltpu.VMEM((s_pad, LANES), MXU_DT)],
        compiler_params=_cparams(1),
        name="mla_sample",
    )(qcat, cache_ckv, cache_kr, new_ckv, new_kr, wuk, wuv)


def _dsa_select_kernel(past, ts, topk, qi_ref, wi_ref, cki_ref, nkiab_ref, madd_ref, kia_ref, kib_ref, key_ref):
    bi = pl.program_id(0)
    n_keys = past + ts
    rows, s_pad = madd_ref.shape

    @pl.when(bi == 0)
    def _():
        kia_ref[...] = jnp.zeros_like(kia_ref)
        kib_ref[...] = jnp.zeros_like(kib_ref)

    cki = _mx(cki_ref[...])
    kia_ref[0:past, 0:IDX_DIM] = cki
    kib_ref[0:past, IDX_DIM:LANES] = cki
    kia_ref[past:n_keys, :] = nkiab_ref[:, :LANES]
    kib_ref[past:n_keys, :] = nkiab_ref[:, LANES:]
    madd_ref[pl.ds(pl.multiple_of(bi * ts, ts), ts), :] = _indexer_scores_stacked(
        qi_ref, wi_ref[...], kia_ref[...], kib_ref[...])

    @pl.when(bi == pl.num_programs(0) - 1)
    def _():
        col = lax.broadcasted_iota(jnp.int32, (rows, s_pad), 1)
        _topk_mask(madd_ref[...], col < n_keys, topk, key_ref, madd_ref)


def _dsa_select(qi, wi, cache_ki, new_kiab, ts, topk):
    nb, past, _ = cache_ki.shape
    s_pad = pl.cdiv(past + ts, LANES) * LANES
    row = lambda bi: (bi, 0)
    return pl.pallas_call(
        functools.partial(_dsa_select_kernel, past, ts, topk),
        grid=(nb,),
        in_specs=[pl.BlockSpec((ts, qi.shape[1]), row),
                  pl.BlockSpec((ts, LANES), row),
                  pl.BlockSpec((None, past, IDX_DIM), lambda bi: (bi, 0, 0)),
                  pl.BlockSpec((ts, 2 * LANES), row)],
        out_specs=pl.BlockSpec((nb * ts, s_pad), lambda bi: (0, 0)),
        out_shape=jax.ShapeDtypeStruct((nb * ts, s_pad), jnp.float32),
        scratch_shapes=[pltpu.VMEM((s_pad, LANES), MXU_DT), pltpu.VMEM((s_pad, LANES), MXU_DT),
                        pltpu.VMEM((nb * ts, s_pad), jnp.float32)],
        compiler_params=_cparams(1),
        name="dsa_select",
    )(qi, wi, cache_ki, new_kiab)


def _dsa_sample_kernel(n_heads, past, ts, qb_ref, madd_ref, ck_ref, cv_ref, nk_ref, nv_ref, tab_ref, expand_ref,
                       o_ref, kflat_ref, vflat_ref, bias_ref, biasw_ref):
    n_keys = past + ts
    wide = n_heads * LANES
    n_blocks = pl.cdiv(n_keys, LANES)
    widths = [min(wide, (n_keys - j * LANES) * n_heads) for j in range(n_blocks)]

    @pl.when(pl.program_id(0) == 0)
    def _():
        for h in range(n_heads):
            bias_ref[h * ts:(h + 1) * ts, :] = _toeplitz_bias(tab_ref, h, 0, ts, n_blocks * LANES)
        shape = (n_heads * ts, wide)
        same_head = (lax.broadcasted_iota(jnp.int32, shape, 0) // ts
                     == lax.broadcasted_iota(jnp.int32, shape, 1) % n_heads)
        for j in range(n_blocks):
            b = bias_ref[:, j * LANES:(j + 1) * LANES]
            hi = _mx(b)
            rest = b - hi.astype(jnp.float32)
            mid = _mx(rest)
            lo = _mx(rest - mid.astype(jnp.float32))
            piece = _dot(hi, expand_ref[...]) + _dot(mid, expand_ref[...]) + _dot(lo, expand_ref[...])
            biasw_ref[:, j * wide:j * wide + widths[j]] = jnp.where(same_head, piece, NEG_BIG)[:, :widths[j]]

    kflat_ref[0:past * n_heads, :] = _mx(ck_ref[...])
    kflat_ref[past * n_heads:, :] = nk_ref[...]
    vflat_ref[0:past * n_heads, :] = _mx(cv_ref[...])
    vflat_ref[past * n_heads:, :] = nv_ref[...]
    sel = _mx(jnp.where(madd_ref[...] == 0.0, 1.0, 0.0))
    pieces = [_dot(sel[:, j * LANES:(j + 1) * LANES], expand_ref[...])[:, :widths[j]] for j in range(n_blocks)]
    sel_wide = jnp.concatenate(pieces, axis=1)
    sel_wide = jnp.concatenate([sel_wide] * n_heads, axis=0)
    q_all = jnp.concatenate([qb_ref[:, h * DSA_HEAD_DIM:(h + 1) * DSA_HEAD_DIM] for h in range(n_heads)], axis=0)
    s = _dot_nt(q_all, kflat_ref[...]) + biasw_ref[...] + jnp.where(sel_wide > 0.5, 0.0, NEG_BIG)
    o = _mx(_softmax_pv(s, vflat_ref[...]))
    for h in range(n_heads):
        o_ref[:, h * DSA_HEAD_DIM:(h + 1) * DSA_HEAD_DIM] = o[h * ts:(h + 1) * ts]


def _dsa_sample(qb, madd, cache_k, cache_v, new_k, new_v, bias_tab, ts):
    nb, past, n_heads, _ = cache_k.shape
    wb = n_heads * DSA_HEAD_DIM
    n_keys = past + ts
    assert (n_keys % LANES * n_heads) % LANES == 0
    expand = _mx(jnp.repeat(jnp.eye(LANES, dtype=jnp.float32), n_heads, axis=1))
    row = lambda bi: (bi, 0)
    per_b = lambda bi: (bi, 0, 0)
    return pl.pallas_call(
        functools.partial(_dsa_sample_kernel, n_heads, past, ts),
        grid=(nb,),
        in_specs=[pl.BlockSpec((ts, wb), row),
                  pl.BlockSpec((ts, madd.shape[1]), row),
                  pl.BlockSpec((None, past * n_heads, DSA_HEAD_DIM), per_b),
                  pl.BlockSpec((None, past * n_heads, DSA_HEAD_DIM), per_b),
                  pl.BlockSpec((ts * n_heads, DSA_HEAD_DIM), row),
                  pl.BlockSpec((ts * n_heads, DSA_HEAD_DIM), row),
                  _resident(bias_tab.shape), _resident(expand.shape)],
        out_specs=pl.BlockSpec((ts, wb), row),
        out_shape=jax.ShapeDtypeStruct((nb * ts, wb), MXU_DT),
        scratch_shapes=[pltpu.VMEM((n_keys * n_heads, DSA_HEAD_DIM), MXU_DT),
                        pltpu.VMEM((n_keys * n_heads, DSA_HEAD_DIM), MXU_DT),
                        pltpu.VMEM((n_heads * ts, pl.cdiv(n_keys, LANES) * LANES), jnp.float32),
                        pltpu.VMEM((n_heads * ts, n_keys * n_heads), jnp.float32)],
        compiler_params=_cparams(1),
        name="dsa_sample",
    )(qb, madd, cache_k.reshape(nb, past * n_heads, DSA_HEAD_DIM), cache_v.reshape(nb, past * n_heads, DSA_HEAD_DIM),
      new_k.reshape(nb * ts * n_heads, DSA_HEAD_DIM), new_v.reshape(nb * ts * n_heads, DSA_HEAD_DIM),
      bias_tab, expand)


def _out_kernel(oa_ref, ob_ref, ga_ref, gb_ref, x_ref, gate_ref, gna_ref, gnb_ref, w_ref, fg_ref, y_ref):
    def gated(o_ref, g_ref, gain_ref):
        o = o_ref[...].astype(jnp.float32)
        on = o * lax.rsqrt(jnp.mean(o * o, axis=-1, keepdims=True) + EPS) * gain_ref[...]
        return _mx(on * g_ref[...].astype(jnp.float32))

    wa = oa_ref.shape[1]
    out = _dot(gated(oa_ref, ga_ref, gna_ref), w_ref[0:wa, :]) + _dot(gated(ob_ref, gb_ref, gnb_ref), w_ref[wa:, :])
    xn = x_ref[...] + gate_ref[...] * out
    y_ref[...] = xn * lax.rsqrt(jnp.mean(xn * xn, axis=-1, keepdims=True) + EPS) * fg_ref[...]


def _out_proj(oa, ob, ga, gb, x3, gate, gain_a, gain_b, w_out, final_gain, tm):
    bv, tv, d = x3.shape
    nt = tv // tm
    wa = oa.shape[1]
    x_spec, g_spec = _row_specs(x3, gate.shape[1], tm)
    row = lambda b, i: (b * nt + i, 0)
    return pl.pallas_call(
        _out_kernel,
        grid=(bv, nt),
        in_specs=[pl.BlockSpec((tm, wa), row), pl.BlockSpec((tm, wa), row),
                  pl.BlockSpec((tm, wa), row), pl.BlockSpec((tm, wa), row),
                  x_spec, g_spec, _resident((1, wa)), _resident((1, wa)),
                  _resident(w_out.shape), _resident((1, d))],
        out_specs=pl.BlockSpec((None, tm, d), lambda b, i: (b, i, 0)),
        out_shape=jax.ShapeDtypeStruct((bv, tv, d), jnp.float32),
        compiler_params=_cparams(2),
        name="out_proj",
    )(oa, ob, ga, gb, x3, gate, gain_a.reshape(1, wa), gain_b.reshape(1, wa), w_out, final_gain.reshape(1, d))


def _pack_kernel(n_heads, wt_ref, qa_ref, ckv_ref, kr_ref, ga_ref, gb_ref, qb_ref, kb_ref, vb_ref, qi_ref, kiw_ref):
    wa = n_heads * MLA_V
    half = MLA_ROPE // 2
    q_head = MLA_NOPE + MLA_ROPE
    step = 2 * LANES

    def panel(ref, off):
        for c in range(0, ref.shape[1], step):
            w = min(step, ref.shape[1] - c)
            ref[:, c:c + w] = _mx(wt_ref[off + c:off + c + w, :].T)

    def dup_rope(off):
        x1, x2 = wt_ref[off:off + half, :], wt_ref[off + half:off + 2 * half, :]
        return _mx(jnp.concatenate([x1, x2, x2, x1], axis=0).T)

    for h in range(n_heads):
        src, dst = h * q_head, h * HEAD_SLOT
        qa_ref[:, dst:dst + LANES] = _mx(wt_ref[src:src + MLA_NOPE, :].T)
        qa_ref[:, dst + LANES:dst + HEAD_SLOT] = dup_rope(src + MLA_NOPE)
    off = n_heads * q_head
    panel(ckv_ref, off)
    off += KV_LORA
    kr_ref[...] = dup_rope(off)
    off += MLA_ROPE
    for ref in (ga_ref, qb_ref, kb_ref, vb_ref):
        panel(ref, off)
        off += wa
    pairs = IDX_HEADS // 2
    for j in range(pairs):
        lo = wt_ref[off + IDX_DIM * j:off + IDX_DIM * (j + 1), :]
        hi = wt_ref[off + IDX_DIM * (j + pairs):off + IDX_DIM * (j + pairs + 1), :]
        qi_ref[:, LANES * j:LANES * (j + 1)] = _mx(jnp.concatenate([lo, hi], axis=0).T)
    off += IDX_HEADS * IDX_DIM
    n_kw = IDX_DIM + IDX_HEADS
    kiw = jnp.concatenate([wt_ref[off:off + n_kw, :], jnp.zeros((LANES - n_kw, wt_ref.shape[1]), jnp.float32)], axis=0)
    kiw_ref[...] = _mx(kiw.T)
    off += n_kw
    panel(gb_ref, off)


def _pack_w_in(w_in, n_heads):
    wt = w_in.T
    n, d = wt.shape
    wa = n_heads * MLA_V
    assert n == n_heads * (MLA_NOPE + MLA_ROPE) + KV_LORA + MLA_ROPE + 5 * wa + IDX_HEADS * IDX_DIM + IDX_DIM + IDX_HEADS
    assert 2 * IDX_DIM == LANES and 2 * MLA_ROPE == LANES and MLA_NOPE == LANES
    slab = 256
    widths = [n_heads * HEAD_SLOT, KV_LORA, LANES, wa, wa, wa, wa, wa, IDX_HEADS * IDX_DIM, LANES]
    q_a, ckv, kr, g_a, g_b, q_b, k_b, v_b, q_i, kiw = pl.pallas_call(
        functools.partial(_pack_kernel, n_heads),
        grid=(d // slab,),
        in_specs=[pl.BlockSpec((n, slab), lambda i: (0, i))],
        out_specs=[pl.BlockSpec((slab, w), lambda i: (i, 0)) for w in widths],
        out_shape=[jax.ShapeDtypeStruct((d, w), MXU_DT) for w in widths],
        compiler_params=_cparams(1),
        name="pack_w_in",
    )(wt)
    return [q_a, ckv, kr, g_a, g_b], [q_b, k_b, v_b, q_i, kiw]


def _rope_table(pos):
    half = MLA_ROPE // 2
    freqs = jnp.power(ROPE_THETA, -jnp.arange(half, dtype=jnp.float32) / half)
    ang = pos.astype(jnp.float32)[:, None] * freqs
    cos, sin = jnp.cos(ang), jnp.sin(ang)
    return jnp.concatenate([cos, cos, -sin, sin], axis=1)


def _rel_bucket(rel):
    nb = N_BUCKETS // 2
    max_exact = nb // 2
    n = jnp.abs(rel)
    nf = jnp.maximum(n, 1).astype(jnp.float32)
    large = max_exact + (jnp.log(nf / max_exact) / math.log(MAX_DISTANCE / max_exact)
                         * (nb - max_exact)).astype(jnp.int32)
    large = jnp.minimum(large, nb - 1)
    return jnp.where(rel > 0, nb, 0) + jnp.where(n < max_exact, n, large)


def _bucket_bias(rel_bias, rel):
    return (rel_bias * LOG2E)[_rel_bucket(rel)]


def kernel(x_prompt, x_sample, cache_mla_ckv, cache_mla_krope, cache_dsa_k, cache_dsa_v, cache_idx_k,
           c_prompt, c_sample, w_ada, b_ada, ln_gain, w_in, mla_kv_gain, w_uk, w_uv, rel_bias,
           out_gain_a, out_gain_b, w_out, final_gain):
    assert w_ada.shape[0] == 1, "single-layer step"
    b, t, d = x_prompt.shape
    nb, ts, _ = x_sample.shape
    past = cache_mla_ckv.shape[2]
    n_heads = d // 256
    wb = n_heads * DSA_HEAD_DIM
    s_all = past + ts
    assert t % MLA_TQ == 0 and t % DSA_TQ == 0 and MLA_TQ % CHUNK == 0 and DSA_TQ % CHUNK == 0
    assert (s_all - 1) // CHUNK <= past // CHUNK

    mod = _ada_mod(jnp.concatenate([c_prompt, c_sample], axis=0), w_ada[0], b_ada[0])
    shift, scale, gate = mod[:, :d], mod[:, d:2 * d], mod[:, 2 * d:]
    mod_p = [v[:b].reshape(b, 1, d) for v in (scale, shift, gate)]
    mod_s = [jnp.broadcast_to(v[b:, None, :], (nb, ts, d)).reshape(1, nb * ts, d) for v in (scale, shift, gate)]

    w_a, w_b = _pack_w_in(w_in[0], n_heads)
    wkv = _mx(jnp.concatenate([w_uk[0].transpose(2, 0, 1).reshape(KV_LORA, n_heads * MLA_NOPE),
                               w_uv[0].transpose(2, 0, 1).reshape(KV_LORA, n_heads * MLA_V)], axis=1))
    w_o = _mx(w_out[0])
    pos_p = jnp.arange(t, dtype=jnp.int32)
    pos_s = past + jnp.arange(ts, dtype=jnp.int32)
    tab_p = _rope_table(pos_p)
    tab_s = jnp.tile(_rope_table(pos_s), (nb, 1))

    tm = 512
    qcat, oa_buf = _proj_q(x_prompt, mod_p[0], mod_p[1], ln_gain[0], tab_p, w_a[0], tm, True)
    ckv_p, krope_p, ga, gb, kcat, vup = _proj_a(
        x_prompt, mod_p[0], mod_p[1], ln_gain[0], tab_p, w_a[1:], mla_kv_gain[0], wkv, tm)
    qb, k_p, v_p, kbb, vbb, qi, idxk_p, kiab, wi, ob_buf = _proj_b(
        x_prompt, mod_p[0], mod_p[1], ln_gain[0], w_b, tm, True)
    o_a = _mla_prompt(qcat, kcat, vup, oa_buf, b, t)
    near = _bucket_bias(rel_bias, jnp.arange(-MAX_DISTANCE - 1, MAX_DISTANCE, dtype=jnp.int32))
    far = near[:1]
    bias_tab = jnp.concatenate([jnp.broadcast_to(far, (t - 1 - MAX_DISTANCE, n_heads)), near[1:],
                                jnp.broadcast_to(far, (t + 1 - MAX_DISTANCE, n_heads))], axis=0).T
    o_b = _dsa_prompt(qb, qi, wi, kbb, vbb, kiab, bias_tab, ob_buf, b, t, min(TOPK_MAX, t // 4))
    y_prompt = _out_proj(o_a, o_b, ga, gb, x_prompt, mod_p[2], out_gain_a[0], out_gain_b[0], w_o, final_gain, tm)

    xs3 = x_sample.reshape(1, nb * ts, d)
    tms = min(256, nb * ts)
    (qcat_s,) = _proj_q(xs3, mod_s[0], mod_s[1], ln_gain[0], tab_s, w_a[0], tms, False)
    ckv_s, krope_s, ga_s, gb_s = _proj_a(
        xs3, mod_s[0], mod_s[1], ln_gain[0], tab_s, w_a[1:], mla_kv_gain[0], None, tms)
    qb_s, k_s, v_s, kbb_s, vbb_s, qi_s, idxk_s, kiab_s, wi_s = _proj_b(
        xs3, mod_s[0], mod_s[1], ln_gain[0], w_b, tms, False)
    oa_s = _mla_sample(qcat_s, cache_mla_ckv[0], cache_mla_krope[0], ckv_s, krope_s,
                       _mx(w_uk[0]), _mx(w_uv[0]), ts)
    s_pad = pl.cdiv(s_all, LANES) * LANES
    tab_s = _bucket_bias(rel_bias, jnp.arange(s_pad + LANES, dtype=jnp.int32) - (s_all - 1)).T
    madd_s = _dsa_select(qi_s, wi_s, cache_idx_k[0], kiab_s, ts, min(TOPK_MAX, s_all // 4))
    ob_s = _dsa_sample(qb_s, madd_s, cache_dsa_k[0], cache_dsa_v[0], kbb_s, vbb_s, tab_s, ts)
    y_sample = _out_proj(oa_s, ob_s, ga_s, gb_s, xs3, mod_s[2], out_gain_a[0], out_gain_b[0], w_o, final_gain, tms)

    hd = (n_heads, DSA_HEAD_DIM)
    return (y_prompt, y_sample.reshape(nb, ts, d),
            ckv_p.reshape(1, b, t, KV_LORA), krope_p.reshape(1, b, t, MLA_ROPE),
            k_p.reshape(1, b, t, *hd), v_p.reshape(1, b, t, *hd), idxk_p.reshape(1, b, t, IDX_DIM),
            ckv_s.reshape(1, nb, ts, KV_LORA), krope_s.reshape(1, nb, ts, MLA_ROPE),
            k_s.reshape(1, nb, ts, *hd), v_s.reshape(1, nb, ts, *hd), idxk_s.reshape(1, nb, ts, IDX_DIM))
```

```python
import functools
import math

import jax
import jax.numpy as jnp
from jax import lax
from jax.experimental import pallas as pl
from jax.experimental.pallas import tpu as pltpu

MXU_DT = jnp.bfloat16

CHUNK = 64
MLA_NOPE = 128
MLA_ROPE = 64
MLA_V = 128
KV_LORA = 512
DSA_HEAD_DIM = 128
IDX_HEADS = 16
IDX_DIM = 64
TOPK_MAX = 256
N_BUCKETS = 32
MAX_DISTANCE = 128
ROPE_THETA = 10000.0
EPS = 1e-6

LANES = 128
HEAD_SLOT = 256
NEG_BIG = -1e30
LOG2E = math.log2(math.e)
INT_MIN = -2 ** 31
KEY_NEG_INF = INT_MIN + 0x7FFFFF
VMEM_LIMIT = 56 * 1024 * 1024


def _cparams(n_grid, vmem=VMEM_LIMIT):
    return pltpu.CompilerParams(dimension_semantics=("arbitrary",) * n_grid, vmem_limit_bytes=vmem)


def _mx(v):
    return v.astype(MXU_DT)


def _dot(a, b):
    return jnp.dot(a, b, preferred_element_type=jnp.float32)


def _dot_nt(a, b):
    return lax.dot_general(a, b, (((1,), (1,)), ((), ())), preferred_element_type=jnp.float32)


def _silu(v):
    return v * (1.0 / (1.0 + jnp.exp(-v)))


def _resident(shape):
    nd = len(shape)
    return pl.BlockSpec(shape, lambda *_: (0,) * nd, pipeline_mode=pl.Buffered(1))


def _ada_kernel(c_ref, w_ref, b_ref, o_ref):
    a = _mx(_silu(c_ref[...]))
    o_ref[...] = _dot(a, _mx(w_ref[...])) + b_ref[...]


def _ada_mod(c_all, w_ada, b_ada):
    m, d = c_all.shape
    n = w_ada.shape[1]
    tn = 1024
    return pl.pallas_call(
        _ada_kernel,
        grid=(n // tn,),
        in_specs=[pl.BlockSpec((m, d), lambda j: (0, 0)),
                  pl.BlockSpec((d, tn), lambda j: (0, j)),
                  pl.BlockSpec((1, tn), lambda j: (0, j))],
        out_specs=pl.BlockSpec((m, tn), lambda j: (0, j)),
        out_shape=jax.ShapeDtypeStruct((m, n), jnp.float32),
        compiler_params=_cparams(1),
        name="ada_mod",
    )(c_all, w_ada, b_ada.reshape(1, n))


def _modulated_norm(x_ref, scale_ref, shift_ref, lng_ref):
    x = x_ref[...]
    xn = x * lax.rsqrt(jnp.mean(x * x, axis=-1, keepdims=True) + EPS) * lng_ref[...]
    return _mx(xn * (1.0 + scale_ref[...]) + shift_ref[...])


def _rope128(a, tab):
    t = a * tab
    return t + pltpu.roll(t, 64, 1)


def _proj_a_kernel(n_heads, emit_kv, x_ref, scale_ref, shift_ref, lng_ref, tab_ref,
                   wq_ref, wc_ref, wr_ref, wga_ref, wgb_ref, kvg_ref, *rest):
    if emit_kv:
        wkv_ref, qcat_ref, ckv_ref, krope_ref, ga_ref, gb_ref, kcat_ref, vup_ref, attn_ref = rest
        attn_ref[...] = jnp.zeros_like(attn_ref)
    else:
        qcat_ref, ckv_ref, krope_ref, ga_ref, gb_ref = rest
    hb = _modulated_norm(x_ref, scale_ref, shift_ref, lng_ref)
    tab = tab_ref[...]
    qscale = (MLA_NOPE + MLA_ROPE) ** -0.5 * LOG2E
    for h in range(0, n_heads, 2):
        a = _dot(hb, wq_ref[:, h * LANES:(h + 2) * LANES])
        qcat_ref[:, h * HEAD_SLOT:h * HEAD_SLOT + LANES] = _mx(a[:, :LANES] * qscale)
        qcat_ref[:, (h + 1) * HEAD_SLOT:(h + 1) * HEAD_SLOT + LANES] = _mx(a[:, LANES:] * qscale)
    lane = lax.broadcasted_iota(jnp.int32, tab.shape, 1)
    tab_r = pltpu.roll(tab, MLA_ROPE, 1)
    cos4 = jnp.where(lane < MLA_ROPE, tab, tab_r)
    sin4 = jnp.where(lane < MLA_ROPE, tab_r, tab)
    is_x1 = lane % MLA_ROPE < MLA_ROPE // 2
    n_nope = n_heads * MLA_NOPE
    for h in range(0, n_heads, 4):
        a = _dot(hb, wq_ref[:, n_nope + (h // 2) * LANES:n_nope + (h // 2 + 2) * LANES])
        for u in range(2):
            ap = a[:, u * LANES:(u + 1) * LANES]
            swapped = jnp.where(is_x1, pltpu.roll(ap, LANES - MLA_ROPE // 2, 1), pltpu.roll(ap, MLA_ROPE // 2, 1))
            roped = (ap * cos4 + swapped * sin4) * qscale
            c0 = (h + 2 * u) * HEAD_SLOT
            qcat_ref[:, c0 + LANES:c0 + HEAD_SLOT] = _mx(roped)
            qcat_ref[:, c0 + HEAD_SLOT + LANES:c0 + 2 * HEAD_SLOT] = _mx(pltpu.roll(roped, MLA_ROPE, 1))
    c = _dot(hb, wc_ref[...])
    cn = c * lax.rsqrt(jnp.mean(c * c, axis=-1, keepdims=True) + EPS) * kvg_ref[...]
    ckv_ref[...] = cn
    r = _rope128(_dot(hb, wr_ref[...]), tab)
    krope_ref[...] = r[:, :MLA_ROPE]
    ga_ref[...] = _mx(_silu(_dot(hb, wga_ref[...])))
    gb_ref[...] = _mx(_silu(_dot(hb, wgb_ref[...])))
    if emit_kv:
        cb = _mx(cn)
        lane = lax.broadcasted_iota(jnp.int32, r.shape, 1)
        krz = _mx(jnp.where(lane < MLA_ROPE, r, 0.0))
        kn = _dot(cb, wkv_ref[:, :n_heads * MLA_NOPE])
        for h in range(n_heads):
            c0 = h * HEAD_SLOT
            kcat_ref[:, c0:c0 + LANES] = _mx(kn[:, h * MLA_NOPE:(h + 1) * MLA_NOPE])
            kcat_ref[:, c0 + LANES:c0 + HEAD_SLOT] = krz
        vup_ref[...] = _mx(_dot(cb, wkv_ref[:, n_heads * MLA_NOPE:]))


def _proj_b_kernel(n_heads, x_ref, scale_ref, shift_ref, lng_ref, wq_ref, wk_ref, wv_ref, wqi_ref, wkw_ref,
                   qb_ref, kb_ref, vb_ref, kbb_ref, vbb_ref, qi_ref, ki_ref, kiab_ref, wi_ref, *attn_ref):
    if attn_ref:
        attn_ref[0][...] = jnp.zeros_like(attn_ref[0])
    hb = _modulated_norm(x_ref, scale_ref, shift_ref, lng_ref)
    dscale = DSA_HEAD_DIM ** -0.5 * LOG2E
    step = 512
    for c0 in range(0, wq_ref.shape[1], step):
        qb_ref[:, c0:c0 + step] = _mx(_dot(hb, wq_ref[:, c0:c0 + step]) * dscale)
    for w_ref, f_ref, b_ref in ((wk_ref, kb_ref, kbb_ref), (wv_ref, vb_ref, vbb_ref)):
        for c0 in range(0, w_ref.shape[1], step):
            kv = _dot(hb, w_ref[:, c0:c0 + step])
            f_ref[:, c0:c0 + step] = kv
            b_ref[:, c0:c0 + step] = _mx(kv)
    for c0 in range(0, wqi_ref.shape[1], step):
        qi_ref[:, c0:c0 + step] = _mx(_dot(hb, wqi_ref[:, c0:c0 + step]))
    a = _dot(hb, wkw_ref[...])
    ki_ref[...] = a[:, :IDX_DIM]
    lane = lax.broadcasted_iota(jnp.int32, a.shape, 1)
    kz = jnp.where(lane < IDX_DIM, a, 0.0)
    kiab_ref[:, :LANES] = _mx(kz)
    kiab_ref[:, LANES:] = _mx(pltpu.roll(kz, IDX_DIM, 1))
    wi_ref[...] = a * (IDX_HEADS ** -0.5 * IDX_DIM ** -0.5)


def _row_specs(x3, mod_rows, tm):
    bv, tv, d = x3.shape
    x_spec = pl.BlockSpec((None, tm, d), lambda b, i: (b, i, 0))
    if mod_rows == 1:
        m_spec = pl.BlockSpec((None, 1, d), lambda b, i: (b, 0, 0))
    else:
        m_spec = pl.BlockSpec((None, tm, d), lambda b, i: (b, i, 0))
    return x_spec, m_spec


def _out2d(m, width, dtype, tm, nt):
    return (jax.ShapeDtypeStruct((m, width), dtype),
            pl.BlockSpec((tm, width), lambda b, i: (b * nt + i, 0)))


def _proj_a(x3, scale, shift, ln_gain, tab, ws_a, kv_gain, wkv, tm):
    bv, tv, d = x3.shape
    nt = tv // tm
    m = bv * tv
    n_heads = d // 256
    wa = n_heads * MLA_V
    emit_kv = wkv is not None
    x_spec, m_spec = _row_specs(x3, scale.shape[1], tm)
    in_specs = ([x_spec, m_spec, m_spec, _resident((1, d)), pl.BlockSpec((tm, LANES), lambda b, i: (i, 0))]
                + [_resident(w.shape) for w in ws_a] + [_resident((1, KV_LORA))])
    args = [x3, scale, shift, ln_gain.reshape(1, d), tab, *ws_a, kv_gain.reshape(1, KV_LORA)]
    outs = [_out2d(m, n_heads * HEAD_SLOT, MXU_DT, tm, nt),
            _out2d(m, KV_LORA, jnp.float32, tm, nt),
            _out2d(m, MLA_ROPE, jnp.float32, tm, nt),
            _out2d(m, wa, MXU_DT, tm, nt),
            _out2d(m, wa, MXU_DT, tm, nt)]
    if emit_kv:
        in_specs.append(_resident(wkv.shape))
        args.append(wkv)
        outs += [_out2d(m, n_heads * HEAD_SLOT, MXU_DT, tm, nt),
                 _out2d(m, wa, MXU_DT, tm, nt),
                 _out2d(m, wa, MXU_DT, tm, nt)]
    return pl.pallas_call(
        functools.partial(_proj_a_kernel, n_heads, emit_kv),
        grid=(bv, nt),
        in_specs=in_specs,
        out_specs=[o[1] for o in outs],
        out_shape=[o[0] for o in outs],
        compiler_params=_cparams(2),
        name="proj_a",
    )(*args)


def _proj_b(x3, scale, shift, ln_gain, ws_b, tm, emit_attn_buffer):
    bv, tv, d = x3.shape
    nt = tv // tm
    m = bv * tv
    n_heads = d // 256
    width_b = n_heads * DSA_HEAD_DIM
    x_spec, m_spec = _row_specs(x3, scale.shape[1], tm)
    outs = [_out2d(m, width_b, MXU_DT, tm, nt),
            _out2d(m, width_b, jnp.float32, tm, nt),
            _out2d(m, width_b, jnp.float32, tm, nt),
            _out2d(m, width_b, MXU_DT, tm, nt),
            _out2d(m, width_b, MXU_DT, tm, nt),
            _out2d(m, IDX_HEADS * IDX_DIM, MXU_DT, tm, nt),
            _out2d(m, IDX_DIM, jnp.float32, tm, nt),
            _out2d(m, 2 * LANES, MXU_DT, tm, nt),
            _out2d(m, LANES, jnp.float32, tm, nt)]
    if emit_attn_buffer:
        outs.append(_out2d(m, width_b, MXU_DT, tm, nt))
    return pl.pallas_call(
        functools.partial(_proj_b_kernel, n_heads),
        grid=(bv, nt),
        in_specs=[x_spec, m_spec, m_spec, _resident((1, d))] + [_resident(w.shape) for w in ws_b],
        out_specs=[o[1] for o in outs],
        out_shape=[o[0] for o in outs],
        compiler_params=_cparams(2),
        name="proj_b",
    )(x3, scale, shift, ln_gain.reshape(1, d), *ws_b)


def _softmax_pv(s, v):
    m = jnp.max(s, axis=-1, keepdims=True)
    p = jnp.exp2(s - m)
    l = jnp.sum(p, axis=-1, keepdims=True)
    return _dot(_mx(p), v) * (1.0 / l)


def _topk_mask(score, adm, topk, sc_ref, madd_ref):
    rows, n = score.shape
    if n <= topk:
        madd_ref[...] = jnp.where(adm, 0.0, NEG_BIG)
        return
    sc_ref[...] = jnp.where(adm, score, -jnp.inf)
    kf = float(topk)

    def count(pred):
        return jnp.sum(jnp.where(pred, 1.0, 0.0), axis=-1, keepdims=True)

    def key_to_float(key):
        return pltpu.bitcast(key ^ ((key >> 31) & 0x7FFFFFFF), jnp.float32)

    def bit_step(i, thr):
        inc = lax.shift_left(jnp.int32(1), 31 - i)
        cand = thr + inc
        ok = count(sc_ref[...] >= key_to_float(cand)) >= kf
        return jnp.where(ok, cand, thr)

    thr = lax.fori_loop(0, 32, bit_step, jnp.full((rows, 1), INT_MIN, jnp.int32))
    thr = key_to_float(jnp.maximum(thr, KEY_NEG_INF))
    ge = sc_ref[...] >= thr
    madd_ref[...] = jnp.where(ge, 0.0, NEG_BIG)

    @pl.when(jnp.max(count(ge)) > kf)
    def _():
        sc = sc_ref[...]
        gt = sc > thr
        need = kf - count(gt)
        eqf = jnp.where(sc == thr, 1.0, 0.0)
        col = lax.broadcasted_iota(jnp.int32, (rows, n), 1)
        nbits = max(1, int(n).bit_length())

        def col_step(i, bound):
            cand = bound + lax.shift_left(jnp.int32(1), nbits - 1 - i)
            taken = jnp.sum(jnp.where(col < cand, eqf, 0.0), axis=-1, keepdims=True)
            return jnp.where(taken <= need, cand, bound)

        bound = lax.fori_loop(0, nbits, col_step, jnp.zeros((rows, 1), jnp.int32))
        tie_madd = jnp.where(jnp.where(col < bound, eqf, 0.0) > 0.5, 0.0, NEG_BIG)
        madd_ref[...] = jnp.where(adm, jnp.where(gt, 0.0, tie_madd), NEG_BIG)


def _indexer_scores(qi_ref, wi, kia, kib):
    half = IDX_HEADS // 2
    score = None
    for j in range(half):
        qp = qi_ref[:, j * LANES:(j + 1) * LANES]
        da = jnp.maximum(_dot_nt(qp, kia), 0.0) * wi[:, IDX_DIM + j:IDX_DIM + j + 1]
        db = jnp.maximum(_dot_nt(qp, kib), 0.0) * wi[:, IDX_DIM + half + j:IDX_DIM + half + j + 1]
        score = da + db if score is None else score + da + db
    return score


def _indexer_scores_stacked(qi_ref, wi, kia, kib):
    half = IDX_HEADS // 2
    ts = qi_ref.shape[0]
    q_all = jnp.concatenate([qi_ref[:, j * LANES:(j + 1) * LANES] for j in range(half)], axis=0)
    wa = jnp.concatenate([wi[:, IDX_DIM + j:IDX_DIM + j + 1] for j in range(half)], axis=0)
    wb = jnp.concatenate([wi[:, IDX_DIM + half + j:IDX_DIM + half + j + 1] for j in range(half)], axis=0)
    part = jnp.maximum(_dot_nt(q_all, kia), 0.0) * wa + jnp.maximum(_dot_nt(q_all, kib), 0.0) * wb
    score = part[0:ts]
    for j in range(1, half):
        score = score + part[j * ts:(j + 1) * ts]
    return score


MLA_TQ = 512
DSA_TQ = 256


def _chunk_madd(row0, tq, n_keys):
    qc = (row0 + lax.broadcasted_iota(jnp.int32, (tq, n_keys), 0)) // CHUNK
    kc = lax.broadcasted_iota(jnp.int32, (tq, n_keys), 1) // CHUNK
    return kc <= qc


def _chained_tile_calls(n_tiles, make_call, out):
    for c in range(n_tiles):
        out = make_call(c, out)
    return out


def _mla_prompt_kernel(n_heads, row0, q_ref, k_ref, v_ref, prev_ref, o_ref):
    del prev_ref
    tq, n_keys = q_ref.shape[0], k_ref.shape[0]
    madd = jnp.where(_chunk_madd(row0, tq, n_keys), 0.0, NEG_BIG)
    for h in range(n_heads):
        c0 = h * HEAD_SLOT
        s = _dot_nt(q_ref[:, c0:c0 + HEAD_SLOT], k_ref[:, c0:c0 + HEAD_SLOT]) + madd
        o_ref[:, h * MLA_V:(h + 1) * MLA_V] = _mx(_softmax_pv(s, v_ref[:, h * MLA_V:(h + 1) * MLA_V]))


def _mla_prompt(qcat, kcat, vup, out_init, b, t):
    tq = MLA_TQ
    n_heads = qcat.shape[1] // HEAD_SLOT
    nq = t // tq
    wa = n_heads * MLA_V
    k3 = kcat.reshape(b, t, kcat.shape[1])
    v3 = vup.reshape(b, t, wa)

    def make_call(c, out):
        n_keys = (c + 1) * tq
        row = lambda bi: (bi * nq + c, 0)
        return pl.pallas_call(
            functools.partial(_mla_prompt_kernel, n_heads, c * tq),
            grid=(b,),
            in_specs=[pl.BlockSpec((tq, qcat.shape[1]), row),
                      pl.BlockSpec((None, n_keys, kcat.shape[1]), lambda bi: (bi, 0, 0)),
                      pl.BlockSpec((None, n_keys, wa), lambda bi: (bi, 0, 0)),
                      pl.BlockSpec(memory_space=pl.ANY)],
            out_specs=pl.BlockSpec((tq, wa), row),
            out_shape=jax.ShapeDtypeStruct(out.shape, out.dtype),
            input_output_aliases={3: 0},
            compiler_params=_cparams(1),
            name="mla_prompt_%d" % c,
        )(qcat, k3, v3, out)

    return _chained_tile_calls(nq, make_call, out_init)


def _toeplitz_bias(tab_ref, h, start, n_rows, n_cols):
    shift0 = LANES - (n_rows - 1)
    from_left = (lax.broadcasted_iota(jnp.int32, (n_rows, LANES), 1)
                 < shift0 + lax.broadcasted_iota(jnp.int32, (n_rows, LANES), 0))
    pieces = []
    prev = None
    for n in range(n_cols // LANES + 1):
        seg = tab_ref[h:h + 1, start + n * LANES:start + (n + 1) * LANES]
        rot = pltpu.roll(jnp.broadcast_to(seg, (n_rows, LANES)), shift0 % LANES, 1, stride=1, stride_axis=0)
        if prev is not None:
            pieces.append(jnp.where(from_left, prev, rot))
        prev = rot
    return jnp.concatenate(pieces, axis=1)


def _dsa_prompt_kernel(n_heads, row0, t, topk, qb_ref, qi_ref, wi_ref, k_ref, v_ref, kiab_ref, tab_ref, prev_ref,
                       o_ref, key_ref, madd_ref):
    del prev_ref
    tq, n_keys = qb_ref.shape[0], k_ref.shape[0]
    score = _indexer_scores(qi_ref, wi_ref[...], kiab_ref[:, :LANES], kiab_ref[:, LANES:])
    _topk_mask(score, _chunk_madd(row0, tq, n_keys), topk, key_ref, madd_ref)
    near0 = max(0, row0 - LANES)
    n_near = n_keys - near0
    for h in range(n_heads):
        c0 = h * DSA_HEAD_DIM
        near = jnp.concatenate(
            [_toeplitz_bias(tab_ref, h, near0 - (row0 + u * LANES) - LANES + t, LANES, n_near)
             for u in range(tq // LANES)], axis=0)
        if near0 > 0:
            far = jnp.broadcast_to(tab_ref[h:h + 1, 0:1], (tq, near0))
            bias = jnp.concatenate([far, near], axis=1)
        else:
            bias = near
        s = _dot_nt(qb_ref[:, c0:c0 + DSA_HEAD_DIM], k_ref[:, c0:c0 + DSA_HEAD_DIM]) + bias + madd_ref[...]
        o_ref[:, c0:c0 + DSA_HEAD_DIM] = _mx(_softmax_pv(s, v_ref[:, c0:c0 + DSA_HEAD_DIM]))


def _dsa_prompt(qb, qi, wi, kbb, vbb, kiab, bias_tab, out_init, b, t, topk):
    tq = DSA_TQ
    assert MAX_DISTANCE <= LANES and tq % LANES == 0
    n_heads = qb.shape[1] // DSA_HEAD_DIM
    nq = t // tq
    wb = qb.shape[1]
    k3, v3, ki3 = kbb.reshape(b, t, wb), vbb.reshape(b, t, wb), kiab.reshape(b, t, 2 * LANES)
    per_b = lambda bi: (bi, 0, 0)

    def make_call(c, out):
        n_keys = (c + 1) * tq
        row = lambda bi: (bi * nq + c, 0)
        return pl.pallas_call(
            functools.partial(_dsa_prompt_kernel, n_heads, c * tq, t, topk),
            grid=(b,),
            in_specs=[pl.BlockSpec((tq, wb), row),
                      pl.BlockSpec((tq, qi.shape[1]), row),
                      pl.BlockSpec((tq, LANES), row),
                      pl.BlockSpec((None, n_keys, wb), per_b),
                      pl.BlockSpec((None, n_keys, wb), per_b),
                      pl.BlockSpec((None, n_keys, 2 * LANES), per_b),
                      _resident(bias_tab.shape),
                      pl.BlockSpec(memory_space=pl.ANY)],
            out_specs=pl.BlockSpec((tq, wb), row),
            out_shape=jax.ShapeDtypeStruct(out.shape, out.dtype),
            input_output_aliases={7: 0},
            scratch_shapes=[pltpu.VMEM((tq, n_keys), jnp.float32), pltpu.VMEM((tq, n_keys), jnp.float32)],
            compiler_params=_cparams(1),
            name="dsa_prompt_%d" % c,
        )(qb, qi, wi, k3, v3, ki3, bias_tab, out)

    return _chained_tile_calls(nq, make_call, out_init)


def _mla_sample_kernel(n_heads, past, ts, qcat_ref, cckv_ref, ckr_ref, nckv_ref, nkr_ref, wuk_ref, wuv_ref,
                       o_ref, kall_ref, rall_ref):
    n_keys = past + ts

    @pl.when(pl.program_id(0) == 0)
    def _():
        kall_ref[...] = jnp.zeros_like(kall_ref)
        rall_ref[...] = jnp.zeros_like(rall_ref)

    for u in range(cckv_ref.shape[0]):
        rows = slice(u * ts, (u + 1) * ts)
        kall_ref[u, 0:past, :] = _mx(cckv_ref[u])
        kall_ref[u, past:n_keys, :] = _mx(nckv_ref[rows, :])
        rall_ref[u, 0:past, 0:MLA_ROPE] = _mx(ckr_ref[u])
        rall_ref[u, past:n_keys, 0:MLA_ROPE] = _mx(nkr_ref[rows, :])
        qlat, qrope = [], []
        for h in range(n_heads):
            c0 = h * HEAD_SLOT
            qlat.append(_mx(_dot(qcat_ref[rows, c0:c0 + LANES], wuk_ref[h])))
            qrope.append(qcat_ref[rows, c0 + LANES:c0 + HEAD_SLOT])
        qlat = jnp.concatenate(qlat, axis=0)
        qrope = jnp.concatenate(qrope, axis=0)
        s = _dot_nt(qlat, kall_ref[u]) + _dot_nt(qrope, rall_ref[u])
        col = lax.broadcasted_iota(jnp.int32, s.shape, 1)
        s = jnp.where(col < n_keys, s, NEG_BIG)
        olat = _mx(_softmax_pv(s, kall_ref[u]))
        for h in range(n_heads):
            o_ref[rows, h * MLA_V:(h + 1) * MLA_V] = _mx(_dot_nt(olat[h * ts:(h + 1) * ts], wuv_ref[h]))


def _mla_sample(qcat, cache_ckv, cache_kr, new_ckv, new_kr, wuk, wuv, ts):
    nb, past, c = cache_ckv.shape
    n_heads = qcat.shape[1] // HEAD_SLOT
    wa = n_heads * MLA_V
    s_pad = pl.cdiv(past + ts, LANES) * LANES
    per = 2 if nb % 2 == 0 else 1
    row = lambda bi: (bi, 0)
    return pl.pallas_call(
        functools.partial(_mla_sample_kernel, n_heads, past, ts),
        grid=(nb // per,),
        in_specs=[pl.BlockSpec((per * ts, qcat.shape[1]), row),
                  pl.BlockSpec((per, past, c), lambda bi: (bi, 0, 0)),
                  pl.BlockSpec((per, past, MLA_ROPE), lambda bi: (bi, 0, 0)),
                  pl.BlockSpec((per * ts, c), row),
                  pl.BlockSpec((per * ts, MLA_ROPE), row),
                  _resident(wuk.shape), _resident(wuv.shape)],
        out_specs=pl.BlockSpec((per * ts, wa), row),
        out_shape=jax.ShapeDtypeStruct((nb * ts, wa), MXU_DT),
        scratch_shapes=[pltpu.VMEM((per, s_pad, c), MXU_DT), pltpu.VMEM((per, s_pad, LANES), MXU_DT)],
        compiler_params=_cparams(1),
        name="mla_sample",
    )(qcat, cache_ckv, cache_kr, new_ckv, new_kr, wuk, wuv)


def _dsa_select_kernel(past, ts, topk, qi_ref, wi_ref, cki_ref, nkiab_ref, madd_ref, kia_ref, kib_ref, key_ref):
    bi = pl.program_id(0)
    n_keys = past + ts
    rows, s_pad = madd_ref.shape

    @pl.when(bi == 0)
    def _():
        kia_ref[...] = jnp.zeros_like(kia_ref)
        kib_ref[...] = jnp.zeros_like(kib_ref)

    cki = _mx(cki_ref[...])
    kia_ref[0:past, 0:IDX_DIM] = cki
    kib_ref[0:past, IDX_DIM:LANES] = cki
    kia_ref[past:n_keys, :] = nkiab_ref[:, :LANES]
    kib_ref[past:n_keys, :] = nkiab_ref[:, LANES:]
    madd_ref[pl.ds(pl.multiple_of(bi * ts, ts), ts), :] = _indexer_scores_stacked(
        qi_ref, wi_ref[...], kia_ref[...], kib_ref[...])

    @pl.when(bi == pl.num_programs(0) - 1)
    def _():
        col = lax.broadcasted_iota(jnp.int32, (rows, s_pad), 1)
        _topk_mask(madd_ref[...], col < n_keys, topk, key_ref, madd_ref)


def _dsa_select(qi, wi, cache_ki, new_kiab, ts, topk):
    nb, past, _ = cache_ki.shape
    s_pad = pl.cdiv(past + ts, LANES) * LANES
    row = lambda bi: (bi, 0)
    return pl.pallas_call(
        functools.partial(_dsa_select_kernel, past, ts, topk),
        grid=(nb,),
        in_specs=[pl.BlockSpec((ts, qi.shape[1]), row),
                  pl.BlockSpec((ts, LANES), row),
                  pl.BlockSpec((None, past, IDX_DIM), lambda bi: (bi, 0, 0)),
                  pl.BlockSpec((ts, 2 * LANES), row)],
        out_specs=pl.BlockSpec((nb * ts, s_pad), lambda bi: (0, 0)),
        out_shape=jax.ShapeDtypeStruct((nb * ts, s_pad), jnp.float32),
        scratch_shapes=[pltpu.VMEM((s_pad, LANES), MXU_DT), pltpu.VMEM((s_pad, LANES), MXU_DT),
                        pltpu.VMEM((nb * ts, s_pad), jnp.float32)],
        compiler_params=_cparams(1),
        name="dsa_select",
    )(qi, wi, cache_ki, new_kiab)


def _dsa_sample_kernel(n_heads, past, ts, qb_ref, madd_ref, ck_ref, cv_ref, nk_ref, nv_ref, tab_ref, expand_ref,
                       o_ref, kflat_ref, vflat_ref, bias_ref, biasw_ref):
    n_keys = past + ts
    wide = n_heads * LANES
    n_blocks = pl.cdiv(n_keys, LANES)
    widths = [min(wide, (n_keys - j * LANES) * n_heads) for j in range(n_blocks)]

    @pl.when(pl.program_id(0) == 0)
    def _():
        for h in range(n_heads):
            bias_ref[h * ts:(h + 1) * ts, :] = _toeplitz_bias(tab_ref, h, 0, ts, n_blocks * LANES)
        shape = (n_heads * ts, wide)
        same_head = (lax.broadcasted_iota(jnp.int32, shape, 0) // ts
                     == lax.broadcasted_iota(jnp.int32, shape, 1) % n_heads)
        for j in range(n_blocks):
            b = bias_ref[:, j * LANES:(j + 1) * LANES]
            hi = _mx(b)
            rest = b - hi.astype(jnp.float32)
            mid = _mx(rest)
            lo = _mx(rest - mid.astype(jnp.float32))
            piece = _dot(hi, expand_ref[...]) + _dot(mid, expand_ref[...]) + _dot(lo, expand_ref[...])
            biasw_ref[:, j * wide:j * wide + widths[j]] = jnp.where(same_head, piece, NEG_BIG)[:, :widths[j]]

    kflat_ref[0:past * n_heads, :] = _mx(ck_ref[...])
    kflat_ref[past * n_heads:, :] = nk_ref[...]
    vflat_ref[0:past * n_heads, :] = _mx(cv_ref[...])
    vflat_ref[past * n_heads:, :] = nv_ref[...]
    sel = _mx(jnp.where(madd_ref[...] == 0.0, 1.0, 0.0))
    pieces = [_dot(sel[:, j * LANES:(j + 1) * LANES], expand_ref[...])[:, :widths[j]] for j in range(n_blocks)]
    sel_wide = jnp.concatenate(pieces, axis=1)
    sel_wide = jnp.concatenate([sel_wide] * n_heads, axis=0)
    q_all = jnp.concatenate([qb_ref[:, h * DSA_HEAD_DIM:(h + 1) * DSA_HEAD_DIM] for h in range(n_heads)], axis=0)
    s = _dot_nt(q_all, kflat_ref[...]) + biasw_ref[...] + jnp.where(sel_wide > 0.5, 0.0, NEG_BIG)
    o = _mx(_softmax_pv(s, vflat_ref[...]))
    for h in range(n_heads):
        o_ref[:, h * DSA_HEAD_DIM:(h + 1) * DSA_HEAD_DIM] = o[h * ts:(h + 1) * ts]


def _dsa_sample(qb, madd, cache_k, cache_v, new_k, new_v, bias_tab, ts):
    nb, past, n_heads, _ = cache_k.shape
    wb = n_heads * DSA_HEAD_DIM
    n_keys = past + ts
    assert (n_keys % LANES * n_heads) % LANES == 0
    expand = _mx(jnp.repeat(jnp.eye(LANES, dtype=jnp.float32), n_heads, axis=1))
    row = lambda bi: (bi, 0)
    per_b = lambda bi: (bi, 0, 0)
    return pl.pallas_call(
        functools.partial(_dsa_sample_kernel, n_heads, past, ts),
        grid=(nb,),
        in_specs=[pl.BlockSpec((ts, wb), row),
                  pl.BlockSpec((ts, madd.shape[1]), row),
                  pl.BlockSpec((None, past * n_heads, DSA_HEAD_DIM), per_b),
                  pl.BlockSpec((None, past * n_heads, DSA_HEAD_DIM), per_b),
                  pl.BlockSpec((ts * n_heads, DSA_HEAD_DIM), row),
                  pl.BlockSpec((ts * n_heads, DSA_HEAD_DIM), row),
                  _resident(bias_tab.shape), _resident(expand.shape)],
        out_specs=pl.BlockSpec((ts, wb), row),
        out_shape=jax.ShapeDtypeStruct((nb * ts, wb), MXU_DT),
        scratch_shapes=[pltpu.VMEM((n_keys * n_heads, DSA_HEAD_DIM), MXU_DT),
                        pltpu.VMEM((n_keys * n_heads, DSA_HEAD_DIM), MXU_DT),
                        pltpu.VMEM((n_heads * ts, pl.cdiv(n_keys, LANES) * LANES), jnp.float32),
                        pltpu.VMEM((n_heads * ts, n_keys * n_heads), jnp.float32)],
        compiler_params=_cparams(1),
        name="dsa_sample",
    )(qb, madd, cache_k.reshape(nb, past * n_heads, DSA_HEAD_DIM), cache_v.reshape(nb, past * n_heads, DSA_HEAD_DIM),
      new_k.reshape(nb * ts * n_heads, DSA_HEAD_DIM), new_v.reshape(nb * ts * n_heads, DSA_HEAD_DIM),
      bias_tab, expand)


def _out_kernel(oa_ref, ob_ref, ga_ref, gb_ref, x_ref, gate_ref, gna_ref, gnb_ref, w_ref, fg_ref, y_ref):
    def gated(o_ref, g_ref, gain_ref):
        o = o_ref[...].astype(jnp.float32)
        on = o * lax.rsqrt(jnp.mean(o * o, axis=-1, keepdims=True) + EPS) * gain_ref[...]
        return _mx(on * g_ref[...].astype(jnp.float32))

    wa = oa_ref.shape[1]
    out = _dot(gated(oa_ref, ga_ref, gna_ref), w_ref[0:wa, :]) + _dot(gated(ob_ref, gb_ref, gnb_ref), w_ref[wa:, :])
    xn = x_ref[...] + gate_ref[...] * out
    y_ref[...] = xn * lax.rsqrt(jnp.mean(xn * xn, axis=-1, keepdims=True) + EPS) * fg_ref[...]


def _out_proj(oa, ob, ga, gb, x3, gate, gain_a, gain_b, w_out, final_gain, tm):
    bv, tv, d = x3.shape
    nt = tv // tm
    wa = oa.shape[1]
    x_spec, g_spec = _row_specs(x3, gate.shape[1], tm)
    row = lambda b, i: (b * nt + i, 0)
    return pl.pallas_call(
        _out_kernel,
        grid=(bv, nt),
        in_specs=[pl.BlockSpec((tm, wa), row), pl.BlockSpec((tm, wa), row),
                  pl.BlockSpec((tm, wa), row), pl.BlockSpec((tm, wa), row),
                  x_spec, g_spec, _resident((1, wa)), _resident((1, wa)),
                  _resident(w_out.shape), _resident((1, d))],
        out_specs=pl.BlockSpec((None, tm, d), lambda b, i: (b, i, 0)),
        out_shape=jax.ShapeDtypeStruct((bv, tv, d), jnp.float32),
        compiler_params=_cparams(2),
        name="out_proj",
    )(oa, ob, ga, gb, x3, gate, gain_a.reshape(1, wa), gain_b.reshape(1, wa), w_out, final_gain.reshape(1, d))


def _pack_kernel(n_heads, wt_ref, qa_ref, ckv_ref, kr_ref, ga_ref, gb_ref, qb_ref, kb_ref, vb_ref, qi_ref, kiw_ref):
    wa = n_heads * MLA_V
    half = MLA_ROPE // 2
    q_head = MLA_NOPE + MLA_ROPE
    step = 2 * LANES

    def panel(ref, off):
        for c in range(0, ref.shape[1], step):
            w = min(step, ref.shape[1] - c)
            ref[:, c:c + w] = _mx(wt_ref[off + c:off + c + w, :].T)

    def dup_rope(off):
        x1, x2 = wt_ref[off:off + half, :], wt_ref[off + half:off + 2 * half, :]
        return _mx(jnp.concatenate([x1, x2, x2, x1], axis=0).T)

    for h in range(n_heads):
        qa_ref[:, h * LANES:(h + 1) * LANES] = _mx(wt_ref[h * q_head:h * q_head + MLA_NOPE, :].T)
    for h in range(0, n_heads, 2):
        r0 = wt_ref[h * q_head + MLA_NOPE:(h + 1) * q_head, :]
        r1 = wt_ref[(h + 1) * q_head + MLA_NOPE:(h + 2) * q_head, :]
        dst = n_heads * MLA_NOPE + (h // 2) * LANES
        qa_ref[:, dst:dst + LANES] = _mx(jnp.concatenate([r0, r1], axis=0).T)
    off = n_heads * q_head
    panel(ckv_ref, off)
    off += KV_LORA
    kr_ref[...] = dup_rope(off)
    off += MLA_ROPE
    for ref in (ga_ref, qb_ref, kb_ref, vb_ref):
        panel(ref, off)
        off += wa
    pairs = IDX_HEADS // 2
    for j in range(pairs):
        lo = wt_ref[off + IDX_DIM * j:off + IDX_DIM * (j + 1), :]
        hi = wt_ref[off + IDX_DIM * (j + pairs):off + IDX_DIM * (j + pairs + 1), :]
        qi_ref[:, LANES * j:LANES * (j + 1)] = _mx(jnp.concatenate([lo, hi], axis=0).T)
    off += IDX_HEADS * IDX_DIM
    n_kw = IDX_DIM + IDX_HEADS
    kiw = jnp.concatenate([wt_ref[off:off + n_kw, :], jnp.zeros((LANES - n_kw, wt_ref.shape[1]), jnp.float32)], axis=0)
    kiw_ref[...] = _mx(kiw.T)
    off += n_kw
    panel(gb_ref, off)


def _pack_w_in(w_in, n_heads):
    wt = w_in.T
    n, d = wt.shape
    wa = n_heads * MLA_V
    assert n == n_heads * (MLA_NOPE + MLA_ROPE) + KV_LORA + MLA_ROPE + 5 * wa + IDX_HEADS * IDX_DIM + IDX_DIM + IDX_HEADS
    assert 2 * IDX_DIM == LANES and 2 * MLA_ROPE == LANES and MLA_NOPE == LANES
    slab = 256
    assert n_heads % 4 == 0
    widths = [n_heads * (MLA_NOPE + MLA_ROPE), KV_LORA, LANES, wa, wa, wa, wa, wa, IDX_HEADS * IDX_DIM, LANES]
    q_a, ckv, kr, g_a, g_b, q_b, k_b, v_b, q_i, kiw = pl.pallas_call(
        functools.partial(_pack_kernel, n_heads),
        grid=(d // slab,),
        in_specs=[pl.BlockSpec((n, slab), lambda i: (0, i))],
        out_specs=[pl.BlockSpec((slab, w), lambda i: (i, 0)) for w in widths],
        out_shape=[jax.ShapeDtypeStruct((d, w), MXU_DT) for w in widths],
        compiler_params=_cparams(1),
        name="pack_w_in",
    )(wt)
    return [q_a, ckv, kr, g_a, g_b], [q_b, k_b, v_b, q_i, kiw]


def _rope_table(pos):
    half = MLA_ROPE // 2
    freqs = jnp.power(ROPE_THETA, -jnp.arange(half, dtype=jnp.float32) / half)
    ang = pos.astype(jnp.float32)[:, None] * freqs
    cos, sin = jnp.cos(ang), jnp.sin(ang)
    return jnp.concatenate([cos, cos, -sin, sin], axis=1)


def _rel_bucket(rel):
    nb = N_BUCKETS // 2
    max_exact = nb // 2
    n = jnp.abs(rel)
    nf = jnp.maximum(n, 1).astype(jnp.float32)
    large = max_exact + (jnp.log(nf / max_exact) / math.log(MAX_DISTANCE / max_exact)
                         * (nb - max_exact)).astype(jnp.int32)
    large = jnp.minimum(large, nb - 1)
    return jnp.where(rel > 0, nb, 0) + jnp.where(n < max_exact, n, large)


def _bucket_bias(rel_bias, rel):
    return (rel_bias * LOG2E)[_rel_bucket(rel)]


def kernel(x_prompt, x_sample, cache_mla_ckv, cache_mla_krope, cache_dsa_k, cache_dsa_v, cache_idx_k,
           c_prompt, c_sample, w_ada, b_ada, ln_gain, w_in, mla_kv_gain, w_uk, w_uv, rel_bias,
           out_gain_a, out_gain_b, w_out, final_gain):
    assert w_ada.shape[0] == 1, "single-layer step"
    b, t, d = x_prompt.shape
    nb, ts, _ = x_sample.shape
    past = cache_mla_ckv.shape[2]
    n_heads = d // 256
    wb = n_heads * DSA_HEAD_DIM
    s_all = past + ts
    assert t % MLA_TQ == 0 and t % DSA_TQ == 0 and MLA_TQ % CHUNK == 0 and DSA_TQ % CHUNK == 0
    assert (s_all - 1) // CHUNK <= past // CHUNK

    mod = _ada_mod(jnp.concatenate([c_prompt, c_sample], axis=0), w_ada[0], b_ada[0])
    shift, scale, gate = mod[:, :d], mod[:, d:2 * d], mod[:, 2 * d:]
    mod_p = [v[:b].reshape(b, 1, d) for v in (scale, shift, gate)]
    mod_s = [jnp.broadcast_to(v[b:, None, :], (nb, ts, d)).reshape(1, nb * ts, d) for v in (scale, shift, gate)]

    w_a, w_b = _pack_w_in(w_in[0], n_heads)
    wkv = _mx(jnp.concatenate([w_uk[0].transpose(2, 0, 1).reshape(KV_LORA, n_heads * MLA_NOPE),
                               w_uv[0].transpose(2, 0, 1).reshape(KV_LORA, n_heads * MLA_V)], axis=1))
    w_o = _mx(w_out[0])
    pos_p = jnp.arange(t, dtype=jnp.int32)
    pos_s = past + jnp.arange(ts, dtype=jnp.int32)
    tab_p = _rope_table(pos_p)
    tab_s = jnp.tile(_rope_table(pos_s), (nb, 1))

    tm = 256
    qcat, ckv_p, krope_p, ga, gb, kcat, vup, oa_buf = _proj_a(
        x_prompt, mod_p[0], mod_p[1], ln_gain[0], tab_p, w_a, mla_kv_gain[0], wkv, tm)
    qb, k_p, v_p, kbb, vbb, qi, idxk_p, kiab, wi, ob_buf = _proj_b(
        x_prompt, mod_p[0], mod_p[1], ln_gain[0], w_b, tm, True)
    o_a = _mla_prompt(qcat, kcat, vup, oa_buf, b, t)
    near = _bucket_bias(rel_bias, jnp.arange(-MAX_DISTANCE - 1, MAX_DISTANCE, dtype=jnp.int32))
    far = near[:1]
    bias_tab = jnp.concatenate([jnp.broadcast_to(far, (t - 1 - MAX_DISTANCE, n_heads)), near[1:],
                                jnp.broadcast_to(far, (t + 1 - MAX_DISTANCE, n_heads))], axis=0).T
    o_b = _dsa_prompt(qb, qi, wi, kbb, vbb, kiab, bias_tab, ob_buf, b, t, min(TOPK_MAX, t // 4))
    y_prompt = _out_proj(o_a, o_b, ga, gb, x_prompt, mod_p[2], out_gain_a[0], out_gain_b[0], w_o, final_gain, 2 * tm)

    xs3 = x_sample.reshape(1, nb * ts, d)
    tms = min(256, nb * ts)
    qcat_s, ckv_s, krope_s, ga_s, gb_s = _proj_a(
        xs3, mod_s[0], mod_s[1], ln_gain[0], tab_s, w_a, mla_kv_gain[0], None, tms)
    qb_s, k_s, v_s, kbb_s, vbb_s, qi_s, idxk_s, kiab_s, wi_s = _proj_b(
        xs3, mod_s[0], mod_s[1], ln_gain[0], w_b, tms, False)
    oa_s = _mla_sample(qcat_s, cache_mla_ckv[0], cache_mla_krope[0], ckv_s, krope_s,
                       _mx(w_uk[0]), _mx(w_uv[0]), ts)
    s_pad = pl.cdiv(s_all, LANES) * LANES
    tab_s = _bucket_bias(rel_bias, jnp.arange(s_pad + LANES, dtype=jnp.int32) - (s_all - 1)).T
    madd_s = _dsa_select(qi_s, wi_s, cache_idx_k[0], kiab_s, ts, min(TOPK_MAX, s_all // 4))
    ob_s = _dsa_sample(qb_s, madd_s, cache_dsa_k[0], cache_dsa_v[0], kbb_s, vbb_s, tab_s, ts)
    y_sample = _out_proj(oa_s, ob_s, ga_s, gb_s, xs3, mod_s[2], out_gain_a[0], out_gain_b[0], w_o, final_gain, tms)

    hd = (n_heads, DSA_HEAD_DIM)
    return (y_prompt, y_sample.reshape(nb, ts, d),
            ckv_p.reshape(1, b, t, KV_LORA), krope_p.reshape(1, b, t, MLA_ROPE),
            k_p.reshape(1, b, t, *hd), v_p.reshape(1, b, t, *hd), idxk_p.reshape(1, b, t, IDX_DIM),
            ckv_s.reshape(1, nb, ts, KV_LORA), krope_s.reshape(1, nb, ts, MLA_ROPE),
            k_s.reshape(1, nb, ts, *hd), v_s.reshape(1, nb, ts, *hd), idxk_s.reshape(1, nb, ts, IDX_DIM))
```

```python
import functools
import math

import jax
import jax.numpy as jnp
from jax import lax
from jax.experimental import pallas as pl
from jax.experimental.pallas import tpu as pltpu

MXU_DT = jnp.bfloat16

CHUNK = 64
MLA_NOPE = 128
MLA_ROPE = 64
MLA_V = 128
KV_LORA = 512
DSA_HEAD_DIM = 128
IDX_HEADS = 16
IDX_DIM = 64
TOPK_MAX = 256
N_BUCKETS = 32
MAX_DISTANCE = 128
ROPE_THETA = 10000.0
EPS = 1e-6

LANES = 128
HEAD_SLOT = 256
NEG_BIG = -1e30
LOG2E = math.log2(math.e)
INT_MIN = -2 ** 31
KEY_NEG_INF = INT_MIN + 0x7FFFFF
VMEM_LIMIT = 56 * 1024 * 1024


def _cparams(n_grid, vmem=VMEM_LIMIT):
    return pltpu.CompilerParams(dimension_semantics=("arbitrary",) * n_grid, vmem_limit_bytes=vmem)


def _mx(v):
    return v.astype(MXU_DT)


def _dot(a, b):
    return jnp.dot(a, b, preferred_element_type=jnp.float32)


def _dot_nt(a, b):
    return lax.dot_general(a, b, (((1,), (1,)), ((), ())), preferred_element_type=jnp.float32)


def _silu(v):
    return v * (1.0 / (1.0 + jnp.exp(-v)))


def _resident(shape):
    nd = len(shape)
    return pl.BlockSpec(shape, lambda *_: (0,) * nd, pipeline_mode=pl.Buffered(1))


def _ada_kernel(c_ref, w_ref, b_ref, o_ref):
    a = _mx(_silu(c_ref[...]))
    o_ref[...] = _dot(a, _mx(w_ref[...])) + b_ref[...]


def _ada_mod(c_all, w_ada, b_ada):
    m, d = c_all.shape
    n = w_ada.shape[1]
    tn = 1024
    return pl.pallas_call(
        _ada_kernel,
        grid=(n // tn,),
        in_specs=[pl.BlockSpec((m, d), lambda j: (0, 0)),
                  pl.BlockSpec((d, tn), lambda j: (0, j)),
                  pl.BlockSpec((1, tn), lambda j: (0, j))],
        out_specs=pl.BlockSpec((m, tn), lambda j: (0, j)),
        out_shape=jax.ShapeDtypeStruct((m, n), jnp.float32),
        compiler_params=_cparams(1),
        name="ada_mod",
    )(c_all, w_ada, b_ada.reshape(1, n))


def _modulated_norm(x_ref, scale_ref, shift_ref, lng_ref):
    x = x_ref[...]
    xn = x * lax.rsqrt(jnp.mean(x * x, axis=-1, keepdims=True) + EPS) * lng_ref[...]
    return _mx(xn * (1.0 + scale_ref[...]) + shift_ref[...])


def _rope128(a, tab):
    t = a * tab
    return t + pltpu.roll(t, 64, 1)


def _proj_a_kernel(n_heads, emit_kv, x_ref, scale_ref, shift_ref, lng_ref, tab_ref,
                   wq_ref, wc_ref, wr_ref, wga_ref, wgb_ref, kvg_ref, *rest):
    if emit_kv:
        wkv_ref, qcat_ref, ckv_ref, krope_ref, ga_ref, gb_ref, kcat_ref, vup_ref, attn_ref = rest
        attn_ref[...] = jnp.zeros_like(attn_ref)
    else:
        qcat_ref, ckv_ref, krope_ref, ga_ref, gb_ref = rest
    hb = _modulated_norm(x_ref, scale_ref, shift_ref, lng_ref)
    tab = tab_ref[...]
    qscale = (MLA_NOPE + MLA_ROPE) ** -0.5 * LOG2E
    for h in range(0, n_heads, 2):
        a = _dot(hb, wq_ref[:, h * LANES:(h + 2) * LANES])
        qcat_ref[:, h * HEAD_SLOT:h * HEAD_SLOT + LANES] = _mx(a[:, :LANES] * qscale)
        qcat_ref[:, (h + 1) * HEAD_SLOT:(h + 1) * HEAD_SLOT + LANES] = _mx(a[:, LANES:] * qscale)
    lane = lax.broadcasted_iota(jnp.int32, tab.shape, 1)
    tab_r = pltpu.roll(tab, MLA_ROPE, 1)
    cos4 = jnp.where(lane < MLA_ROPE, tab, tab_r)
    sin4 = jnp.where(lane < MLA_ROPE, tab_r, tab)
    is_x1 = lane % MLA_ROPE < MLA_ROPE // 2
    n_nope = n_heads * MLA_NOPE
    for h in range(0, n_heads, 4):
        a = _dot(hb, wq_ref[:, n_nope + (h // 2) * LANES:n_nope + (h // 2 + 2) * LANES])
        for u in range(2):
            ap = a[:, u * LANES:(u + 1) * LANES]
            swapped = jnp.where(is_x1, pltpu.roll(ap, LANES - MLA_ROPE // 2, 1), pltpu.roll(ap, MLA_ROPE // 2, 1))
            roped = (ap * cos4 + swapped * sin4) * qscale
            c0 = (h + 2 * u) * HEAD_SLOT
            qcat_ref[:, c0 + LANES:c0 + HEAD_SLOT] = _mx(roped)
            qcat_ref[:, c0 + HEAD_SLOT + LANES:c0 + 2 * HEAD_SLOT] = _mx(pltpu.roll(roped, MLA_ROPE, 1))
    c = _dot(hb, wc_ref[...])
    cn = c * lax.rsqrt(jnp.mean(c * c, axis=-1, keepdims=True) + EPS) * kvg_ref[...]
    ckv_ref[...] = cn
    r = _rope128(_dot(hb, wr_ref[...]), tab)
    krope_ref[...] = r[:, :MLA_ROPE]
    ga_ref[...] = _mx(_silu(_dot(hb, wga_ref[...])))
    gb_ref[...] = _mx(_silu(_dot(hb, wgb_ref[...])))
    if emit_kv:
        cb = _mx(cn)
        lane = lax.broadcasted_iota(jnp.int32, r.shape, 1)
        krz = _mx(jnp.where(lane < MLA_ROPE, r, 0.0))
        kn = _dot(cb, wkv_ref[:, :n_heads * MLA_NOPE])
        for h in range(n_heads):
            c0 = h * HEAD_SLOT
            kcat_ref[:, c0:c0 + LANES] = _mx(kn[:, h * MLA_NOPE:(h + 1) * MLA_NOPE])
            kcat_ref[:, c0 + LANES:c0 + HEAD_SLOT] = krz
        vup_ref[...] = _mx(_dot(cb, wkv_ref[:, n_heads * MLA_NOPE:]))


def _proj_b_kernel(n_heads, x_ref, scale_ref, shift_ref, lng_ref, wq_ref, wk_ref, wv_ref, wqi_ref, wkw_ref,
                   qb_ref, kb_ref, vb_ref, kbb_ref, vbb_ref, qi_ref, ki_ref, kiab_ref, wi_ref, *attn_ref):
    if attn_ref:
        attn_ref[0][...] = jnp.zeros_like(attn_ref[0])
    hb = _modulated_norm(x_ref, scale_ref, shift_ref, lng_ref)
    dscale = DSA_HEAD_DIM ** -0.5 * LOG2E
    step = 512
    for c0 in range(0, wq_ref.shape[1], step):
        qb_ref[:, c0:c0 + step] = _mx(_dot(hb, wq_ref[:, c0:c0 + step]) * dscale)
    for w_ref, f_ref, b_ref in ((wk_ref, kb_ref, kbb_ref), (wv_ref, vb_ref, vbb_ref)):
        for c0 in range(0, w_ref.shape[1], step):
            kv = _dot(hb, w_ref[:, c0:c0 + step])
            f_ref[:, c0:c0 + step] = kv
            b_ref[:, c0:c0 + step] = _mx(kv)
    for c0 in range(0, wqi_ref.shape[1], step):
        qi_ref[:, c0:c0 + step] = _mx(_dot(hb, wqi_ref[:, c0:c0 + step]))
    a = _dot(hb, wkw_ref[...])
    ki_ref[...] = a[:, :IDX_DIM]
    lane = lax.broadcasted_iota(jnp.int32, a.shape, 1)
    kz = jnp.where(lane < IDX_DIM, a, 0.0)
    kiab_ref[:, :LANES] = _mx(kz)
    kiab_ref[:, LANES:] = _mx(pltpu.roll(kz, IDX_DIM, 1))
    wi_ref[...] = a * (IDX_HEADS ** -0.5 * IDX_DIM ** -0.5)


def _row_specs(x3, mod_rows, tm):
    bv, tv, d = x3.shape
    x_spec = pl.BlockSpec((None, tm, d), lambda b, i: (b, i, 0))
    if mod_rows == 1:
        m_spec = pl.BlockSpec((None, 1, d), lambda b, i: (b, 0, 0))
    else:
        m_spec = pl.BlockSpec((None, tm, d), lambda b, i: (b, i, 0))
    return x_spec, m_spec


def _out2d(m, width, dtype, tm, nt):
    return (jax.ShapeDtypeStruct((m, width), dtype),
            pl.BlockSpec((tm, width), lambda b, i: (b * nt + i, 0)))


def _proj_a(x3, scale, shift, ln_gain, tab, ws_a, kv_gain, wkv, tm):
    bv, tv, d = x3.shape
    nt = tv // tm
    m = bv * tv
    n_heads = d // 256
    wa = n_heads * MLA_V
    emit_kv = wkv is not None
    x_spec, m_spec = _row_specs(x3, scale.shape[1], tm)
    in_specs = ([x_spec, m_spec, m_spec, _resident((1, d)), pl.BlockSpec((tm, LANES), lambda b, i: (i, 0))]
                + [_resident(w.shape) for w in ws_a] + [_resident((1, KV_LORA))])
    args = [x3, scale, shift, ln_gain.reshape(1, d), tab, *ws_a, kv_gain.reshape(1, KV_LORA)]
    outs = [_out2d(m, n_heads * HEAD_SLOT, MXU_DT, tm, nt),
            _out2d(m, KV_LORA, jnp.float32, tm, nt),
            _out2d(m, MLA_ROPE, jnp.float32, tm, nt),
            _out2d(m, wa, MXU_DT, tm, nt),
            _out2d(m, wa, MXU_DT, tm, nt)]
    if emit_kv:
        in_specs.append(_resident(wkv.shape))
        args.append(wkv)
        outs += [_out2d(m, n_heads * HEAD_SLOT, MXU_DT, tm, nt),
                 _out2d(m, wa, MXU_DT, tm, nt),
                 _out2d(m, wa, MXU_DT, tm, nt)]
    return pl.pallas_call(
        functools.partial(_proj_a_kernel, n_heads, emit_kv),
        grid=(bv, nt),
        in_specs=in_specs,
        out_specs=[o[1] for o in outs],
        out_shape=[o[0] for o in outs],
        compiler_params=_cparams(2),
        name="proj_a",
    )(*args)


def _proj_b(x3, scale, shift, ln_gain, ws_b, tm, emit_attn_buffer):
    bv, tv, d = x3.shape
    nt = tv // tm
    m = bv * tv
    n_heads = d // 256
    width_b = n_heads * DSA_HEAD_DIM
    x_spec, m_spec = _row_specs(x3, scale.shape[1], tm)
    outs = [_out2d(m, width_b, MXU_DT, tm, nt),
            _out2d(m, width_b, jnp.float32, tm, nt),
            _out2d(m, width_b, jnp.float32, tm, nt),
            _out2d(m, width_b, MXU_DT, tm, nt),
            _out2d(m, width_b, MXU_DT, tm, nt),
            _out2d(m, IDX_HEADS * IDX_DIM, MXU_DT, tm, nt),
            _out2d(m, IDX_DIM, jnp.float32, tm, nt),
            _out2d(m, 2 * LANES, MXU_DT, tm, nt),
            _out2d(m, LANES, jnp.float32, tm, nt)]
    if emit_attn_buffer:
        outs.append(_out2d(m, width_b, MXU_DT, tm, nt))
    return pl.pallas_call(
        functools.partial(_proj_b_kernel, n_heads),
        grid=(bv, nt),
        in_specs=[x_spec, m_spec, m_spec, _resident((1, d))] + [_resident(w.shape) for w in ws_b],
        out_specs=[o[1] for o in outs],
        out_shape=[o[0] for o in outs],
        compiler_params=_cparams(2),
        name="proj_b",
    )(x3, scale, shift, ln_gain.reshape(1, d), *ws_b)


def _softmax_pv(s, v):
    m = jnp.max(s, axis=-1, keepdims=True)
    p = jnp.exp2(s - m)
    l = jnp.sum(p, axis=-1, keepdims=True)
    return _dot(_mx(p), v) * (1.0 / l)


def _topk_mask(score, adm, topk, sc_ref, madd_ref):
    rows, n = score.shape
    if n <= topk:
        madd_ref[...] = jnp.where(adm, 0.0, NEG_BIG)
        return
    sc_ref[...] = jnp.where(adm, score, -jnp.inf)
    kf = float(topk)

    def count(pred):
        return jnp.sum(jnp.where(pred, 1.0, 0.0), axis=-1, keepdims=True)

    def key_to_float(key):
        return pltpu.bitcast(key ^ ((key >> 31) & 0x7FFFFFFF), jnp.float32)

    def bit_step(i, thr):
        inc = lax.shift_left(jnp.int32(1), 31 - i)
        cand = thr + inc
        ok = count(sc_ref[...] >= key_to_float(cand)) >= kf
        return jnp.where(ok, cand, thr)

    thr = lax.fori_loop(0, 32, bit_step, jnp.full((rows, 1), INT_MIN, jnp.int32))
    thr = key_to_float(jnp.maximum(thr, KEY_NEG_INF))
    ge = sc_ref[...] >= thr
    madd_ref[...] = jnp.where(ge, 0.0, NEG_BIG)

    @pl.when(jnp.max(count(ge)) > kf)
    def _():
        sc = sc_ref[...]
        gt = sc > thr
        need = kf - count(gt)
        eqf = jnp.where(sc == thr, 1.0, 0.0)
        col = lax.broadcasted_iota(jnp.int32, (rows, n), 1)
        nbits = max(1, int(n).bit_length())

        def col_step(i, bound):
            cand = bound + lax.shift_left(jnp.int32(1), nbits - 1 - i)
            taken = jnp.sum(jnp.where(col < cand, eqf, 0.0), axis=-1, keepdims=True)
            return jnp.where(taken <= need, cand, bound)

        bound = lax.fori_loop(0, nbits, col_step, jnp.zeros((rows, 1), jnp.int32))
        tie_madd = jnp.where(jnp.where(col < bound, eqf, 0.0) > 0.5, 0.0, NEG_BIG)
        madd_ref[...] = jnp.where(adm, jnp.where(gt, 0.0, tie_madd), NEG_BIG)


def _indexer_scores(qi_ref, wi, kia, kib):
    half = IDX_HEADS // 2
    score = None
    for j in range(half):
        qp = qi_ref[:, j * LANES:(j + 1) * LANES]
        da = jnp.maximum(_dot_nt(qp, kia), 0.0) * wi[:, IDX_DIM + j:IDX_DIM + j + 1]
        db = jnp.maximum(_dot_nt(qp, kib), 0.0) * wi[:, IDX_DIM + half + j:IDX_DIM + half + j + 1]
        score = da + db if score is None else score + da + db
    return score


def _indexer_scores_stacked(qi_ref, wi, kia, kib):
    half = IDX_HEADS // 2
    ts = qi_ref.shape[0]
    q_all = jnp.concatenate([qi_ref[:, j * LANES:(j + 1) * LANES] for j in range(half)], axis=0)
    wa = jnp.concatenate([wi[:, IDX_DIM + j:IDX_DIM + j + 1] for j in range(half)], axis=0)
    wb = jnp.concatenate([wi[:, IDX_DIM + half + j:IDX_DIM + half + j + 1] for j in range(half)], axis=0)
    part = jnp.maximum(_dot_nt(q_all, kia), 0.0) * wa + jnp.maximum(_dot_nt(q_all, kib), 0.0) * wb
    score = part[0:ts]
    for j in range(1, half):
        score = score + part[j * ts:(j + 1) * ts]
    return score


MLA_TQ = 512
DSA_TQ = 256
DSA_PAIR_MAX_KEYS = 1024


def _chunk_madd(row0, tq, n_keys):
    qc = (row0 + lax.broadcasted_iota(jnp.int32, (tq, n_keys), 0)) // CHUNK
    kc = lax.broadcasted_iota(jnp.int32, (tq, n_keys), 1) // CHUNK
    return kc <= qc


def _chained_tile_calls(n_tiles, make_call, out):
    for c in range(n_tiles):
        out = make_call(c, out)
    return out


def _mla_prompt_kernel(n_heads, row0, q_ref, k_ref, v_ref, prev_ref, o_ref):
    del prev_ref
    tq, n_keys = q_ref.shape[0], k_ref.shape[0]
    madd = jnp.where(_chunk_madd(row0, tq, n_keys), 0.0, NEG_BIG)
    for h in range(n_heads):
        c0 = h * HEAD_SLOT
        s = _dot_nt(q_ref[:, c0:c0 + HEAD_SLOT], k_ref[:, c0:c0 + HEAD_SLOT]) + madd
        o_ref[:, h * MLA_V:(h + 1) * MLA_V] = _mx(_softmax_pv(s, v_ref[:, h * MLA_V:(h + 1) * MLA_V]))


def _mla_prompt(qcat, kcat, vup, out_init, b, t):
    tq = MLA_TQ
    n_heads = qcat.shape[1] // HEAD_SLOT
    nq = t // tq
    wa = n_heads * MLA_V
    k3 = kcat.reshape(b, t, kcat.shape[1])
    v3 = vup.reshape(b, t, wa)

    def make_call(c, out):
        n_keys = (c + 1) * tq
        row = lambda bi: (bi * nq + c, 0)
        return pl.pallas_call(
            functools.partial(_mla_prompt_kernel, n_heads, c * tq),
            grid=(b,),
            in_specs=[pl.BlockSpec((tq, qcat.shape[1]), row),
                      pl.BlockSpec((None, n_keys, kcat.shape[1]), lambda bi: (bi, 0, 0)),
                      pl.BlockSpec((None, n_keys, wa), lambda bi: (bi, 0, 0)),
                      pl.BlockSpec(memory_space=pl.ANY)],
            out_specs=pl.BlockSpec((tq, wa), row),
            out_shape=jax.ShapeDtypeStruct(out.shape, out.dtype),
            input_output_aliases={3: 0},
            compiler_params=_cparams(1),
            name="mla_prompt_%d" % c,
        )(qcat, k3, v3, out)

    return _chained_tile_calls(nq, make_call, out_init)


def _toeplitz_bias(tab_ref, h, start, n_rows, n_cols):
    shift0 = LANES - (n_rows - 1)
    from_left = (lax.broadcasted_iota(jnp.int32, (n_rows, LANES), 1)
                 < shift0 + lax.broadcasted_iota(jnp.int32, (n_rows, LANES), 0))
    pieces = []
    prev = None
    for n in range(n_cols // LANES + 1):
        seg = tab_ref[h:h + 1, start + n * LANES:start + (n + 1) * LANES]
        rot = pltpu.roll(jnp.broadcast_to(seg, (n_rows, LANES)), shift0 % LANES, 1, stride=1, stride_axis=0)
        if prev is not None:
            pieces.append(jnp.where(from_left, prev, rot))
        prev = rot
    return jnp.concatenate(pieces, axis=1)


def _dsa_prompt_kernel(n_heads, row0, t, topk, qb_ref, qi_ref, wi_ref, k_ref, v_ref, kiab_ref, tab_ref, prev_ref,
                       o_ref, key_ref, madd_ref):
    del prev_ref
    per, tq, _ = qb_ref.shape
    n_keys = k_ref.shape[1]
    score = jnp.concatenate(
        [_indexer_scores(qi_ref.at[u], wi_ref[u], kiab_ref[u, :, :LANES], kiab_ref[u, :, LANES:])
         for u in range(per)], axis=0)
    adm = _chunk_madd(row0, tq, n_keys)
    _topk_mask(score, jnp.concatenate([adm] * per, axis=0), topk, key_ref, madd_ref)
    near0 = max(0, row0 - LANES)
    n_near = n_keys - near0
    for h in range(n_heads):
        c0 = h * DSA_HEAD_DIM
        near = jnp.concatenate(
            [_toeplitz_bias(tab_ref, h, near0 - (row0 + u * LANES) - LANES + t, LANES, n_near)
             for u in range(tq // LANES)], axis=0)
        if near0 > 0:
            far = jnp.broadcast_to(tab_ref[h:h + 1, 0:1], (tq, near0))
            bias = jnp.concatenate([far, near], axis=1)
        else:
            bias = near
        for u in range(per):
            s = (_dot_nt(qb_ref[u, :, c0:c0 + DSA_HEAD_DIM], k_ref[u, :, c0:c0 + DSA_HEAD_DIM]) + bias
                 + madd_ref[u * tq:(u + 1) * tq, :])
            o_ref[u, :, c0:c0 + DSA_HEAD_DIM] = _mx(_softmax_pv(s, v_ref[u, :, c0:c0 + DSA_HEAD_DIM]))


def _dsa_prompt(qb, qi, wi, kbb, vbb, kiab, bias_tab, out_init, b, t, topk):
    tq = DSA_TQ
    assert MAX_DISTANCE <= LANES and tq % LANES == 0
    n_heads = qb.shape[1] // DSA_HEAD_DIM
    nq = t // tq
    wb = qb.shape[1]
    k3, v3, ki3 = kbb.reshape(b, t, wb), vbb.reshape(b, t, wb), kiab.reshape(b, t, 2 * LANES)
    qb3, qi3, wi3 = qb.reshape(b, t, wb), qi.reshape(b, t, qi.shape[1]), wi.reshape(b, t, LANES)
    keys_of = lambda bi: (bi, 0, 0)

    def make_call(c, out):
        n_keys = (c + 1) * tq
        per = 2 if (b % 2 == 0 and n_keys <= DSA_PAIR_MAX_KEYS) else 1
        rows_of = lambda bi: (bi, c, 0)
        return pl.pallas_call(
            functools.partial(_dsa_prompt_kernel, n_heads, c * tq, t, topk),
            grid=(b // per,),
            in_specs=[pl.BlockSpec((per, tq, wb), rows_of),
                      pl.BlockSpec((per, tq, qi.shape[1]), rows_of),
                      pl.BlockSpec((per, tq, LANES), rows_of),
                      pl.BlockSpec((per, n_keys, wb), keys_of),
                      pl.BlockSpec((per, n_keys, wb), keys_of),
                      pl.BlockSpec((per, n_keys, 2 * LANES), keys_of),
                      _resident(bias_tab.shape),
                      pl.BlockSpec(memory_space=pl.ANY)],
            out_specs=pl.BlockSpec((per, tq, wb), rows_of),
            out_shape=jax.ShapeDtypeStruct(out.shape, out.dtype),
            input_output_aliases={7: 0},
            scratch_shapes=[pltpu.VMEM((per * tq, n_keys), jnp.float32),
                            pltpu.VMEM((per * tq, n_keys), jnp.float32)],
            compiler_params=_cparams(1),
            name="dsa_prompt_%d" % c,
        )(qb3, qi3, wi3, k3, v3, ki3, bias_tab, out)

    return _chained_tile_calls(nq, make_call, out_init.reshape(b, t, wb)).reshape(b * t, wb)


def _mla_sample_kernel(n_heads, past, ts, qcat_ref, cckv_ref, ckr_ref, nckv_ref, nkr_ref, wuk_ref, wuv_ref,
                       o_ref, kall_ref, rall_ref):
    n_keys = past + ts

    @pl.when(pl.program_id(0) == 0)
    def _():
        kall_ref[...] = jnp.zeros_like(kall_ref)
        rall_ref[...] = jnp.zeros_like(rall_ref)

    for u in range(cckv_ref.shape[0]):
        rows = slice(u * ts, (u + 1) * ts)
        kall_ref[u, 0:past, :] = _mx(cckv_ref[u])
        kall_ref[u, past:n_keys, :] = _mx(nckv_ref[rows, :])
        rall_ref[u, 0:past, 0:MLA_ROPE] = _mx(ckr_ref[u])
        rall_ref[u, past:n_keys, 0:MLA_ROPE] = _mx(nkr_ref[rows, :])
        qlat, qrope = [], []
        for h in range(n_heads):
            c0 = h * HEAD_SLOT
            qlat.append(_mx(_dot(qcat_ref[rows, c0:c0 + LANES], wuk_ref[h])))
            qrope.append(qcat_ref[rows, c0 + LANES:c0 + HEAD_SLOT])
        qlat = jnp.concatenate(qlat, axis=0)
        qrope = jnp.concatenate(qrope, axis=0)
        s = _dot_nt(qlat, kall_ref[u]) + _dot_nt(qrope, rall_ref[u])
        col = lax.broadcasted_iota(jnp.int32, s.shape, 1)
        s = jnp.where(col < n_keys, s, NEG_BIG)
        olat = _mx(_softmax_pv(s, kall_ref[u]))
        for h in range(n_heads):
            o_ref[rows, h * MLA_V:(h + 1) * MLA_V] = _mx(_dot_nt(olat[h * ts:(h + 1) * ts], wuv_ref[h]))


def _mla_sample(qcat, cache_ckv, cache_kr, new_ckv, new_kr, wuk, wuv, ts):
    nb, past, c = cache_ckv.shape
    n_heads = qcat.shape[1] // HEAD_SLOT
    wa = n_heads * MLA_V
    s_pad = pl.cdiv(past + ts, LANES) * LANES
    per = 2 if nb % 2 == 0 else 1
    row = lambda bi: (bi, 0)
    return pl.pallas_call(
        functools.partial(_mla_sample_kernel, n_heads, past, ts),
        grid=(nb // per,),
        in_specs=[pl.BlockSpec((per * ts, qcat.shape[1]), row),
                  pl.BlockSpec((per, past, c), lambda bi: (bi, 0, 0)),
                  pl.BlockSpec((per, past, MLA_ROPE), lambda bi: (bi, 0, 0)),
                  pl.BlockSpec((per * ts, c), row),
                  pl.BlockSpec((per * ts, MLA_ROPE), row),
                  _resident(wuk.shape), _resident(wuv.shape)],
        out_specs=pl.BlockSpec((per * ts, wa), row),
        out_shape=jax.ShapeDtypeStruct((nb * ts, wa), MXU_DT),
        scratch_shapes=[pltpu.VMEM((per, s_pad, c), MXU_DT), pltpu.VMEM((per, s_pad, LANES), MXU_DT)],
        compiler_params=_cparams(1),
        name="mla_sample",
    )(qcat, cache_ckv, cache_kr, new_ckv, new_kr, wuk, wuv)


def _dsa_select_kernel(past, ts, topk, qi_ref, wi_ref, cki_ref, nkiab_ref, madd_ref, kia_ref, kib_ref, key_ref):
    bi = pl.program_id(0)
    n_keys = past + ts
    rows, s_pad = madd_ref.shape

    @pl.when(bi == 0)
    def _():
        kia_ref[...] = jnp.zeros_like(kia_ref)
        kib_ref[...] = jnp.zeros_like(kib_ref)

    cki = _mx(cki_ref[...])
    kia_ref[0:past, 0:IDX_DIM] = cki
    kib_ref[0:past, IDX_DIM:LANES] = cki
    kia_ref[past:n_keys, :] = nkiab_ref[:, :LANES]
    kib_ref[past:n_keys, :] = nkiab_ref[:, LANES:]
    madd_ref[pl.ds(pl.multiple_of(bi * ts, ts), ts), :] = _indexer_scores_stacked(
        qi_ref, wi_ref[...], kia_ref[...], kib_ref[...])

    @pl.when(bi == pl.num_programs(0) - 1)
    def _():
        col = lax.broadcasted_iota(jnp.int32, (rows, s_pad), 1)
        _topk_mask(madd_ref[...], col < n_keys, topk, key_ref, madd_ref)


def _dsa_select(qi, wi, cache_ki, new_kiab, ts, topk):
    nb, past, _ = cache_ki.shape
    s_pad = pl.cdiv(past + ts, LANES) * LANES
    row = lambda bi: (bi, 0)
    return pl.pallas_call(
        functools.partial(_dsa_select_kernel, past, ts, topk),
        grid=(nb,),
        in_specs=[pl.BlockSpec((ts, qi.shape[1]), row),
                  pl.BlockSpec((ts, LANES), row),
                  pl.BlockSpec((None, past, IDX_DIM), lambda bi: (bi, 0, 0)),
                  pl.BlockSpec((ts, 2 * LANES), row)],
        out_specs=pl.BlockSpec((nb * ts, s_pad), lambda bi: (0, 0)),
        out_shape=jax.ShapeDtypeStruct((nb * ts, s_pad), jnp.float32),
        scratch_shapes=[pltpu.VMEM((s_pad, LANES), MXU_DT), pltpu.VMEM((s_pad, LANES), MXU_DT),
                        pltpu.VMEM((nb * ts, s_pad), jnp.float32)],
        compiler_params=_cparams(1),
        name="dsa_select",
    )(qi, wi, cache_ki, new_kiab)


def _dsa_sample_kernel(n_heads, past, ts, qb_ref, madd_ref, ck_ref, cv_ref, nk_ref, nv_ref, tab_ref, expand_ref,
                       o_ref, kflat_ref, vflat_ref, bias_ref, biasw_ref):
    n_keys = past + ts
    wide = n_heads * LANES
    n_blocks = pl.cdiv(n_keys, LANES)
    widths = [min(wide, (n_keys - j * LANES) * n_heads) for j in range(n_blocks)]

    @pl.when(pl.program_id(0) == 0)
    def _():
        for h in range(n_heads):
            bias_ref[h * ts:(h + 1) * ts, :] = _toeplitz_bias(tab_ref, h, 0, ts, n_blocks * LANES)
        shape = (n_heads * ts, wide)
        same_head = (lax.broadcasted_iota(jnp.int32, shape, 0) // ts
                     == lax.broadcasted_iota(jnp.int32, shape, 1) % n_heads)
        for j in range(n_blocks):
            b = bias_ref[:, j * LANES:(j + 1) * LANES]
            hi = _mx(b)
            rest = b - hi.astype(jnp.float32)
            mid = _mx(rest)
            lo = _mx(rest - mid.astype(jnp.float32))
            piece = _dot(hi, expand_ref[...]) + _dot(mid, expand_ref[...]) + _dot(lo, expand_ref[...])
            biasw_ref[:, j * wide:j * wide + widths[j]] = jnp.where(same_head, piece, NEG_BIG)[:, :widths[j]]

    kflat_ref[0:past * n_heads, :] = _mx(ck_ref[...])
    kflat_ref[past * n_heads:, :] = nk_ref[...]
    vflat_ref[0:past * n_heads, :] = _mx(cv_ref[...])
    vflat_ref[past * n_heads:, :] = nv_ref[...]
    sel = _mx(jnp.where(madd_ref[...] == 0.0, 1.0, 0.0))
    pieces = [_dot(sel[:, j * LANES:(j + 1) * LANES], expand_ref[...])[:, :widths[j]] for j in range(n_blocks)]
    sel_wide = jnp.concatenate(pieces, axis=1)
    sel_wide = jnp.concatenate([sel_wide] * n_heads, axis=0)
    q_all = jnp.concatenate([qb_ref[:, h * DSA_HEAD_DIM:(h + 1) * DSA_HEAD_DIM] for h in range(n_heads)], axis=0)
    s = _dot_nt(q_all, kflat_ref[...]) + biasw_ref[...] + jnp.where(sel_wide > 0.5, 0.0, NEG_BIG)
    o = _mx(_softmax_pv(s, vflat_ref[...]))
    for h in range(n_heads):
        o_ref[:, h * DSA_HEAD_DIM:(h + 1) * DSA_HEAD_DIM] = o[h * ts:(h + 1) * ts]


def _dsa_sample(qb, madd, cache_k, cache_v, new_k, new_v, bias_tab, ts):
    nb, past, n_heads, _ = cache_k.shape
    wb = n_heads * DSA_HEAD_DIM
    n_keys = past + ts
    assert (n_keys % LANES * n_heads) % LANES == 0
    expand = _mx(jnp.repeat(jnp.eye(LANES, dtype=jnp.float32), n_heads, axis=1))
    row = lambda bi: (bi, 0)
    per_b = lambda bi: (bi, 0, 0)
    return pl.pallas_call(
        functools.partial(_dsa_sample_kernel, n_heads, past, ts),
        grid=(nb,),
        in_specs=[pl.BlockSpec((ts, wb), row),
                  pl.BlockSpec((ts, madd.shape[1]), row),
                  pl.BlockSpec((None, past * n_heads, DSA_HEAD_DIM), per_b),
                  pl.BlockSpec((None, past * n_heads, DSA_HEAD_DIM), per_b),
                  pl.BlockSpec((ts * n_heads, DSA_HEAD_DIM), row),
                  pl.BlockSpec((ts * n_heads, DSA_HEAD_DIM), row),
                  _resident(bias_tab.shape), _resident(expand.shape)],
        out_specs=pl.BlockSpec((ts, wb), row),
        out_shape=jax.ShapeDtypeStruct((nb * ts, wb), MXU_DT),
        scratch_shapes=[pltpu.VMEM((n_keys * n_heads, DSA_HEAD_DIM), MXU_DT),
                        pltpu.VMEM((n_keys * n_heads, DSA_HEAD_DIM), MXU_DT),
                        pltpu.VMEM((n_heads * ts, pl.cdiv(n_keys, LANES) * LANES), jnp.float32),
                        pltpu.VMEM((n_heads * ts, n_keys * n_heads), jnp.float32)],
        compiler_params=_cparams(1),
        name="dsa_sample",
    )(qb, madd, cache_k.reshape(nb, past * n_heads, DSA_HEAD_DIM), cache_v.reshape(nb, past * n_heads, DSA_HEAD_DIM),
      new_k.reshape(nb * ts * n_heads, DSA_HEAD_DIM), new_v.reshape(nb * ts * n_heads, DSA_HEAD_DIM),
      bias_tab, expand)


def _out_kernel(oa_ref, ob_ref, ga_ref, gb_ref, x_ref, gate_ref, gna_ref, gnb_ref, w_ref, fg_ref, y_ref):
    def gated(o_ref, g_ref, gain_ref):
        o = o_ref[...].astype(jnp.float32)
        on = o * lax.rsqrt(jnp.mean(o * o, axis=-1, keepdims=True) + EPS) * gain_ref[...]
        return _mx(on * g_ref[...].astype(jnp.float32))

    wa = oa_ref.shape[1]
    out = _dot(gated(oa_ref, ga_ref, gna_ref), w_ref[0:wa, :]) + _dot(gated(ob_ref, gb_ref, gnb_ref), w_ref[wa:, :])
    xn = x_ref[...] + gate_ref[...] * out
    y_ref[...] = xn * lax.rsqrt(jnp.mean(xn * xn, axis=-1, keepdims=True) + EPS) * fg_ref[...]


def _out_proj(oa, ob, ga, gb, x3, gate, gain_a, gain_b, w_out, final_gain, tm):
    bv, tv, d = x3.shape
    nt = tv // tm
    wa = oa.shape[1]
    x_spec, g_spec = _row_specs(x3, gate.shape[1], tm)
    row = lambda b, i: (b * nt + i, 0)
    return pl.pallas_call(
        _out_kernel,
        grid=(bv, nt),
        in_specs=[pl.BlockSpec((tm, wa), row), pl.BlockSpec((tm, wa), row),
                  pl.BlockSpec((tm, wa), row), pl.BlockSpec((tm, wa), row),
                  x_spec, g_spec, _resident((1, wa)), _resident((1, wa)),
                  _resident(w_out.shape), _resident((1, d))],
        out_specs=pl.BlockSpec((None, tm, d), lambda b, i: (b, i, 0)),
        out_shape=jax.ShapeDtypeStruct((bv, tv, d), jnp.float32),
        compiler_params=_cparams(2),
        name="out_proj",
    )(oa, ob, ga, gb, x3, gate, gain_a.reshape(1, wa), gain_b.reshape(1, wa), w_out, final_gain.reshape(1, d))


def _pack_kernel(n_heads, wt_ref, qa_ref, ckv_ref, kr_ref, ga_ref, gb_ref, qb_ref, kb_ref, vb_ref, qi_ref, kiw_ref):
    wa = n_heads * MLA_V
    half = MLA_ROPE // 2
    q_head = MLA_NOPE + MLA_ROPE
    step = 2 * LANES

    def panel(ref, off):
        for c in range(0, ref.shape[1], step):
            w = min(step, ref.shape[1] - c)
            ref[:, c:c + w] = _mx(wt_ref[off + c:off + c + w, :].T)

    def dup_rope(off):
        x1, x2 = wt_ref[off:off + half, :], wt_ref[off + half:off + 2 * half, :]
        return _mx(jnp.concatenate([x1, x2, x2, x1], axis=0).T)

    for h in range(n_heads):
        qa_ref[:, h * LANES:(h + 1) * LANES] = _mx(wt_ref[h * q_head:h * q_head + MLA_NOPE, :].T)
    for h in range(0, n_heads, 2):
        r0 = wt_ref[h * q_head + MLA_NOPE:(h + 1) * q_head, :]
        r1 = wt_ref[(h + 1) * q_head + MLA_NOPE:(h + 2) * q_head, :]
        dst = n_heads * MLA_NOPE + (h // 2) * LANES
        qa_ref[:, dst:dst + LANES] = _mx(jnp.concatenate([r0, r1], axis=0).T)
    off = n_heads * q_head
    panel(ckv_ref, off)
    off += KV_LORA
    kr_ref[...] = dup_rope(off)
    off += MLA_ROPE
    for ref in (ga_ref, qb_ref, kb_ref, vb_ref):
        panel(ref, off)
        off += wa
    pairs = IDX_HEADS // 2
    for j in range(pairs):
        lo = wt_ref[off + IDX_DIM * j:off + IDX_DIM * (j + 1), :]
        hi = wt_ref[off + IDX_DIM * (j + pairs):off + IDX_DIM * (j + pairs + 1), :]
        qi_ref[:, LANES * j:LANES * (j + 1)] = _mx(jnp.concatenate([lo, hi], axis=0).T)
    off += IDX_HEADS * IDX_DIM
    n_kw = IDX_DIM + IDX_HEADS
    kiw = jnp.concatenate([wt_ref[off:off + n_kw, :], jnp.zeros((LANES - n_kw, wt_ref.shape[1]), jnp.float32)], axis=0)
    kiw_ref[...] = _mx(kiw.T)
    off += n_kw
    panel(gb_ref, off)


def _pack_w_in(w_in, n_heads):
    wt = w_in.T
    n, d = wt.shape
    wa = n_heads * MLA_V
    assert n == n_heads * (MLA_NOPE + MLA_ROPE) + KV_LORA + MLA_ROPE + 5 * wa + IDX_HEADS * IDX_DIM + IDX_DIM + IDX_HEADS
    assert 2 * IDX_DIM == LANES and 2 * MLA_ROPE == LANES and MLA_NOPE == LANES
    slab = 256
    assert n_heads % 4 == 0
    widths = [n_heads * (MLA_NOPE + MLA_ROPE), KV_LORA, LANES, wa, wa, wa, wa, wa, IDX_HEADS * IDX_DIM, LANES]
    q_a, ckv, kr, g_a, g_b, q_b, k_b, v_b, q_i, kiw = pl.pallas_call(
        functools.partial(_pack_kernel, n_heads),
        grid=(d // slab,),
        in_specs=[pl.BlockSpec((n, slab), lambda i: (0, i))],
        out_specs=[pl.BlockSpec((slab, w), lambda i: (i, 0)) for w in widths],
        out_shape=[jax.ShapeDtypeStruct((d, w), MXU_DT) for w in widths],
        compiler_params=_cparams(1),
        name="pack_w_in",
    )(wt)
    return [q_a, ckv, kr, g_a, g_b], [q_b, k_b, v_b, q_i, kiw]


def _rope_table(pos):
    half = MLA_ROPE // 2
    freqs = jnp.power(ROPE_THETA, -jnp.arange(half, dtype=jnp.float32) / half)
    ang = pos.astype(jnp.float32)[:, None] * freqs
    cos, sin = jnp.cos(ang), jnp.sin(ang)
    return jnp.concatenate([cos, cos, -sin, sin], axis=1)


def _rel_bucket(rel):
    nb = N_BUCKETS // 2
    max_exact = nb // 2
    n = jnp.abs(rel)
    nf = jnp.maximum(n, 1).astype(jnp.float32)
    large = max_exact + (jnp.log(nf / max_exact) / math.log(MAX_DISTANCE / max_exact)
                         * (nb - max_exact)).astype(jnp.int32)
    large = jnp.minimum(large, nb - 1)
    return jnp.where(rel > 0, nb, 0) + jnp.where(n < max_exact, n, large)


def _bucket_bias(rel_bias, rel):
    return (rel_bias * LOG2E)[_rel_bucket(rel)]


def kernel(x_prompt, x_sample, cache_mla_ckv, cache_mla_krope, cache_dsa_k, cache_dsa_v, cache_idx_k,
           c_prompt, c_sample, w_ada, b_ada, ln_gain, w_in, mla_kv_gain, w_uk, w_uv, rel_bias,
           out_gain_a, out_gain_b, w_out, final_gain):
    assert w_ada.shape[0] == 1, "single-layer step"
    b, t, d = x_prompt.shape
    nb, ts, _ = x_sample.shape
    past = cache_mla_ckv.shape[2]
    n_heads = d // 256
    wb = n_heads * DSA_HEAD_DIM
    s_all = past + ts
    assert t % MLA_TQ == 0 and t % DSA_TQ == 0 and MLA_TQ % CHUNK == 0 and DSA_TQ % CHUNK == 0
    assert (s_all - 1) // CHUNK <= past // CHUNK

    mod = _ada_mod(jnp.concatenate([c_prompt, c_sample], axis=0), w_ada[0], b_ada[0])
    shift, scale, gate = mod[:, :d], mod[:, d:2 * d], mod[:, 2 * d:]
    mod_p = [v[:b].reshape(b, 1, d) for v in (scale, shift, gate)]
    mod_s = [jnp.broadcast_to(v[b:, None, :], (nb, ts, d)).reshape(1, nb * ts, d) for v in (scale, shift, gate)]

    w_a, w_b = _pack_w_in(w_in[0], n_heads)
    wkv = _mx(jnp.concatenate([w_uk[0].transpose(2, 0, 1).reshape(KV_LORA, n_heads * MLA_NOPE),
                               w_uv[0].transpose(2, 0, 1).reshape(KV_LORA, n_heads * MLA_V)], axis=1))
    w_o = _mx(w_out[0])
    pos_p = jnp.arange(t, dtype=jnp.int32)
    pos_s = past + jnp.arange(ts, dtype=jnp.int32)
    tab_p = _rope_table(pos_p)
    tab_s = jnp.tile(_rope_table(pos_s), (nb, 1))

    tm = 256
    qcat, ckv_p, krope_p, ga, gb, kcat, vup, oa_buf = _proj_a(
        x_prompt, mod_p[0], mod_p[1], ln_gain[0], tab_p, w_a, mla_kv_gain[0], wkv, tm)
    qb, k_p, v_p, kbb, vbb, qi, idxk_p, kiab, wi, ob_buf = _proj_b(
        x_prompt, mod_p[0], mod_p[1], ln_gain[0], w_b, tm, True)
    o_a = _mla_prompt(qcat, kcat, vup, oa_buf, b, t)
    near = _bucket_bias(rel_bias, jnp.arange(-MAX_DISTANCE - 1, MAX_DISTANCE, dtype=jnp.int32))
    far = near[:1]
    bias_tab = jnp.concatenate([jnp.broadcast_to(far, (t - 1 - MAX_DISTANCE, n_heads)), near[1:],
                                jnp.broadcast_to(far, (t + 1 - MAX_DISTANCE, n_heads))], axis=0).T
    o_b = _dsa_prompt(qb, qi, wi, kbb, vbb, kiab, bias_tab, ob_buf, b, t, min(TOPK_MAX, t // 4))
    y_prompt = _out_proj(o_a, o_b, ga, gb, x_prompt, mod_p[2], out_gain_a[0], out_gain_b[0], w_o, final_gain, 2 * tm)

    xs3 = x_sample.reshape(1, nb * ts, d)
    tms = min(256, nb * ts)
    qcat_s, ckv_s, krope_s, ga_s, gb_s = _proj_a(
        xs3, mod_s[0], mod_s[1], ln_gain[0], tab_s, w_a, mla_kv_gain[0], None, tms)
    qb_s, k_s, v_s, kbb_s, vbb_s, qi_s, idxk_s, kiab_s, wi_s = _proj_b(
        xs3, mod_s[0], mod_s[1], ln_gain[0], w_b, tms, False)
    oa_s = _mla_sample(qcat_s, cache_mla_ckv[0], cache_mla_krope[0], ckv_s, krope_s,
                       _mx(w_uk[0]), _mx(w_uv[0]), ts)
    s_pad = pl.cdiv(s_all, LANES) * LANES
    tab_s = _bucket_bias(rel_bias, jnp.arange(s_pad + LANES, dtype=jnp.int32) - (s_all - 1)).T
    madd_s = _dsa_select(qi_s, wi_s, cache_idx_k[0], kiab_s, ts, min(TOPK_MAX, s_all // 4))
    ob_s = _dsa_sample(qb_s, madd_s, cache_dsa_k[0], cache_dsa_v[0], kbb_s, vbb_s, tab_s, ts)
    y_sample = _out_proj(oa_s, ob_s, ga_s, gb_s, xs3, mod_s[2], out_gain_a[0], out_gain_b[0], w_o, final_gain, tms)

    hd = (n_heads, DSA_HEAD_DIM)
    return (y_prompt, y_sample.reshape(nb, ts, d),
            ckv_p.reshape(1, b, t, KV_LORA), krope_p.reshape(1, b, t, MLA_ROPE),
            k_p.reshape(1, b, t, *hd), v_p.reshape(1, b, t, *hd), idxk_p.reshape(1, b, t, IDX_DIM),
            ckv_s.reshape(1, nb, ts, KV_LORA), krope_s.reshape(1, nb, ts, MLA_ROPE),
            k_s.reshape(1, nb, ts, *hd), v_s.reshape(1, nb, ts, *hd), idxk_s.reshape(1, nb, ts, IDX_DIM))
```

```python
import functools
import math

import jax
import jax.numpy as jnp
from jax import lax
from jax.experimental import pallas as pl
from jax.experimental.pallas import tpu as pltpu

MXU_DT = jnp.bfloat16

CHUNK = 64
MLA_NOPE = 128
MLA_ROPE = 64
MLA_V = 128
KV_LORA = 512
DSA_HEAD_DIM = 128
IDX_HEADS = 16
IDX_DIM = 64
TOPK_MAX = 256
N_BUCKETS = 32
MAX_DISTANCE = 128
ROPE_THETA = 10000.0
EPS = 1e-6

LANES = 128
HEAD_SLOT = 256
NEG_BIG = -1e30
LOG2E = math.log2(math.e)
INT_MIN = -2 ** 31
KEY_NEG_INF = INT_MIN + 0x7FFFFF
VMEM_LIMIT = 56 * 1024 * 1024


def _cparams(n_grid, vmem=VMEM_LIMIT):
    return pltpu.CompilerParams(dimension_semantics=("arbitrary",) * n_grid, vmem_limit_bytes=vmem)


def _mx(v):
    return v.astype(MXU_DT)


def _dot(a, b):
    return jnp.dot(a, b, preferred_element_type=jnp.float32)


def _dot_nt(a, b):
    return lax.dot_general(a, b, (((1,), (1,)), ((), ())), preferred_element_type=jnp.float32)


def _silu(v):
    return v * (1.0 / (1.0 + jnp.exp(-v)))


def _resident(shape):
    nd = len(shape)
    return pl.BlockSpec(shape, lambda *_: (0,) * nd, pipeline_mode=pl.Buffered(1))


def _ada_kernel(c_ref, w_ref, b_ref, o_ref):
    a = _mx(_silu(c_ref[...]))
    o_ref[...] = _dot(a, _mx(w_ref[...])) + b_ref[...]


def _ada_mod(c_all, w_ada, b_ada):
    m, d = c_all.shape
    n = w_ada.shape[1]
    tn = 1024
    return pl.pallas_call(
        _ada_kernel,
        grid=(n // tn,),
        in_specs=[pl.BlockSpec((m, d), lambda j: (0, 0)),
                  pl.BlockSpec((d, tn), lambda j: (0, j)),
                  pl.BlockSpec((1, tn), lambda j: (0, j))],
        out_specs=pl.BlockSpec((m, tn), lambda j: (0, j)),
        out_shape=jax.ShapeDtypeStruct((m, n), jnp.float32),
        compiler_params=_cparams(1),
        name="ada_mod",
    )(c_all, w_ada, b_ada.reshape(1, n))


def _modulated_norm(x_ref, scale_ref, shift_ref, lng_ref):
    x = x_ref[...]
    xn = x * lax.rsqrt(jnp.mean(x * x, axis=-1, keepdims=True) + EPS) * lng_ref[...]
    return _mx(xn * (1.0 + scale_ref[...]) + shift_ref[...])


def _rope128(a, tab):
    t = a * tab
    return t + pltpu.roll(t, 64, 1)


def _proj_a_kernel(n_heads, emit_kv, x_ref, scale_ref, shift_ref, lng_ref, tab_ref,
                   wq_ref, wc_ref, wr_ref, wga_ref, wgb_ref, kvg_ref, *rest):
    if emit_kv:
        wkv_ref, qcat_ref, ckv_ref, krope_ref, ga_ref, gb_ref, kcat_ref, vup_ref, attn_ref = rest
        attn_ref[...] = jnp.zeros_like(attn_ref)
    else:
        qcat_ref, ckv_ref, krope_ref, ga_ref, gb_ref = rest
    hb = _modulated_norm(x_ref, scale_ref, shift_ref, lng_ref)
    tab = tab_ref[...]
    qscale = (MLA_NOPE + MLA_ROPE) ** -0.5 * LOG2E
    for h in range(0, n_heads, 2):
        a = _dot(hb, wq_ref[:, h * LANES:(h + 2) * LANES])
        qcat_ref[:, h * HEAD_SLOT:h * HEAD_SLOT + LANES] = _mx(a[:, :LANES] * qscale)
        qcat_ref[:, (h + 1) * HEAD_SLOT:(h + 1) * HEAD_SLOT + LANES] = _mx(a[:, LANES:] * qscale)
    lane = lax.broadcasted_iota(jnp.int32, tab.shape, 1)
    tab_r = pltpu.roll(tab, MLA_ROPE, 1)
    cos4 = jnp.where(lane < MLA_ROPE, tab, tab_r)
    sin4 = jnp.where(lane < MLA_ROPE, tab_r, tab)
    is_x1 = lane % MLA_ROPE < MLA_ROPE // 2
    n_nope = n_heads * MLA_NOPE
    for h in range(0, n_heads, 4):
        a = _dot(hb, wq_ref[:, n_nope + (h // 2) * LANES:n_nope + (h // 2 + 2) * LANES])
        for u in range(2):
            ap = a[:, u * LANES:(u + 1) * LANES]
            swapped = jnp.where(is_x1, pltpu.roll(ap, LANES - MLA_ROPE // 2, 1), pltpu.roll(ap, MLA_ROPE // 2, 1))
            roped = (ap * cos4 + swapped * sin4) * qscale
            c0 = (h + 2 * u) * HEAD_SLOT
            qcat_ref[:, c0 + LANES:c0 + HEAD_SLOT] = _mx(roped)
            qcat_ref[:, c0 + HEAD_SLOT + LANES:c0 + 2 * HEAD_SLOT] = _mx(pltpu.roll(roped, MLA_ROPE, 1))
    c = _dot(hb, wc_ref[...])
    cn = c * lax.rsqrt(jnp.mean(c * c, axis=-1, keepdims=True) + EPS) * kvg_ref[...]
    ckv_ref[...] = cn
    r = _rope128(_dot(hb, wr_ref[...]), tab)
    krope_ref[...] = r[:, :MLA_ROPE]
    ga_ref[...] = _mx(_silu(_dot(hb, wga_ref[...])))
    gb_ref[...] = _mx(_silu(_dot(hb, wgb_ref[...])))
    if emit_kv:
        cb = _mx(cn)
        lane = lax.broadcasted_iota(jnp.int32, r.shape, 1)
        krz = _mx(jnp.where(lane < MLA_ROPE, r, 0.0))
        kn = _dot(cb, wkv_ref[:, :n_heads * MLA_NOPE])
        for h in range(n_heads):
            c0 = h * HEAD_SLOT
            kcat_ref[:, c0:c0 + LANES] = _mx(kn[:, h * MLA_NOPE:(h + 1) * MLA_NOPE])
            kcat_ref[:, c0 + LANES:c0 + HEAD_SLOT] = krz
        vup_ref[...] = _mx(_dot(cb, wkv_ref[:, n_heads * MLA_NOPE:]))


def _proj_b_kernel(n_heads, emit_attn, x_ref, scale_ref, shift_ref, lng_ref, wq_ref, wk_ref, wv_ref, wqi_ref,
                   wkw_ref, qb_ref, kb_hbm, vb_hbm, kbb_ref, vbb_ref, qi_ref, ki_ref, kiab_ref, wi_ref, *rest):
    if emit_attn:
        rest[0][...] = jnp.zeros_like(rest[0])
    rows_ref, sem = rest[-2:]
    tm = x_ref.shape[0]
    n_steps = pl.num_programs(0) * pl.num_programs(1)
    step_id = pl.program_id(0) * pl.num_programs(1) + pl.program_id(1)
    slot = step_id % 2

    def row_copies(slot_, step_, which):
        dst = (kb_hbm, vb_hbm)[which]
        return [pltpu.make_async_copy(rows_ref.at[slot_, which, :, h * DSA_HEAD_DIM:(h + 1) * DSA_HEAD_DIM],
                                      dst.at[pl.ds(step_ * tm, tm), h, :], sem.at[slot_, which])
                for h in range(n_heads)]

    def wait_slot(slot_, step_):
        for which in range(2):
            for cp in row_copies(slot_, step_, which):
                cp.wait()

    @pl.when(step_id >= 2)
    def _():
        wait_slot(slot, step_id - 2)

    hb = _modulated_norm(x_ref, scale_ref, shift_ref, lng_ref)
    dscale = DSA_HEAD_DIM ** -0.5 * LOG2E
    step = 512
    for c0 in range(0, wq_ref.shape[1], step):
        qb_ref[:, c0:c0 + step] = _mx(_dot(hb, wq_ref[:, c0:c0 + step]) * dscale)
    for which, (w_ref, b_ref) in enumerate(((wk_ref, kbb_ref), (wv_ref, vbb_ref))):
        for c0 in range(0, w_ref.shape[1], step):
            kv = _dot(hb, w_ref[:, c0:c0 + step])
            rows_ref[slot, which, :, c0:c0 + step] = kv
            b_ref[:, c0:c0 + step] = _mx(kv)
        for cp in row_copies(slot, step_id, which):
            cp.start()

    @pl.when(step_id == n_steps - 1)
    def _():
        wait_slot(slot, step_id)

        @pl.when(n_steps >= 2)
        def _():
            wait_slot(1 - slot, step_id - 1)
    for c0 in range(0, wqi_ref.shape[1], step):
        qi_ref[:, c0:c0 + step] = _mx(_dot(hb, wqi_ref[:, c0:c0 + step]))
    a = _dot(hb, wkw_ref[...])
    ki_ref[...] = a[:, :IDX_DIM]
    lane = lax.broadcasted_iota(jnp.int32, a.shape, 1)
    kz = jnp.where(lane < IDX_DIM, a, 0.0)
    kiab_ref[:, :LANES] = _mx(kz)
    kiab_ref[:, LANES:] = _mx(pltpu.roll(kz, IDX_DIM, 1))
    wi_ref[...] = a * (IDX_HEADS ** -0.5 * IDX_DIM ** -0.5)


def _row_specs(x3, mod_rows, tm):
    bv, tv, d = x3.shape
    x_spec = pl.BlockSpec((None, tm, d), lambda b, i: (b, i, 0))
    if mod_rows == 1:
        m_spec = pl.BlockSpec((None, 1, d), lambda b, i: (b, 0, 0))
    else:
        m_spec = pl.BlockSpec((None, tm, d), lambda b, i: (b, i, 0))
    return x_spec, m_spec


def _out2d(m, width, dtype, tm, nt):
    return (jax.ShapeDtypeStruct((m, width), dtype),
            pl.BlockSpec((tm, width), lambda b, i: (b * nt + i, 0)))


def _proj_a(x3, scale, shift, ln_gain, tab, ws_a, kv_gain, wkv, tm):
    bv, tv, d = x3.shape
    nt = tv // tm
    m = bv * tv
    n_heads = d // 256
    wa = n_heads * MLA_V
    emit_kv = wkv is not None
    x_spec, m_spec = _row_specs(x3, scale.shape[1], tm)
    in_specs = ([x_spec, m_spec, m_spec, _resident((1, d)), pl.BlockSpec((tm, LANES), lambda b, i: (i, 0))]
                + [_resident(w.shape) for w in ws_a] + [_resident((1, KV_LORA))])
    args = [x3, scale, shift, ln_gain.reshape(1, d), tab, *ws_a, kv_gain.reshape(1, KV_LORA)]
    outs = [_out2d(m, n_heads * HEAD_SLOT, MXU_DT, tm, nt),
            _out2d(m, KV_LORA, jnp.float32, tm, nt),
            _out2d(m, MLA_ROPE, jnp.float32, tm, nt),
            _out2d(m, wa, MXU_DT, tm, nt),
            _out2d(m, wa, MXU_DT, tm, nt)]
    if emit_kv:
        in_specs.append(_resident(wkv.shape))
        args.append(wkv)
        outs += [_out2d(m, n_heads * HEAD_SLOT, MXU_DT, tm, nt),
                 _out2d(m, wa, MXU_DT, tm, nt),
                 _out2d(m, wa, MXU_DT, tm, nt)]
    return pl.pallas_call(
        functools.partial(_proj_a_kernel, n_heads, emit_kv),
        grid=(bv, nt),
        in_specs=in_specs,
        out_specs=[o[1] for o in outs],
        out_shape=[o[0] for o in outs],
        compiler_params=_cparams(2),
        name="proj_a",
    )(*args)


def _proj_b(x3, scale, shift, ln_gain, ws_b, tm, emit_attn_buffer):
    bv, tv, d = x3.shape
    nt = tv // tm
    m = bv * tv
    n_heads = d // 256
    width_b = n_heads * DSA_HEAD_DIM
    x_spec, m_spec = _row_specs(x3, scale.shape[1], tm)
    cache_rows = (jax.ShapeDtypeStruct((m, n_heads, DSA_HEAD_DIM), jnp.float32), pl.BlockSpec(memory_space=pl.ANY))
    outs = [_out2d(m, width_b, MXU_DT, tm, nt),
            cache_rows,
            cache_rows,
            _out2d(m, width_b, MXU_DT, tm, nt),
            _out2d(m, width_b, MXU_DT, tm, nt),
            _out2d(m, IDX_HEADS * IDX_DIM, MXU_DT, tm, nt),
            _out2d(m, IDX_DIM, jnp.float32, tm, nt),
            _out2d(m, 2 * LANES, MXU_DT, tm, nt),
            _out2d(m, LANES, jnp.float32, tm, nt)]
    if emit_attn_buffer:
        outs.append(_out2d(m, width_b, MXU_DT, tm, nt))
    return pl.pallas_call(
        functools.partial(_proj_b_kernel, n_heads, emit_attn_buffer),
        grid=(bv, nt),
        in_specs=[x_spec, m_spec, m_spec, _resident((1, d))] + [_resident(w.shape) for w in ws_b],
        out_specs=[o[1] for o in outs],
        out_shape=[o[0] for o in outs],
        scratch_shapes=[pltpu.VMEM((2, 2, tm, width_b), jnp.float32), pltpu.SemaphoreType.DMA((2, 2))],
        compiler_params=_cparams(2),
        name="proj_b",
    )(x3, scale, shift, ln_gain.reshape(1, d), *ws_b)


def _softmax_pv(s, v):
    m = jnp.max(s, axis=-1, keepdims=True)
    p = jnp.exp2(s - m)
    l = jnp.sum(p, axis=-1, keepdims=True)
    return _dot(_mx(p), v) * (1.0 / l)


def _topk_mask(score, adm, topk, sc_ref, madd_ref):
    rows, n = score.shape
    if n <= topk:
        madd_ref[...] = jnp.where(adm, 0.0, NEG_BIG)
        return
    sc_ref[...] = jnp.where(adm, score, -jnp.inf)
    kf = float(topk)

    def count(pred):
        return jnp.sum(jnp.where(pred, 1.0, 0.0), axis=-1, keepdims=True)

    def key_to_float(key):
        return pltpu.bitcast(key ^ ((key >> 31) & 0x7FFFFFFF), jnp.float32)

    def bit_step(i, thr):
        inc = lax.shift_left(jnp.int32(1), 31 - i)
        cand = thr + inc
        ok = count(sc_ref[...] >= key_to_float(cand)) >= kf
        return jnp.where(ok, cand, thr)

    thr = lax.fori_loop(0, 32, bit_step, jnp.full((rows, 1), INT_MIN, jnp.int32))
    thr = key_to_float(jnp.maximum(thr, KEY_NEG_INF))
    ge = sc_ref[...] >= thr
    madd_ref[...] = jnp.where(ge, 0.0, NEG_BIG)

    @pl.when(jnp.max(count(ge)) > kf)
    def _():
        sc = sc_ref[...]
        gt = sc > thr
        need = kf - count(gt)
        eqf = jnp.where(sc == thr, 1.0, 0.0)
        col = lax.broadcasted_iota(jnp.int32, (rows, n), 1)
        nbits = max(1, int(n).bit_length())

        def col_step(i, bound):
            cand = bound + lax.shift_left(jnp.int32(1), nbits - 1 - i)
            taken = jnp.sum(jnp.where(col < cand, eqf, 0.0), axis=-1, keepdims=True)
            return jnp.where(taken <= need, cand, bound)

        bound = lax.fori_loop(0, nbits, col_step, jnp.zeros((rows, 1), jnp.int32))
        tie_madd = jnp.where(jnp.where(col < bound, eqf, 0.0) > 0.5, 0.0, NEG_BIG)
        madd_ref[...] = jnp.where(adm, jnp.where(gt, 0.0, tie_madd), NEG_BIG)


def _indexer_scores(qi_ref, wi, kia, kib):
    half = IDX_HEADS // 2
    score = None
    for j in range(half):
        qp = qi_ref[:, j * LANES:(j + 1) * LANES]
        da = jnp.maximum(_dot_nt(qp, kia), 0.0) * wi[:, IDX_DIM + j:IDX_DIM + j + 1]
        db = jnp.maximum(_dot_nt(qp, kib), 0.0) * wi[:, IDX_DIM + half + j:IDX_DIM + half + j + 1]
        score = da + db if score is None else score + da + db
    return score


def _indexer_scores_stacked(qi_ref, wi, kia, kib):
    half = IDX_HEADS // 2
    ts = qi_ref.shape[0]
    q_all = jnp.concatenate([qi_ref[:, j * LANES:(j + 1) * LANES] for j in range(half)], axis=0)
    wa = jnp.concatenate([wi[:, IDX_DIM + j:IDX_DIM + j + 1] for j in range(half)], axis=0)
    wb = jnp.concatenate([wi[:, IDX_DIM + half + j:IDX_DIM + half + j + 1] for j in range(half)], axis=0)
    part = jnp.maximum(_dot_nt(q_all, kia), 0.0) * wa + jnp.maximum(_dot_nt(q_all, kib), 0.0) * wb
    score = part[0:ts]
    for j in range(1, half):
        score = score + part[j * ts:(j + 1) * ts]
    return score


MLA_TQ = 512
DSA_TQ = 256
DSA_STEP_MAX_KEYS = 2048


def _chunk_madd(row0, tq, n_keys):
    qc = (row0 + lax.broadcasted_iota(jnp.int32, (tq, n_keys), 0)) // CHUNK
    kc = lax.broadcasted_iota(jnp.int32, (tq, n_keys), 1) // CHUNK
    return kc <= qc


def _chained_tile_calls(n_tiles, make_call, out):
    for c in range(n_tiles):
        out = make_call(c, out)
    return out


def _mla_prompt_kernel(n_heads, row0, q_ref, k_ref, v_ref, prev_ref, o_ref):
    del prev_ref
    tq, n_keys = q_ref.shape[0], k_ref.shape[0]
    madd = jnp.where(_chunk_madd(row0, tq, n_keys), 0.0, NEG_BIG)
    for h in range(n_heads):
        c0 = h * HEAD_SLOT
        s = _dot_nt(q_ref[:, c0:c0 + HEAD_SLOT], k_ref[:, c0:c0 + HEAD_SLOT]) + madd
        o_ref[:, h * MLA_V:(h + 1) * MLA_V] = _mx(_softmax_pv(s, v_ref[:, h * MLA_V:(h + 1) * MLA_V]))


def _mla_prompt(qcat, kcat, vup, out_init, b, t):
    tq = MLA_TQ
    n_heads = qcat.shape[1] // HEAD_SLOT
    nq = t // tq
    wa = n_heads * MLA_V
    k3 = kcat.reshape(b, t, kcat.shape[1])
    v3 = vup.reshape(b, t, wa)

    def make_call(c, out):
        n_keys = (c + 1) * tq
        row = lambda bi: (bi * nq + c, 0)
        return pl.pallas_call(
            functools.partial(_mla_prompt_kernel, n_heads, c * tq),
            grid=(b,),
            in_specs=[pl.BlockSpec((tq, qcat.shape[1]), row),
                      pl.BlockSpec((None, n_keys, kcat.shape[1]), lambda bi: (bi, 0, 0)),
                      pl.BlockSpec((None, n_keys, wa), lambda bi: (bi, 0, 0)),
                      pl.BlockSpec(memory_space=pl.ANY)],
            out_specs=pl.BlockSpec((tq, wa), row),
            out_shape=jax.ShapeDtypeStruct(out.shape, out.dtype),
            input_output_aliases={3: 0},
            compiler_params=_cparams(1),
            name="mla_prompt_%d" % c,
        )(qcat, k3, v3, out)

    return _chained_tile_calls(nq, make_call, out_init)


def _toeplitz_bias(tab_ref, h, start, n_rows, n_cols):
    shift0 = LANES - (n_rows - 1)
    from_left = (lax.broadcasted_iota(jnp.int32, (n_rows, LANES), 1)
                 < shift0 + lax.broadcasted_iota(jnp.int32, (n_rows, LANES), 0))
    pieces = []
    prev = None
    for n in range(n_cols // LANES + 1):
        seg = tab_ref[h:h + 1, start + n * LANES:start + (n + 1) * LANES]
        rot = pltpu.roll(jnp.broadcast_to(seg, (n_rows, LANES)), shift0 % LANES, 1, stride=1, stride_axis=0)
        if prev is not None:
            pieces.append(jnp.where(from_left, prev, rot))
        prev = rot
    return jnp.concatenate(pieces, axis=1)


def _dsa_prompt_kernel(n_heads, row0, t, topk, qb_ref, qi_ref, wi_ref, k_ref, v_ref, kiab_ref, tab_ref, prev_ref,
                       o_ref, key_ref, madd_ref):
    del prev_ref
    per, tq, _ = qb_ref.shape
    n_keys = k_ref.shape[1]
    score = jnp.concatenate(
        [_indexer_scores(qi_ref.at[u], wi_ref[u], kiab_ref[u, :, :LANES], kiab_ref[u, :, LANES:])
         for u in range(per)], axis=0)
    adm = _chunk_madd(row0, tq, n_keys)
    _topk_mask(score, jnp.concatenate([adm] * per, axis=0), topk, key_ref, madd_ref)
    near0 = max(0, row0 - LANES)
    n_near = n_keys - near0
    for h in range(n_heads):
        c0 = h * DSA_HEAD_DIM
        near = jnp.concatenate(
            [_toeplitz_bias(tab_ref, h, near0 - (row0 + u * LANES) - LANES + t, LANES, n_near)
             for u in range(tq // LANES)], axis=0)
        if near0 > 0:
            far = jnp.broadcast_to(tab_ref[h:h + 1, 0:1], (tq, near0))
            bias = jnp.concatenate([far, near], axis=1)
        else:
            bias = near
        for u in range(per):
            s = (_dot_nt(qb_ref[u, :, c0:c0 + DSA_HEAD_DIM], k_ref[u, :, c0:c0 + DSA_HEAD_DIM]) + bias
                 + madd_ref[u * tq:(u + 1) * tq, :])
            o_ref[u, :, c0:c0 + DSA_HEAD_DIM] = _mx(_softmax_pv(s, v_ref[u, :, c0:c0 + DSA_HEAD_DIM]))


def _dsa_prompt(qb, qi, wi, kbb, vbb, kiab, bias_tab, out_init, b, t, topk):
    tq = DSA_TQ
    assert MAX_DISTANCE <= LANES and tq % LANES == 0
    n_heads = qb.shape[1] // DSA_HEAD_DIM
    nq = t // tq
    wb = qb.shape[1]
    k3, v3, ki3 = kbb.reshape(b, t, wb), vbb.reshape(b, t, wb), kiab.reshape(b, t, 2 * LANES)
    qb3, qi3, wi3 = qb.reshape(b, t, wb), qi.reshape(b, t, qi.shape[1]), wi.reshape(b, t, LANES)
    keys_of = lambda bi: (bi, 0, 0)

    def make_call(c, out):
        n_keys = (c + 1) * tq
        per = max(p for p in (1, 2, 4) if p == 1 or (b % p == 0 and p * n_keys <= DSA_STEP_MAX_KEYS))
        rows_of = lambda bi: (bi, c, 0)
        return pl.pallas_call(
            functools.partial(_dsa_prompt_kernel, n_heads, c * tq, t, topk),
            grid=(b // per,),
            in_specs=[pl.BlockSpec((per, tq, wb), rows_of),
                      pl.BlockSpec((per, tq, qi.shape[1]), rows_of),
                      pl.BlockSpec((per, tq, LANES), rows_of),
                      pl.BlockSpec((per, n_keys, wb), keys_of),
                      pl.BlockSpec((per, n_keys, wb), keys_of),
                      pl.BlockSpec((per, n_keys, 2 * LANES), keys_of),
                      _resident(bias_tab.shape),
                      pl.BlockSpec(memory_space=pl.ANY)],
            out_specs=pl.BlockSpec((per, tq, wb), rows_of),
            out_shape=jax.ShapeDtypeStruct(out.shape, out.dtype),
            input_output_aliases={7: 0},
            scratch_shapes=[pltpu.VMEM((per * tq, n_keys), jnp.float32),
                            pltpu.VMEM((per * tq, n_keys), jnp.float32)],
            compiler_params=_cparams(1),
            name="dsa_prompt_%d" % c,
        )(qb3, qi3, wi3, k3, v3, ki3, bias_tab, out)

    return _chained_tile_calls(nq, make_call, out_init.reshape(b, t, wb)).reshape(b * t, wb)


def _mla_sample_kernel(n_heads, past, ts, qcat_ref, cckv_ref, ckr_ref, nckv_ref, nkr_ref, wuk_ref, wuv_ref,
                       o_ref, kall_ref, rall_ref):
    n_keys = past + ts

    @pl.when(pl.program_id(0) == 0)
    def _():
        kall_ref[...] = jnp.zeros_like(kall_ref)
        rall_ref[...] = jnp.zeros_like(rall_ref)

    for u in range(cckv_ref.shape[0]):
        rows = slice(u * ts, (u + 1) * ts)
        kall_ref[u, 0:past, :] = _mx(cckv_ref[u])
        kall_ref[u, past:n_keys, :] = _mx(nckv_ref[rows, :])
        rall_ref[u, 0:past, 0:MLA_ROPE] = _mx(ckr_ref[u])
        rall_ref[u, past:n_keys, 0:MLA_ROPE] = _mx(nkr_ref[rows, :])
        qlat, qrope = [], []
        for h in range(n_heads):
            c0 = h * HEAD_SLOT
            qlat.append(_mx(_dot(qcat_ref[rows, c0:c0 + LANES], wuk_ref[h])))
            qrope.append(qcat_ref[rows, c0 + LANES:c0 + HEAD_SLOT])
        qlat = jnp.concatenate(qlat, axis=0)
        qrope = jnp.concatenate(qrope, axis=0)
        s = _dot_nt(qlat, kall_ref[u]) + _dot_nt(qrope, rall_ref[u])
        col = lax.broadcasted_iota(jnp.int32, s.shape, 1)
        s = jnp.where(col < n_keys, s, NEG_BIG)
        olat = _mx(_softmax_pv(s, kall_ref[u]))
        for h in range(n_heads):
            o_ref[rows, h * MLA_V:(h + 1) * MLA_V] = _mx(_dot_nt(olat[h * ts:(h + 1) * ts], wuv_ref[h]))


def _mla_sample(qcat, cache_ckv, cache_kr, new_ckv, new_kr, wuk, wuv, ts):
    nb, past, c = cache_ckv.shape
    n_heads = qcat.shape[1] // HEAD_SLOT
    wa = n_heads * MLA_V
    s_pad = pl.cdiv(past + ts, LANES) * LANES
    per = 2 if nb % 2 == 0 else 1
    row = lambda bi: (bi, 0)
    return pl.pallas_call(
        functools.partial(_mla_sample_kernel, n_heads, past, ts),
        grid=(nb // per,),
        in_specs=[pl.BlockSpec((per * ts, qcat.shape[1]), row),
                  pl.BlockSpec((per, past, c), lambda bi: (bi, 0, 0)),
                  pl.BlockSpec((per, past, MLA_ROPE), lambda bi: (bi, 0, 0)),
                  pl.BlockSpec((per * ts, c), row),
                  pl.BlockSpec((per * ts, MLA_ROPE), row),
                  _resident(wuk.shape), _resident(wuv.shape)],
        out_specs=pl.BlockSpec((per * ts, wa), row),
        out_shape=jax.ShapeDtypeStruct((nb * ts, wa), MXU_DT),
        scratch_shapes=[pltpu.VMEM((per, s_pad, c), MXU_DT), pltpu.VMEM((per, s_pad, LANES), MXU_DT)],
        compiler_params=_cparams(1),
        name="mla_sample",
    )(qcat, cache_ckv, cache_kr, new_ckv, new_kr, wuk, wuv)


def _dsa_select_kernel(past, ts, topk, qi_ref, wi_ref, cki_ref, nkiab_ref, madd_ref, kia_ref, kib_ref, key_ref):
    bi = pl.program_id(0)
    n_keys = past + ts
    rows, s_pad = madd_ref.shape

    @pl.when(bi == 0)
    def _():
        kia_ref[...] = jnp.zeros_like(kia_ref)
        kib_ref[...] = jnp.zeros_like(kib_ref)

    cki = _mx(cki_ref[...])
    kia_ref[0:past, 0:IDX_DIM] = cki
    kib_ref[0:past, IDX_DIM:LANES] = cki
    kia_ref[past:n_keys, :] = nkiab_ref[:, :LANES]
    kib_ref[past:n_keys, :] = nkiab_ref[:, LANES:]
    madd_ref[pl.ds(pl.multiple_of(bi * ts, ts), ts), :] = _indexer_scores_stacked(
        qi_ref, wi_ref[...], kia_ref[...], kib_ref[...])

    @pl.when(bi == pl.num_programs(0) - 1)
    def _():
        col = lax.broadcasted_iota(jnp.int32, (rows, s_pad), 1)
        _topk_mask(madd_ref[...], col < n_keys, topk, key_ref, madd_ref)


def _dsa_select(qi, wi, cache_ki, new_kiab, ts, topk):
    nb, past, _ = cache_ki.shape
    s_pad = pl.cdiv(past + ts, LANES) * LANES
    row = lambda bi: (bi, 0)
    return pl.pallas_call(
        functools.partial(_dsa_select_kernel, past, ts, topk),
        grid=(nb,),
        in_specs=[pl.BlockSpec((ts, qi.shape[1]), row),
                  pl.BlockSpec((ts, LANES), row),
                  pl.BlockSpec((None, past, IDX_DIM), lambda bi: (bi, 0, 0)),
                  pl.BlockSpec((ts, 2 * LANES), row)],
        out_specs=pl.BlockSpec((nb * ts, s_pad), lambda bi: (0, 0)),
        out_shape=jax.ShapeDtypeStruct((nb * ts, s_pad), jnp.float32),
        scratch_shapes=[pltpu.VMEM((s_pad, LANES), MXU_DT), pltpu.VMEM((s_pad, LANES), MXU_DT),
                        pltpu.VMEM((nb * ts, s_pad), jnp.float32)],
        compiler_params=_cparams(1),
        name="dsa_select",
    )(qi, wi, cache_ki, new_kiab)


def _dsa_sample_kernel(n_heads, past, ts, qb_ref, madd_ref, ck_ref, cv_ref, nk_ref, nv_ref, tab_ref, expand_ref,
                       o_ref, kflat_ref, vflat_ref, bias_ref, biasw_ref):
    n_keys = past + ts
    wide = n_heads * LANES
    n_blocks = pl.cdiv(n_keys, LANES)
    widths = [min(wide, (n_keys - j * LANES) * n_heads) for j in range(n_blocks)]

    @pl.when(pl.program_id(0) == 0)
    def _():
        for h in range(n_heads):
            bias_ref[h * ts:(h + 1) * ts, :] = _toeplitz_bias(tab_ref, h, 0, ts, n_blocks * LANES)
        shape = (n_heads * ts, wide)
        same_head = (lax.broadcasted_iota(jnp.int32, shape, 0) // ts
                     == lax.broadcasted_iota(jnp.int32, shape, 1) % n_heads)
        for j in range(n_blocks):
            b = bias_ref[:, j * LANES:(j + 1) * LANES]
            hi = _mx(b)
            rest = b - hi.astype(jnp.float32)
            mid = _mx(rest)
            lo = _mx(rest - mid.astype(jnp.float32))
            piece = _dot(hi, expand_ref[...]) + _dot(mid, expand_ref[...]) + _dot(lo, expand_ref[...])
            biasw_ref[:, j * wide:j * wide + widths[j]] = jnp.where(same_head, piece, NEG_BIG)[:, :widths[j]]

    kflat_ref[0:past * n_heads, :] = _mx(ck_ref[...])
    kflat_ref[past * n_heads:, :] = nk_ref[...]
    vflat_ref[0:past * n_heads, :] = _mx(cv_ref[...])
    vflat_ref[past * n_heads:, :] = nv_ref[...]
    sel = _mx(jnp.where(madd_ref[...] == 0.0, 1.0, 0.0))
    pieces = [_dot(sel[:, j * LANES:(j + 1) * LANES], expand_ref[...])[:, :widths[j]] for j in range(n_blocks)]
    sel_wide = jnp.concatenate(pieces, axis=1)
    sel_wide = jnp.concatenate([sel_wide] * n_heads, axis=0)
    q_all = jnp.concatenate([qb_ref[:, h * DSA_HEAD_DIM:(h + 1) * DSA_HEAD_DIM] for h in range(n_heads)], axis=0)
    s = _dot_nt(q_all, kflat_ref[...]) + biasw_ref[...] + jnp.where(sel_wide > 0.5, 0.0, NEG_BIG)
    o = _mx(_softmax_pv(s, vflat_ref[...]))
    for h in range(n_heads):
        o_ref[:, h * DSA_HEAD_DIM:(h + 1) * DSA_HEAD_DIM] = o[h * ts:(h + 1) * ts]


def _dsa_sample(qb, madd, cache_k, cache_v, new_k, new_v, bias_tab, ts):
    nb, past, n_heads, _ = cache_k.shape
    wb = n_heads * DSA_HEAD_DIM
    n_keys = past + ts
    assert (n_keys % LANES * n_heads) % LANES == 0
    expand = _mx(jnp.repeat(jnp.eye(LANES, dtype=jnp.float32), n_heads, axis=1))
    row = lambda bi: (bi, 0)
    per_b = lambda bi: (bi, 0, 0)
    return pl.pallas_call(
        functools.partial(_dsa_sample_kernel, n_heads, past, ts),
        grid=(nb,),
        in_specs=[pl.BlockSpec((ts, wb), row),
                  pl.BlockSpec((ts, madd.shape[1]), row),
                  pl.BlockSpec((None, past * n_heads, DSA_HEAD_DIM), per_b),
                  pl.BlockSpec((None, past * n_heads, DSA_HEAD_DIM), per_b),
                  pl.BlockSpec((ts * n_heads, DSA_HEAD_DIM), row),
                  pl.BlockSpec((ts * n_heads, DSA_HEAD_DIM), row),
                  _resident(bias_tab.shape), _resident(expand.shape)],
        out_specs=pl.BlockSpec((ts, wb), row),
        out_shape=jax.ShapeDtypeStruct((nb * ts, wb), MXU_DT),
        scratch_shapes=[pltpu.VMEM((n_keys * n_heads, DSA_HEAD_DIM), MXU_DT),
                        pltpu.VMEM((n_keys * n_heads, DSA_HEAD_DIM), MXU_DT),
                        pltpu.VMEM((n_heads * ts, pl.cdiv(n_keys, LANES) * LANES), jnp.float32),
                        pltpu.VMEM((n_heads * ts, n_keys * n_heads), jnp.float32)],
        compiler_params=_cparams(1),
        name="dsa_sample",
    )(qb, madd, cache_k.reshape(nb, past * n_heads, DSA_HEAD_DIM), cache_v.reshape(nb, past * n_heads, DSA_HEAD_DIM),
      new_k.reshape(nb * ts * n_heads, DSA_HEAD_DIM), new_v.reshape(nb * ts * n_heads, DSA_HEAD_DIM),
      bias_tab, expand)


def _out_kernel(oa_ref, ob_ref, ga_ref, gb_ref, x_ref, gate_ref, gna_ref, gnb_ref, w_ref, fg_ref, y_ref):
    def gated(o_ref, g_ref, gain_ref):
        o = o_ref[...].astype(jnp.float32)
        on = o * lax.rsqrt(jnp.mean(o * o, axis=-1, keepdims=True) + EPS) * gain_ref[...]
        return _mx(on * g_ref[...].astype(jnp.float32))

    wa = oa_ref.shape[1]
    out = _dot(gated(oa_ref, ga_ref, gna_ref), w_ref[0:wa, :]) + _dot(gated(ob_ref, gb_ref, gnb_ref), w_ref[wa:, :])
    xn = x_ref[...] + gate_ref[...] * out
    y_ref[...] = xn * lax.rsqrt(jnp.mean(xn * xn, axis=-1, keepdims=True) + EPS) * fg_ref[...]


def _out_proj(oa, ob, ga, gb, x3, gate, gain_a, gain_b, w_out, final_gain, tm):
    bv, tv, d = x3.shape
    nt = tv // tm
    wa = oa.shape[1]
    x_spec, g_spec = _row_specs(x3, gate.shape[1], tm)
    row = lambda b, i: (b * nt + i, 0)
    return pl.pallas_call(
        _out_kernel,
        grid=(bv, nt),
        in_specs=[pl.BlockSpec((tm, wa), row), pl.BlockSpec((tm, wa), row),
                  pl.BlockSpec((tm, wa), row), pl.BlockSpec((tm, wa), row),
                  x_spec, g_spec, _resident((1, wa)), _resident((1, wa)),
                  _resident(w_out.shape), _resident((1, d))],
        out_specs=pl.BlockSpec((None, tm, d), lambda b, i: (b, i, 0)),
        out_shape=jax.ShapeDtypeStruct((bv, tv, d), jnp.float32),
        compiler_params=_cparams(2),
        name="out_proj",
    )(oa, ob, ga, gb, x3, gate, gain_a.reshape(1, wa), gain_b.reshape(1, wa), w_out, final_gain.reshape(1, d))


def _pack_kernel(n_heads, wt_ref, qa_ref, ckv_ref, kr_ref, ga_ref, gb_ref, qb_ref, kb_ref, vb_ref, qi_ref, kiw_ref):
    wa = n_heads * MLA_V
    half = MLA_ROPE // 2
    q_head = MLA_NOPE + MLA_ROPE
    step = 2 * LANES

    def panel(ref, off):
        for c in range(0, ref.shape[1], step):
            w = min(step, ref.shape[1] - c)
            ref[:, c:c + w] = _mx(wt_ref[off + c:off + c + w, :].T)

    def dup_rope(off):
        x1, x2 = wt_ref[off:off + half, :], wt_ref[off + half:off + 2 * half, :]
        return _mx(jnp.concatenate([x1, x2, x2, x1], axis=0).T)

    for h in range(n_heads):
        qa_ref[:, h * LANES:(h + 1) * LANES] = _mx(wt_ref[h * q_head:h * q_head + MLA_NOPE, :].T)
    for h in range(0, n_heads, 2):
        r0 = wt_ref[h * q_head + MLA_NOPE:(h + 1) * q_head, :]
        r1 = wt_ref[(h + 1) * q_head + MLA_NOPE:(h + 2) * q_head, :]
        dst = n_heads * MLA_NOPE + (h // 2) * LANES
        qa_ref[:, dst:dst + LANES] = _mx(jnp.concatenate([r0, r1], axis=0).T)
    off = n_heads * q_head
    panel(ckv_ref, off)
    off += KV_LORA
    kr_ref[...] = dup_rope(off)
    off += MLA_ROPE
    for ref in (ga_ref, qb_ref, kb_ref, vb_ref):
        panel(ref, off)
        off += wa
    pairs = IDX_HEADS // 2
    for j in range(pairs):
        lo = wt_ref[off + IDX_DIM * j:off + IDX_DIM * (j + 1), :]
        hi = wt_ref[off + IDX_DIM * (j + pairs):off + IDX_DIM * (j + pairs + 1), :]
        qi_ref[:, LANES * j:LANES * (j + 1)] = _mx(jnp.concatenate([lo, hi], axis=0).T)
    off += IDX_HEADS * IDX_DIM
    n_kw = IDX_DIM + IDX_HEADS
    kiw = jnp.concatenate([wt_ref[off:off + n_kw, :], jnp.zeros((LANES - n_kw, wt_ref.shape[1]), jnp.float32)], axis=0)
    kiw_ref[...] = _mx(kiw.T)
    off += n_kw
    panel(gb_ref, off)


def _pack_w_in(w_in, n_heads):
    wt = w_in.T
    n, d = wt.shape
    wa = n_heads * MLA_V
    assert n == n_heads * (MLA_NOPE + MLA_ROPE) + KV_LORA + MLA_ROPE + 5 * wa + IDX_HEADS * IDX_DIM + IDX_DIM + IDX_HEADS
    assert 2 * IDX_DIM == LANES and 2 * MLA_ROPE == LANES and MLA_NOPE == LANES
    slab = 256
    assert n_heads % 4 == 0
    widths = [n_heads * (MLA_NOPE + MLA_ROPE), KV_LORA, LANES, wa, wa, wa, wa, wa, IDX_HEADS * IDX_DIM, LANES]
    q_a, ckv, kr, g_a, g_b, q_b, k_b, v_b, q_i, kiw = pl.pallas_call(
        functools.partial(_pack_kernel, n_heads),
        grid=(d // slab,),
        in_specs=[pl.BlockSpec((n, slab), lambda i: (0, i))],
        out_specs=[pl.BlockSpec((slab, w), lambda i: (i, 0)) for w in widths],
        out_shape=[jax.ShapeDtypeStruct((d, w), MXU_DT) for w in widths],
        compiler_params=_cparams(1),
        name="pack_w_in",
    )(wt)
    return [q_a, ckv, kr, g_a, g_b], [q_b, k_b, v_b, q_i, kiw]


def _rope_table(pos):
    half = MLA_ROPE // 2
    freqs = jnp.power(ROPE_THETA, -jnp.arange(half, dtype=jnp.float32) / half)
    ang = pos.astype(jnp.float32)[:, None] * freqs
    cos, sin = jnp.cos(ang), jnp.sin(ang)
    return jnp.concatenate([cos, cos, -sin, sin], axis=1)


def _rel_bucket(rel):
    nb = N_BUCKETS // 2
    max_exact = nb // 2
    n = jnp.abs(rel)
    nf = jnp.maximum(n, 1).astype(jnp.float32)
    large = max_exact + (jnp.log(nf / max_exact) / math.log(MAX_DISTANCE / max_exact)
                         * (nb - max_exact)).astype(jnp.int32)
    large = jnp.minimum(large, nb - 1)
    return jnp.where(rel > 0, nb, 0) + jnp.where(n < max_exact, n, large)


def _bucket_bias(rel_bias, rel):
    return (rel_bias * LOG2E)[_rel_bucket(rel)]


def kernel(x_prompt, x_sample, cache_mla_ckv, cache_mla_krope, cache_dsa_k, cache_dsa_v, cache_idx_k,
           c_prompt, c_sample, w_ada, b_ada, ln_gain, w_in, mla_kv_gain, w_uk, w_uv, rel_bias,
           out_gain_a, out_gain_b, w_out, final_gain):
    assert w_ada.shape[0] == 1, "single-layer step"
    b, t, d = x_prompt.shape
    nb, ts, _ = x_sample.shape
    past = cache_mla_ckv.shape[2]
    n_heads = d // 256
    wb = n_heads * DSA_HEAD_DIM
    s_all = past + ts
    assert t % MLA_TQ == 0 and t % DSA_TQ == 0 and MLA_TQ % CHUNK == 0 and DSA_TQ % CHUNK == 0
    assert (s_all - 1) // CHUNK <= past // CHUNK

    mod = _ada_mod(jnp.concatenate([c_prompt, c_sample], axis=0), w_ada[0], b_ada[0])
    shift, scale, gate = mod[:, :d], mod[:, d:2 * d], mod[:, 2 * d:]
    mod_p = [v[:b].reshape(b, 1, d) for v in (scale, shift, gate)]
    mod_s = [jnp.broadcast_to(v[b:, None, :], (nb, ts, d)).reshape(1, nb * ts, d) for v in (scale, shift, gate)]

    w_a, w_b = _pack_w_in(w_in[0], n_heads)
    wkv = _mx(jnp.concatenate([w_uk[0].transpose(2, 0, 1).reshape(KV_LORA, n_heads * MLA_NOPE),
                               w_uv[0].transpose(2, 0, 1).reshape(KV_LORA, n_heads * MLA_V)], axis=1))
    w_o = _mx(w_out[0])
    pos_p = jnp.arange(t, dtype=jnp.int32)
    pos_s = past + jnp.arange(ts, dtype=jnp.int32)
    tab_p = _rope_table(pos_p)
    tab_s = jnp.tile(_rope_table(pos_s), (nb, 1))

    tm = 256
    qcat, ckv_p, krope_p, ga, gb, kcat, vup, oa_buf = _proj_a(
        x_prompt, mod_p[0], mod_p[1], ln_gain[0], tab_p, w_a, mla_kv_gain[0], wkv, tm)
    qb, k_p, v_p, kbb, vbb, qi, idxk_p, kiab, wi, ob_buf = _proj_b(
        x_prompt, mod_p[0], mod_p[1], ln_gain[0], w_b, tm, True)
    o_a = _mla_prompt(qcat, kcat, vup, oa_buf, b, t)
    near = _bucket_bias(rel_bias, jnp.arange(-MAX_DISTANCE - 1, MAX_DISTANCE, dtype=jnp.int32))
    far = near[:1]
    bias_tab = jnp.concatenate([jnp.broadcast_to(far, (t - 1 - MAX_DISTANCE, n_heads)), near[1:],
                                jnp.broadcast_to(far, (t + 1 - MAX_DISTANCE, n_heads))], axis=0).T
    o_b = _dsa_prompt(qb, qi, wi, kbb, vbb, kiab, bias_tab, ob_buf, b, t, min(TOPK_MAX, t // 4))
    y_prompt = _out_proj(o_a, o_b, ga, gb, x_prompt, mod_p[2], out_gain_a[0], out_gain_b[0], w_o, final_gain, 2 * tm)

    xs3 = x_sample.reshape(1, nb * ts, d)
    tms = min(256, nb * ts)
    qcat_s, ckv_s, krope_s, ga_s, gb_s = _proj_a(
        xs3, mod_s[0], mod_s[1], ln_gain[0], tab_s, w_a, mla_kv_gain[0], None, tms)
    qb_s, k_s, v_s, kbb_s, vbb_s, qi_s, idxk_s, kiab_s, wi_s = _proj_b(
        xs3, mod_s[0], mod_s[1], ln_gain[0], w_b, tms, False)
    oa_s = _mla_sample(qcat_s, cache_mla_ckv[0], cache_mla_krope[0], ckv_s, krope_s,
                       _mx(w_uk[0]), _mx(w_uv[0]), ts)
    s_pad = pl.cdiv(s_all, LANES) * LANES
    tab_s = _bucket_bias(rel_bias, jnp.arange(s_pad + LANES, dtype=jnp.int32) - (s_all - 1)).T
    madd_s = _dsa_select(qi_s, wi_s, cache_idx_k[0], kiab_s, ts, min(TOPK_MAX, s_all // 4))
    ob_s = _dsa_sample(qb_s, madd_s, cache_dsa_k[0], cache_dsa_v[0], kbb_s, vbb_s, tab_s, ts)
    y_sample = _out_proj(oa_s, ob_s, ga_s, gb_s, xs3, mod_s[2], out_gain_a[0], out_gain_b[0], w_o, final_gain, tms)

    hd = (n_heads, DSA_HEAD_DIM)
    return (y_prompt, y_sample.reshape(nb, ts, d),
            ckv_p.reshape(1, b, t, KV_LORA), krope_p.reshape(1, b, t, MLA_ROPE),
            k_p.reshape(1, b, t, *hd), v_p.reshape(1, b, t, *hd), idxk_p.reshape(1, b, t, IDX_DIM),
            ckv_s.reshape(1, nb, ts, KV_LORA), krope_s.reshape(1, nb, ts, MLA_ROPE),
            k_s.reshape(1, nb, ts, *hd), v_s.reshape(1, nb, ts, *hd), idxk_s.reshape(1, nb, ts, IDX_DIM))
```

```python
import functools
import math

import jax
import jax.numpy as jnp
from jax import lax
from jax.experimental import pallas as pl
from jax.experimental.pallas import tpu as pltpu

MXU_DT = jnp.bfloat16

CHUNK = 64
MLA_NOPE = 128
MLA_ROPE = 64
MLA_V = 128
KV_LORA = 512
DSA_HEAD_DIM = 128
IDX_HEADS = 16
IDX_DIM = 64
TOPK_MAX = 256
N_BUCKETS = 32
MAX_DISTANCE = 128
ROPE_THETA = 10000.0
EPS = 1e-6

LANES = 128
HEAD_SLOT = 256
NEG_BIG = -1e30
LOG2E = math.log2(math.e)
INT_MIN = -2 ** 31
KEY_NEG_INF = INT_MIN + 0x7FFFFF
VMEM_LIMIT = 56 * 1024 * 1024


def _cparams(n_grid, vmem=VMEM_LIMIT):
    return pltpu.CompilerParams(dimension_semantics=("arbitrary",) * n_grid, vmem_limit_bytes=vmem)


def _mx(v):
    return v.astype(MXU_DT)


def _dot(a, b):
    return jnp.dot(a, b, preferred_element_type=jnp.float32)


def _dot_nt(a, b):
    return lax.dot_general(a, b, (((1,), (1,)), ((), ())), preferred_element_type=jnp.float32)


def _silu(v):
    return v * (1.0 / (1.0 + jnp.exp(-v)))


def _resident(shape):
    nd = len(shape)
    return pl.BlockSpec(shape, lambda *_: (0,) * nd, pipeline_mode=pl.Buffered(1))


def _ada_kernel(c_ref, w_ref, b_ref, o_ref):
    a = _mx(_silu(c_ref[...]))
    o_ref[...] = _dot(a, _mx(w_ref[...])) + b_ref[...]


def _ada_mod(c_all, w_ada, b_ada):
    m, d = c_all.shape
    n = w_ada.shape[1]
    tn = 1024
    return pl.pallas_call(
        _ada_kernel,
        grid=(n // tn,),
        in_specs=[pl.BlockSpec((m, d), lambda j: (0, 0)),
                  pl.BlockSpec((d, tn), lambda j: (0, j)),
                  pl.BlockSpec((1, tn), lambda j: (0, j))],
        out_specs=pl.BlockSpec((m, tn), lambda j: (0, j)),
        out_shape=jax.ShapeDtypeStruct((m, n), jnp.float32),
        compiler_params=_cparams(1),
        name="ada_mod",
    )(c_all, w_ada, b_ada.reshape(1, n))


def _modulated_norm(x_ref, scale_ref, shift_ref, lng_ref):
    x = x_ref[...]
    xn = x * lax.rsqrt(jnp.mean(x * x, axis=-1, keepdims=True) + EPS) * lng_ref[...]
    return _mx(xn * (1.0 + scale_ref[...]) + shift_ref[...])


def _rope128(a, tab):
    t = a * tab
    return t + pltpu.roll(t, 64, 1)


def _proj_a_kernel(n_heads, emit_kv, x_ref, scale_ref, shift_ref, lng_ref, tab_ref,
                   wq_ref, wc_ref, wr_ref, wga_ref, wgb_ref, kvg_ref, *rest):
    if emit_kv:
        wkv_ref, qcat_ref, ckv_ref, krope_ref, ga_ref, gb_ref, kcat_ref, vup_ref, attn_ref = rest
        attn_ref[...] = jnp.zeros_like(attn_ref)
    else:
        qcat_ref, ckv_ref, krope_ref, ga_ref, gb_ref = rest
    hb = _modulated_norm(x_ref, scale_ref, shift_ref, lng_ref)
    tab = tab_ref[...]
    qscale = (MLA_NOPE + MLA_ROPE) ** -0.5 * LOG2E
    for h in range(0, n_heads, 2):
        a = _dot(hb, wq_ref[:, h * LANES:(h + 2) * LANES])
        qcat_ref[:, h * HEAD_SLOT:h * HEAD_SLOT + LANES] = _mx(a[:, :LANES] * qscale)
        qcat_ref[:, (h + 1) * HEAD_SLOT:(h + 1) * HEAD_SLOT + LANES] = _mx(a[:, LANES:] * qscale)
    lane = lax.broadcasted_iota(jnp.int32, tab.shape, 1)
    tab_r = pltpu.roll(tab, MLA_ROPE, 1)
    cos4 = jnp.where(lane < MLA_ROPE, tab, tab_r)
    sin4 = jnp.where(lane < MLA_ROPE, tab_r, tab)
    is_x1 = lane % MLA_ROPE < MLA_ROPE // 2
    n_nope = n_heads * MLA_NOPE
    for h in range(0, n_heads, 4):
        a = _dot(hb, wq_ref[:, n_nope + (h // 2) * LANES:n_nope + (h // 2 + 2) * LANES])
        for u in range(2):
            ap = a[:, u * LANES:(u + 1) * LANES]
            swapped = jnp.where(is_x1, pltpu.roll(ap, LANES - MLA_ROPE // 2, 1), pltpu.roll(ap, MLA_ROPE // 2, 1))
            roped = (ap * cos4 + swapped * sin4) * qscale
            c0 = (h + 2 * u) * HEAD_SLOT
            qcat_ref[:, c0 + LANES:c0 + HEAD_SLOT] = _mx(roped)
            qcat_ref[:, c0 + HEAD_SLOT + LANES:c0 + 2 * HEAD_SLOT] = _mx(pltpu.roll(roped, MLA_ROPE, 1))
    c = _dot(hb, wc_ref[...])
    cn = c * lax.rsqrt(jnp.mean(c * c, axis=-1, keepdims=True) + EPS) * kvg_ref[...]
    ckv_ref[...] = cn
    r = _rope128(_dot(hb, wr_ref[...]), tab)
    krope_ref[...] = r[:, :MLA_ROPE]
    ga_ref[...] = _mx(_silu(_dot(hb, wga_ref[...])))
    gb_ref[...] = _mx(_silu(_dot(hb, wgb_ref[...])))
    if emit_kv:
        cb = _mx(cn)
        lane = lax.broadcasted_iota(jnp.int32, r.shape, 1)
        krz = _mx(jnp.where(lane < MLA_ROPE, r, 0.0))
        kn = _dot(cb, wkv_ref[:, :n_heads * MLA_NOPE])
        for h in range(n_heads):
            c0 = h * HEAD_SLOT
            kcat_ref[:, c0:c0 + LANES] = _mx(kn[:, h * MLA_NOPE:(h + 1) * MLA_NOPE])
            kcat_ref[:, c0 + LANES:c0 + HEAD_SLOT] = krz
        vup_ref[...] = _mx(_dot(cb, wkv_ref[:, n_heads * MLA_NOPE:]))


def _proj_b_kernel(n_heads, emit_attn, x_ref, scale_ref, shift_ref, lng_ref, wq_ref, wk_ref, wv_ref, wqi_ref,
                   wkw_ref, qb_ref, kb_hbm, vb_hbm, kbb_ref, vbb_ref, qi_ref, ki_ref, kiab_ref, wi_ref, *rest):
    if emit_attn:
        rest[0][...] = jnp.zeros_like(rest[0])
    rows_ref, sem = rest[-2:]
    tm = x_ref.shape[0]
    n_steps = pl.num_programs(0) * pl.num_programs(1)
    step_id = pl.program_id(0) * pl.num_programs(1) + pl.program_id(1)
    slot = step_id % 2

    def row_copies(slot_, step_, which):
        dst = (kb_hbm, vb_hbm)[which]
        return [pltpu.make_async_copy(rows_ref.at[slot_, which, :, h * DSA_HEAD_DIM:(h + 1) * DSA_HEAD_DIM],
                                      dst.at[pl.ds(step_ * tm, tm), h, :], sem.at[slot_, which])
                for h in range(n_heads)]

    def wait_slot(slot_, step_):
        for which in range(2):
            for cp in row_copies(slot_, step_, which):
                cp.wait()

    @pl.when(step_id >= 2)
    def _():
        wait_slot(slot, step_id - 2)

    hb = _modulated_norm(x_ref, scale_ref, shift_ref, lng_ref)
    dscale = DSA_HEAD_DIM ** -0.5 * LOG2E
    step = 512
    for c0 in range(0, wq_ref.shape[1], step):
        qb_ref[:, c0:c0 + step] = _mx(_dot(hb, wq_ref[:, c0:c0 + step]) * dscale)
    for which, (w_ref, b_ref) in enumerate(((wk_ref, kbb_ref), (wv_ref, vbb_ref))):
        for c0 in range(0, w_ref.shape[1], step):
            kv = _dot(hb, w_ref[:, c0:c0 + step])
            rows_ref[slot, which, :, c0:c0 + step] = kv
            b_ref[:, c0:c0 + step] = _mx(kv)
        for cp in row_copies(slot, step_id, which):
            cp.start()

    @pl.when(step_id == n_steps - 1)
    def _():
        wait_slot(slot, step_id)

        @pl.when(n_steps >= 2)
        def _():
            wait_slot(1 - slot, step_id - 1)
    for c0 in range(0, wqi_ref.shape[1], step):
        qi_ref[:, c0:c0 + step] = _mx(_dot(hb, wqi_ref[:, c0:c0 + step]))
    a = _dot(hb, wkw_ref[...])
    ki_ref[...] = a[:, :IDX_DIM]
    lane = lax.broadcasted_iota(jnp.int32, a.shape, 1)
    kz = jnp.where(lane < IDX_DIM, a, 0.0)
    kiab_ref[:, :LANES] = _mx(kz)
    kiab_ref[:, LANES:] = _mx(pltpu.roll(kz, IDX_DIM, 1))
    wi_ref[...] = a * (IDX_HEADS ** -0.5 * IDX_DIM ** -0.5)


def _row_specs(x3, mod_rows, tm):
    bv, tv, d = x3.shape
    x_spec = pl.BlockSpec((None, tm, d), lambda b, i: (b, i, 0))
    if mod_rows == 1:
        m_spec = pl.BlockSpec((None, 1, d), lambda b, i: (b, 0, 0))
    else:
        m_spec = pl.BlockSpec((None, tm, d), lambda b, i: (b, i, 0))
    return x_spec, m_spec


def _out2d(m, width, dtype, tm, nt):
    return (jax.ShapeDtypeStruct((m, width), dtype),
            pl.BlockSpec((tm, width), lambda b, i: (b * nt + i, 0)))


def _proj_a(x3, scale, shift, ln_gain, tab, ws_a, kv_gain, wkv, tm):
    bv, tv, d = x3.shape
    nt = tv // tm
    m = bv * tv
    n_heads = d // 256
    wa = n_heads * MLA_V
    emit_kv = wkv is not None
    x_spec, m_spec = _row_specs(x3, scale.shape[1], tm)
    in_specs = ([x_spec, m_spec, m_spec, _resident((1, d)), pl.BlockSpec((tm, LANES), lambda b, i: (i, 0))]
                + [_resident(w.shape) for w in ws_a] + [_resident((1, KV_LORA))])
    args = [x3, scale, shift, ln_gain.reshape(1, d), tab, *ws_a, kv_gain.reshape(1, KV_LORA)]
    outs = [_out2d(m, n_heads * HEAD_SLOT, MXU_DT, tm, nt),
            _out2d(m, KV_LORA, jnp.float32, tm, nt),
            _out2d(m, MLA_ROPE, jnp.float32, tm, nt),
            _out2d(m, wa, MXU_DT, tm, nt),
            _out2d(m, wa, MXU_DT, tm, nt)]
    if emit_kv:
        in_specs.append(_resident(wkv.shape))
        args.append(wkv)
        outs += [_out2d(m, n_heads * HEAD_SLOT, MXU_DT, tm, nt),
                 _out2d(m, wa, MXU_DT, tm, nt),
                 _out2d(m, wa, MXU_DT, tm, nt)]
    return pl.pallas_call(
        functools.partial(_proj_a_kernel, n_heads, emit_kv),
        grid=(bv, nt),
        in_specs=in_specs,
        out_specs=[o[1] for o in outs],
        out_shape=[o[0] for o in outs],
        compiler_params=_cparams(2),
        name="proj_a",
    )(*args)


def _proj_b(x3, scale, shift, ln_gain, ws_b, tm, emit_attn_buffer):
    bv, tv, d = x3.shape
    nt = tv // tm
    m = bv * tv
    n_heads = d // 256
    width_b = n_heads * DSA_HEAD_DIM
    x_spec, m_spec = _row_specs(x3, scale.shape[1], tm)
    cache_rows = (jax.ShapeDtypeStruct((m, n_heads, DSA_HEAD_DIM), jnp.float32), pl.BlockSpec(memory_space=pl.ANY))
    outs = [_out2d(m, width_b, MXU_DT, tm, nt),
            cache_rows,
            cache_rows,
            _out2d(m, width_b, MXU_DT, tm, nt),
            _out2d(m, width_b, MXU_DT, tm, nt),
            _out2d(m, IDX_HEADS * IDX_DIM, MXU_DT, tm, nt),
            _out2d(m, IDX_DIM, jnp.float32, tm, nt),
            _out2d(m, 2 * LANES, MXU_DT, tm, nt),
            _out2d(m, LANES, jnp.float32, tm, nt)]
    if emit_attn_buffer:
        outs.append(_out2d(m, width_b, MXU_DT, tm, nt))
    return pl.pallas_call(
        functools.partial(_proj_b_kernel, n_heads, emit_attn_buffer),
        grid=(bv, nt),
        in_specs=[x_spec, m_spec, m_spec, _resident((1, d))] + [_resident(w.shape) for w in ws_b],
        out_specs=[o[1] for o in outs],
        out_shape=[o[0] for o in outs],
        scratch_shapes=[pltpu.VMEM((2, 2, tm, width_b), jnp.float32), pltpu.SemaphoreType.DMA((2, 2))],
        compiler_params=_cparams(2),
        name="proj_b",
    )(x3, scale, shift, ln_gain.reshape(1, d), *ws_b)


def _softmax_pv(s, v):
    m = jnp.max(s, axis=-1, keepdims=True)
    p = jnp.exp2(s - m)
    l = jnp.sum(p, axis=-1, keepdims=True)
    return _dot(_mx(p), v) * (1.0 / l)


def _topk_mask(score, adm, topk, sc_ref, madd_ref):
    rows, n = score.shape
    if n <= topk:
        madd_ref[...] = jnp.where(adm, 0.0, NEG_BIG)
        return
    sc_ref[...] = jnp.where(adm, score, -jnp.inf)
    kf = float(topk)

    def count(pred):
        return jnp.sum(jnp.where(pred, 1.0, 0.0), axis=-1, keepdims=True)

    def key_to_float(key):
        return pltpu.bitcast(key ^ ((key >> 31) & 0x7FFFFFFF), jnp.float32)

    def bit_step(i, thr):
        inc = lax.shift_left(jnp.int32(1), 31 - i)
        cand = thr + inc
        ok = count(sc_ref[...] >= key_to_float(cand)) >= kf
        return jnp.where(ok, cand, thr)

    thr = lax.fori_loop(0, 32, bit_step, jnp.full((rows, 1), INT_MIN, jnp.int32))
    thr = key_to_float(jnp.maximum(thr, KEY_NEG_INF))
    ge = sc_ref[...] >= thr
    madd_ref[...] = jnp.where(ge, 0.0, NEG_BIG)

    @pl.when(jnp.max(count(ge)) > kf)
    def _():
        sc = sc_ref[...]
        gt = sc > thr
        need = kf - count(gt)
        eqf = jnp.where(sc == thr, 1.0, 0.0)
        col = lax.broadcasted_iota(jnp.int32, (rows, n), 1)
        nbits = max(1, int(n).bit_length())

        def col_step(i, bound):
            cand = bound + lax.shift_left(jnp.int32(1), nbits - 1 - i)
            taken = jnp.sum(jnp.where(col < cand, eqf, 0.0), axis=-1, keepdims=True)
            return jnp.where(taken <= need, cand, bound)

        bound = lax.fori_loop(0, nbits, col_step, jnp.zeros((rows, 1), jnp.int32))
        tie_madd = jnp.where(jnp.where(col < bound, eqf, 0.0) > 0.5, 0.0, NEG_BIG)
        madd_ref[...] = jnp.where(adm, jnp.where(gt, 0.0, tie_madd), NEG_BIG)


def _indexer_scores(qi_ref, wi, kia, kib):
    half = IDX_HEADS // 2
    score = None
    for j in range(half):
        qp = qi_ref[:, j * LANES:(j + 1) * LANES]
        da = jnp.maximum(_dot_nt(qp, kia), 0.0) * wi[:, IDX_DIM + j:IDX_DIM + j + 1]
        db = jnp.maximum(_dot_nt(qp, kib), 0.0) * wi[:, IDX_DIM + half + j:IDX_DIM + half + j + 1]
        score = da + db if score is None else score + da + db
    return score


def _indexer_scores_stacked(qi_ref, wi, dots_a, dots_b):
    half = IDX_HEADS // 2
    ts = qi_ref.shape[0]
    q_all = jnp.concatenate([qi_ref[:, j * LANES:(j + 1) * LANES] for j in range(half)], axis=0)
    wa = jnp.concatenate([wi[:, IDX_DIM + j:IDX_DIM + j + 1] for j in range(half)], axis=0)
    wb = jnp.concatenate([wi[:, IDX_DIM + half + j:IDX_DIM + half + j + 1] for j in range(half)], axis=0)
    part = jnp.maximum(dots_a(q_all), 0.0) * wa + jnp.maximum(dots_b(q_all), 0.0) * wb
    score = part[0:ts]
    for j in range(1, half):
        score = score + part[j * ts:(j + 1) * ts]
    return score


MLA_TQ = 512
DSA_TQ = 256
DSA_STEP_MAX_KEYS = 2048


def _chunk_madd(row0, tq, n_keys):
    qc = (row0 + lax.broadcasted_iota(jnp.int32, (tq, n_keys), 0)) // CHUNK
    kc = lax.broadcasted_iota(jnp.int32, (tq, n_keys), 1) // CHUNK
    return kc <= qc


def _chained_tile_calls(n_tiles, make_call, out):
    for c in range(n_tiles):
        out = make_call(c, out)
    return out


def _mla_prompt_kernel(n_heads, row0, q_ref, k_ref, v_ref, prev_ref, o_ref):
    del prev_ref
    tq, n_keys = q_ref.shape[0], k_ref.shape[0]
    madd = jnp.where(_chunk_madd(row0, tq, n_keys), 0.0, NEG_BIG)
    for h in range(n_heads):
        c0 = h * HEAD_SLOT
        s = _dot_nt(q_ref[:, c0:c0 + HEAD_SLOT], k_ref[:, c0:c0 + HEAD_SLOT]) + madd
        o_ref[:, h * MLA_V:(h + 1) * MLA_V] = _mx(_softmax_pv(s, v_ref[:, h * MLA_V:(h + 1) * MLA_V]))


def _mla_prompt(qcat, kcat, vup, out_init, b, t):
    tq = MLA_TQ
    n_heads = qcat.shape[1] // HEAD_SLOT
    nq = t // tq
    wa = n_heads * MLA_V
    k3 = kcat.reshape(b, t, kcat.shape[1])
    v3 = vup.reshape(b, t, wa)

    def make_call(c, out):
        n_keys = (c + 1) * tq
        row = lambda bi: (bi * nq + c, 0)
        return pl.pallas_call(
            functools.partial(_mla_prompt_kernel, n_heads, c * tq),
            grid=(b,),
            in_specs=[pl.BlockSpec((tq, qcat.shape[1]), row),
                      pl.BlockSpec((None, n_keys, kcat.shape[1]), lambda bi: (bi, 0, 0)),
                      pl.BlockSpec((None, n_keys, wa), lambda bi: (bi, 0, 0)),
                      pl.BlockSpec(memory_space=pl.ANY)],
            out_specs=pl.BlockSpec((tq, wa), row),
            out_shape=jax.ShapeDtypeStruct(out.shape, out.dtype),
            input_output_aliases={3: 0},
            compiler_params=_cparams(1),
            name="mla_prompt_%d" % c,
        )(qcat, k3, v3, out)

    return _chained_tile_calls(nq, make_call, out_init)


def _toeplitz_bias(tab_ref, h, start, n_rows, n_cols):
    shift0 = LANES - (n_rows - 1)
    from_left = (lax.broadcasted_iota(jnp.int32, (n_rows, LANES), 1)
                 < shift0 + lax.broadcasted_iota(jnp.int32, (n_rows, LANES), 0))
    pieces = []
    prev = None
    for n in range(n_cols // LANES + 1):
        seg = tab_ref[h:h + 1, start + n * LANES:start + (n + 1) * LANES]
        rot = pltpu.roll(jnp.broadcast_to(seg, (n_rows, LANES)), shift0 % LANES, 1, stride=1, stride_axis=0)
        if prev is not None:
            pieces.append(jnp.where(from_left, prev, rot))
        prev = rot
    return jnp.concatenate(pieces, axis=1)


def _dsa_prompt_kernel(n_heads, row0, t, topk, qb_ref, qi_ref, wi_ref, k_ref, v_ref, kiab_ref, tab_ref, prev_ref,
                       o_ref, key_ref, madd_ref):
    del prev_ref
    per, tq, _ = qb_ref.shape
    n_keys = k_ref.shape[1]
    score = jnp.concatenate(
        [_indexer_scores(qi_ref.at[u], wi_ref[u], kiab_ref[u, :, :LANES], kiab_ref[u, :, LANES:])
         for u in range(per)], axis=0)
    adm = _chunk_madd(row0, tq, n_keys)
    _topk_mask(score, jnp.concatenate([adm] * per, axis=0), topk, key_ref, madd_ref)
    near0 = max(0, row0 - LANES)
    n_near = n_keys - near0
    for h in range(n_heads):
        c0 = h * DSA_HEAD_DIM
        near = jnp.concatenate(
            [_toeplitz_bias(tab_ref, h, near0 - (row0 + u * LANES) - LANES + t, LANES, n_near)
             for u in range(tq // LANES)], axis=0)
        if near0 > 0:
            far = jnp.broadcast_to(tab_ref[h:h + 1, 0:1], (tq, near0))
            bias = jnp.concatenate([far, near], axis=1)
        else:
            bias = near
        for u in range(per):
            s = (_dot_nt(qb_ref[u, :, c0:c0 + DSA_HEAD_DIM], k_ref[u, :, c0:c0 + DSA_HEAD_DIM]) + bias
                 + madd_ref[u * tq:(u + 1) * tq, :])
            o_ref[u, :, c0:c0 + DSA_HEAD_DIM] = _mx(_softmax_pv(s, v_ref[u, :, c0:c0 + DSA_HEAD_DIM]))


def _dsa_prompt(qb, qi, wi, kbb, vbb, kiab, bias_tab, out_init, b, t, topk):
    tq = DSA_TQ
    assert MAX_DISTANCE <= LANES and tq % LANES == 0
    n_heads = qb.shape[1] // DSA_HEAD_DIM
    nq = t // tq
    wb = qb.shape[1]
    k3, v3, ki3 = kbb.reshape(b, t, wb), vbb.reshape(b, t, wb), kiab.reshape(b, t, 2 * LANES)
    qb3, qi3, wi3 = qb.reshape(b, t, wb), qi.reshape(b, t, qi.shape[1]), wi.reshape(b, t, LANES)
    keys_of = lambda bi: (bi, 0, 0)

    def make_call(c, out):
        n_keys = (c + 1) * tq
        per = max(p for p in (1, 2, 4) if p == 1 or (b % p == 0 and p * n_keys <= DSA_STEP_MAX_KEYS))
        rows_of = lambda bi: (bi, c, 0)
        return pl.pallas_call(
            functools.partial(_dsa_prompt_kernel, n_heads, c * tq, t, topk),
            grid=(b // per,),
            in_specs=[pl.BlockSpec((per, tq, wb), rows_of),
                      pl.BlockSpec((per, tq, qi.shape[1]), rows_of),
                      pl.BlockSpec((per, tq, LANES), rows_of),
                      pl.BlockSpec((per, n_keys, wb), keys_of),
                      pl.BlockSpec((per, n_keys, wb), keys_of),
                      pl.BlockSpec((per, n_keys, 2 * LANES), keys_of),
                      _resident(bias_tab.shape),
                      pl.BlockSpec(memory_space=pl.ANY)],
            out_specs=pl.BlockSpec((per, tq, wb), rows_of),
            out_shape=jax.ShapeDtypeStruct(out.shape, out.dtype),
            input_output_aliases={7: 0},
            scratch_shapes=[pltpu.VMEM((per * tq, n_keys), jnp.float32),
                            pltpu.VMEM((per * tq, n_keys), jnp.float32)],
            compiler_params=_cparams(1),
            name="dsa_prompt_%d" % c,
        )(qb3, qi3, wi3, k3, v3, ki3, bias_tab, out)

    return _chained_tile_calls(nq, make_call, out_init.reshape(b, t, wb)).reshape(b * t, wb)


def _mla_sample_kernel(n_heads, past, ts, qcat_ref, cckv_ref, ckrt_ref, nckv_ref, nkr_ref, wuk_ref, wuv_ref,
                       o_ref, kall_ref, rt_ref, rnew_ref):
    n_keys = past + ts

    @pl.when(pl.program_id(0) == 0)
    def _():
        kall_ref[...] = jnp.zeros_like(kall_ref)
        rt_ref[...] = jnp.zeros_like(rt_ref)
        rnew_ref[...] = jnp.zeros_like(rnew_ref)

    for u in range(cckv_ref.shape[0]):
        rows = slice(u * ts, (u + 1) * ts)
        kall_ref[u, 0:past, :] = _mx(cckv_ref[u])
        kall_ref[u, past:n_keys, :] = _mx(nckv_ref[rows, :])
        rt_ref[u, 0:MLA_ROPE, :] = _mx(ckrt_ref[u])
        rnew_ref[u, 0:ts, 0:MLA_ROPE] = _mx(nkr_ref[rows, :])
        qlat, qrope = [], []
        for h in range(n_heads):
            c0 = h * HEAD_SLOT
            qlat.append(_mx(_dot(qcat_ref[rows, c0:c0 + LANES], wuk_ref[h])))
            qrope.append(qcat_ref[rows, c0 + LANES:c0 + HEAD_SLOT])
        qlat = jnp.concatenate(qlat, axis=0)
        qrope = jnp.concatenate(qrope, axis=0)
        s = _dot_nt(qlat, kall_ref[u]) + jnp.concatenate(
            [_dot(qrope, rt_ref[u]), _dot_nt(qrope, rnew_ref[u])], axis=1)
        col = lax.broadcasted_iota(jnp.int32, s.shape, 1)
        s = jnp.where(col < n_keys, s, NEG_BIG)
        olat = _mx(_softmax_pv(s, kall_ref[u]))
        for h in range(n_heads):
            o_ref[rows, h * MLA_V:(h + 1) * MLA_V] = _mx(_dot_nt(olat[h * ts:(h + 1) * ts], wuv_ref[h]))


def _mla_sample(qcat, cache_ckv, cache_kr_t, new_ckv, new_kr, wuk, wuv, ts):
    nb, past, c = cache_ckv.shape
    assert past % LANES == 0
    n_heads = qcat.shape[1] // HEAD_SLOT
    wa = n_heads * MLA_V
    s_pad = pl.cdiv(past + ts, LANES) * LANES
    per = 2 if nb % 2 == 0 else 1
    row = lambda bi: (bi, 0)
    return pl.pallas_call(
        functools.partial(_mla_sample_kernel, n_heads, past, ts),
        grid=(nb // per,),
        in_specs=[pl.BlockSpec((per * ts, qcat.shape[1]), row),
                  pl.BlockSpec((per, past, c), lambda bi: (bi, 0, 0)),
                  pl.BlockSpec((per, MLA_ROPE, past), lambda bi: (bi, 0, 0)),
                  pl.BlockSpec((per * ts, c), row),
                  pl.BlockSpec((per * ts, MLA_ROPE), row),
                  _resident(wuk.shape), _resident(wuv.shape)],
        out_specs=pl.BlockSpec((per * ts, wa), row),
        out_shape=jax.ShapeDtypeStruct((nb * ts, wa), MXU_DT),
        scratch_shapes=[pltpu.VMEM((per, s_pad, c), MXU_DT), pltpu.VMEM((per, LANES, past), MXU_DT),
                        pltpu.VMEM((per, s_pad - past, LANES), MXU_DT)],
        compiler_params=_cparams(1),
        name="mla_sample",
    )(qcat, cache_ckv, cache_kr_t, new_ckv, new_kr, wuk, wuv)


def _dsa_select_kernel(past, ts, topk, qi_ref, wi_ref, ckit_ref, nkiab_ref, madd_ref,
                       kat_ref, kbt_ref, anew_ref, bnew_ref, key_ref):
    bi = pl.program_id(0)
    n_keys = past + ts
    rows, s_pad = madd_ref.shape

    @pl.when(bi == 0)
    def _():
        for ref in (kat_ref, kbt_ref, anew_ref, bnew_ref):
            ref[...] = jnp.zeros_like(ref)

    ckit = _mx(ckit_ref[...])
    kat_ref[0:IDX_DIM, :] = ckit
    kbt_ref[IDX_DIM:LANES, :] = ckit
    anew_ref[0:ts, :] = nkiab_ref[:, :LANES]
    bnew_ref[0:ts, :] = nkiab_ref[:, LANES:]
    madd_ref[pl.ds(pl.multiple_of(bi * ts, ts), ts), :] = _indexer_scores_stacked(
        qi_ref, wi_ref[...],
        lambda q: jnp.concatenate([_dot(q, kat_ref[...]), _dot_nt(q, anew_ref[...])], axis=1),
        lambda q: jnp.concatenate([_dot(q, kbt_ref[...]), _dot_nt(q, bnew_ref[...])], axis=1))

    @pl.when(bi == pl.num_programs(0) - 1)
    def _():
        col = lax.broadcasted_iota(jnp.int32, (rows, s_pad), 1)
        _topk_mask(madd_ref[...], col < n_keys, topk, key_ref, madd_ref)


def _dsa_select(qi, wi, cache_ki_t, new_kiab, ts, topk):
    nb, _, past = cache_ki_t.shape
    assert past % LANES == 0
    s_pad = pl.cdiv(past + ts, LANES) * LANES
    row = lambda bi: (bi, 0)
    return pl.pallas_call(
        functools.partial(_dsa_select_kernel, past, ts, topk),
        grid=(nb,),
        in_specs=[pl.BlockSpec((ts, qi.shape[1]), row),
                  pl.BlockSpec((ts, LANES), row),
                  pl.BlockSpec((None, IDX_DIM, past), lambda bi: (bi, 0, 0)),
                  pl.BlockSpec((ts, 2 * LANES), row)],
        out_specs=pl.BlockSpec((nb * ts, s_pad), lambda bi: (0, 0)),
        out_shape=jax.ShapeDtypeStruct((nb * ts, s_pad), jnp.float32),
        scratch_shapes=[pltpu.VMEM((LANES, past), MXU_DT), pltpu.VMEM((LANES, past), MXU_DT),
                        pltpu.VMEM((s_pad - past, LANES), MXU_DT), pltpu.VMEM((s_pad - past, LANES), MXU_DT),
                        pltpu.VMEM((nb * ts, s_pad), jnp.float32)],
        compiler_params=_cparams(1),
        name="dsa_select",
    )(qi, wi, cache_ki_t, new_kiab)


def _dsa_sample_kernel(n_heads, past, ts, qb_ref, madd_ref, ck_ref, cv_ref, nk_ref, nv_ref, tab_ref, expand_ref,
                       o_ref, kflat_ref, vflat_ref, bias_ref, biasw_ref):
    n_keys = past + ts
    wide = n_heads * LANES
    n_blocks = pl.cdiv(n_keys, LANES)
    widths = [min(wide, (n_keys - j * LANES) * n_heads) for j in range(n_blocks)]

    @pl.when(pl.program_id(0) == 0)
    def _():
        for h in range(n_heads):
            bias_ref[h * ts:(h + 1) * ts, :] = _toeplitz_bias(tab_ref, h, 0, ts, n_blocks * LANES)
        shape = (n_heads * ts, wide)
        same_head = (lax.broadcasted_iota(jnp.int32, shape, 0) // ts
                     == lax.broadcasted_iota(jnp.int32, shape, 1) % n_heads)
        for j in range(n_blocks):
            b = bias_ref[:, j * LANES:(j + 1) * LANES]
            hi = _mx(b)
            rest = b - hi.astype(jnp.float32)
            mid = _mx(rest)
            lo = _mx(rest - mid.astype(jnp.float32))
            piece = _dot(hi, expand_ref[...]) + _dot(mid, expand_ref[...]) + _dot(lo, expand_ref[...])
            biasw_ref[:, j * wide:j * wide + widths[j]] = jnp.where(same_head, piece, NEG_BIG)[:, :widths[j]]

    kflat_ref[0:past * n_heads, :] = _mx(ck_ref[...])
    kflat_ref[past * n_heads:, :] = nk_ref[...]
    vflat_ref[0:past * n_heads, :] = _mx(cv_ref[...])
    vflat_ref[past * n_heads:, :] = nv_ref[...]
    sel = _mx(jnp.where(madd_ref[...] == 0.0, 1.0, 0.0))
    pieces = [_dot(sel[:, j * LANES:(j + 1) * LANES], expand_ref[...])[:, :widths[j]] for j in range(n_blocks)]
    sel_wide = jnp.concatenate(pieces, axis=1)
    sel_wide = jnp.concatenate([sel_wide] * n_heads, axis=0)
    q_all = jnp.concatenate([qb_ref[:, h * DSA_HEAD_DIM:(h + 1) * DSA_HEAD_DIM] for h in range(n_heads)], axis=0)
    s = _dot_nt(q_all, kflat_ref[...]) + biasw_ref[...] + jnp.where(sel_wide > 0.5, 0.0, NEG_BIG)
    o = _mx(_softmax_pv(s, vflat_ref[...]))
    for h in range(n_heads):
        o_ref[:, h * DSA_HEAD_DIM:(h + 1) * DSA_HEAD_DIM] = o[h * ts:(h + 1) * ts]


def _dsa_sample(qb, madd, cache_k, cache_v, new_k, new_v, bias_tab, ts):
    nb, past, n_heads, _ = cache_k.shape
    wb = n_heads * DSA_HEAD_DIM
    n_keys = past + ts
    assert (n_keys % LANES * n_heads) % LANES == 0
    expand = _mx(jnp.repeat(jnp.eye(LANES, dtype=jnp.float32), n_heads, axis=1))
    row = lambda bi: (bi, 0)
    per_b = lambda bi: (bi, 0, 0)
    return pl.pallas_call(
        functools.partial(_dsa_sample_kernel, n_heads, past, ts),
        grid=(nb,),
        in_specs=[pl.BlockSpec((ts, wb), row),
                  pl.BlockSpec((ts, madd.shape[1]), row),
                  pl.BlockSpec((None, past * n_heads, DSA_HEAD_DIM), per_b),
                  pl.BlockSpec((None, past * n_heads, DSA_HEAD_DIM), per_b),
                  pl.BlockSpec((ts * n_heads, DSA_HEAD_DIM), row),
                  pl.BlockSpec((ts * n_heads, DSA_HEAD_DIM), row),
                  _resident(bias_tab.shape), _resident(expand.shape)],
        out_specs=pl.BlockSpec((ts, wb), row),
        out_shape=jax.ShapeDtypeStruct((nb * ts, wb), MXU_DT),
        scratch_shapes=[pltpu.VMEM((n_keys * n_heads, DSA_HEAD_DIM), MXU_DT),
                        pltpu.VMEM((n_keys * n_heads, DSA_HEAD_DIM), MXU_DT),
                        pltpu.VMEM((n_heads * ts, pl.cdiv(n_keys, LANES) * LANES), jnp.float32),
                        pltpu.VMEM((n_heads * ts, n_keys * n_heads), jnp.float32)],
        compiler_params=_cparams(1),
        name="dsa_sample",
    )(qb, madd, cache_k.reshape(nb, past * n_heads, DSA_HEAD_DIM), cache_v.reshape(nb, past * n_heads, DSA_HEAD_DIM),
      new_k.reshape(nb * ts * n_heads, DSA_HEAD_DIM), new_v.reshape(nb * ts * n_heads, DSA_HEAD_DIM),
      bias_tab, expand)


def _out_kernel(oa_ref, ob_ref, ga_ref, gb_ref, x_ref, gate_ref, gna_ref, gnb_ref, w_ref, fg_ref, y_ref):
    def gated(o_ref, g_ref, gain_ref):
        o = o_ref[...].astype(jnp.float32)
        on = o * lax.rsqrt(jnp.mean(o * o, axis=-1, keepdims=True) + EPS) * gain_ref[...]
        return _mx(on * g_ref[...].astype(jnp.float32))

    wa = oa_ref.shape[1]
    out = _dot(gated(oa_ref, ga_ref, gna_ref), w_ref[0:wa, :]) + _dot(gated(ob_ref, gb_ref, gnb_ref), w_ref[wa:, :])
    xn = x_ref[...] + gate_ref[...] * out
    y_ref[...] = xn * lax.rsqrt(jnp.mean(xn * xn, axis=-1, keepdims=True) + EPS) * fg_ref[...]


def _out_proj(oa, ob, ga, gb, x3, gate, gain_a, gain_b, w_out, final_gain, tm):
    bv, tv, d = x3.shape
    nt = tv // tm
    wa = oa.shape[1]
    x_spec, g_spec = _row_specs(x3, gate.shape[1], tm)
    row = lambda b, i: (b * nt + i, 0)
    return pl.pallas_call(
        _out_kernel,
        grid=(bv, nt),
        in_specs=[pl.BlockSpec((tm, wa), row), pl.BlockSpec((tm, wa), row),
                  pl.BlockSpec((tm, wa), row), pl.BlockSpec((tm, wa), row),
                  x_spec, g_spec, _resident((1, wa)), _resident((1, wa)),
                  _resident(w_out.shape), _resident((1, d))],
        out_specs=pl.BlockSpec((None, tm, d), lambda b, i: (b, i, 0)),
        out_shape=jax.ShapeDtypeStruct((bv, tv, d), jnp.float32),
        compiler_params=_cparams(2),
        name="out_proj",
    )(oa, ob, ga, gb, x3, gate, gain_a.reshape(1, wa), gain_b.reshape(1, wa), w_out, final_gain.reshape(1, d))


def _pack_kernel(n_heads, wt_ref, qa_ref, ckv_ref, kr_ref, ga_ref, gb_ref, qb_ref, kb_ref, vb_ref, qi_ref, kiw_ref):
    wa = n_heads * MLA_V
    half = MLA_ROPE // 2
    q_head = MLA_NOPE + MLA_ROPE
    step = 2 * LANES

    def panel(ref, off):
        for c in range(0, ref.shape[1], step):
            w = min(step, ref.shape[1] - c)
            ref[:, c:c + w] = _mx(wt_ref[off + c:off + c + w, :].T)

    def dup_rope(off):
        x1, x2 = wt_ref[off:off + half, :], wt_ref[off + half:off + 2 * half, :]
        return _mx(jnp.concatenate([x1, x2, x2, x1], axis=0).T)

    for h in range(n_heads):
        qa_ref[:, h * LANES:(h + 1) * LANES] = _mx(wt_ref[h * q_head:h * q_head + MLA_NOPE, :].T)
    for h in range(0, n_heads, 2):
        r0 = wt_ref[h * q_head + MLA_NOPE:(h + 1) * q_head, :]
        r1 = wt_ref[(h + 1) * q_head + MLA_NOPE:(h + 2) * q_head, :]
        dst = n_heads * MLA_NOPE + (h // 2) * LANES
        qa_ref[:, dst:dst + LANES] = _mx(jnp.concatenate([r0, r1], axis=0).T)
    off = n_heads * q_head
    panel(ckv_ref, off)
    off += KV_LORA
    kr_ref[...] = dup_rope(off)
    off += MLA_ROPE
    for ref in (ga_ref, qb_ref, kb_ref, vb_ref):
        panel(ref, off)
        off += wa
    pairs = IDX_HEADS // 2
    for j in range(pairs):
        lo = wt_ref[off + IDX_DIM * j:off + IDX_DIM * (j + 1), :]
        hi = wt_ref[off + IDX_DIM * (j + pairs):off + IDX_DIM * (j + pairs + 1), :]
        qi_ref[:, LANES * j:LANES * (j + 1)] = _mx(jnp.concatenate([lo, hi], axis=0).T)
    off += IDX_HEADS * IDX_DIM
    n_kw = IDX_DIM + IDX_HEADS
    kiw = jnp.concatenate([wt_ref[off:off + n_kw, :], jnp.zeros((LANES - n_kw, wt_ref.shape[1]), jnp.float32)], axis=0)
    kiw_ref[...] = _mx(kiw.T)
    off += n_kw
    panel(gb_ref, off)


def _pack_w_in(w_in, n_heads):
    wt = w_in.T
    n, d = wt.shape
    wa = n_heads * MLA_V
    assert n == n_heads * (MLA_NOPE + MLA_ROPE) + KV_LORA + MLA_ROPE + 5 * wa + IDX_HEADS * IDX_DIM + IDX_DIM + IDX_HEADS
    assert 2 * IDX_DIM == LANES and 2 * MLA_ROPE == LANES and MLA_NOPE == LANES
    slab = 256
    assert n_heads % 4 == 0
    widths = [n_heads * (MLA_NOPE + MLA_ROPE), KV_LORA, LANES, wa, wa, wa, wa, wa, IDX_HEADS * IDX_DIM, LANES]
    q_a, ckv, kr, g_a, g_b, q_b, k_b, v_b, q_i, kiw = pl.pallas_call(
        functools.partial(_pack_kernel, n_heads),
        grid=(d // slab,),
        in_specs=[pl.BlockSpec((n, slab), lambda i: (0, i))],
        out_specs=[pl.BlockSpec((slab, w), lambda i: (i, 0)) for w in widths],
        out_shape=[jax.ShapeDtypeStruct((d, w), MXU_DT) for w in widths],
        compiler_params=_cparams(1),
        name="pack_w_in",
    )(wt)
    return [q_a, ckv, kr, g_a, g_b], [q_b, k_b, v_b, q_i, kiw]


def _rope_table(pos):
    half = MLA_ROPE // 2
    freqs = jnp.power(ROPE_THETA, -jnp.arange(half, dtype=jnp.float32) / half)
    ang = pos.astype(jnp.float32)[:, None] * freqs
    cos, sin = jnp.cos(ang), jnp.sin(ang)
    return jnp.concatenate([cos, cos, -sin, sin], axis=1)


def _rel_bucket(rel):
    nb = N_BUCKETS // 2
    max_exact = nb // 2
    n = jnp.abs(rel)
    nf = jnp.maximum(n, 1).astype(jnp.float32)
    large = max_exact + (jnp.log(nf / max_exact) / math.log(MAX_DISTANCE / max_exact)
                         * (nb - max_exact)).astype(jnp.int32)
    large = jnp.minimum(large, nb - 1)
    return jnp.where(rel > 0, nb, 0) + jnp.where(n < max_exact, n, large)


def _bucket_bias(rel_bias, rel):
    return (rel_bias * LOG2E)[_rel_bucket(rel)]


def kernel(x_prompt, x_sample, cache_mla_ckv, cache_mla_krope, cache_dsa_k, cache_dsa_v, cache_idx_k,
           c_prompt, c_sample, w_ada, b_ada, ln_gain, w_in, mla_kv_gain, w_uk, w_uv, rel_bias,
           out_gain_a, out_gain_b, w_out, final_gain):
    assert w_ada.shape[0] == 1, "single-layer step"
    b, t, d = x_prompt.shape
    nb, ts, _ = x_sample.shape
    past = cache_mla_ckv.shape[2]
    n_heads = d // 256
    wb = n_heads * DSA_HEAD_DIM
    s_all = past + ts
    assert t % MLA_TQ == 0 and t % DSA_TQ == 0 and MLA_TQ % CHUNK == 0 and DSA_TQ % CHUNK == 0
    assert (s_all - 1) // CHUNK <= past // CHUNK

    mod = _ada_mod(jnp.concatenate([c_prompt, c_sample], axis=0), w_ada[0], b_ada[0])
    shift, scale, gate = mod[:, :d], mod[:, d:2 * d], mod[:, 2 * d:]
    mod_p = [v[:b].reshape(b, 1, d) for v in (scale, shift, gate)]
    mod_s = [jnp.broadcast_to(v[b:, None, :], (nb, ts, d)).reshape(1, nb * ts, d) for v in (scale, shift, gate)]

    w_a, w_b = _pack_w_in(w_in[0], n_heads)
    wkv = _mx(jnp.concatenate([w_uk[0].transpose(2, 0, 1).reshape(KV_LORA, n_heads * MLA_NOPE),
                               w_uv[0].transpose(2, 0, 1).reshape(KV_LORA, n_heads * MLA_V)], axis=1))
    w_o = _mx(w_out[0])
    pos_p = jnp.arange(t, dtype=jnp.int32)
    pos_s = past + jnp.arange(ts, dtype=jnp.int32)
    tab_p = _rope_table(pos_p)
    tab_s = jnp.tile(_rope_table(pos_s), (nb, 1))

    tm = 256
    qcat, ckv_p, krope_p, ga, gb, kcat, vup, oa_buf = _proj_a(
        x_prompt, mod_p[0], mod_p[1], ln_gain[0], tab_p, w_a, mla_kv_gain[0], wkv, tm)
    qb, k_p, v_p, kbb, vbb, qi, idxk_p, kiab, wi, ob_buf = _proj_b(
        x_prompt, mod_p[0], mod_p[1], ln_gain[0], w_b, tm, True)
    o_a = _mla_prompt(qcat, kcat, vup, oa_buf, b, t)
    near = _bucket_bias(rel_bias, jnp.arange(-MAX_DISTANCE - 1, MAX_DISTANCE, dtype=jnp.int32))
    far = near[:1]
    bias_tab = jnp.concatenate([jnp.broadcast_to(far, (t - 1 - MAX_DISTANCE, n_heads)), near[1:],
                                jnp.broadcast_to(far, (t + 1 - MAX_DISTANCE, n_heads))], axis=0).T
    o_b = _dsa_prompt(qb, qi, wi, kbb, vbb, kiab, bias_tab, ob_buf, b, t, min(TOPK_MAX, t // 4))
    y_prompt = _out_proj(o_a, o_b, ga, gb, x_prompt, mod_p[2], out_gain_a[0], out_gain_b[0], w_o, final_gain, 2 * tm)

    xs3 = x_sample.reshape(1, nb * ts, d)
    tms = min(256, nb * ts)
    qcat_s, ckv_s, krope_s, ga_s, gb_s = _proj_a(
        xs3, mod_s[0], mod_s[1], ln_gain[0], tab_s, w_a, mla_kv_gain[0], None, tms)
    qb_s, k_s, v_s, kbb_s, vbb_s, qi_s, idxk_s, kiab_s, wi_s = _proj_b(
        xs3, mod_s[0], mod_s[1], ln_gain[0], w_b, tms, False)
    oa_s = _mla_sample(qcat_s, cache_mla_ckv[0], jnp.swapaxes(cache_mla_krope[0], 1, 2), ckv_s, krope_s,
                       _mx(w_uk[0]), _mx(w_uv[0]), ts)
    s_pad = pl.cdiv(s_all, LANES) * LANES
    tab_s = _bucket_bias(rel_bias, jnp.arange(s_pad + LANES, dtype=jnp.int32) - (s_all - 1)).T
    madd_s = _dsa_select(qi_s, wi_s, jnp.swapaxes(cache_idx_k[0], 1, 2), kiab_s, ts, min(TOPK_MAX, s_all // 4))
    ob_s = _dsa_sample(qb_s, madd_s, cache_dsa_k[0], cache_dsa_v[0], kbb_s, vbb_s, tab_s, ts)
    y_sample = _out_proj(oa_s, ob_s, ga_s, gb_s, xs3, mod_s[2], out_gain_a[0], out_gain_b[0], w_o, final_gain, tms)

    hd = (n_heads, DSA_HEAD_DIM)
    return (y_prompt, y_sample.reshape(nb, ts, d),
            ckv_p.reshape(1, b, t, KV_LORA), krope_p.reshape(1, b, t, MLA_ROPE),
            k_p.reshape(1, b, t, *hd), v_p.reshape(1, b, t, *hd), idxk_p.reshape(1, b, t, IDX_DIM),
            ckv_s.reshape(1, nb, ts, KV_LORA), krope_s.reshape(1, nb, ts, MLA_ROPE),
            k_s.reshape(1, nb, ts, *hd), v_s.reshape(1, nb, ts, *hd), idxk_s.reshape(1, nb, ts, IDX_DIM))
```

```python
import functools
import math

import jax
import jax.numpy as jnp
from jax import lax
from jax.experimental import pallas as pl
from jax.experimental.pallas import tpu as pltpu

MXU_DT = jnp.bfloat16

CHUNK = 64
MLA_NOPE = 128
MLA_ROPE = 64
MLA_V = 128
KV_LORA = 512
DSA_HEAD_DIM = 128
IDX_HEADS = 16
IDX_DIM = 64
TOPK_MAX = 256
N_BUCKETS = 32
MAX_DISTANCE = 128
ROPE_THETA = 10000.0
EPS = 1e-6

LANES = 128
HEAD_SLOT = 256
NEG_BIG = -1e30
LOG2E = math.log2(math.e)
INT_MIN = -2 ** 31
KEY_NEG_INF = INT_MIN + 0x7FFFFF
VMEM_LIMIT = 56 * 1024 * 1024


def _cparams(n_grid, vmem=VMEM_LIMIT):
    return pltpu.CompilerParams(dimension_semantics=("arbitrary",) * n_grid, vmem_limit_bytes=vmem)


def _mx(v):
    return v.astype(MXU_DT)


def _dot(a, b):
    return jnp.dot(a, b, preferred_element_type=jnp.float32)


def _dot_nt(a, b):
    return lax.dot_general(a, b, (((1,), (1,)), ((), ())), preferred_element_type=jnp.float32)


def _silu(v):
    return v * (1.0 / (1.0 + jnp.exp(-v)))


def _resident(shape):
    nd = len(shape)
    return pl.BlockSpec(shape, lambda *_: (0,) * nd, pipeline_mode=pl.Buffered(1))


def _ada_kernel(c_ref, w_ref, b_ref, o_ref):
    a = _mx(_silu(c_ref[...]))
    o_ref[...] = _dot(a, _mx(w_ref[...])) + b_ref[...]


def _ada_mod(c_all, w_ada, b_ada):
    m, d = c_all.shape
    n = w_ada.shape[1]
    tn = 1024
    return pl.pallas_call(
        _ada_kernel,
        grid=(n // tn,),
        in_specs=[pl.BlockSpec((m, d), lambda j: (0, 0)),
                  pl.BlockSpec((d, tn), lambda j: (0, j)),
                  pl.BlockSpec((1, tn), lambda j: (0, j))],
        out_specs=pl.BlockSpec((m, tn), lambda j: (0, j)),
        out_shape=jax.ShapeDtypeStruct((m, n), jnp.float32),
        compiler_params=_cparams(1),
        name="ada_mod",
    )(c_all, w_ada, b_ada.reshape(1, n))


def _modulated_norm(x_ref, scale_ref, shift_ref, lng_ref):
    x = x_ref[...]
    xn = x * lax.rsqrt(jnp.mean(x * x, axis=-1, keepdims=True) + EPS) * lng_ref[...]
    return _mx(xn * (1.0 + scale_ref[...]) + shift_ref[...])


def _rope128(a, tab):
    t = a * tab
    return t + pltpu.roll(t, 64, 1)


def _proj_a_kernel(n_heads, emit_kv, x_ref, scale_ref, shift_ref, lng_ref, tab_ref,
                   wq_ref, wc_ref, wr_ref, wga_ref, wgb_ref, kvg_ref, *rest):
    if emit_kv:
        wkv_ref, qcat_ref, ckv_ref, krope_ref, ga_ref, gb_ref, kcat_ref, vup_ref, attn_ref = rest
        attn_ref[...] = jnp.zeros_like(attn_ref)
    else:
        qcat_ref, ckv_ref, krope_ref, ga_ref, gb_ref = rest
    hb = _modulated_norm(x_ref, scale_ref, shift_ref, lng_ref)
    tab = tab_ref[...]
    qscale = (MLA_NOPE + MLA_ROPE) ** -0.5 * LOG2E
    for h in range(0, n_heads, 2):
        a = _dot(hb, wq_ref[:, h * LANES:(h + 2) * LANES])
        qcat_ref[:, h * HEAD_SLOT:h * HEAD_SLOT + LANES] = _mx(a[:, :LANES] * qscale)
        qcat_ref[:, (h + 1) * HEAD_SLOT:(h + 1) * HEAD_SLOT + LANES] = _mx(a[:, LANES:] * qscale)
    lane = lax.broadcasted_iota(jnp.int32, tab.shape, 1)
    tab_r = pltpu.roll(tab, MLA_ROPE, 1)
    cos4 = jnp.where(lane < MLA_ROPE, tab, tab_r)
    sin4 = jnp.where(lane < MLA_ROPE, tab_r, tab)
    is_x1 = lane % MLA_ROPE < MLA_ROPE // 2
    n_nope = n_heads * MLA_NOPE
    for h in range(0, n_heads, 4):
        a = _dot(hb, wq_ref[:, n_nope + (h // 2) * LANES:n_nope + (h // 2 + 2) * LANES])
        for u in range(2):
            ap = a[:, u * LANES:(u + 1) * LANES]
            swapped = jnp.where(is_x1, pltpu.roll(ap, LANES - MLA_ROPE // 2, 1), pltpu.roll(ap, MLA_ROPE // 2, 1))
            roped = (ap * cos4 + swapped * sin4) * qscale
            c0 = (h + 2 * u) * HEAD_SLOT
            qcat_ref[:, c0 + LANES:c0 + HEAD_SLOT] = _mx(roped)
            qcat_ref[:, c0 + HEAD_SLOT + LANES:c0 + 2 * HEAD_SLOT] = _mx(pltpu.roll(roped, MLA_ROPE, 1))
    c = _dot(hb, wc_ref[...])
    cn = c * lax.rsqrt(jnp.mean(c * c, axis=-1, keepdims=True) + EPS) * kvg_ref[...]
    ckv_ref[...] = cn
    r = _rope128(_dot(hb, wr_ref[...]), tab)
    krope_ref[...] = r[:, :MLA_ROPE]
    ga_ref[...] = _mx(_silu(_dot(hb, wga_ref[...])))
    gb_ref[...] = _mx(_silu(_dot(hb, wgb_ref[...])))
    if emit_kv:
        cb = _mx(cn)
        lane = lax.broadcasted_iota(jnp.int32, r.shape, 1)
        krz = _mx(jnp.where(lane < MLA_ROPE, r, 0.0))
        kn = _dot(cb, wkv_ref[:, :n_heads * MLA_NOPE])
        for h in range(n_heads):
            c0 = h * HEAD_SLOT
            kcat_ref[:, c0:c0 + LANES] = _mx(kn[:, h * MLA_NOPE:(h + 1) * MLA_NOPE])
            kcat_ref[:, c0 + LANES:c0 + HEAD_SLOT] = krz
        vup_ref[...] = _mx(_dot(cb, wkv_ref[:, n_heads * MLA_NOPE:]))


def _proj_b_kernel(n_heads, emit_attn, x_ref, scale_ref, shift_ref, lng_ref, wq_ref, wk_ref, wv_ref, wqi_ref,
                   wkw_ref, qb_ref, kb_hbm, vb_hbm, kbb_ref, vbb_ref, qi_ref, ki_ref, kiab_ref, wi_ref, *rest):
    if emit_attn:
        rest[0][...] = jnp.zeros_like(rest[0])
    rows_ref, sem = rest[-2:]
    tm = x_ref.shape[0]
    n_steps = pl.num_programs(0) * pl.num_programs(1)
    step_id = pl.program_id(0) * pl.num_programs(1) + pl.program_id(1)
    slot = step_id % 2

    def row_copies(slot_, step_, which):
        dst = (kb_hbm, vb_hbm)[which]
        return [pltpu.make_async_copy(rows_ref.at[slot_, which, :, h * DSA_HEAD_DIM:(h + 1) * DSA_HEAD_DIM],
                                      dst.at[pl.ds(step_ * tm, tm), h, :], sem.at[slot_, which])
                for h in range(n_heads)]

    def wait_slot(slot_, step_):
        for which in range(2):
            for cp in row_copies(slot_, step_, which):
                cp.wait()

    @pl.when(step_id >= 2)
    def _():
        wait_slot(slot, step_id - 2)

    hb = _modulated_norm(x_ref, scale_ref, shift_ref, lng_ref)
    dscale = DSA_HEAD_DIM ** -0.5 * LOG2E
    step = 512
    for c0 in range(0, wq_ref.shape[1], step):
        qb_ref[:, c0:c0 + step] = _mx(_dot(hb, wq_ref[:, c0:c0 + step]) * dscale)
    for which, (w_ref, b_ref) in enumerate(((wk_ref, kbb_ref), (wv_ref, vbb_ref))):
        for c0 in range(0, w_ref.shape[1], step):
            kv = _dot(hb, w_ref[:, c0:c0 + step])
            rows_ref[slot, which, :, c0:c0 + step] = kv
            b_ref[:, c0:c0 + step] = _mx(kv)
        for cp in row_copies(slot, step_id, which):
            cp.start()

    @pl.when(step_id == n_steps - 1)
    def _():
        wait_slot(slot, step_id)

        @pl.when(n_steps >= 2)
        def _():
            wait_slot(1 - slot, step_id - 1)
    for c0 in range(0, wqi_ref.shape[1], step):
        qi_ref[:, c0:c0 + step] = _mx(_dot(hb, wqi_ref[:, c0:c0 + step]))
    a = _dot(hb, wkw_ref[...])
    ki_ref[...] = a[:, :IDX_DIM]
    lane = lax.broadcasted_iota(jnp.int32, a.shape, 1)
    kz = jnp.where(lane < IDX_DIM, a, 0.0)
    kiab_ref[:, :LANES] = _mx(kz)
    kiab_ref[:, LANES:] = _mx(pltpu.roll(kz, IDX_DIM, 1))
    wi_ref[...] = a * (IDX_HEADS ** -0.5 * IDX_DIM ** -0.5)


def _row_specs(x3, mod_rows, tm):
    bv, tv, d = x3.shape
    x_spec = pl.BlockSpec((None, tm, d), lambda b, i: (b, i, 0))
    if mod_rows == 1:
        m_spec = pl.BlockSpec((None, 1, d), lambda b, i: (b, 0, 0))
    else:
        m_spec = pl.BlockSpec((None, tm, d), lambda b, i: (b, i, 0))
    return x_spec, m_spec


def _out2d(m, width, dtype, tm, nt):
    return (jax.ShapeDtypeStruct((m, width), dtype),
            pl.BlockSpec((tm, width), lambda b, i: (b * nt + i, 0)))


def _proj_a(x3, scale, shift, ln_gain, tab, ws_a, kv_gain, wkv, tm):
    bv, tv, d = x3.shape
    nt = tv // tm
    m = bv * tv
    n_heads = d // 256
    wa = n_heads * MLA_V
    emit_kv = wkv is not None
    x_spec, m_spec = _row_specs(x3, scale.shape[1], tm)
    in_specs = ([x_spec, m_spec, m_spec, _resident((1, d)), pl.BlockSpec((tm, LANES), lambda b, i: (i, 0))]
                + [_resident(w.shape) for w in ws_a] + [_resident((1, KV_LORA))])
    args = [x3, scale, shift, ln_gain.reshape(1, d), tab, *ws_a, kv_gain.reshape(1, KV_LORA)]
    outs = [_out2d(m, n_heads * HEAD_SLOT, MXU_DT, tm, nt),
            _out2d(m, KV_LORA, jnp.float32, tm, nt),
            _out2d(m, MLA_ROPE, jnp.float32, tm, nt),
            _out2d(m, wa, MXU_DT, tm, nt),
            _out2d(m, wa, MXU_DT, tm, nt)]
    if emit_kv:
        in_specs.append(_resident(wkv.shape))
        args.append(wkv)
        outs += [_out2d(m, n_heads * HEAD_SLOT, MXU_DT, tm, nt),
                 _out2d(m, wa, MXU_DT, tm, nt),
                 _out2d(m, wa, MXU_DT, tm, nt)]
    return pl.pallas_call(
        functools.partial(_proj_a_kernel, n_heads, emit_kv),
        grid=(bv, nt),
        in_specs=in_specs,
        out_specs=[o[1] for o in outs],
        out_shape=[o[0] for o in outs],
        compiler_params=_cparams(2),
        name="proj_a",
    )(*args)


def _proj_b(x3, scale, shift, ln_gain, ws_b, tm, emit_attn_buffer):
    bv, tv, d = x3.shape
    nt = tv // tm
    m = bv * tv
    n_heads = d // 256
    width_b = n_heads * DSA_HEAD_DIM
    x_spec, m_spec = _row_specs(x3, scale.shape[1], tm)
    cache_rows = (jax.ShapeDtypeStruct((m, n_heads, DSA_HEAD_DIM), jnp.float32), pl.BlockSpec(memory_space=pl.ANY))
    outs = [_out2d(m, width_b, MXU_DT, tm, nt),
            cache_rows,
            cache_rows,
            _out2d(m, width_b, MXU_DT, tm, nt),
            _out2d(m, width_b, MXU_DT, tm, nt),
            _out2d(m, IDX_HEADS * IDX_DIM, MXU_DT, tm, nt),
            _out2d(m, IDX_DIM, jnp.float32, tm, nt),
            _out2d(m, 2 * LANES, MXU_DT, tm, nt),
            _out2d(m, LANES, jnp.float32, tm, nt)]
    if emit_attn_buffer:
        outs.append(_out2d(m, width_b, MXU_DT, tm, nt))
    return pl.pallas_call(
        functools.partial(_proj_b_kernel, n_heads, emit_attn_buffer),
        grid=(bv, nt),
        in_specs=[x_spec, m_spec, m_spec, _resident((1, d))] + [_resident(w.shape) for w in ws_b],
        out_specs=[o[1] for o in outs],
        out_shape=[o[0] for o in outs],
        scratch_shapes=[pltpu.VMEM((2, 2, tm, width_b), jnp.float32), pltpu.SemaphoreType.DMA((2, 2))],
        compiler_params=_cparams(2),
        name="proj_b",
    )(x3, scale, shift, ln_gain.reshape(1, d), *ws_b)


def _softmax_pv(s, v):
    m = jnp.max(s, axis=-1, keepdims=True)
    p = jnp.exp2(s - m)
    l = jnp.sum(p, axis=-1, keepdims=True)
    return _dot(_mx(p), v) * (1.0 / l)


def _topk_mask(score, adm, topk, sc_ref, madd_ref):
    rows, n = score.shape
    if n <= topk:
        madd_ref[...] = jnp.where(adm, 0.0, NEG_BIG)
        return
    sc_ref[...] = jnp.where(adm, score, -jnp.inf)
    kf = float(topk)

    def count(pred):
        return jnp.sum(jnp.where(pred, 1.0, 0.0), axis=-1, keepdims=True)

    def key_to_float(key):
        return pltpu.bitcast(key ^ ((key >> 31) & 0x7FFFFFFF), jnp.float32)

    def bit_step(i, thr):
        inc = lax.shift_left(jnp.int32(1), 31 - i)
        cand = thr + inc
        ok = count(sc_ref[...] >= key_to_float(cand)) >= kf
        return jnp.where(ok, cand, thr)

    thr = lax.fori_loop(0, 32, bit_step, jnp.full((rows, 1), INT_MIN, jnp.int32))
    thr = key_to_float(jnp.maximum(thr, KEY_NEG_INF))
    ge = sc_ref[...] >= thr
    madd_ref[...] = jnp.where(ge, 0.0, NEG_BIG)

    @pl.when(jnp.max(count(ge)) > kf)
    def _():
        sc = sc_ref[...]
        gt = sc > thr
        need = kf - count(gt)
        eqf = jnp.where(sc == thr, 1.0, 0.0)
        col = lax.broadcasted_iota(jnp.int32, (rows, n), 1)
        nbits = max(1, int(n).bit_length())

        def col_step(i, bound):
            cand = bound + lax.shift_left(jnp.int32(1), nbits - 1 - i)
            taken = jnp.sum(jnp.where(col < cand, eqf, 0.0), axis=-1, keepdims=True)
            return jnp.where(taken <= need, cand, bound)

        bound = lax.fori_loop(0, nbits, col_step, jnp.zeros((rows, 1), jnp.int32))
        tie_madd = jnp.where(jnp.where(col < bound, eqf, 0.0) > 0.5, 0.0, NEG_BIG)
        madd_ref[...] = jnp.where(adm, jnp.where(gt, 0.0, tie_madd), NEG_BIG)


def _indexer_scores(qi_ref, wi, kia, kib):
    half = IDX_HEADS // 2
    score = None
    for j in range(half):
        qp = qi_ref[:, j * LANES:(j + 1) * LANES]
        da = jnp.maximum(_dot_nt(qp, kia), 0.0) * wi[:, IDX_DIM + j:IDX_DIM + j + 1]
        db = jnp.maximum(_dot_nt(qp, kib), 0.0) * wi[:, IDX_DIM + half + j:IDX_DIM + half + j + 1]
        score = da + db if score is None else score + da + db
    return score


def _indexer_scores_stacked(qi_ref, wi, dots_a, dots_b):
    half = IDX_HEADS // 2
    ts = qi_ref.shape[0]
    q_all = jnp.concatenate([qi_ref[:, j * LANES:(j + 1) * LANES] for j in range(half)], axis=0)
    wa = jnp.concatenate([wi[:, IDX_DIM + j:IDX_DIM + j + 1] for j in range(half)], axis=0)
    wb = jnp.concatenate([wi[:, IDX_DIM + half + j:IDX_DIM + half + j + 1] for j in range(half)], axis=0)
    part = jnp.maximum(dots_a(q_all), 0.0) * wa + jnp.maximum(dots_b(q_all), 0.0) * wb
    score = part[0:ts]
    for j in range(1, half):
        score = score + part[j * ts:(j + 1) * ts]
    return score


MLA_TQ = 512
DSA_TQ = 256
DSA_STEP_MAX_KEYS = 2048


def _chunk_madd(row0, tq, n_keys):
    qc = (row0 + lax.broadcasted_iota(jnp.int32, (tq, n_keys), 0)) // CHUNK
    kc = lax.broadcasted_iota(jnp.int32, (tq, n_keys), 1) // CHUNK
    return kc <= qc


def _chained_tile_calls(n_tiles, make_call, out):
    for c in range(n_tiles):
        out = make_call(c, out)
    return out


def _mla_prompt_kernel(n_heads, row0, q_ref, k_ref, v_ref, prev_ref, o_ref):
    del prev_ref
    tq, n_keys = q_ref.shape[0], k_ref.shape[0]
    madd = jnp.where(_chunk_madd(row0, tq, n_keys), 0.0, NEG_BIG)
    for h in range(n_heads):
        c0 = h * HEAD_SLOT
        s = _dot_nt(q_ref[:, c0:c0 + HEAD_SLOT], k_ref[:, c0:c0 + HEAD_SLOT]) + madd
        o_ref[:, h * MLA_V:(h + 1) * MLA_V] = _mx(_softmax_pv(s, v_ref[:, h * MLA_V:(h + 1) * MLA_V]))


def _mla_prompt(qcat, kcat, vup, out_init, b, t):
    tq = MLA_TQ
    n_heads = qcat.shape[1] // HEAD_SLOT
    nq = t // tq
    wa = n_heads * MLA_V
    k3 = kcat.reshape(b, t, kcat.shape[1])
    v3 = vup.reshape(b, t, wa)

    def make_call(c, out):
        n_keys = (c + 1) * tq
        row = lambda bi: (bi * nq + c, 0)
        return pl.pallas_call(
            functools.partial(_mla_prompt_kernel, n_heads, c * tq),
            grid=(b,),
            in_specs=[pl.BlockSpec((tq, qcat.shape[1]), row),
                      pl.BlockSpec((None, n_keys, kcat.shape[1]), lambda bi: (bi, 0, 0)),
                      pl.BlockSpec((None, n_keys, wa), lambda bi: (bi, 0, 0)),
                      pl.BlockSpec(memory_space=pl.ANY)],
            out_specs=pl.BlockSpec((tq, wa), row),
            out_shape=jax.ShapeDtypeStruct(out.shape, out.dtype),
            input_output_aliases={3: 0},
            compiler_params=_cparams(1),
            name="mla_prompt_%d" % c,
        )(qcat, k3, v3, out)

    return _chained_tile_calls(nq, make_call, out_init)


def _toeplitz_bias(tab_ref, h, start, n_rows, n_cols):
    shift0 = LANES - (n_rows - 1)
    from_left = (lax.broadcasted_iota(jnp.int32, (n_rows, LANES), 1)
                 < shift0 + lax.broadcasted_iota(jnp.int32, (n_rows, LANES), 0))
    pieces = []
    prev = None
    for n in range(n_cols // LANES + 1):
        seg = tab_ref[h:h + 1, start + n * LANES:start + (n + 1) * LANES]
        rot = pltpu.roll(jnp.broadcast_to(seg, (n_rows, LANES)), shift0 % LANES, 1, stride=1, stride_axis=0)
        if prev is not None:
            pieces.append(jnp.where(from_left, prev, rot))
        prev = rot
    return jnp.concatenate(pieces, axis=1)


def _dsa_prompt_kernel(n_heads, row0, t, topk, qb_ref, qi_ref, wi_ref, k_ref, v_ref, kiab_ref, tab_ref, prev_ref,
                       o_ref, key_ref, madd_ref, near_ref):
    del prev_ref
    per, tq, _ = qb_ref.shape
    n_keys = k_ref.shape[1]
    score = jnp.concatenate(
        [_indexer_scores(qi_ref.at[u], wi_ref[u], kiab_ref[u, :, :LANES], kiab_ref[u, :, LANES:])
         for u in range(per)], axis=0)
    adm = _chunk_madd(row0, tq, n_keys)
    _topk_mask(score, jnp.concatenate([adm] * per, axis=0), topk, key_ref, madd_ref)
    near0 = max(0, row0 - LANES)
    n_near = n_keys - near0

    @pl.when(pl.program_id(0) == 0)
    def _():
        for h in range(n_heads):
            near_ref[h] = jnp.concatenate(
                [_toeplitz_bias(tab_ref, h, near0 - (row0 + u * LANES) - LANES + t, LANES, n_near)
                 for u in range(tq // LANES)], axis=0)

    for h in range(n_heads):
        c0 = h * DSA_HEAD_DIM
        if near0 > 0:
            far = jnp.broadcast_to(tab_ref[h:h + 1, 0:1], (tq, near0))
            bias = jnp.concatenate([far, near_ref[h]], axis=1)
        else:
            bias = near_ref[h]
        for u in range(per):
            s = (_dot_nt(qb_ref[u, :, c0:c0 + DSA_HEAD_DIM], k_ref[u, :, c0:c0 + DSA_HEAD_DIM]) + bias
                 + madd_ref[u * tq:(u + 1) * tq, :])
            o_ref[u, :, c0:c0 + DSA_HEAD_DIM] = _mx(_softmax_pv(s, v_ref[u, :, c0:c0 + DSA_HEAD_DIM]))


def _dsa_prompt(qb, qi, wi, kbb, vbb, kiab, bias_tab, out_init, b, t, topk):
    tq = DSA_TQ
    assert MAX_DISTANCE <= LANES and tq % LANES == 0
    n_heads = qb.shape[1] // DSA_HEAD_DIM
    nq = t // tq
    wb = qb.shape[1]
    k3, v3, ki3 = kbb.reshape(b, t, wb), vbb.reshape(b, t, wb), kiab.reshape(b, t, 2 * LANES)
    qb3, qi3, wi3 = qb.reshape(b, t, wb), qi.reshape(b, t, qi.shape[1]), wi.reshape(b, t, LANES)
    keys_of = lambda bi: (bi, 0, 0)

    def make_call(c, out):
        n_keys = (c + 1) * tq
        per = max(p for p in (1, 2, 4) if p == 1 or (b % p == 0 and p * n_keys <= DSA_STEP_MAX_KEYS))
        rows_of = lambda bi: (bi, c, 0)
        return pl.pallas_call(
            functools.partial(_dsa_prompt_kernel, n_heads, c * tq, t, topk),
            grid=(b // per,),
            in_specs=[pl.BlockSpec((per, tq, wb), rows_of),
                      pl.BlockSpec((per, tq, qi.shape[1]), rows_of),
                      pl.BlockSpec((per, tq, LANES), rows_of),
                      pl.BlockSpec((per, n_keys, wb), keys_of),
                      pl.BlockSpec((per, n_keys, wb), keys_of),
                      pl.BlockSpec((per, n_keys, 2 * LANES), keys_of),
                      _resident(bias_tab.shape),
                      pl.BlockSpec(memory_space=pl.ANY)],
            out_specs=pl.BlockSpec((per, tq, wb), rows_of),
            out_shape=jax.ShapeDtypeStruct(out.shape, out.dtype),
            input_output_aliases={7: 0},
            scratch_shapes=[pltpu.VMEM((per * tq, n_keys), jnp.float32),
                            pltpu.VMEM((per * tq, n_keys), jnp.float32),
                            pltpu.VMEM((n_heads, tq, n_keys - max(0, c * tq - LANES)), jnp.float32)],
            compiler_params=_cparams(1),
            name="dsa_prompt_%d" % c,
        )(qb3, qi3, wi3, k3, v3, ki3, bias_tab, out)

    return _chained_tile_calls(nq, make_call, out_init.reshape(b, t, wb)).reshape(b * t, wb)


def _mla_sample_kernel(n_heads, past, ts, qcat_ref, cckv_ref, ckrt_ref, nckv_ref, nkr_ref, wuk_ref, wuv_ref,
                       o_ref, kall_ref, rt_ref, rnew_ref):
    n_keys = past + ts

    @pl.when(pl.program_id(0) == 0)
    def _():
        kall_ref[...] = jnp.zeros_like(kall_ref)
        rt_ref[...] = jnp.zeros_like(rt_ref)
        rnew_ref[...] = jnp.zeros_like(rnew_ref)

    for u in range(cckv_ref.shape[0]):
        rows = slice(u * ts, (u + 1) * ts)
        kall_ref[u, 0:past, :] = _mx(cckv_ref[u])
        kall_ref[u, past:n_keys, :] = _mx(nckv_ref[rows, :])
        rt_ref[u, 0:MLA_ROPE, :] = _mx(ckrt_ref[u])
        rnew_ref[u, 0:ts, 0:MLA_ROPE] = _mx(nkr_ref[rows, :])
        qlat, qrope = [], []
        for h in range(n_heads):
            c0 = h * HEAD_SLOT
            qlat.append(_mx(_dot(qcat_ref[rows, c0:c0 + LANES], wuk_ref[h])))
            qrope.append(qcat_ref[rows, c0 + LANES:c0 + HEAD_SLOT])
        qlat = jnp.concatenate(qlat, axis=0)
        qrope = jnp.concatenate(qrope, axis=0)
        s = _dot_nt(qlat, kall_ref[u]) + jnp.concatenate(
            [_dot(qrope, rt_ref[u]), _dot_nt(qrope, rnew_ref[u])], axis=1)
        col = lax.broadcasted_iota(jnp.int32, s.shape, 1)
        s = jnp.where(col < n_keys, s, NEG_BIG)
        olat = _mx(_softmax_pv(s, kall_ref[u]))
        for h in range(n_heads):
            o_ref[rows, h * MLA_V:(h + 1) * MLA_V] = _mx(_dot_nt(olat[h * ts:(h + 1) * ts], wuv_ref[h]))


def _mla_sample(qcat, cache_ckv, cache_kr_t, new_ckv, new_kr, wuk, wuv, ts):
    nb, past, c = cache_ckv.shape
    assert past % LANES == 0
    n_heads = qcat.shape[1] // HEAD_SLOT
    wa = n_heads * MLA_V
    s_pad = pl.cdiv(past + ts, LANES) * LANES
    per = max(p for p in (1, 2, 4) if nb % p == 0)
    row = lambda bi: (bi, 0)
    return pl.pallas_call(
        functools.partial(_mla_sample_kernel, n_heads, past, ts),
        grid=(nb // per,),
        in_specs=[pl.BlockSpec((per * ts, qcat.shape[1]), row),
                  pl.BlockSpec((per, past, c), lambda bi: (bi, 0, 0)),
                  pl.BlockSpec((per, MLA_ROPE, past), lambda bi: (bi, 0, 0)),
                  pl.BlockSpec((per * ts, c), row),
                  pl.BlockSpec((per * ts, MLA_ROPE), row),
                  _resident(wuk.shape), _resident(wuv.shape)],
        out_specs=pl.BlockSpec((per * ts, wa), row),
        out_shape=jax.ShapeDtypeStruct((nb * ts, wa), MXU_DT),
        scratch_shapes=[pltpu.VMEM((per, s_pad, c), MXU_DT), pltpu.VMEM((per, LANES, past), MXU_DT),
                        pltpu.VMEM((per, s_pad - past, LANES), MXU_DT)],
        compiler_params=_cparams(1),
        name="mla_sample",
    )(qcat, cache_ckv, cache_kr_t, new_ckv, new_kr, wuk, wuv)


def _dsa_select_kernel(past, ts, topk, qi_ref, wi_ref, ckit_ref, nkiab_ref, madd_ref,
                       kat_ref, kbt_ref, anew_ref, bnew_ref, key_ref):
    bi = pl.program_id(0)
    n_keys = past + ts
    rows, s_pad = madd_ref.shape

    @pl.when(bi == 0)
    def _():
        for ref in (kat_ref, kbt_ref, anew_ref, bnew_ref):
            ref[...] = jnp.zeros_like(ref)

    ckit = _mx(ckit_ref[...])
    kat_ref[0:IDX_DIM, :] = ckit
    kbt_ref[IDX_DIM:LANES, :] = ckit
    anew_ref[0:ts, :] = nkiab_ref[:, :LANES]
    bnew_ref[0:ts, :] = nkiab_ref[:, LANES:]
    madd_ref[pl.ds(pl.multiple_of(bi * ts, ts), ts), :] = _indexer_scores_stacked(
        qi_ref, wi_ref[...],
        lambda q: jnp.concatenate([_dot(q, kat_ref[...]), _dot_nt(q, anew_ref[...])], axis=1),
        lambda q: jnp.concatenate([_dot(q, kbt_ref[...]), _dot_nt(q, bnew_ref[...])], axis=1))

    @pl.when(bi == pl.num_programs(0) - 1)
    def _():
        col = lax.broadcasted_iota(jnp.int32, (rows, s_pad), 1)
        _topk_mask(madd_ref[...], col < n_keys, topk, key_ref, madd_ref)


def _dsa_select(qi, wi, cache_ki_t, new_kiab, ts, topk):
    nb, _, past = cache_ki_t.shape
    assert past % LANES == 0
    s_pad = pl.cdiv(past + ts, LANES) * LANES
    row = lambda bi: (bi, 0)
    return pl.pallas_call(
        functools.partial(_dsa_select_kernel, past, ts, topk),
        grid=(nb,),
        in_specs=[pl.BlockSpec((ts, qi.shape[1]), row),
                  pl.BlockSpec((ts, LANES), row),
                  pl.BlockSpec((None, IDX_DIM, past), lambda bi: (bi, 0, 0)),
                  pl.BlockSpec((ts, 2 * LANES), row)],
        out_specs=pl.BlockSpec((nb * ts, s_pad), lambda bi: (0, 0)),
        out_shape=jax.ShapeDtypeStruct((nb * ts, s_pad), jnp.float32),
        scratch_shapes=[pltpu.VMEM((LANES, past), MXU_DT), pltpu.VMEM((LANES, past), MXU_DT),
                        pltpu.VMEM((s_pad - past, LANES), MXU_DT), pltpu.VMEM((s_pad - past, LANES), MXU_DT),
                        pltpu.VMEM((nb * ts, s_pad), jnp.float32)],
        compiler_params=_cparams(1),
        name="dsa_select",
    )(qi, wi, cache_ki_t, new_kiab)


def _dsa_sample_kernel(n_heads, past, ts, qb_ref, madd_ref, ck_ref, cv_ref, nk_ref, nv_ref, tab_ref, expand_ref,
                       o_ref, kflat_ref, vflat_ref, bias_ref, biasw_ref):
    n_keys = past + ts
    wide = n_heads * LANES
    n_blocks = pl.cdiv(n_keys, LANES)
    widths = [min(wide, (n_keys - j * LANES) * n_heads) for j in range(n_blocks)]

    @pl.when(pl.program_id(0) == 0)
    def _():
        for h in range(n_heads):
            bias_ref[h * ts:(h + 1) * ts, :] = _toeplitz_bias(tab_ref, h, 0, ts, n_blocks * LANES)
        shape = (n_heads * ts, wide)
        same_head = (lax.broadcasted_iota(jnp.int32, shape, 0) // ts
                     == lax.broadcasted_iota(jnp.int32, shape, 1) % n_heads)
        for j in range(n_blocks):
            b = bias_ref[:, j * LANES:(j + 1) * LANES]
            hi = _mx(b)
            rest = b - hi.astype(jnp.float32)
            mid = _mx(rest)
            lo = _mx(rest - mid.astype(jnp.float32))
            piece = _dot(hi, expand_ref[...]) + _dot(mid, expand_ref[...]) + _dot(lo, expand_ref[...])
            biasw_ref[:, j * wide:j * wide + widths[j]] = jnp.where(same_head, piece, NEG_BIG)[:, :widths[j]]

    kflat_ref[0:past * n_heads, :] = _mx(ck_ref[...])
    kflat_ref[past * n_heads:, :] = nk_ref[...]
    vflat_ref[0:past * n_heads, :] = _mx(cv_ref[...])
    vflat_ref[past * n_heads:, :] = nv_ref[...]
    sel = _mx(jnp.where(madd_ref[...] == 0.0, 1.0, 0.0))
    pieces = [_dot(sel[:, j * LANES:(j + 1) * LANES], expand_ref[...])[:, :widths[j]] for j in range(n_blocks)]
    sel_wide = jnp.concatenate(pieces, axis=1)
    sel_wide = jnp.concatenate([sel_wide] * n_heads, axis=0)
    q_all = jnp.concatenate([qb_ref[:, h * DSA_HEAD_DIM:(h + 1) * DSA_HEAD_DIM] for h in range(n_heads)], axis=0)
    s = _dot_nt(q_all, kflat_ref[...]) + biasw_ref[...] + jnp.where(sel_wide > 0.5, 0.0, NEG_BIG)
    o = _mx(_softmax_pv(s, vflat_ref[...]))
    for h in range(n_heads):
        o_ref[:, h * DSA_HEAD_DIM:(h + 1) * DSA_HEAD_DIM] = o[h * ts:(h + 1) * ts]


def _dsa_sample(qb, madd, cache_k, cache_v, new_k, new_v, bias_tab, ts):
    nb, past, n_heads, _ = cache_k.shape
    wb = n_heads * DSA_HEAD_DIM
    n_keys = past + ts
    assert (n_keys % LANES * n_heads) % LANES == 0
    expand = _mx(jnp.repeat(jnp.eye(LANES, dtype=jnp.float32), n_heads, axis=1))
    row = lambda bi: (bi, 0)
    per_b = lambda bi: (bi, 0, 0)
    return pl.pallas_call(
        functools.partial(_dsa_sample_kernel, n_heads, past, ts),
        grid=(nb,),
        in_specs=[pl.BlockSpec((ts, wb), row),
                  pl.BlockSpec((ts, madd.shape[1]), row),
                  pl.BlockSpec((None, past * n_heads, DSA_HEAD_DIM), per_b),
                  pl.BlockSpec((None, past * n_heads, DSA_HEAD_DIM), per_b),
                  pl.BlockSpec((ts * n_heads, DSA_HEAD_DIM), row),
                  pl.BlockSpec((ts * n_heads, DSA_HEAD_DIM), row),
                  _resident(bias_tab.shape), _resident(expand.shape)],
        out_specs=pl.BlockSpec((ts, wb), row),
        out_shape=jax.ShapeDtypeStruct((nb * ts, wb), MXU_DT),
        scratch_shapes=[pltpu.VMEM((n_keys * n_heads, DSA_HEAD_DIM), MXU_DT),
                        pltpu.VMEM((n_keys * n_heads, DSA_HEAD_DIM), MXU_DT),
                        pltpu.VMEM((n_heads * ts, pl.cdiv(n_keys, LANES) * LANES), jnp.float32),
                        pltpu.VMEM((n_heads * ts, n_keys * n_heads), jnp.float32)],
        compiler_params=_cparams(1),
        name="dsa_sample",
    )(qb, madd, cache_k.reshape(nb, past * n_heads, DSA_HEAD_DIM), cache_v.reshape(nb, past * n_heads, DSA_HEAD_DIM),
      new_k.reshape(nb * ts * n_heads, DSA_HEAD_DIM), new_v.reshape(nb * ts * n_heads, DSA_HEAD_DIM),
      bias_tab, expand)


def _out_kernel(oa_ref, ob_ref, ga_ref, gb_ref, x_ref, gate_ref, gna_ref, gnb_ref, w_ref, fg_ref, y_ref):
    def gated(o_ref, g_ref, gain_ref):
        o = o_ref[...].astype(jnp.float32)
        on = o * lax.rsqrt(jnp.mean(o * o, axis=-1, keepdims=True) + EPS) * gain_ref[...]
        return _mx(on * g_ref[...].astype(jnp.float32))

    wa = oa_ref.shape[1]
    tm = x_ref.shape[0]
    n_parts = 2 if tm % (2 * LANES) == 0 else 1
    rows = tm // n_parts
    for part in range(n_parts):
        r = slice(part * rows, (part + 1) * rows)
        out = (_dot(gated(oa_ref.at[r], ga_ref.at[r], gna_ref), w_ref[0:wa, :])
               + _dot(gated(ob_ref.at[r], gb_ref.at[r], gnb_ref), w_ref[wa:, :]))
        gate = gate_ref[...] if gate_ref.shape[0] == 1 else gate_ref[r, :]
        xn = x_ref[r, :] + gate * out
        y_ref[r, :] = xn * lax.rsqrt(jnp.mean(xn * xn, axis=-1, keepdims=True) + EPS) * fg_ref[...]


def _out_proj(oa, ob, ga, gb, x3, gate, gain_a, gain_b, w_out, final_gain, tm):
    bv, tv, d = x3.shape
    nt = tv // tm
    wa = oa.shape[1]
    x_spec, g_spec = _row_specs(x3, gate.shape[1], tm)
    row = lambda b, i: (b * nt + i, 0)
    return pl.pallas_call(
        _out_kernel,
        grid=(bv, nt),
        in_specs=[pl.BlockSpec((tm, wa), row), pl.BlockSpec((tm, wa), row),
                  pl.BlockSpec((tm, wa), row), pl.BlockSpec((tm, wa), row),
                  x_spec, g_spec, _resident((1, wa)), _resident((1, wa)),
                  _resident(w_out.shape), _resident((1, d))],
        out_specs=pl.BlockSpec((None, tm, d), lambda b, i: (b, i, 0)),
        out_shape=jax.ShapeDtypeStruct((bv, tv, d), jnp.float32),
        compiler_params=_cparams(2),
        name="out_proj",
    )(oa, ob, ga, gb, x3, gate, gain_a.reshape(1, wa), gain_b.reshape(1, wa), w_out, final_gain.reshape(1, d))


def _pack_kernel(n_heads, wt_ref, qa_ref, ckv_ref, kr_ref, ga_ref, gb_ref, qb_ref, kb_ref, vb_ref, qi_ref, kiw_ref):
    wa = n_heads * MLA_V
    half = MLA_ROPE // 2
    q_head = MLA_NOPE + MLA_ROPE
    step = 2 * LANES

    def panel(ref, off):
        for c in range(0, ref.shape[1], step):
            w = min(step, ref.shape[1] - c)
            ref[:, c:c + w] = _mx(wt_ref[off + c:off + c + w, :].T)

    def dup_rope(off):
        x1, x2 = wt_ref[off:off + half, :], wt_ref[off + half:off + 2 * half, :]
        return _mx(jnp.concatenate([x1, x2, x2, x1], axis=0).T)

    for h in range(n_heads):
        qa_ref[:, h * LANES:(h + 1) * LANES] = _mx(wt_ref[h * q_head:h * q_head + MLA_NOPE, :].T)
    for h in range(0, n_heads, 2):
        r0 = wt_ref[h * q_head + MLA_NOPE:(h + 1) * q_head, :]
        r1 = wt_ref[(h + 1) * q_head + MLA_NOPE:(h + 2) * q_head, :]
        dst = n_heads * MLA_NOPE + (h // 2) * LANES
        qa_ref[:, dst:dst + LANES] = _mx(jnp.concatenate([r0, r1], axis=0).T)
    off = n_heads * q_head
    panel(ckv_ref, off)
    off += KV_LORA
    kr_ref[...] = dup_rope(off)
    off += MLA_ROPE
    for ref in (ga_ref, qb_ref, kb_ref, vb_ref):
        panel(ref, off)
        off += wa
    pairs = IDX_HEADS // 2
    for j in range(pairs):
        lo = wt_ref[off + IDX_DIM * j:off + IDX_DIM * (j + 1), :]
        hi = wt_ref[off + IDX_DIM * (j + pairs):off + IDX_DIM * (j + pairs + 1), :]
        qi_ref[:, LANES * j:LANES * (j + 1)] = _mx(jnp.concatenate([lo, hi], axis=0).T)
    off += IDX_HEADS * IDX_DIM
    n_kw = IDX_DIM + IDX_HEADS
    kiw = jnp.concatenate([wt_ref[off:off + n_kw, :], jnp.zeros((LANES - n_kw, wt_ref.shape[1]), jnp.float32)], axis=0)
    kiw_ref[...] = _mx(kiw.T)
    off += n_kw
    panel(gb_ref, off)


def _pack_w_in(w_in, n_heads):
    wt = w_in.T
    n, d = wt.shape
    wa = n_heads * MLA_V
    assert n == n_heads * (MLA_NOPE + MLA_ROPE) + KV_LORA + MLA_ROPE + 5 * wa + IDX_HEADS * IDX_DIM + IDX_DIM + IDX_HEADS
    assert 2 * IDX_DIM == LANES and 2 * MLA_ROPE == LANES and MLA_NOPE == LANES
    slab = 256
    assert n_heads % 4 == 0
    widths = [n_heads * (MLA_NOPE + MLA_ROPE), KV_LORA, LANES, wa, wa, wa, wa, wa, IDX_HEADS * IDX_DIM, LANES]
    q_a, ckv, kr, g_a, g_b, q_b, k_b, v_b, q_i, kiw = pl.pallas_call(
        functools.partial(_pack_kernel, n_heads),
        grid=(d // slab,),
        in_specs=[pl.BlockSpec((n, slab), lambda i: (0, i))],
        out_specs=[pl.BlockSpec((slab, w), lambda i: (i, 0)) for w in widths],
        out_shape=[jax.ShapeDtypeStruct((d, w), MXU_DT) for w in widths],
        compiler_params=_cparams(1),
        name="pack_w_in",
    )(wt)
    return [q_a, ckv, kr, g_a, g_b], [q_b, k_b, v_b, q_i, kiw]


def _rope_table(pos):
    half = MLA_ROPE // 2
    freqs = jnp.power(ROPE_THETA, -jnp.arange(half, dtype=jnp.float32) / half)
    ang = pos.astype(jnp.float32)[:, None] * freqs
    cos, sin = jnp.cos(ang), jnp.sin(ang)
    return jnp.concatenate([cos, cos, -sin, sin], axis=1)


def _rel_bucket(rel):
    nb = N_BUCKETS // 2
    max_exact = nb // 2
    n = jnp.abs(rel)
    nf = jnp.maximum(n, 1).astype(jnp.float32)
    large = max_exact + (jnp.log(nf / max_exact) / math.log(MAX_DISTANCE / max_exact)
                         * (nb - max_exact)).astype(jnp.int32)
    large = jnp.minimum(large, nb - 1)
    return jnp.where(rel > 0, nb, 0) + jnp.where(n < max_exact, n, large)


def _bucket_bias(rel_bias, rel):
    return (rel_bias * LOG2E)[_rel_bucket(rel)]


def kernel(x_prompt, x_sample, cache_mla_ckv, cache_mla_krope, cache_dsa_k, cache_dsa_v, cache_idx_k,
           c_prompt, c_sample, w_ada, b_ada, ln_gain, w_in, mla_kv_gain, w_uk, w_uv, rel_bias,
           out_gain_a, out_gain_b, w_out, final_gain):
    assert w_ada.shape[0] == 1, "single-layer step"
    b, t, d = x_prompt.shape
    nb, ts, _ = x_sample.shape
    past = cache_mla_ckv.shape[2]
    n_heads = d // 256
    wb = n_heads * DSA_HEAD_DIM
    s_all = past + ts
    assert t % MLA_TQ == 0 and t % DSA_TQ == 0 and MLA_TQ % CHUNK == 0 and DSA_TQ % CHUNK == 0
    assert (s_all - 1) // CHUNK <= past // CHUNK

    mod = _ada_mod(jnp.concatenate([c_prompt, c_sample], axis=0), w_ada[0], b_ada[0])
    shift, scale, gate = mod[:, :d], mod[:, d:2 * d], mod[:, 2 * d:]
    mod_p = [v[:b].reshape(b, 1, d) for v in (scale, shift, gate)]
    mod_s = [jnp.broadcast_to(v[b:, None, :], (nb, ts, d)).reshape(1, nb * ts, d) for v in (scale, shift, gate)]

    w_a, w_b = _pack_w_in(w_in[0], n_heads)
    wkv = _mx(jnp.concatenate([w_uk[0].transpose(2, 0, 1).reshape(KV_LORA, n_heads * MLA_NOPE),
                               w_uv[0].transpose(2, 0, 1).reshape(KV_LORA, n_heads * MLA_V)], axis=1))
    w_o = _mx(w_out[0])
    pos_p = jnp.arange(t, dtype=jnp.int32)
    pos_s = past + jnp.arange(ts, dtype=jnp.int32)
    tab_p = _rope_table(pos_p)
    tab_s = jnp.tile(_rope_table(pos_s), (nb, 1))

    tm = 256
    qcat, ckv_p, krope_p, ga, gb, kcat, vup, oa_buf = _proj_a(
        x_prompt, mod_p[0], mod_p[1], ln_gain[0], tab_p, w_a, mla_kv_gain[0], wkv, tm)
    qb, k_p, v_p, kbb, vbb, qi, idxk_p, kiab, wi, ob_buf = _proj_b(
        x_prompt, mod_p[0], mod_p[1], ln_gain[0], w_b, tm, True)
    o_a = _mla_prompt(qcat, kcat, vup, oa_buf, b, t)
    near = _bucket_bias(rel_bias, jnp.arange(-MAX_DISTANCE - 1, MAX_DISTANCE, dtype=jnp.int32))
    far = near[:1]
    bias_tab = jnp.concatenate([jnp.broadcast_to(far, (t - 1 - MAX_DISTANCE, n_heads)), near[1:],
                                jnp.broadcast_to(far, (t + 1 - MAX_DISTANCE, n_heads))], axis=0).T
    o_b = _dsa_prompt(qb, qi, wi, kbb, vbb, kiab, bias_tab, ob_buf, b, t, min(TOPK_MAX, t // 4))
    y_prompt = _out_proj(o_a, o_b, ga, gb, x_prompt, mod_p[2], out_gain_a[0], out_gain_b[0], w_o, final_gain, 2 * tm)

    xs3 = x_sample.reshape(1, nb * ts, d)
    tms = min(256, nb * ts)
    qcat_s, ckv_s, krope_s, ga_s, gb_s = _proj_a(
        xs3, mod_s[0], mod_s[1], ln_gain[0], tab_s, w_a, mla_kv_gain[0], None, tms)
    qb_s, k_s, v_s, kbb_s, vbb_s, qi_s, idxk_s, kiab_s, wi_s = _proj_b(
        xs3, mod_s[0], mod_s[1], ln_gain[0], w_b, tms, False)
    oa_s = _mla_sample(qcat_s, cache_mla_ckv[0], jnp.swapaxes(cache_mla_krope[0], 1, 2), ckv_s, krope_s,
                       _mx(w_uk[0]), _mx(w_uv[0]), ts)
    s_pad = pl.cdiv(s_all, LANES) * LANES
    tab_s = _bucket_bias(rel_bias, jnp.arange(s_pad + LANES, dtype=jnp.int32) - (s_all - 1)).T
    madd_s = _dsa_select(qi_s, wi_s, jnp.swapaxes(cache_idx_k[0], 1, 2), kiab_s, ts, min(TOPK_MAX, s_all // 4))
    ob_s = _dsa_sample(qb_s, madd_s, cache_dsa_k[0], cache_dsa_v[0], kbb_s, vbb_s, tab_s, ts)
    y_sample = _out_proj(oa_s, ob_s, ga_s, gb_s, xs3, mod_s[2], out_gain_a[0], out_gain_b[0], w_o, final_gain, tms)

    hd = (n_heads, DSA_HEAD_DIM)
    return (y_prompt, y_sample.reshape(nb, ts, d),
            ckv_p.reshape(1, b, t, KV_LORA), krope_p.reshape(1, b, t, MLA_ROPE),
            k_p.reshape(1, b, t, *hd), v_p.reshape(1, b, t, *hd), idxk_p.reshape(1, b, t, IDX_DIM),
            ckv_s.reshape(1, nb, ts, KV_LORA), krope_s.reshape(1, nb, ts, MLA_ROPE),
            k_s.reshape(1, nb, ts, *hd), v_s.reshape(1, nb, ts, *hd), idxk_s.reshape(1, nb, ts, IDX_DIM))
```

```python
import functools
import math

import jax
import jax.numpy as jnp
from jax import lax
from jax.experimental import pallas as pl
from jax.experimental.pallas import tpu as pltpu

MXU_DT = jnp.bfloat16

CHUNK = 64
MLA_NOPE = 128
MLA_ROPE = 64
MLA_V = 128
KV_LORA = 512
DSA_HEAD_DIM = 128
IDX_HEADS = 16
IDX_DIM = 64
TOPK_MAX = 256
N_BUCKETS = 32
MAX_DISTANCE = 128
ROPE_THETA = 10000.0
EPS = 1e-6

LANES = 128
HEAD_SLOT = 256
NEG_BIG = -1e30
LOG2E = math.log2(math.e)
INT_MIN = -2 ** 31
KEY_NEG_INF = INT_MIN + 0x7FFFFF
VMEM_LIMIT = 56 * 1024 * 1024
PROJ_TM = 256


def _cparams(n_grid, vmem=VMEM_LIMIT):
    return pltpu.CompilerParams(dimension_semantics=("arbitrary",) * n_grid, vmem_limit_bytes=vmem)


def _mx(v):
    return v.astype(MXU_DT)


def _dot(a, b):
    return jnp.dot(a, b, preferred_element_type=jnp.float32)


def _dot_nt(a, b):
    return lax.dot_general(a, b, (((1,), (1,)), ((), ())), preferred_element_type=jnp.float32)


def _silu(v):
    return v * (1.0 / (1.0 + jnp.exp(-v)))


def _resident(shape):
    nd = len(shape)
    return pl.BlockSpec(shape, lambda *_: (0,) * nd, pipeline_mode=pl.Buffered(1))


def _ada_kernel(c_ref, w_ref, b_ref, o_ref):
    a = _mx(_silu(c_ref[...]))
    o_ref[...] = _dot(a, _mx(w_ref[...])) + b_ref[...]


def _ada_mod(c_all, w_ada, b_ada):
    m, d = c_all.shape
    n = w_ada.shape[1]
    tn = 1024
    return pl.pallas_call(
        _ada_kernel,
        grid=(n // tn,),
        in_specs=[pl.BlockSpec((m, d), lambda j: (0, 0)),
                  pl.BlockSpec((d, tn), lambda j: (0, j)),
                  pl.BlockSpec((1, tn), lambda j: (0, j))],
        out_specs=pl.BlockSpec((m, tn), lambda j: (0, j)),
        out_shape=jax.ShapeDtypeStruct((m, n), jnp.float32),
        compiler_params=_cparams(1),
        name="ada_mod",
    )(c_all, w_ada, b_ada.reshape(1, n))


def _modulated_norm(x_ref, scale_ref, shift_ref, lng_ref):
    x = x_ref[...]
    xn = x * lax.rsqrt(jnp.mean(x * x, axis=-1, keepdims=True) + EPS) * lng_ref[...]
    return _mx(xn * (1.0 + scale_ref[...]) + shift_ref[...])


def _rope128(a, tab):
    t = a * tab
    return t + pltpu.roll(t, 64, 1)


def _proj_a_kernel(n_heads, emit_kv, x_ref, scale_ref, shift_ref, lng_ref, tab_ref,
                   wq_ref, wc_ref, wr_ref, wga_ref, wgb_ref, kvg_ref, *rest):
    if emit_kv:
        wkv_ref, qcat_ref, ckv_ref, krope_ref, ga_ref, gb_ref, kcat_ref, vup_ref, attn_ref = rest
        attn_ref[...] = jnp.zeros_like(attn_ref)
    else:
        qcat_ref, ckv_ref, krope_ref, ga_ref, gb_ref = rest
    hb = _modulated_norm(x_ref, scale_ref, shift_ref, lng_ref)
    tab = tab_ref[...]
    qscale = (MLA_NOPE + MLA_ROPE) ** -0.5 * LOG2E
    for h in range(0, n_heads, 2):
        a = _dot(hb, wq_ref[:, h * LANES:(h + 2) * LANES])
        qcat_ref[:, h * HEAD_SLOT:h * HEAD_SLOT + LANES] = _mx(a[:, :LANES] * qscale)
        qcat_ref[:, (h + 1) * HEAD_SLOT:(h + 1) * HEAD_SLOT + LANES] = _mx(a[:, LANES:] * qscale)
    lane = lax.broadcasted_iota(jnp.int32, tab.shape, 1)
    tab_r = pltpu.roll(tab, MLA_ROPE, 1)
    cos4 = jnp.where(lane < MLA_ROPE, tab, tab_r)
    sin4 = jnp.where(lane < MLA_ROPE, tab_r, tab)
    is_x1 = lane % MLA_ROPE < MLA_ROPE // 2
    n_nope = n_heads * MLA_NOPE
    for h in range(0, n_heads, 4):
        a = _dot(hb, wq_ref[:, n_nope + (h // 2) * LANES:n_nope + (h // 2 + 2) * LANES])
        for u in range(2):
            ap = a[:, u * LANES:(u + 1) * LANES]
            swapped = jnp.where(is_x1, pltpu.roll(ap, LANES - MLA_ROPE // 2, 1), pltpu.roll(ap, MLA_ROPE // 2, 1))
            roped = (ap * cos4 + swapped * sin4) * qscale
            c0 = (h + 2 * u) * HEAD_SLOT
            qcat_ref[:, c0 + LANES:c0 + HEAD_SLOT] = _mx(roped)
            qcat_ref[:, c0 + HEAD_SLOT + LANES:c0 + 2 * HEAD_SLOT] = _mx(pltpu.roll(roped, MLA_ROPE, 1))
    c = _dot(hb, wc_ref[...])
    cn = c * lax.rsqrt(jnp.mean(c * c, axis=-1, keepdims=True) + EPS) * kvg_ref[...]
    ckv_ref[...] = cn
    r = _rope128(_dot(hb, wr_ref[...]), tab)
    krope_ref[...] = r[:, :MLA_ROPE]
    ga_ref[...] = _mx(_silu(_dot(hb, wga_ref[...])))
    gb_ref[...] = _mx(_silu(_dot(hb, wgb_ref[...])))
    if emit_kv:
        cb = _mx(cn)
        lane = lax.broadcasted_iota(jnp.int32, r.shape, 1)
        krz = _mx(jnp.where(lane < MLA_ROPE, r, 0.0))
        kn = _dot(cb, wkv_ref[:, :n_heads * MLA_NOPE])
        for h in range(n_heads):
            c0 = h * HEAD_SLOT
            kcat_ref[:, c0:c0 + LANES] = _mx(kn[:, h * MLA_NOPE:(h + 1) * MLA_NOPE])
            kcat_ref[:, c0 + LANES:c0 + HEAD_SLOT] = krz
        vup_ref[...] = _mx(_dot(cb, wkv_ref[:, n_heads * MLA_NOPE:]))


def _proj_b_kernel(n_heads, emit_attn, x_ref, scale_ref, shift_ref, lng_ref, wq_ref, wk_ref, wv_ref, wqi_ref,
                   wkw_ref, qb_ref, kb_hbm, vb_hbm, kbb_ref, vbb_ref, qi_ref, ki_ref, kiab_ref, wi_ref, *rest):
    if emit_attn:
        rest[0][...] = jnp.zeros_like(rest[0])
    rows_ref, sem = rest[-2:]
    tm = x_ref.shape[0]
    n_steps = pl.num_programs(0) * pl.num_programs(1)
    step_id = pl.program_id(0) * pl.num_programs(1) + pl.program_id(1)
    slot = step_id % 2

    def row_copies(slot_, step_, which):
        dst = (kb_hbm, vb_hbm)[which]
        return [pltpu.make_async_copy(rows_ref.at[slot_, which, :, h * DSA_HEAD_DIM:(h + 1) * DSA_HEAD_DIM],
                                      dst.at[pl.ds(step_ * tm, tm), h, :], sem.at[slot_, which])
                for h in range(n_heads)]

    def wait_slot(slot_, step_):
        for which in range(2):
            for cp in row_copies(slot_, step_, which):
                cp.wait()

    hb = _modulated_norm(x_ref, scale_ref, shift_ref, lng_ref)
    dscale = DSA_HEAD_DIM ** -0.5 * LOG2E
    step = 512
    for c0 in range(0, wq_ref.shape[1], step):
        qb_ref[:, c0:c0 + step] = _mx(_dot(hb, wq_ref[:, c0:c0 + step]) * dscale)
    for c0 in range(0, wqi_ref.shape[1], step):
        qi_ref[:, c0:c0 + step] = _mx(_dot(hb, wqi_ref[:, c0:c0 + step]))
    a = _dot(hb, wkw_ref[...])
    ki_ref[...] = a[:, :IDX_DIM]
    lane = lax.broadcasted_iota(jnp.int32, a.shape, 1)
    kz = jnp.where(lane < IDX_DIM, a, 0.0)
    kiab_ref[:, :LANES] = _mx(kz)
    kiab_ref[:, LANES:] = _mx(pltpu.roll(kz, IDX_DIM, 1))
    wi_ref[...] = a * (IDX_HEADS ** -0.5 * IDX_DIM ** -0.5)

    @pl.when(step_id >= 2)
    def _():
        wait_slot(slot, step_id - 2)

    for which, (w_ref, b_ref) in enumerate(((wk_ref, kbb_ref), (wv_ref, vbb_ref))):
        for c0 in range(0, w_ref.shape[1], step):
            kv = _dot(hb, w_ref[:, c0:c0 + step])
            rows_ref[slot, which, :, c0:c0 + step] = kv
            b_ref[:, c0:c0 + step] = _mx(kv)
        for cp in row_copies(slot, step_id, which):
            cp.start()

    @pl.when(step_id == n_steps - 1)
    def _():
        wait_slot(slot, step_id)

        @pl.when(n_steps >= 2)
        def _():
            wait_slot(1 - slot, step_id - 1)


def _row_specs(x3, mod_rows, tm):
    bv, tv, d = x3.shape
    x_spec = pl.BlockSpec((None, tm, d), lambda b, i: (b, i, 0))
    if mod_rows == 1:
        m_spec = pl.BlockSpec((None, 1, d), lambda b, i: (b, 0, 0))
    else:
        m_spec = pl.BlockSpec((None, tm, d), lambda b, i: (b, i, 0))
    return x_spec, m_spec


def _out2d(m, width, dtype, tm, nt):
    return (jax.ShapeDtypeStruct((m, width), dtype),
            pl.BlockSpec((tm, width), lambda b, i: (b * nt + i, 0)))


def _proj_a(x3, scale, shift, ln_gain, tab, ws_a, kv_gain, wkv, tm):
    bv, tv, d = x3.shape
    nt = tv // tm
    m = bv * tv
    n_heads = d // 256
    wa = n_heads * MLA_V
    emit_kv = wkv is not None
    x_spec, m_spec = _row_specs(x3, scale.shape[1], tm)
    in_specs = ([x_spec, m_spec, m_spec, _resident((1, d)), pl.BlockSpec((tm, LANES), lambda b, i: (i, 0))]
                + [_resident(w.shape) for w in ws_a] + [_resident((1, KV_LORA))])
    args = [x3, scale, shift, ln_gain.reshape(1, d), tab, *ws_a, kv_gain.reshape(1, KV_LORA)]
    outs = [_out2d(m, n_heads * HEAD_SLOT, MXU_DT, tm, nt),
            _out2d(m, KV_LORA, jnp.float32, tm, nt),
            _out2d(m, MLA_ROPE, jnp.float32, tm, nt),
            _out2d(m, wa, MXU_DT, tm, nt),
            _out2d(m, wa, MXU_DT, tm, nt)]
    if emit_kv:
        in_specs.append(_resident(wkv.shape))
        args.append(wkv)
        outs += [_out2d(m, n_heads * HEAD_SLOT, MXU_DT, tm, nt),
                 _out2d(m, wa, MXU_DT, tm, nt),
                 _out2d(m, wa, MXU_DT, tm, nt)]
    return pl.pallas_call(
        functools.partial(_proj_a_kernel, n_heads, emit_kv),
        grid=(bv, nt),
        in_specs=in_specs,
        out_specs=[o[1] for o in outs],
        out_shape=[o[0] for o in outs],
        compiler_params=_cparams(2),
        name="proj_a",
    )(*args)


def _proj_b(x3, scale, shift, ln_gain, ws_b, tm, emit_attn_buffer):
    bv, tv, d = x3.shape
    nt = tv // tm
    m = bv * tv
    n_heads = d // 256
    width_b = n_heads * DSA_HEAD_DIM
    x_spec, m_spec = _row_specs(x3, scale.shape[1], tm)
    cache_rows = (jax.ShapeDtypeStruct((m, n_heads, DSA_HEAD_DIM), jnp.float32), pl.BlockSpec(memory_space=pl.ANY))
    outs = [_out2d(m, width_b, MXU_DT, tm, nt),
            cache_rows,
            cache_rows,
            _out2d(m, width_b, MXU_DT, tm, nt),
            _out2d(m, width_b, MXU_DT, tm, nt),
            _out2d(m, IDX_HEADS * IDX_DIM, MXU_DT, tm, nt),
            _out2d(m, IDX_DIM, jnp.float32, tm, nt),
            _out2d(m, 2 * LANES, MXU_DT, tm, nt),
            _out2d(m, LANES, jnp.float32, tm, nt)]
    if emit_attn_buffer:
        outs.append(_out2d(m, width_b, MXU_DT, tm, nt))
    return pl.pallas_call(
        functools.partial(_proj_b_kernel, n_heads, emit_attn_buffer),
        grid=(bv, nt),
        in_specs=[x_spec, m_spec, m_spec, _resident((1, d))] + [_resident(w.shape) for w in ws_b],
        out_specs=[o[1] for o in outs],
        out_shape=[o[0] for o in outs],
        scratch_shapes=[pltpu.VMEM((2, 2, tm, width_b), jnp.float32), pltpu.SemaphoreType.DMA((2, 2))],
        compiler_params=_cparams(2),
        name="proj_b",
    )(x3, scale, shift, ln_gain.reshape(1, d), *ws_b)


def _softmax_pv(s, v):
    m = jnp.max(s, axis=-1, keepdims=True)
    p = jnp.exp2(s - m)
    l = jnp.sum(p, axis=-1, keepdims=True)
    return _dot(_mx(p), v) * (1.0 / l)


def _topk_mask(score, adm, topk, sc_ref, madd_ref):
    rows, n = score.shape
    if n <= topk:
        madd_ref[...] = jnp.where(adm, 0.0, NEG_BIG)
        return
    sc_ref[...] = jnp.where(adm, score, -jnp.inf)
    kf = float(topk)

    def count(pred):
        return jnp.sum(jnp.where(pred, 1.0, 0.0), axis=-1, keepdims=True)

    def key_to_float(key):
        return pltpu.bitcast(key ^ ((key >> 31) & 0x7FFFFFFF), jnp.float32)

    def bit_step(i, thr):
        inc = lax.shift_left(jnp.int32(1), 31 - i)
        cand = thr + inc
        ok = count(sc_ref[...] >= key_to_float(cand)) >= kf
        return jnp.where(ok, cand, thr)

    thr = lax.fori_loop(0, 32, bit_step, jnp.full((rows, 1), INT_MIN, jnp.int32))
    thr = key_to_float(jnp.maximum(thr, KEY_NEG_INF))
    ge = sc_ref[...] >= thr
    madd_ref[...] = jnp.where(ge, 0.0, NEG_BIG)

    @pl.when(jnp.max(count(ge)) > kf)
    def _():
        sc = sc_ref[...]
        gt = sc > thr
        need = kf - count(gt)
        eqf = jnp.where(sc == thr, 1.0, 0.0)
        col = lax.broadcasted_iota(jnp.int32, (rows, n), 1)
        nbits = max(1, int(n).bit_length())

        def col_step(i, bound):
            cand = bound + lax.shift_left(jnp.int32(1), nbits - 1 - i)
            taken = jnp.sum(jnp.where(col < cand, eqf, 0.0), axis=-1, keepdims=True)
            return jnp.where(taken <= need, cand, bound)

        bound = lax.fori_loop(0, nbits, col_step, jnp.zeros((rows, 1), jnp.int32))
        tie_madd = jnp.where(jnp.where(col < bound, eqf, 0.0) > 0.5, 0.0, NEG_BIG)
        madd_ref[...] = jnp.where(adm, jnp.where(gt, 0.0, tie_madd), NEG_BIG)


def _indexer_scores(qi_ref, wi, kia, kib):
    half = IDX_HEADS // 2
    score = None
    for j in range(half):
        qp = qi_ref[:, j * LANES:(j + 1) * LANES]
        da = jnp.maximum(_dot_nt(qp, kia), 0.0) * wi[:, IDX_DIM + j:IDX_DIM + j + 1]
        db = jnp.maximum(_dot_nt(qp, kib), 0.0) * wi[:, IDX_DIM + half + j:IDX_DIM + half + j + 1]
        score = da + db if score is None else score + da + db
    return score


def _indexer_scores_stacked(qi_ref, wi, dots_a, dots_b):
    half = IDX_HEADS // 2
    ts = qi_ref.shape[0]
    q_all = jnp.concatenate([qi_ref[:, j * LANES:(j + 1) * LANES] for j in range(half)], axis=0)
    wa = jnp.concatenate([wi[:, IDX_DIM + j:IDX_DIM + j + 1] for j in range(half)], axis=0)
    wb = jnp.concatenate([wi[:, IDX_DIM + half + j:IDX_DIM + half + j + 1] for j in range(half)], axis=0)
    part = jnp.maximum(dots_a(q_all), 0.0) * wa + jnp.maximum(dots_b(q_all), 0.0) * wb
    score = part[0:ts]
    for j in range(1, half):
        score = score + part[j * ts:(j + 1) * ts]
    return score


MLA_TQ = 512
DSA_TQ = 256
DSA_STEP_MAX_KEYS = 2048


def _chunk_madd(row0, tq, n_keys):
    qc = (row0 + lax.broadcasted_iota(jnp.int32, (tq, n_keys), 0)) // CHUNK
    kc = lax.broadcasted_iota(jnp.int32, (tq, n_keys), 1) // CHUNK
    return kc <= qc


def _chained_tile_calls(n_tiles, make_call, out):
    for c in range(n_tiles):
        out = make_call(c, out)
    return out


def _mla_prompt_kernel(n_heads, row0, q_ref, k_ref, v_ref, prev_ref, o_ref):
    del prev_ref
    tq, n_keys = q_ref.shape[0], k_ref.shape[0]
    madd = jnp.where(_chunk_madd(row0, tq, n_keys), 0.0, NEG_BIG)
    for h in range(n_heads):
        c0 = h * HEAD_SLOT
        s = _dot_nt(q_ref[:, c0:c0 + HEAD_SLOT], k_ref[:, c0:c0 + HEAD_SLOT]) + madd
        o_ref[:, h * MLA_V:(h + 1) * MLA_V] = _mx(_softmax_pv(s, v_ref[:, h * MLA_V:(h + 1) * MLA_V]))


def _mla_prompt(qcat, kcat, vup, out_init, b, t):
    tq = MLA_TQ
    n_heads = qcat.shape[1] // HEAD_SLOT
    nq = t // tq
    wa = n_heads * MLA_V
    k3 = kcat.reshape(b, t, kcat.shape[1])
    v3 = vup.reshape(b, t, wa)

    def make_call(c, out):
        n_keys = (c + 1) * tq
        row = lambda bi: (bi * nq + c, 0)
        return pl.pallas_call(
            functools.partial(_mla_prompt_kernel, n_heads, c * tq),
            grid=(b,),
            in_specs=[pl.BlockSpec((tq, qcat.shape[1]), row),
                      pl.BlockSpec((None, n_keys, kcat.shape[1]), lambda bi: (bi, 0, 0)),
                      pl.BlockSpec((None, n_keys, wa), lambda bi: (bi, 0, 0)),
                      pl.BlockSpec(memory_space=pl.ANY)],
            out_specs=pl.BlockSpec((tq, wa), row),
            out_shape=jax.ShapeDtypeStruct(out.shape, out.dtype),
            input_output_aliases={3: 0},
            compiler_params=_cparams(1),
            name="mla_prompt_%d" % c,
        )(qcat, k3, v3, out)

    return _chained_tile_calls(nq, make_call, out_init)


def _toeplitz_bias(tab_ref, h, start, n_rows, n_cols):
    shift0 = LANES - (n_rows - 1)
    from_left = (lax.broadcasted_iota(jnp.int32, (n_rows, LANES), 1)
                 < shift0 + lax.broadcasted_iota(jnp.int32, (n_rows, LANES), 0))
    pieces = []
    prev = None
    for n in range(n_cols // LANES + 1):
        seg = tab_ref[h:h + 1, start + n * LANES:start + (n + 1) * LANES]
        rot = pltpu.roll(jnp.broadcast_to(seg, (n_rows, LANES)), shift0 % LANES, 1, stride=1, stride_axis=0)
        if prev is not None:
            pieces.append(jnp.where(from_left, prev, rot))
        prev = rot
    return jnp.concatenate(pieces, axis=1)


def _dsa_prompt_kernel(n_heads, row0, t, topk, qb_ref, qi_ref, wi_ref, k_ref, v_ref, kiab_ref, tab_ref, prev_ref,
                       o_ref, key_ref, madd_ref, near_ref):
    del prev_ref
    per, tq, _ = qb_ref.shape
    n_keys = k_ref.shape[1]
    score = jnp.concatenate(
        [_indexer_scores(qi_ref.at[u], wi_ref[u], kiab_ref[u, :, :LANES], kiab_ref[u, :, LANES:])
         for u in range(per)], axis=0)
    adm = _chunk_madd(row0, tq, n_keys)
    _topk_mask(score, jnp.concatenate([adm] * per, axis=0), topk, key_ref, madd_ref)
    near0 = max(0, row0 - LANES)
    n_near = n_keys - near0

    @pl.when(pl.program_id(0) == 0)
    def _():
        for h in range(n_heads):
            near_ref[h] = jnp.concatenate(
                [_toeplitz_bias(tab_ref, h, near0 - (row0 + u * LANES) - LANES + t, LANES, n_near)
                 for u in range(tq // LANES)], axis=0)

    for h in range(n_heads):
        c0 = h * DSA_HEAD_DIM
        if near0 > 0:
            far = jnp.broadcast_to(tab_ref[h:h + 1, 0:1], (tq, near0))
            bias = jnp.concatenate([far, near_ref[h]], axis=1)
        else:
            bias = near_ref[h]
        for u in range(per):
            s = (_dot_nt(qb_ref[u, :, c0:c0 + DSA_HEAD_DIM], k_ref[u, :, c0:c0 + DSA_HEAD_DIM]) + bias
                 + madd_ref[u * tq:(u + 1) * tq, :])
            o_ref[u, :, c0:c0 + DSA_HEAD_DIM] = _mx(_softmax_pv(s, v_ref[u, :, c0:c0 + DSA_HEAD_DIM]))


def _dsa_prompt(qb, qi, wi, kbb, vbb, kiab, bias_tab, out_init, b, t, topk):
    tq = DSA_TQ
    assert MAX_DISTANCE <= LANES and tq % LANES == 0
    n_heads = qb.shape[1] // DSA_HEAD_DIM
    nq = t // tq
    wb = qb.shape[1]
    k3, v3, ki3 = kbb.reshape(b, t, wb), vbb.reshape(b, t, wb), kiab.reshape(b, t, 2 * LANES)
    qb3, qi3, wi3 = qb.reshape(b, t, wb), qi.reshape(b, t, qi.shape[1]), wi.reshape(b, t, LANES)
    keys_of = lambda bi: (bi, 0, 0)

    def make_call(c, out):
        n_keys = (c + 1) * tq
        per = max(p for p in (1, 2, 4) if p == 1 or (b % p == 0 and p * n_keys <= DSA_STEP_MAX_KEYS))
        rows_of = lambda bi: (bi, c, 0)
        return pl.pallas_call(
            functools.partial(_dsa_prompt_kernel, n_heads, c * tq, t, topk),
            grid=(b // per,),
            in_specs=[pl.BlockSpec((per, tq, wb), rows_of),
                      pl.BlockSpec((per, tq, qi.shape[1]), rows_of),
                      pl.BlockSpec((per, tq, LANES), rows_of),
                      pl.BlockSpec((per, n_keys, wb), keys_of),
                      pl.BlockSpec((per, n_keys, wb), keys_of),
                      pl.BlockSpec((per, n_keys, 2 * LANES), keys_of),
                      _resident(bias_tab.shape),
                      pl.BlockSpec(memory_space=pl.ANY)],
            out_specs=pl.BlockSpec((per, tq, wb), rows_of),
            out_shape=jax.ShapeDtypeStruct(out.shape, out.dtype),
            input_output_aliases={7: 0},
            scratch_shapes=[pltpu.VMEM((per * tq, n_keys), jnp.float32),
                            pltpu.VMEM((per * tq, n_keys), jnp.float32),
                            pltpu.VMEM((n_heads, tq, n_keys - max(0, c * tq - LANES)), jnp.float32)],
            compiler_params=_cparams(1),
            name="dsa_prompt_%d" % c,
        )(qb3, qi3, wi3, k3, v3, ki3, bias_tab, out)

    return _chained_tile_calls(nq, make_call, out_init.reshape(b, t, wb)).reshape(b * t, wb)


def _mla_sample_kernel(n_heads, past, ts, qcat_ref, cckv_ref, ckrt_ref, nckv_ref, nkr_ref, wuk_ref, wuv_ref,
                       o_ref, kall_ref, rt_ref, rnew_ref):
    n_keys = past + ts

    @pl.when(pl.program_id(0) == 0)
    def _():
        kall_ref[...] = jnp.zeros_like(kall_ref)
        rt_ref[...] = jnp.zeros_like(rt_ref)
        rnew_ref[...] = jnp.zeros_like(rnew_ref)

    for u in range(cckv_ref.shape[0]):
        rows = slice(u * ts, (u + 1) * ts)
        kall_ref[u, 0:past, :] = _mx(cckv_ref[u])
        kall_ref[u, past:n_keys, :] = _mx(nckv_ref[rows, :])
        rt_ref[u, 0:MLA_ROPE, :] = _mx(ckrt_ref[u])
        rnew_ref[u, 0:ts, 0:MLA_ROPE] = _mx(nkr_ref[rows, :])
        qlat, qrope = [], []
        for h in range(n_heads):
            c0 = h * HEAD_SLOT
            qlat.append(_mx(_dot(qcat_ref[rows, c0:c0 + LANES], wuk_ref[h])))
            qrope.append(qcat_ref[rows, c0 + LANES:c0 + HEAD_SLOT])
        qlat = jnp.concatenate(qlat, axis=0)
        qrope = jnp.concatenate(qrope, axis=0)
        s = _dot_nt(qlat, kall_ref[u]) + jnp.concatenate(
            [_dot(qrope, rt_ref[u]), _dot_nt(qrope, rnew_ref[u])], axis=1)
        col = lax.broadcasted_iota(jnp.int32, s.shape, 1)
        s = jnp.where(col < n_keys, s, NEG_BIG)
        olat = _mx(_softmax_pv(s, kall_ref[u]))
        for h in range(n_heads):
            o_ref[rows, h * MLA_V:(h + 1) * MLA_V] = _mx(_dot_nt(olat[h * ts:(h + 1) * ts], wuv_ref[h]))


def _mla_sample(qcat, cache_ckv, cache_kr_t, new_ckv, new_kr, wuk, wuv, ts):
    nb, past, c = cache_ckv.shape
    assert past % LANES == 0
    n_heads = qcat.shape[1] // HEAD_SLOT
    wa = n_heads * MLA_V
    s_pad = pl.cdiv(past + ts, LANES) * LANES
    per = max(p for p in (1, 2, 4) if nb % p == 0)
    row = lambda bi: (bi, 0)
    return pl.pallas_call(
        functools.partial(_mla_sample_kernel, n_heads, past, ts),
        grid=(nb // per,),
        in_specs=[pl.BlockSpec((per * ts, qcat.shape[1]), row),
                  pl.BlockSpec((per, past, c), lambda bi: (bi, 0, 0)),
                  pl.BlockSpec((per, MLA_ROPE, past), lambda bi: (bi, 0, 0)),
                  pl.BlockSpec((per * ts, c), row),
                  pl.BlockSpec((per * ts, MLA_ROPE), row),
                  _resident(wuk.shape), _resident(wuv.shape)],
        out_specs=pl.BlockSpec((per * ts, wa), row),
        out_shape=jax.ShapeDtypeStruct((nb * ts, wa), MXU_DT),
        scratch_shapes=[pltpu.VMEM((per, s_pad, c), MXU_DT), pltpu.VMEM((per, LANES, past), MXU_DT),
                        pltpu.VMEM((per, s_pad - past, LANES), MXU_DT)],
        compiler_params=_cparams(1),
        name="mla_sample",
    )(qcat, cache_ckv, cache_kr_t, new_ckv, new_kr, wuk, wuv)


def _dsa_select_kernel(past, ts, topk, qi_ref, wi_ref, ckit_ref, nkiab_ref, madd_ref,
                       kat_ref, kbt_ref, anew_ref, bnew_ref, key_ref):
    bi = pl.program_id(0)
    n_keys = past + ts
    rows, s_pad = madd_ref.shape

    @pl.when(bi == 0)
    def _():
        for ref in (kat_ref, kbt_ref, anew_ref, bnew_ref):
            ref[...] = jnp.zeros_like(ref)

    ckit = _mx(ckit_ref[...])
    kat_ref[0:IDX_DIM, :] = ckit
    kbt_ref[IDX_DIM:LANES, :] = ckit
    anew_ref[0:ts, :] = nkiab_ref[:, :LANES]
    bnew_ref[0:ts, :] = nkiab_ref[:, LANES:]
    madd_ref[pl.ds(pl.multiple_of(bi * ts, ts), ts), :] = _indexer_scores_stacked(
        qi_ref, wi_ref[...],
        lambda q: jnp.concatenate([_dot(q, kat_ref[...]), _dot_nt(q, anew_ref[...])], axis=1),
        lambda q: jnp.concatenate([_dot(q, kbt_ref[...]), _dot_nt(q, bnew_ref[...])], axis=1))

    @pl.when(bi == pl.num_programs(0) - 1)
    def _():
        col = lax.broadcasted_iota(jnp.int32, (rows, s_pad), 1)
        _topk_mask(madd_ref[...], col < n_keys, topk, key_ref, madd_ref)


def _dsa_select(qi, wi, cache_ki_t, new_kiab, ts, topk):
    nb, _, past = cache_ki_t.shape
    assert past % LANES == 0
    s_pad = pl.cdiv(past + ts, LANES) * LANES
    row = lambda bi: (bi, 0)
    return pl.pallas_call(
        functools.partial(_dsa_select_kernel, past, ts, topk),
        grid=(nb,),
        in_specs=[pl.BlockSpec((ts, qi.shape[1]), row),
                  pl.BlockSpec((ts, LANES), row),
                  pl.BlockSpec((None, IDX_DIM, past), lambda bi: (bi, 0, 0)),
                  pl.BlockSpec((ts, 2 * LANES), row)],
        out_specs=pl.BlockSpec((nb * ts, s_pad), lambda bi: (0, 0)),
        out_shape=jax.ShapeDtypeStruct((nb * ts, s_pad), jnp.float32),
        scratch_shapes=[pltpu.VMEM((LANES, past), MXU_DT), pltpu.VMEM((LANES, past), MXU_DT),
                        pltpu.VMEM((s_pad - past, LANES), MXU_DT), pltpu.VMEM((s_pad - past, LANES), MXU_DT),
                        pltpu.VMEM((nb * ts, s_pad), jnp.float32)],
        compiler_params=_cparams(1),
        name="dsa_select",
    )(qi, wi, cache_ki_t, new_kiab)


def _dsa_sample_kernel(n_heads, past, ts, qb_ref, madd_ref, ck_ref, cv_ref, nk_ref, nv_ref, tab_ref, expand_ref,
                       o_ref, kflat_ref, vflat_ref, bias_ref, biasw_ref):
    n_keys = past + ts
    wide = n_heads * LANES
    n_blocks = pl.cdiv(n_keys, LANES)
    widths = [min(wide, (n_keys - j * LANES) * n_heads) for j in range(n_blocks)]

    @pl.when(pl.program_id(0) == 0)
    def _():
        for h in range(n_heads):
            bias_ref[h * ts:(h + 1) * ts, :] = _toeplitz_bias(tab_ref, h, 0, ts, n_blocks * LANES)
        shape = (n_heads * ts, wide)
        same_head = (lax.broadcasted_iota(jnp.int32, shape, 0) // ts
                     == lax.broadcasted_iota(jnp.int32, shape, 1) % n_heads)
        for j in range(n_blocks):
            b = bias_ref[:, j * LANES:(j + 1) * LANES]
            hi = _mx(b)
            rest = b - hi.astype(jnp.float32)
            mid = _mx(rest)
            lo = _mx(rest - mid.astype(jnp.float32))
            piece = _dot(hi, expand_ref[...]) + _dot(mid, expand_ref[...]) + _dot(lo, expand_ref[...])
            biasw_ref[:, j * wide:j * wide + widths[j]] = jnp.where(same_head, piece, NEG_BIG)[:, :widths[j]]

    kflat_ref[0:past * n_heads, :] = _mx(ck_ref[...])
    kflat_ref[past * n_heads:, :] = nk_ref[...]
    vflat_ref[0:past * n_heads, :] = _mx(cv_ref[...])
    vflat_ref[past * n_heads:, :] = nv_ref[...]
    sel = _mx(jnp.where(madd_ref[...] == 0.0, 1.0, 0.0))
    pieces = [_dot(sel[:, j * LANES:(j + 1) * LANES], expand_ref[...])[:, :widths[j]] for j in range(n_blocks)]
    sel_wide = jnp.concatenate(pieces, axis=1)
    sel_wide = jnp.concatenate([sel_wide] * n_heads, axis=0)
    q_all = jnp.concatenate([qb_ref[:, h * DSA_HEAD_DIM:(h + 1) * DSA_HEAD_DIM] for h in range(n_heads)], axis=0)
    s = _dot_nt(q_all, kflat_ref[...]) + biasw_ref[...] + jnp.where(sel_wide > 0.5, 0.0, NEG_BIG)
    o = _mx(_softmax_pv(s, vflat_ref[...]))
    for h in range(n_heads):
        o_ref[:, h * DSA_HEAD_DIM:(h + 1) * DSA_HEAD_DIM] = o[h * ts:(h + 1) * ts]


def _dsa_sample(qb, madd, cache_k, cache_v, new_k, new_v, bias_tab, ts):
    nb, past, n_heads, _ = cache_k.shape
    wb = n_heads * DSA_HEAD_DIM
    n_keys = past + ts
    assert (n_keys % LANES * n_heads) % LANES == 0
    expand = _mx(jnp.repeat(jnp.eye(LANES, dtype=jnp.float32), n_heads, axis=1))
    row = lambda bi: (bi, 0)
    per_b = lambda bi: (bi, 0, 0)
    return pl.pallas_call(
        functools.partial(_dsa_sample_kernel, n_heads, past, ts),
        grid=(nb,),
        in_specs=[pl.BlockSpec((ts, wb), row),
                  pl.BlockSpec((ts, madd.shape[1]), row),
                  pl.BlockSpec((None, past * n_heads, DSA_HEAD_DIM), per_b),
                  pl.BlockSpec((None, past * n_heads, DSA_HEAD_DIM), per_b),
                  pl.BlockSpec((ts * n_heads, DSA_HEAD_DIM), row),
                  pl.BlockSpec((ts * n_heads, DSA_HEAD_DIM), row),
                  _resident(bias_tab.shape), _resident(expand.shape)],
        out_specs=pl.BlockSpec((ts, wb), row),
        out_shape=jax.ShapeDtypeStruct((nb * ts, wb), MXU_DT),
        scratch_shapes=[pltpu.VMEM((n_keys * n_heads, DSA_HEAD_DIM), MXU_DT),
                        pltpu.VMEM((n_keys * n_heads, DSA_HEAD_DIM), MXU_DT),
                        pltpu.VMEM((n_heads * ts, pl.cdiv(n_keys, LANES) * LANES), jnp.float32),
                        pltpu.VMEM((n_heads * ts, n_keys * n_heads), jnp.float32)],
        compiler_params=_cparams(1),
        name="dsa_sample",
    )(qb, madd, cache_k.reshape(nb, past * n_heads, DSA_HEAD_DIM), cache_v.reshape(nb, past * n_heads, DSA_HEAD_DIM),
      new_k.reshape(nb * ts * n_heads, DSA_HEAD_DIM), new_v.reshape(nb * ts * n_heads, DSA_HEAD_DIM),
      bias_tab, expand)


def _out_kernel(oa_ref, ob_ref, ga_ref, gb_ref, x_ref, gate_ref, gna_ref, gnb_ref, w_ref, fg_ref, y_ref):
    def gated(o_ref, g_ref, gain_ref):
        o = o_ref[...].astype(jnp.float32)
        on = o * lax.rsqrt(jnp.mean(o * o, axis=-1, keepdims=True) + EPS) * gain_ref[...]
        return _mx(on * g_ref[...].astype(jnp.float32))

    wa = oa_ref.shape[1]
    out = _dot(gated(oa_ref, ga_ref, gna_ref), w_ref[0:wa, :]) + _dot(gated(ob_ref, gb_ref, gnb_ref), w_ref[wa:, :])
    xn = x_ref[...] + gate_ref[...] * out
    y_ref[...] = xn * lax.rsqrt(jnp.mean(xn * xn, axis=-1, keepdims=True) + EPS) * fg_ref[...]


def _out_proj(oa, ob, ga, gb, x3, gate, gain_a, gain_b, w_out, final_gain, tm):
    bv, tv, d = x3.shape
    nt = tv // tm
    wa = oa.shape[1]
    x_spec, g_spec = _row_specs(x3, gate.shape[1], tm)
    row = lambda b, i: (b * nt + i, 0)
    return pl.pallas_call(
        _out_kernel,
        grid=(bv, nt),
        in_specs=[pl.BlockSpec((tm, wa), row), pl.BlockSpec((tm, wa), row),
                  pl.BlockSpec((tm, wa), row), pl.BlockSpec((tm, wa), row),
                  x_spec, g_spec, _resident((1, wa)), _resident((1, wa)),
                  _resident(w_out.shape), _resident((1, d))],
        out_specs=pl.BlockSpec((None, tm, d), lambda b, i: (b, i, 0)),
        out_shape=jax.ShapeDtypeStruct((bv, tv, d), jnp.float32),
        compiler_params=_cparams(2),
        name="out_proj",
    )(oa, ob, ga, gb, x3, gate, gain_a.reshape(1, wa), gain_b.reshape(1, wa), w_out, final_gain.reshape(1, d))


def _pack_kernel(n_heads, wt_ref, qa_ref, ckv_ref, kr_ref, ga_ref, gb_ref, qb_ref, kb_ref, vb_ref, qi_ref, kiw_ref):
    wa = n_heads * MLA_V
    half = MLA_ROPE // 2
    q_head = MLA_NOPE + MLA_ROPE
    step = 2 * LANES

    def panel(ref, off):
        for c in range(0, ref.shape[1], step):
            w = min(step, ref.shape[1] - c)
            ref[:, c:c + w] = _mx(wt_ref[off + c:off + c + w, :].T)

    def dup_rope(off):
        x1, x2 = wt_ref[off:off + half, :], wt_ref[off + half:off + 2 * half, :]
        return _mx(jnp.concatenate([x1, x2, x2, x1], axis=0).T)

    for h in range(n_heads):
        qa_ref[:, h * LANES:(h + 1) * LANES] = _mx(wt_ref[h * q_head:h * q_head + MLA_NOPE, :].T)
    for h in range(0, n_heads, 2):
        r0 = wt_ref[h * q_head + MLA_NOPE:(h + 1) * q_head, :]
        r1 = wt_ref[(h + 1) * q_head + MLA_NOPE:(h + 2) * q_head, :]
        dst = n_heads * MLA_NOPE + (h // 2) * LANES
        qa_ref[:, dst:dst + LANES] = _mx(jnp.concatenate([r0, r1], axis=0).T)
    off = n_heads * q_head
    panel(ckv_ref, off)
    off += KV_LORA
    kr_ref[...] = dup_rope(off)
    off += MLA_ROPE
    for ref in (ga_ref, qb_ref, kb_ref, vb_ref):
        panel(ref, off)
        off += wa
    pairs = IDX_HEADS // 2
    for j in range(pairs):
        lo = wt_ref[off + IDX_DIM * j:off + IDX_DIM * (j + 1), :]
        hi = wt_ref[off + IDX_DIM * (j + pairs):off + IDX_DIM * (j + pairs + 1), :]
        qi_ref[:, LANES * j:LANES * (j + 1)] = _mx(jnp.concatenate([lo, hi], axis=0).T)
    off += IDX_HEADS * IDX_DIM
    n_kw = IDX_DIM + IDX_HEADS
    kiw = jnp.concatenate([wt_ref[off:off + n_kw, :], jnp.zeros((LANES - n_kw, wt_ref.shape[1]), jnp.float32)], axis=0)
    kiw_ref[...] = _mx(kiw.T)
    off += n_kw
    panel(gb_ref, off)


def _pack_w_in(w_in, n_heads):
    wt = w_in.T
    n, d = wt.shape
    wa = n_heads * MLA_V
    assert n == n_heads * (MLA_NOPE + MLA_ROPE) + KV_LORA + MLA_ROPE + 5 * wa + IDX_HEADS * IDX_DIM + IDX_DIM + IDX_HEADS
    assert 2 * IDX_DIM == LANES and 2 * MLA_ROPE == LANES and MLA_NOPE == LANES
    slab = 256
    assert n_heads % 4 == 0
    widths = [n_heads * (MLA_NOPE + MLA_ROPE), KV_LORA, LANES, wa, wa, wa, wa, wa, IDX_HEADS * IDX_DIM, LANES]
    q_a, ckv, kr, g_a, g_b, q_b, k_b, v_b, q_i, kiw = pl.pallas_call(
        functools.partial(_pack_kernel, n_heads),
        grid=(d // slab,),
        in_specs=[pl.BlockSpec((n, slab), lambda i: (0, i))],
        out_specs=[pl.BlockSpec((slab, w), lambda i: (i, 0)) for w in widths],
        out_shape=[jax.ShapeDtypeStruct((d, w), MXU_DT) for w in widths],
        compiler_params=_cparams(1),
        name="pack_w_in",
    )(wt)
    return [q_a, ckv, kr, g_a, g_b], [q_b, k_b, v_b, q_i, kiw]


def _rope_table(pos):
    half = MLA_ROPE // 2
    freqs = jnp.power(ROPE_THETA, -jnp.arange(half, dtype=jnp.float32) / half)
    ang = pos.astype(jnp.float32)[:, None] * freqs
    cos, sin = jnp.cos(ang), jnp.sin(ang)
    return jnp.concatenate([cos, cos, -sin, sin], axis=1)


def _rel_bucket(rel):
    nb = N_BUCKETS // 2
    max_exact = nb // 2
    n = jnp.abs(rel)
    nf = jnp.maximum(n, 1).astype(jnp.float32)
    large = max_exact + (jnp.log(nf / max_exact) / math.log(MAX_DISTANCE / max_exact)
                         * (nb - max_exact)).astype(jnp.int32)
    large = jnp.minimum(large, nb - 1)
    return jnp.where(rel > 0, nb, 0) + jnp.where(n < max_exact, n, large)


def _bucket_bias(rel_bias, rel):
    return (rel_bias * LOG2E)[_rel_bucket(rel)]


def kernel(x_prompt, x_sample, cache_mla_ckv, cache_mla_krope, cache_dsa_k, cache_dsa_v, cache_idx_k,
           c_prompt, c_sample, w_ada, b_ada, ln_gain, w_in, mla_kv_gain, w_uk, w_uv, rel_bias,
           out_gain_a, out_gain_b, w_out, final_gain):
    assert w_ada.shape[0] == 1, "single-layer step"
    b, t, d = x_prompt.shape
    nb, ts, _ = x_sample.shape
    past = cache_mla_ckv.shape[2]
    n_heads = d // 256
    s_all = past + ts
    assert t % MLA_TQ == 0 and t % DSA_TQ == 0 and MLA_TQ % CHUNK == 0 and DSA_TQ % CHUNK == 0
    assert (s_all - 1) // CHUNK <= past // CHUNK

    mod = _ada_mod(jnp.concatenate([c_prompt, c_sample], axis=0), w_ada[0], b_ada[0])
    shift, scale, gate = mod[:, :d], mod[:, d:2 * d], mod[:, 2 * d:]
    mod_p = [v[:b].reshape(b, 1, d) for v in (scale, shift, gate)]
    mod_s = [jnp.broadcast_to(v[b:, None, :], (nb, ts, d)).reshape(1, nb * ts, d) for v in (scale, shift, gate)]

    w_a, w_b = _pack_w_in(w_in[0], n_heads)
    wkv = _mx(jnp.concatenate([w_uk[0].transpose(2, 0, 1).reshape(KV_LORA, n_heads * MLA_NOPE),
                               w_uv[0].transpose(2, 0, 1).reshape(KV_LORA, n_heads * MLA_V)], axis=1))
    w_o = _mx(w_out[0])
    pos_p = jnp.arange(t, dtype=jnp.int32)
    pos_s = past + jnp.arange(ts, dtype=jnp.int32)
    tab_p = _rope_table(pos_p)
    tab_s = jnp.tile(_rope_table(pos_s), (nb, 1))

    tm = PROJ_TM
    qcat, ckv_p, krope_p, ga, gb, kcat, vup, oa_buf = _proj_a(
        x_prompt, mod_p[0], mod_p[1], ln_gain[0], tab_p, w_a, mla_kv_gain[0], wkv, tm)
    qb, k_p, v_p, kbb, vbb, qi, idxk_p, kiab, wi, ob_buf = _proj_b(
        x_prompt, mod_p[0], mod_p[1], ln_gain[0], w_b, tm, True)
    o_a = _mla_prompt(qcat, kcat, vup, oa_buf, b, t)
    near = _bucket_bias(rel_bias, jnp.arange(-MAX_DISTANCE - 1, MAX_DISTANCE, dtype=jnp.int32))
    far = near[:1]
    bias_tab = jnp.concatenate([jnp.broadcast_to(far, (t - 1 - MAX_DISTANCE, n_heads)), near[1:],
                                jnp.broadcast_to(far, (t + 1 - MAX_DISTANCE, n_heads))], axis=0).T
    o_b = _dsa_prompt(qb, qi, wi, kbb, vbb, kiab, bias_tab, ob_buf, b, t, min(TOPK_MAX, t // 4))
    y_prompt = _out_proj(o_a, o_b, ga, gb, x_prompt, mod_p[2], out_gain_a[0], out_gain_b[0], w_o, final_gain, 2 * tm)

    xs3 = x_sample.reshape(1, nb * ts, d)
    tms = min(PROJ_TM, nb * ts)
    qcat_s, ckv_s, krope_s, ga_s, gb_s = _proj_a(
        xs3, mod_s[0], mod_s[1], ln_gain[0], tab_s, w_a, mla_kv_gain[0], None, tms)
    qb_s, k_s, v_s, kbb_s, vbb_s, qi_s, idxk_s, kiab_s, wi_s = _proj_b(
        xs3, mod_s[0], mod_s[1], ln_gain[0], w_b, tms, False)
    oa_s = _mla_sample(qcat_s, cache_mla_ckv[0], jnp.swapaxes(cache_mla_krope[0], 1, 2), ckv_s, krope_s,
                       _mx(w_uk[0]), _mx(w_uv[0]), ts)
    s_pad = pl.cdiv(s_all, LANES) * LANES
    tab_s = _bucket_bias(rel_bias, jnp.arange(s_pad + LANES, dtype=jnp.int32) - (s_all - 1)).T
    madd_s = _dsa_select(qi_s, wi_s, jnp.swapaxes(cache_idx_k[0], 1, 2), kiab_s, ts, min(TOPK_MAX, s_all // 4))
    ob_s = _dsa_sample(qb_s, madd_s, cache_dsa_k[0], cache_dsa_v[0], kbb_s, vbb_s, tab_s, ts)
    y_sample = _out_proj(oa_s, ob_s, ga_s, gb_s, xs3, mod_s[2], out_gain_a[0], out_gain_b[0], w_o, final_gain, tms)

    hd = (n_heads, DSA_HEAD_DIM)
    return (y_prompt, y_sample.reshape(nb, ts, d),
            ckv_p.reshape(1, b, t, KV_LORA), krope_p.reshape(1, b, t, MLA_ROPE),
            k_p.reshape(1, b, t, *hd), v_p.reshape(1, b, t, *hd), idxk_p.reshape(1, b, t, IDX_DIM),
            ckv_s.reshape(1, nb, ts, KV_LORA), krope_s.reshape(1, nb, ts, MLA_ROPE),
            k_s.reshape(1, nb, ts, *hd), v_s.reshape(1, nb, ts, *hd), idxk_s.reshape(1, nb, ts, IDX_DIM))
```

```python
import functools
import math

import jax
import jax.numpy as jnp
from jax import lax
from jax.experimental import pallas as pl
from jax.experimental.pallas import tpu as pltpu

MXU_DT = jnp.bfloat16

CHUNK = 64
MLA_NOPE = 128
MLA_ROPE = 64
MLA_V = 128
KV_LORA = 512
DSA_HEAD_DIM = 128
IDX_HEADS = 16
IDX_DIM = 64
TOPK_MAX = 256
N_BUCKETS = 32
MAX_DISTANCE = 128
ROPE_THETA = 10000.0
EPS = 1e-6

LANES = 128
HEAD_SLOT = 256
NEG_BIG = -1e30
LOG2E = math.log2(math.e)
INT_MIN = -2 ** 31
KEY_NEG_INF = INT_MIN + 0x7FFFFF
VMEM_LIMIT = 56 * 1024 * 1024
PROJ_TM = 256


def _cparams(n_grid, vmem=VMEM_LIMIT):
    return pltpu.CompilerParams(dimension_semantics=("arbitrary",) * n_grid, vmem_limit_bytes=vmem)


def _mx(v):
    return v.astype(MXU_DT)


def _dot(a, b):
    return jnp.dot(a, b, preferred_element_type=jnp.float32)


def _dot_nt(a, b):
    return lax.dot_general(a, b, (((1,), (1,)), ((), ())), preferred_element_type=jnp.float32)


def _silu(v):
    return v * (1.0 / (1.0 + jnp.exp(-v)))


def _resident(shape):
    nd = len(shape)
    return pl.BlockSpec(shape, lambda *_: (0,) * nd, pipeline_mode=pl.Buffered(1))


def _ada_kernel(c_ref, w_ref, b_ref, o_ref):
    a = _mx(_silu(c_ref[...]))
    o_ref[...] = _dot(a, _mx(w_ref[...])) + b_ref[...]


def _ada_mod(c_all, w_ada, b_ada):
    m, d = c_all.shape
    n = w_ada.shape[1]
    tn = 1024
    return pl.pallas_call(
        _ada_kernel,
        grid=(n // tn,),
        in_specs=[pl.BlockSpec((m, d), lambda j: (0, 0)),
                  pl.BlockSpec((d, tn), lambda j: (0, j)),
                  pl.BlockSpec((1, tn), lambda j: (0, j))],
        out_specs=pl.BlockSpec((m, tn), lambda j: (0, j)),
        out_shape=jax.ShapeDtypeStruct((m, n), jnp.float32),
        compiler_params=_cparams(1),
        name="ada_mod",
    )(c_all, w_ada, b_ada.reshape(1, n))


def _modulated_norm(x_ref, scale_ref, shift_ref, lng_ref):
    x = x_ref[...]
    xn = x * lax.rsqrt(jnp.mean(x * x, axis=-1, keepdims=True) + EPS) * lng_ref[...]
    return _mx(xn * (1.0 + scale_ref[...]) + shift_ref[...])


def _rope128(a, tab):
    t = a * tab
    return t + pltpu.roll(t, 64, 1)


def _proj_a_kernel(n_heads, emit_kv, x_ref, scale_ref, shift_ref, lng_ref, tab_ref,
                   wq_ref, wc_ref, wr_ref, wga_ref, wgb_ref, kvg_ref, *rest):
    if emit_kv:
        wkv_ref, qcat_ref, ckv_ref, krope_ref, ga_ref, gb_ref, kcat_ref, vup_ref, attn_ref = rest
        attn_ref[...] = jnp.zeros_like(attn_ref)
    else:
        qcat_ref, ckv_ref, krope_ref, ga_ref, gb_ref = rest
    hb = _modulated_norm(x_ref, scale_ref, shift_ref, lng_ref)
    tab = tab_ref[...]
    qscale = (MLA_NOPE + MLA_ROPE) ** -0.5 * LOG2E
    for h in range(0, n_heads, 2):
        a = _dot(hb, wq_ref[:, h * LANES:(h + 2) * LANES])
        qcat_ref[:, h * HEAD_SLOT:h * HEAD_SLOT + LANES] = _mx(a[:, :LANES] * qscale)
        qcat_ref[:, (h + 1) * HEAD_SLOT:(h + 1) * HEAD_SLOT + LANES] = _mx(a[:, LANES:] * qscale)
    lane = lax.broadcasted_iota(jnp.int32, tab.shape, 1)
    tab_r = pltpu.roll(tab, MLA_ROPE, 1)
    cos4 = jnp.where(lane < MLA_ROPE, tab, tab_r)
    sin4 = jnp.where(lane < MLA_ROPE, tab_r, tab)
    is_x1 = lane % MLA_ROPE < MLA_ROPE // 2
    n_nope = n_heads * MLA_NOPE
    for h in range(0, n_heads, 4):
        a = _dot(hb, wq_ref[:, n_nope + (h // 2) * LANES:n_nope + (h // 2 + 2) * LANES])
        for u in range(2):
            ap = a[:, u * LANES:(u + 1) * LANES]
            swapped = jnp.where(is_x1, pltpu.roll(ap, LANES - MLA_ROPE // 2, 1), pltpu.roll(ap, MLA_ROPE // 2, 1))
            roped = (ap * cos4 + swapped * sin4) * qscale
            c0 = (h + 2 * u) * HEAD_SLOT
            qcat_ref[:, c0 + LANES:c0 + HEAD_SLOT] = _mx(roped)
            qcat_ref[:, c0 + HEAD_SLOT + LANES:c0 + 2 * HEAD_SLOT] = _mx(pltpu.roll(roped, MLA_ROPE, 1))
    c = _dot(hb, wc_ref[...])
    cn = c * lax.rsqrt(jnp.mean(c * c, axis=-1, keepdims=True) + EPS) * kvg_ref[...]
    ckv_ref[...] = cn
    r = _rope128(_dot(hb, wr_ref[...]), tab)
    krope_ref[...] = r[:, :MLA_ROPE]
    ga_ref[...] = _mx(_silu(_dot(hb, wga_ref[...])))
    gb_ref[...] = _mx(_silu(_dot(hb, wgb_ref[...])))
    if emit_kv:
        cb = _mx(cn)
        lane = lax.broadcasted_iota(jnp.int32, r.shape, 1)
        krz = _mx(jnp.where(lane < MLA_ROPE, r, 0.0))
        kn = _dot(cb, wkv_ref[:, :n_heads * MLA_NOPE])
        for h in range(n_heads):
            c0 = h * HEAD_SLOT
            kcat_ref[:, c0:c0 + LANES] = _mx(kn[:, h * MLA_NOPE:(h + 1) * MLA_NOPE])
            kcat_ref[:, c0 + LANES:c0 + HEAD_SLOT] = krz
        vup_ref[...] = _mx(_dot(cb, wkv_ref[:, n_heads * MLA_NOPE:]))


def _proj_b_kernel(n_heads, emit_attn, x_ref, scale_ref, shift_ref, lng_ref, wq_ref, wk_ref, wv_ref, wqi_ref,
                   wkw_ref, qb_ref, kb_hbm, vb_hbm, kbb_ref, vbb_ref, qi_ref, ki_ref, kiab_ref, wi_ref, *rest):
    if emit_attn:
        rest[0][...] = jnp.zeros_like(rest[0])
    rows_ref, sem = rest[-2:]
    tm = x_ref.shape[0]
    n_steps = pl.num_programs(0) * pl.num_programs(1)
    step_id = pl.program_id(0) * pl.num_programs(1) + pl.program_id(1)
    slot = step_id % 2

    def row_copies(slot_, step_, which):
        dst = (kb_hbm, vb_hbm)[which]
        return [pltpu.make_async_copy(rows_ref.at[slot_, which, :, h * DSA_HEAD_DIM:(h + 1) * DSA_HEAD_DIM],
                                      dst.at[pl.ds(step_ * tm, tm), h, :], sem.at[slot_, which])
                for h in range(n_heads)]

    def wait_slot(slot_, step_):
        for which in range(2):
            for cp in row_copies(slot_, step_, which):
                cp.wait()

    @pl.when(step_id >= 2)
    def _():
        wait_slot(slot, step_id - 2)

    hb = _modulated_norm(x_ref, scale_ref, shift_ref, lng_ref)
    dscale = DSA_HEAD_DIM ** -0.5 * LOG2E
    step = 512
    for c0 in range(0, wq_ref.shape[1], step):
        qb_ref[:, c0:c0 + step] = _mx(_dot(hb, wq_ref[:, c0:c0 + step]) * dscale)
    for which, (w_ref, b_ref) in enumerate(((wk_ref, kbb_ref), (wv_ref, vbb_ref))):
        for c0 in range(0, w_ref.shape[1], step):
            kv = _dot(hb, w_ref[:, c0:c0 + step])
            rows_ref[slot, which, :, c0:c0 + step] = kv
            b_ref[:, c0:c0 + step] = _mx(kv)
        for cp in row_copies(slot, step_id, which):
            cp.start()

    @pl.when(step_id == n_steps - 1)
    def _():
        wait_slot(slot, step_id)

        @pl.when(n_steps >= 2)
        def _():
            wait_slot(1 - slot, step_id - 1)
    for c0 in range(0, wqi_ref.shape[1], step):
        qi_ref[:, c0:c0 + step] = _mx(_dot(hb, wqi_ref[:, c0:c0 + step]))
    a = _dot(hb, wkw_ref[...])
    ki_ref[...] = a[:, :IDX_DIM]
    lane = lax.broadcasted_iota(jnp.int32, a.shape, 1)
    kz = jnp.where(lane < IDX_DIM, a, 0.0)
    kiab_ref[:, :LANES] = _mx(kz)
    kiab_ref[:, LANES:] = _mx(pltpu.roll(kz, IDX_DIM, 1))
    wi_ref[...] = a * (IDX_HEADS ** -0.5 * IDX_DIM ** -0.5)


def _row_specs(x3, mod_rows, tm):
    bv, tv, d = x3.shape
    x_spec = pl.BlockSpec((None, tm, d), lambda b, i: (b, i, 0))
    if mod_rows == 1:
        m_spec = pl.BlockSpec((None, 1, d), lambda b, i: (b, 0, 0))
    else:
        m_spec = pl.BlockSpec((None, tm, d), lambda b, i: (b, i, 0))
    return x_spec, m_spec


def _out2d(m, width, dtype, tm, nt):
    return (jax.ShapeDtypeStruct((m, width), dtype),
            pl.BlockSpec((tm, width), lambda b, i: (b * nt + i, 0)))


def _proj_a(x3, scale, shift, ln_gain, tab, ws_a, kv_gain, wkv, tm):
    bv, tv, d = x3.shape
    nt = tv // tm
    m = bv * tv
    n_heads = d // 256
    wa = n_heads * MLA_V
    emit_kv = wkv is not None
    x_spec, m_spec = _row_specs(x3, scale.shape[1], tm)
    in_specs = ([x_spec, m_spec, m_spec, _resident((1, d)), pl.BlockSpec((tm, LANES), lambda b, i: (i, 0))]
                + [_resident(w.shape) for w in ws_a] + [_resident((1, KV_LORA))])
    args = [x3, scale, shift, ln_gain.reshape(1, d), tab, *ws_a, kv_gain.reshape(1, KV_LORA)]
    outs = [_out2d(m, n_heads * HEAD_SLOT, MXU_DT, tm, nt),
            _out2d(m, KV_LORA, jnp.float32, tm, nt),
            _out2d(m, MLA_ROPE, jnp.float32, tm, nt),
            _out2d(m, wa, MXU_DT, tm, nt),
            _out2d(m, wa, MXU_DT, tm, nt)]
    if emit_kv:
        in_specs.append(_resident(wkv.shape))
        args.append(wkv)
        outs += [_out2d(m, n_heads * HEAD_SLOT, MXU_DT, tm, nt),
                 _out2d(m, wa, MXU_DT, tm, nt),
                 _out2d(m, wa, MXU_DT, tm, nt)]
    return pl.pallas_call(
        functools.partial(_proj_a_kernel, n_heads, emit_kv),
        grid=(bv, nt),
        in_specs=in_specs,
        out_specs=[o[1] for o in outs],
        out_shape=[o[0] for o in outs],
        compiler_params=_cparams(2),
        name="proj_a",
    )(*args)


def _proj_b(x3, scale, shift, ln_gain, ws_b, tm, emit_attn_buffer):
    bv, tv, d = x3.shape
    nt = tv // tm
    m = bv * tv
    n_heads = d // 256
    width_b = n_heads * DSA_HEAD_DIM
    x_spec, m_spec = _row_specs(x3, scale.shape[1], tm)
    cache_rows = (jax.ShapeDtypeStruct((m, n_heads, DSA_HEAD_DIM), jnp.float32), pl.BlockSpec(memory_space=pl.ANY))
    outs = [_out2d(m, width_b, MXU_DT, tm, nt),
            cache_rows,
            cache_rows,
            _out2d(m, width_b, MXU_DT, tm, nt),
            _out2d(m, width_b, MXU_DT, tm, nt),
            _out2d(m, IDX_HEADS * IDX_DIM, MXU_DT, tm, nt),
            _out2d(m, IDX_DIM, jnp.float32, tm, nt),
            _out2d(m, 2 * LANES, MXU_DT, tm, nt),
            _out2d(m, LANES, jnp.float32, tm, nt)]
    if emit_attn_buffer:
        outs.append(_out2d(m, width_b, MXU_DT, tm, nt))
    return pl.pallas_call(
        functools.partial(_proj_b_kernel, n_heads, emit_attn_buffer),
        grid=(bv, nt),
        in_specs=[x_spec, m_spec, m_spec, _resident((1, d))] + [_resident(w.shape) for w in ws_b],
        out_specs=[o[1] for o in outs],
        out_shape=[o[0] for o in outs],
        scratch_shapes=[pltpu.VMEM((2, 2, tm, width_b), jnp.float32), pltpu.SemaphoreType.DMA((2, 2))],
        compiler_params=_cparams(2),
        name="proj_b",
    )(x3, scale, shift, ln_gain.reshape(1, d), *ws_b)


def _softmax_pv(s, v):
    m = jnp.max(s, axis=-1, keepdims=True)
    p = jnp.exp2(s - m)
    l = jnp.sum(p, axis=-1, keepdims=True)
    return _dot(_mx(p), v) * (1.0 / l)


def _topk_mask(score, adm, topk, sc_ref, madd_ref):
    rows, n = score.shape
    if n <= topk:
        madd_ref[...] = jnp.where(adm, 0.0, NEG_BIG)
        return
    sc_ref[...] = jnp.where(adm, score, -jnp.inf)
    kf = float(topk)

    def count(pred):
        return jnp.sum(jnp.where(pred, 1.0, 0.0), axis=-1, keepdims=True)

    def key_to_float(key):
        return pltpu.bitcast(key ^ ((key >> 31) & 0x7FFFFFFF), jnp.float32)

    def bit_step(i, thr):
        inc = lax.shift_left(jnp.int32(1), 31 - i)
        cand = thr + inc
        ok = count(sc_ref[...] >= key_to_float(cand)) >= kf
        return jnp.where(ok, cand, thr)

    thr = lax.fori_loop(0, 32, bit_step, jnp.full((rows, 1), INT_MIN, jnp.int32))
    thr = key_to_float(jnp.maximum(thr, KEY_NEG_INF))
    ge = sc_ref[...] >= thr
    madd_ref[...] = jnp.where(ge, 0.0, NEG_BIG)

    @pl.when(jnp.max(count(ge)) > kf)
    def _():
        sc = sc_ref[...]
        gt = sc > thr
        need = kf - count(gt)
        eqf = jnp.where(sc == thr, 1.0, 0.0)
        col = lax.broadcasted_iota(jnp.int32, (rows, n), 1)
        nbits = max(1, int(n).bit_length())

        def col_step(i, bound):
            cand = bound + lax.shift_left(jnp.int32(1), nbits - 1 - i)
            taken = jnp.sum(jnp.where(col < cand, eqf, 0.0), axis=-1, keepdims=True)
            return jnp.where(taken <= need, cand, bound)

        bound = lax.fori_loop(0, nbits, col_step, jnp.zeros((rows, 1), jnp.int32))
        tie_madd = jnp.where(jnp.where(col < bound, eqf, 0.0) > 0.5, 0.0, NEG_BIG)
        madd_ref[...] = jnp.where(adm, jnp.where(gt, 0.0, tie_madd), NEG_BIG)


def _indexer_scores(qi_ref, wi, kia, kib):
    half = IDX_HEADS // 2
    score = None
    for j in range(half):
        qp = qi_ref[:, j * LANES:(j + 1) * LANES]
        da = jnp.maximum(_dot_nt(qp, kia), 0.0) * wi[:, IDX_DIM + j:IDX_DIM + j + 1]
        db = jnp.maximum(_dot_nt(qp, kib), 0.0) * wi[:, IDX_DIM + half + j:IDX_DIM + half + j + 1]
        score = da + db if score is None else score + da + db
    return score


def _indexer_scores_stacked(qi_ref, wi, dots_a, dots_b):
    half = IDX_HEADS // 2
    ts = qi_ref.shape[0]
    q_all = jnp.concatenate([qi_ref[:, j * LANES:(j + 1) * LANES] for j in range(half)], axis=0)
    wa = jnp.concatenate([wi[:, IDX_DIM + j:IDX_DIM + j + 1] for j in range(half)], axis=0)
    wb = jnp.concatenate([wi[:, IDX_DIM + half + j:IDX_DIM + half + j + 1] for j in range(half)], axis=0)
    part = jnp.maximum(dots_a(q_all), 0.0) * wa + jnp.maximum(dots_b(q_all), 0.0) * wb
    score = part[0:ts]
    for j in range(1, half):
        score = score + part[j * ts:(j + 1) * ts]
    return score


MLA_TQ = 512
DSA_TQ = 256
DSA_STEP_MAX_KEYS = 2048


def _chunk_madd(row0, tq, n_keys):
    qc = (row0 + lax.broadcasted_iota(jnp.int32, (tq, n_keys), 0)) // CHUNK
    kc = lax.broadcasted_iota(jnp.int32, (tq, n_keys), 1) // CHUNK
    return kc <= qc


def _chained_tile_calls(n_tiles, make_call, out):
    for c in range(n_tiles):
        out = make_call(c, out)
    return out


def _mla_prompt_kernel(n_heads, row0, q_ref, k_ref, v_ref, prev_ref, o_ref):
    del prev_ref
    tq, n_keys = q_ref.shape[0], k_ref.shape[0]
    madd = jnp.where(_chunk_madd(row0, tq, n_keys), 0.0, NEG_BIG)
    for h in range(n_heads):
        c0 = h * HEAD_SLOT
        s = _dot_nt(q_ref[:, c0:c0 + HEAD_SLOT], k_ref[:, c0:c0 + HEAD_SLOT]) + madd
        o_ref[:, h * MLA_V:(h + 1) * MLA_V] = _mx(_softmax_pv(s, v_ref[:, h * MLA_V:(h + 1) * MLA_V]))


def _mla_prompt(qcat, kcat, vup, out_init, b, t):
    tq = MLA_TQ
    n_heads = qcat.shape[1] // HEAD_SLOT
    nq = t // tq
    wa = n_heads * MLA_V
    k3 = kcat.reshape(b, t, kcat.shape[1])
    v3 = vup.reshape(b, t, wa)

    def make_call(c, out):
        n_keys = (c + 1) * tq
        row = lambda bi: (bi * nq + c, 0)
        return pl.pallas_call(
            functools.partial(_mla_prompt_kernel, n_heads, c * tq),
            grid=(b,),
            in_specs=[pl.BlockSpec((tq, qcat.shape[1]), row),
                      pl.BlockSpec((None, n_keys, kcat.shape[1]), lambda bi: (bi, 0, 0)),
                      pl.BlockSpec((None, n_keys, wa), lambda bi: (bi, 0, 0)),
                      pl.BlockSpec(memory_space=pl.ANY)],
            out_specs=pl.BlockSpec((tq, wa), row),
            out_shape=jax.ShapeDtypeStruct(out.shape, out.dtype),
            input_output_aliases={3: 0},
            compiler_params=_cparams(1),
            name="mla_prompt_%d" % c,
        )(qcat, k3, v3, out)

    return _chained_tile_calls(nq, make_call, out_init)


def _toeplitz_bias(tab_ref, h, start, n_rows, n_cols):
    shift0 = LANES - (n_rows - 1)
    from_left = (lax.broadcasted_iota(jnp.int32, (n_rows, LANES), 1)
                 < shift0 + lax.broadcasted_iota(jnp.int32, (n_rows, LANES), 0))
    pieces = []
    prev = None
    for n in range(n_cols // LANES + 1):
        seg = tab_ref[h:h + 1, start + n * LANES:start + (n + 1) * LANES]
        rot = pltpu.roll(jnp.broadcast_to(seg, (n_rows, LANES)), shift0 % LANES, 1, stride=1, stride_axis=0)
        if prev is not None:
            pieces.append(jnp.where(from_left, prev, rot))
        prev = rot
    return jnp.concatenate(pieces, axis=1)


def _dsa_prompt_kernel(n_heads, row0, t, topk, qb_ref, qi_ref, wi_ref, k_ref, v_ref, kiab_ref, tab_ref, prev_ref,
                       o_ref, key_ref, madd_ref, near_ref):
    del prev_ref
    per, tq, _ = qb_ref.shape
    n_keys = k_ref.shape[1]
    score = jnp.concatenate(
        [_indexer_scores(qi_ref.at[u], wi_ref[u], kiab_ref[u, :, :LANES], kiab_ref[u, :, LANES:])
         for u in range(per)], axis=0)
    adm = _chunk_madd(row0, tq, n_keys)
    _topk_mask(score, jnp.concatenate([adm] * per, axis=0), topk, key_ref, madd_ref)
    near0 = max(0, row0 - LANES)
    n_near = n_keys - near0

    @pl.when(pl.program_id(0) == 0)
    def _():
        for h in range(n_heads):
            near_ref[h] = jnp.concatenate(
                [_toeplitz_bias(tab_ref, h, near0 - (row0 + u * LANES) - LANES + t, LANES, n_near)
                 for u in range(tq // LANES)], axis=0)

    for h in range(n_heads):
        c0 = h * DSA_HEAD_DIM
        if near0 > 0:
            far = jnp.broadcast_to(tab_ref[h:h + 1, 0:1], (tq, near0))
            bias = jnp.concatenate([far, near_ref[h]], axis=1)
        else:
            bias = near_ref[h]
        for u in range(per):
            s = (_dot_nt(qb_ref[u, :, c0:c0 + DSA_HEAD_DIM], k_ref[u, :, c0:c0 + DSA_HEAD_DIM]) + bias
                 + madd_ref[u * tq:(u + 1) * tq, :])
            o_ref[u, :, c0:c0 + DSA_HEAD_DIM] = _mx(_softmax_pv(s, v_ref[u, :, c0:c0 + DSA_HEAD_DIM]))


def _dsa_prompt(qb, qi, wi, kbb, vbb, kiab, bias_tab, out_init, b, t, topk):
    tq = DSA_TQ
    assert MAX_DISTANCE <= LANES and tq % LANES == 0
    n_heads = qb.shape[1] // DSA_HEAD_DIM
    nq = t // tq
    wb = qb.shape[1]
    k3, v3, ki3 = kbb.reshape(b, t, wb), vbb.reshape(b, t, wb), kiab.reshape(b, t, 2 * LANES)
    qb3, qi3, wi3 = qb.reshape(b, t, wb), qi.reshape(b, t, qi.shape[1]), wi.reshape(b, t, LANES)
    keys_of = lambda bi: (bi, 0, 0)

    def make_call(c, out):
        n_keys = (c + 1) * tq
        per = max(p for p in (1, 2, 4) if p == 1 or (b % p == 0 and p * n_keys <= DSA_STEP_MAX_KEYS))
        rows_of = lambda bi: (bi, c, 0)
        return pl.pallas_call(
            functools.partial(_dsa_prompt_kernel, n_heads, c * tq, t, topk),
            grid=(b // per,),
            in_specs=[pl.BlockSpec((per, tq, wb), rows_of),
                      pl.BlockSpec((per, tq, qi.shape[1]), rows_of),
                      pl.BlockSpec((per, tq, LANES), rows_of),
                      pl.BlockSpec((per, n_keys, wb), keys_of),
                      pl.BlockSpec((per, n_keys, wb), keys_of),
                      pl.BlockSpec((per, n_keys, 2 * LANES), keys_of),
                      _resident(bias_tab.shape),
                      pl.BlockSpec(memory_space=pl.ANY)],
            out_specs=pl.BlockSpec((per, tq, wb), rows_of),
            out_shape=jax.ShapeDtypeStruct(out.shape, out.dtype),
            input_output_aliases={7: 0},
            scratch_shapes=[pltpu.VMEM((per * tq, n_keys), jnp.float32),
                            pltpu.VMEM((per * tq, n_keys), jnp.float32),
                            pltpu.VMEM((n_heads, tq, n_keys - max(0, c * tq - LANES)), jnp.float32)],
            compiler_params=_cparams(1),
            name="dsa_prompt_%d" % c,
        )(qb3, qi3, wi3, k3, v3, ki3, bias_tab, out)

    return _chained_tile_calls(nq, make_call, out_init.reshape(b, t, wb)).reshape(b * t, wb)


def _mla_sample_kernel(n_heads, past, ts, qcat_ref, cckv_ref, ckrt_ref, nckv_ref, nkr_ref, wuk_ref, wuv_ref,
                       o_ref, kall_ref, rt_ref, rnew_ref):
    n_keys = past + ts

    @pl.when(pl.program_id(0) == 0)
    def _():
        kall_ref[...] = jnp.zeros_like(kall_ref)
        rt_ref[...] = jnp.zeros_like(rt_ref)
        rnew_ref[...] = jnp.zeros_like(rnew_ref)

    for u in range(cckv_ref.shape[0]):
        rows = slice(u * ts, (u + 1) * ts)
        kall_ref[u, 0:past, :] = _mx(cckv_ref[u])
        kall_ref[u, past:n_keys, :] = _mx(nckv_ref[rows, :])
        rt_ref[u, 0:MLA_ROPE, :] = _mx(ckrt_ref[u])
        rnew_ref[u, 0:ts, 0:MLA_ROPE] = _mx(nkr_ref[rows, :])
        qlat, qrope = [], []
        for h in range(n_heads):
            c0 = h * HEAD_SLOT
            qlat.append(_mx(_dot(qcat_ref[rows, c0:c0 + LANES], wuk_ref[h])))
            qrope.append(qcat_ref[rows, c0 + LANES:c0 + HEAD_SLOT])
        qlat = jnp.concatenate(qlat, axis=0)
        qrope = jnp.concatenate(qrope, axis=0)
        s = _dot_nt(qlat, kall_ref[u]) + jnp.concatenate(
            [_dot(qrope, rt_ref[u]), _dot_nt(qrope, rnew_ref[u])], axis=1)
        col = lax.broadcasted_iota(jnp.int32, s.shape, 1)
        s = jnp.where(col < n_keys, s, NEG_BIG)
        olat = _mx(_softmax_pv(s, kall_ref[u]))
        for h in range(n_heads):
            o_ref[rows, h * MLA_V:(h + 1) * MLA_V] = _mx(_dot_nt(olat[h * ts:(h + 1) * ts], wuv_ref[h]))


def _mla_sample(qcat, cache_ckv, cache_kr_t, new_ckv, new_kr, wuk, wuv, ts):
    nb, past, c = cache_ckv.shape
    assert past % LANES == 0
    n_heads = qcat.shape[1] // HEAD_SLOT
    wa = n_heads * MLA_V
    s_pad = pl.cdiv(past + ts, LANES) * LANES
    per = max(p for p in (1, 2, 4) if nb % p == 0)
    row = lambda bi: (bi, 0)
    return pl.pallas_call(
        functools.partial(_mla_sample_kernel, n_heads, past, ts),
        grid=(nb // per,),
        in_specs=[pl.BlockSpec((per * ts, qcat.shape[1]), row),
                  pl.BlockSpec((per, past, c), lambda bi: (bi, 0, 0)),
                  pl.BlockSpec((per, MLA_ROPE, past), lambda bi: (bi, 0, 0)),
                  pl.BlockSpec((per * ts, c), row),
                  pl.BlockSpec((per * ts, MLA_ROPE), row),
                  _resident(wuk.shape), _resident(wuv.shape)],
        out_specs=pl.BlockSpec((per * ts, wa), row),
        out_shape=jax.ShapeDtypeStruct((nb * ts, wa), MXU_DT),
        scratch_shapes=[pltpu.VMEM((per, s_pad, c), MXU_DT), pltpu.VMEM((per, LANES, past), MXU_DT),
                        pltpu.VMEM((per, s_pad - past, LANES), MXU_DT)],
        compiler_params=_cparams(1),
        name="mla_sample",
    )(qcat, cache_ckv, cache_kr_t, new_ckv, new_kr, wuk, wuv)


def _dsa_select_kernel(past, ts, topk, qi_ref, wi_ref, ckit_ref, nkiab_ref, madd_ref,
                       kat_ref, kbt_ref, anew_ref, bnew_ref, key_ref):
    bi = pl.program_id(0)
    n_keys = past + ts
    rows, s_pad = madd_ref.shape

    @pl.when(bi == 0)
    def _():
        for ref in (kat_ref, kbt_ref, anew_ref, bnew_ref):
            ref[...] = jnp.zeros_like(ref)

    per = ckit_ref.shape[0]
    for u in range(per):
        r = slice(u * ts, (u + 1) * ts)
        ckit = _mx(ckit_ref[u])
        kat_ref[u, 0:IDX_DIM, :] = ckit
        kbt_ref[u, IDX_DIM:LANES, :] = ckit
        anew_ref[u, 0:ts, :] = nkiab_ref[r, :LANES]
        bnew_ref[u, 0:ts, :] = nkiab_ref[r, LANES:]
        madd_ref[pl.ds(pl.multiple_of((bi * per + u) * ts, ts), ts), :] = _indexer_scores_stacked(
            qi_ref.at[r], wi_ref[r, :],
            lambda q, u=u: jnp.concatenate([_dot(q, kat_ref[u]), _dot_nt(q, anew_ref[u])], axis=1),
            lambda q, u=u: jnp.concatenate([_dot(q, kbt_ref[u]), _dot_nt(q, bnew_ref[u])], axis=1))

    @pl.when(bi == pl.num_programs(0) - 1)
    def _():
        col = lax.broadcasted_iota(jnp.int32, (rows, s_pad), 1)
        _topk_mask(madd_ref[...], col < n_keys, topk, key_ref, madd_ref)


def _dsa_select(qi, wi, cache_ki_t, new_kiab, ts, topk):
    nb, _, past = cache_ki_t.shape
    assert past % LANES == 0
    s_pad = pl.cdiv(past + ts, LANES) * LANES
    per = max(p for p in (1, 2, 4) if nb % p == 0)
    row = lambda bi: (bi, 0)
    return pl.pallas_call(
        functools.partial(_dsa_select_kernel, past, ts, topk),
        grid=(nb // per,),
        in_specs=[pl.BlockSpec((per * ts, qi.shape[1]), row),
                  pl.BlockSpec((per * ts, LANES), row),
                  pl.BlockSpec((per, IDX_DIM, past), lambda bi: (bi, 0, 0)),
                  pl.BlockSpec((per * ts, 2 * LANES), row)],
        out_specs=pl.BlockSpec((nb * ts, s_pad), lambda bi: (0, 0)),
        out_shape=jax.ShapeDtypeStruct((nb * ts, s_pad), jnp.float32),
        scratch_shapes=[pltpu.VMEM((per, LANES, past), MXU_DT), pltpu.VMEM((per, LANES, past), MXU_DT),
                        pltpu.VMEM((per, s_pad - past, LANES), MXU_DT),
                        pltpu.VMEM((per, s_pad - past, LANES), MXU_DT),
                        pltpu.VMEM((nb * ts, s_pad), jnp.float32)],
        compiler_params=_cparams(1),
        name="dsa_select",
    )(qi, wi, cache_ki_t, new_kiab)


def _dsa_sample_kernel(n_heads, past, ts, qb_ref, madd_ref, ck_ref, cv_ref, nk_ref, nv_ref, tab_ref, expand_ref,
                       o_ref, kflat_ref, vflat_ref, bias_ref, biasw_ref):
    n_keys = past + ts
    wide = n_heads * LANES
    n_blocks = pl.cdiv(n_keys, LANES)
    widths = [min(wide, (n_keys - j * LANES) * n_heads) for j in range(n_blocks)]

    @pl.when(pl.program_id(0) == 0)
    def _():
        for h in range(n_heads):
            bias_ref[h * ts:(h + 1) * ts, :] = _toeplitz_bias(tab_ref, h, 0, ts, n_blocks * LANES)
        shape = (n_heads * ts, wide)
        same_head = (lax.broadcasted_iota(jnp.int32, shape, 0) // ts
                     == lax.broadcasted_iota(jnp.int32, shape, 1) % n_heads)
        for j in range(n_blocks):
            b = bias_ref[:, j * LANES:(j + 1) * LANES]
            hi = _mx(b)
            rest = b - hi.astype(jnp.float32)
            mid = _mx(rest)
            lo = _mx(rest - mid.astype(jnp.float32))
            piece = _dot(hi, expand_ref[...]) + _dot(mid, expand_ref[...]) + _dot(lo, expand_ref[...])
            biasw_ref[:, j * wide:j * wide + widths[j]] = jnp.where(same_head, piece, NEG_BIG)[:, :widths[j]]

    kflat_ref[0:past * n_heads, :] = _mx(ck_ref[...])
    kflat_ref[past * n_heads:, :] = nk_ref[...]
    vflat_ref[0:past * n_heads, :] = _mx(cv_ref[...])
    vflat_ref[past * n_heads:, :] = nv_ref[...]
    sel = _mx(jnp.where(madd_ref[...] == 0.0, 1.0, 0.0))
    pieces = [_dot(sel[:, j * LANES:(j + 1) * LANES], expand_ref[...])[:, :widths[j]] for j in range(n_blocks)]
    sel_wide = jnp.concatenate(pieces, axis=1)
    sel_wide = jnp.concatenate([sel_wide] * n_heads, axis=0)
    q_all = jnp.concatenate([qb_ref[:, h * DSA_HEAD_DIM:(h + 1) * DSA_HEAD_DIM] for h in range(n_heads)], axis=0)
    s = _dot_nt(q_all, kflat_ref[...]) + biasw_ref[...] + jnp.where(sel_wide > 0.5, 0.0, NEG_BIG)
    o = _mx(_softmax_pv(s, vflat_ref[...]))
    for h in range(n_heads):
        o_ref[:, h * DSA_HEAD_DIM:(h + 1) * DSA_HEAD_DIM] = o[h * ts:(h + 1) * ts]


def _dsa_sample(qb, madd, cache_k, cache_v, new_k, new_v, bias_tab, ts):
    nb, past, n_heads, _ = cache_k.shape
    wb = n_heads * DSA_HEAD_DIM
    n_keys = past + ts
    assert (n_keys % LANES * n_heads) % LANES == 0
    expand = _mx(jnp.repeat(jnp.eye(LANES, dtype=jnp.float32), n_heads, axis=1))
    row = lambda bi: (bi, 0)
    per_b = lambda bi: (bi, 0, 0)
    return pl.pallas_call(
        functools.partial(_dsa_sample_kernel, n_heads, past, ts),
        grid=(nb,),
        in_specs=[pl.BlockSpec((ts, wb), row),
                  pl.BlockSpec((ts, madd.shape[1]), row),
                  pl.BlockSpec((None, past * n_heads, DSA_HEAD_DIM), per_b),
                  pl.BlockSpec((None, past * n_heads, DSA_HEAD_DIM), per_b),
                  pl.BlockSpec((ts * n_heads, DSA_HEAD_DIM), row),
                  pl.BlockSpec((ts * n_heads, DSA_HEAD_DIM), row),
                  _resident(bias_tab.shape), _resident(expand.shape)],
        out_specs=pl.BlockSpec((ts, wb), row),
        out_shape=jax.ShapeDtypeStruct((nb * ts, wb), MXU_DT),
        scratch_shapes=[pltpu.VMEM((n_keys * n_heads, DSA_HEAD_DIM), MXU_DT),
                        pltpu.VMEM((n_keys * n_heads, DSA_HEAD_DIM), MXU_DT),
                        pltpu.VMEM((n_heads * ts, pl.cdiv(n_keys, LANES) * LANES), jnp.float32),
                        pltpu.VMEM((n_heads * ts, n_keys * n_heads), jnp.float32)],
        compiler_params=_cparams(1),
        name="dsa_sample",
    )(qb, madd, cache_k.reshape(nb, past * n_heads, DSA_HEAD_DIM), cache_v.reshape(nb, past * n_heads, DSA_HEAD_DIM),
      new_k.reshape(nb * ts * n_heads, DSA_HEAD_DIM), new_v.reshape(nb * ts * n_heads, DSA_HEAD_DIM),
      bias_tab, expand)


def _out_kernel(oa_ref, ob_ref, ga_ref, gb_ref, x_ref, gate_ref, gna_ref, gnb_ref, w_ref, fg_ref, y_ref):
    def gated(o_ref, g_ref, gain_ref):
        o = o_ref[...].astype(jnp.float32)
        on = o * lax.rsqrt(jnp.mean(o * o, axis=-1, keepdims=True) + EPS) * gain_ref[...]
        return _mx(on * g_ref[...].astype(jnp.float32))

    wa = oa_ref.shape[1]
    out = _dot(gated(oa_ref, ga_ref, gna_ref), w_ref[0:wa, :]) + _dot(gated(ob_ref, gb_ref, gnb_ref), w_ref[wa:, :])
    xn = x_ref[...] + gate_ref[...] * out
    y_ref[...] = xn * lax.rsqrt(jnp.mean(xn * xn, axis=-1, keepdims=True) + EPS) * fg_ref[...]


def _out_proj(oa, ob, ga, gb, x3, gate, gain_a, gain_b, w_out, final_gain, tm):
    bv, tv, d = x3.shape
    nt = tv // tm
    wa = oa.shape[1]
    x_spec, g_spec = _row_specs(x3, gate.shape[1], tm)
    row = lambda b, i: (b * nt + i, 0)
    return pl.pallas_call(
        _out_kernel,
        grid=(bv, nt),
        in_specs=[pl.BlockSpec((tm, wa), row), pl.BlockSpec((tm, wa), row),
                  pl.BlockSpec((tm, wa), row), pl.BlockSpec((tm, wa), row),
                  x_spec, g_spec, _resident((1, wa)), _resident((1, wa)),
                  _resident(w_out.shape), _resident((1, d))],
        out_specs=pl.BlockSpec((None, tm, d), lambda b, i: (b, i, 0)),
        out_shape=jax.ShapeDtypeStruct((bv, tv, d), jnp.float32),
        compiler_params=_cparams(2),
        name="out_proj",
    )(oa, ob, ga, gb, x3, gate, gain_a.reshape(1, wa), gain_b.reshape(1, wa), w_out, final_gain.reshape(1, d))


def _pack_kernel(n_heads, wt_ref, qa_ref, ckv_ref, kr_ref, ga_ref, gb_ref, qb_ref, kb_ref, vb_ref, qi_ref, kiw_ref):
    wa = n_heads * MLA_V
    half = MLA_ROPE // 2
    q_head = MLA_NOPE + MLA_ROPE
    step = 2 * LANES

    def panel(ref, off):
        for c in range(0, ref.shape[1], step):
            w = min(step, ref.shape[1] - c)
            ref[:, c:c + w] = _mx(wt_ref[off + c:off + c + w, :].T)

    def dup_rope(off):
        x1, x2 = wt_ref[off:off + half, :], wt_ref[off + half:off + 2 * half, :]
        return _mx(jnp.concatenate([x1, x2, x2, x1], axis=0).T)

    for h in range(n_heads):
        qa_ref[:, h * LANES:(h + 1) * LANES] = _mx(wt_ref[h * q_head:h * q_head + MLA_NOPE, :].T)
    for h in range(0, n_heads, 2):
        r0 = wt_ref[h * q_head + MLA_NOPE:(h + 1) * q_head, :]
        r1 = wt_ref[(h + 1) * q_head + MLA_NOPE:(h + 2) * q_head, :]
        dst = n_heads * MLA_NOPE + (h // 2) * LANES
        qa_ref[:, dst:dst + LANES] = _mx(jnp.concatenate([r0, r1], axis=0).T)
    off = n_heads * q_head
    panel(ckv_ref, off)
    off += KV_LORA
    kr_ref[...] = dup_rope(off)
    off += MLA_ROPE
    for ref in (ga_ref, qb_ref, kb_ref, vb_ref):
        panel(ref, off)
        off += wa
    pairs = IDX_HEADS // 2
    for j in range(pairs):
        lo = wt_ref[off + IDX_DIM * j:off + IDX_DIM * (j + 1), :]
        hi = wt_ref[off + IDX_DIM * (j + pairs):off + IDX_DIM * (j + pairs + 1), :]
        qi_ref[:, LANES * j:LANES * (j + 1)] = _mx(jnp.concatenate([lo, hi], axis=0).T)
    off += IDX_HEADS * IDX_DIM
    n_kw = IDX_DIM + IDX_HEADS
    kiw = jnp.concatenate([wt_ref[off:off + n_kw, :], jnp.zeros((LANES - n_kw, wt_ref.shape[1]), jnp.float32)], axis=0)
    kiw_ref[...] = _mx(kiw.T)
    off += n_kw
    panel(gb_ref, off)


def _pack_w_in(w_in, n_heads):
    wt = w_in.T
    n, d = wt.shape
    wa = n_heads * MLA_V
    assert n == n_heads * (MLA_NOPE + MLA_ROPE) + KV_LORA + MLA_ROPE + 5 * wa + IDX_HEADS * IDX_DIM + IDX_DIM + IDX_HEADS
    assert 2 * IDX_DIM == LANES and 2 * MLA_ROPE == LANES and MLA_NOPE == LANES
    slab = 256
    assert n_heads % 4 == 0
    widths = [n_heads * (MLA_NOPE + MLA_ROPE), KV_LORA, LANES, wa, wa, wa, wa, wa, IDX_HEADS * IDX_DIM, LANES]
    q_a, ckv, kr, g_a, g_b, q_b, k_b, v_b, q_i, kiw = pl.pallas_call(
        functools.partial(_pack_kernel, n_heads),
        grid=(d // slab,),
        in_specs=[pl.BlockSpec((n, slab), lambda i: (0, i))],
        out_specs=[pl.BlockSpec((slab, w), lambda i: (i, 0)) for w in widths],
        out_shape=[jax.ShapeDtypeStruct((d, w), MXU_DT) for w in widths],
        compiler_params=_cparams(1),
        name="pack_w_in",
    )(wt)
    return [q_a, ckv, kr, g_a, g_b], [q_b, k_b, v_b, q_i, kiw]


def _rope_table(pos):
    half = MLA_ROPE // 2
    freqs = jnp.power(ROPE_THETA, -jnp.arange(half, dtype=jnp.float32) / half)
    ang = pos.astype(jnp.float32)[:, None] * freqs
    cos, sin = jnp.cos(ang), jnp.sin(ang)
    return jnp.concatenate([cos, cos, -sin, sin], axis=1)


def _rel_bucket(rel):
    nb = N_BUCKETS // 2
    max_exact = nb // 2
    n = jnp.abs(rel)
    nf = jnp.maximum(n, 1).astype(jnp.float32)
    large = max_exact + (jnp.log(nf / max_exact) / math.log(MAX_DISTANCE / max_exact)
                         * (nb - max_exact)).astype(jnp.int32)
    large = jnp.minimum(large, nb - 1)
    return jnp.where(rel > 0, nb, 0) + jnp.where(n < max_exact, n, large)


def _bucket_bias(rel_bias, rel):
    return (rel_bias * LOG2E)[_rel_bucket(rel)]


def kernel(x_prompt, x_sample, cache_mla_ckv, cache_mla_krope, cache_dsa_k, cache_dsa_v, cache_idx_k,
           c_prompt, c_sample, w_ada, b_ada, ln_gain, w_in, mla_kv_gain, w_uk, w_uv, rel_bias,
           out_gain_a, out_gain_b, w_out, final_gain):
    assert w_ada.shape[0] == 1, "single-layer step"
    b, t, d = x_prompt.shape
    nb, ts, _ = x_sample.shape
    past = cache_mla_ckv.shape[2]
    n_heads = d // 256
    s_all = past + ts
    assert t % MLA_TQ == 0 and t % DSA_TQ == 0 and MLA_TQ % CHUNK == 0 and DSA_TQ % CHUNK == 0
    assert (s_all - 1) // CHUNK <= past // CHUNK

    mod = _ada_mod(jnp.concatenate([c_prompt, c_sample], axis=0), w_ada[0], b_ada[0])
    shift, scale, gate = mod[:, :d], mod[:, d:2 * d], mod[:, 2 * d:]
    mod_p = [v[:b].reshape(b, 1, d) for v in (scale, shift, gate)]
    mod_s = [jnp.broadcast_to(v[b:, None, :], (nb, ts, d)).reshape(1, nb * ts, d) for v in (scale, shift, gate)]

    w_a, w_b = _pack_w_in(w_in[0], n_heads)
    wkv = _mx(jnp.concatenate([w_uk[0].transpose(2, 0, 1).reshape(KV_LORA, n_heads * MLA_NOPE),
                               w_uv[0].transpose(2, 0, 1).reshape(KV_LORA, n_heads * MLA_V)], axis=1))
    w_o = _mx(w_out[0])
    pos_p = jnp.arange(t, dtype=jnp.int32)
    pos_s = past + jnp.arange(ts, dtype=jnp.int32)
    tab_p = _rope_table(pos_p)
    tab_s = jnp.tile(_rope_table(pos_s), (nb, 1))

    tm = PROJ_TM
    qcat, ckv_p, krope_p, ga, gb, kcat, vup, oa_buf = _proj_a(
        x_prompt, mod_p[0], mod_p[1], ln_gain[0], tab_p, w_a, mla_kv_gain[0], wkv, tm)
    qb, k_p, v_p, kbb, vbb, qi, idxk_p, kiab, wi, ob_buf = _proj_b(
        x_prompt, mod_p[0], mod_p[1], ln_gain[0], w_b, tm, True)
    o_a = _mla_prompt(qcat, kcat, vup, oa_buf, b, t)
    near = _bucket_bias(rel_bias, jnp.arange(-MAX_DISTANCE - 1, MAX_DISTANCE, dtype=jnp.int32))
    far = near[:1]
    bias_tab = jnp.concatenate([jnp.broadcast_to(far, (t - 1 - MAX_DISTANCE, n_heads)), near[1:],
                                jnp.broadcast_to(far, (t + 1 - MAX_DISTANCE, n_heads))], axis=0).T
    o_b = _dsa_prompt(qb, qi, wi, kbb, vbb, kiab, bias_tab, ob_buf, b, t, min(TOPK_MAX, t // 4))
    y_prompt = _out_proj(o_a, o_b, ga, gb, x_prompt, mod_p[2], out_gain_a[0], out_gain_b[0], w_o, final_gain, 2 * tm)

    xs3 = x_sample.reshape(1, nb * ts, d)
    tms = min(PROJ_TM, nb * ts)
    qcat_s, ckv_s, krope_s, ga_s, gb_s = _proj_a(
        xs3, mod_s[0], mod_s[1], ln_gain[0], tab_s, w_a, mla_kv_gain[0], None, tms)
    qb_s, k_s, v_s, kbb_s, vbb_s, qi_s, idxk_s, kiab_s, wi_s = _proj_b(
        xs3, mod_s[0], mod_s[1], ln_gain[0], w_b, tms, False)
    oa_s = _mla_sample(qcat_s, cache_mla_ckv[0], jnp.swapaxes(cache_mla_krope[0], 1, 2), ckv_s, krope_s,
                       _mx(w_uk[0]), _mx(w_uv[0]), ts)
    s_pad = pl.cdiv(s_all, LANES) * LANES
    tab_s = _bucket_bias(rel_bias, jnp.arange(s_pad + LANES, dtype=jnp.int32) - (s_all - 1)).T
    madd_s = _dsa_select(qi_s, wi_s, jnp.swapaxes(cache_idx_k[0], 1, 2), kiab_s, ts, min(TOPK_MAX, s_all // 4))
    ob_s = _dsa_sample(qb_s, madd_s, cache_dsa_k[0], cache_dsa_v[0], kbb_s, vbb_s, tab_s, ts)
    y_sample = _out_proj(oa_s, ob_s, ga_s, gb_s, xs3, mod_s[2], out_gain_a[0], out_gain_b[0], w_o, final_gain, tms)

    hd = (n_heads, DSA_HEAD_DIM)
    return (y_prompt, y_sample.reshape(nb, ts, d),
            ckv_p.reshape(1, b, t, KV_LORA), krope_p.reshape(1, b, t, MLA_ROPE),
            k_p.reshape(1, b, t, *hd), v_p.reshape(1, b, t, *hd), idxk_p.reshape(1, b, t, IDX_DIM),
            ckv_s.reshape(1, nb, ts, KV_LORA), krope_s.reshape(1, nb, ts, MLA_ROPE),
            k_s.reshape(1, nb, ts, *hd), v_s.reshape(1, nb, ts, *hd), idxk_s.reshape(1, nb, ts, IDX_DIM))
```

```python
import functools
import math

import jax
import jax.numpy as jnp
from jax import lax
from jax.experimental import pallas as pl
from jax.experimental.pallas import tpu as pltpu

MXU_DT = jnp.bfloat16

CHUNK = 64
MLA_NOPE = 128
MLA_ROPE = 64
MLA_V = 128
KV_LORA = 512
DSA_HEAD_DIM = 128
IDX_HEADS = 16
IDX_DIM = 64
TOPK_MAX = 256
N_BUCKETS = 32
MAX_DISTANCE = 128
ROPE_THETA = 10000.0
EPS = 1e-6

LANES = 128
HEAD_SLOT = 256
NEG_BIG = -1e30
LOG2E = math.log2(math.e)
INT_MIN = -2 ** 31
KEY_NEG_INF = INT_MIN + 0x7FFFFF
VMEM_LIMIT = 56 * 1024 * 1024
PROJ_TM = 256


def _cparams(n_grid, vmem=VMEM_LIMIT):
    return pltpu.CompilerParams(dimension_semantics=("arbitrary",) * n_grid, vmem_limit_bytes=vmem)


def _mx(v):
    return v.astype(MXU_DT)


def _dot(a, b):
    return jnp.dot(a, b, preferred_element_type=jnp.float32)


def _dot_nt(a, b):
    return lax.dot_general(a, b, (((1,), (1,)), ((), ())), preferred_element_type=jnp.float32)


def _silu(v):
    return v * (1.0 / (1.0 + jnp.exp(-v)))


def _resident(shape):
    nd = len(shape)
    return pl.BlockSpec(shape, lambda *_: (0,) * nd, pipeline_mode=pl.Buffered(1))


def _ada_kernel(c_ref, w_ref, b_ref, o_ref):
    a = _mx(_silu(c_ref[...]))
    o_ref[...] = _dot(a, _mx(w_ref[...])) + b_ref[...]


def _ada_mod(c_all, w_ada, b_ada):
    m, d = c_all.shape
    n = w_ada.shape[1]
    tn = 1024
    return pl.pallas_call(
        _ada_kernel,
        grid=(n // tn,),
        in_specs=[pl.BlockSpec((m, d), lambda j: (0, 0)),
                  pl.BlockSpec((d, tn), lambda j: (0, j)),
                  pl.BlockSpec((1, tn), lambda j: (0, j))],
        out_specs=pl.BlockSpec((m, tn), lambda j: (0, j)),
        out_shape=jax.ShapeDtypeStruct((m, n), jnp.float32),
        compiler_params=_cparams(1),
        name="ada_mod",
    )(c_all, w_ada, b_ada.reshape(1, n))


def _modulated_norm(x_ref, scale_ref, shift_ref, lng_ref):
    x = x_ref[...]
    xn = x * lax.rsqrt(jnp.mean(x * x, axis=-1, keepdims=True) + EPS) * lng_ref[...]
    return _mx(xn * (1.0 + scale_ref[...]) + shift_ref[...])


def _rope128(a, tab):
    t = a * tab
    return t + pltpu.roll(t, 64, 1)


def _proj_a_kernel(n_heads, emit_kv, x_ref, scale_ref, shift_ref, lng_ref, tab_ref,
                   wq_ref, wc_ref, wr_ref, wga_ref, wgb_ref, kvg_ref, *rest):
    if emit_kv:
        wkv_ref, qcat_ref, ckv_ref, krope_ref, ga_ref, gb_ref, kcat_ref, vup_ref, attn_ref = rest
        attn_ref[...] = jnp.zeros_like(attn_ref)
    else:
        qcat_ref, ckv_ref, krope_ref, ga_ref, gb_ref = rest
    hb = _modulated_norm(x_ref, scale_ref, shift_ref, lng_ref)
    tab = tab_ref[...]
    qscale = (MLA_NOPE + MLA_ROPE) ** -0.5 * LOG2E
    for h in range(0, n_heads, 2):
        a = _dot(hb, wq_ref[:, h * LANES:(h + 2) * LANES])
        qcat_ref[:, h * HEAD_SLOT:h * HEAD_SLOT + LANES] = _mx(a[:, :LANES] * qscale)
        qcat_ref[:, (h + 1) * HEAD_SLOT:(h + 1) * HEAD_SLOT + LANES] = _mx(a[:, LANES:] * qscale)
    lane = lax.broadcasted_iota(jnp.int32, tab.shape, 1)
    tab_r = pltpu.roll(tab, MLA_ROPE, 1)
    cos4 = jnp.where(lane < MLA_ROPE, tab, tab_r)
    sin4 = jnp.where(lane < MLA_ROPE, tab_r, tab)
    is_x1 = lane % MLA_ROPE < MLA_ROPE // 2
    n_nope = n_heads * MLA_NOPE
    for h in range(0, n_heads, 4):
        a = _dot(hb, wq_ref[:, n_nope + (h // 2) * LANES:n_nope + (h // 2 + 2) * LANES])
        for u in range(2):
            ap = a[:, u * LANES:(u + 1) * LANES]
            swapped = jnp.where(is_x1, pltpu.roll(ap, LANES - MLA_ROPE // 2, 1), pltpu.roll(ap, MLA_ROPE // 2, 1))
            roped = (ap * cos4 + swapped * sin4) * qscale
            c0 = (h + 2 * u) * HEAD_SLOT
            qcat_ref[:, c0 + LANES:c0 + HEAD_SLOT] = _mx(roped)
            qcat_ref[:, c0 + HEAD_SLOT + LANES:c0 + 2 * HEAD_SLOT] = _mx(pltpu.roll(roped, MLA_ROPE, 1))
    c = _dot(hb, wc_ref[...])
    cn = c * lax.rsqrt(jnp.mean(c * c, axis=-1, keepdims=True) + EPS) * kvg_ref[...]
    ckv_ref[...] = cn
    r = _rope128(_dot(hb, wr_ref[...]), tab)
    krope_ref[...] = r[:, :MLA_ROPE]
    ga_ref[...] = _mx(_silu(_dot(hb, wga_ref[...])))
    gb_ref[...] = _mx(_silu(_dot(hb, wgb_ref[...])))
    if emit_kv:
        cb = _mx(cn)
        lane = lax.broadcasted_iota(jnp.int32, r.shape, 1)
        krz = _mx(jnp.where(lane < MLA_ROPE, r, 0.0))
        kn = _dot(cb, wkv_ref[:, :n_heads * MLA_NOPE])
        for h in range(n_heads):
            c0 = h * HEAD_SLOT
            kcat_ref[:, c0:c0 + LANES] = _mx(kn[:, h * MLA_NOPE:(h + 1) * MLA_NOPE])
            kcat_ref[:, c0 + LANES:c0 + HEAD_SLOT] = krz
        vup_ref[...] = _mx(_dot(cb, wkv_ref[:, n_heads * MLA_NOPE:]))


def _proj_b_kernel(n_heads, emit_attn, x_ref, scale_ref, shift_ref, lng_ref, wq_ref, wk_ref, wv_ref, wqi_ref,
                   wkw_ref, qb_ref, kb_hbm, vb_hbm, kbb_ref, vbb_ref, qi_ref, ki_ref, kiab_ref, wi_ref, *rest):
    if emit_attn:
        rest[0][...] = jnp.zeros_like(rest[0])
    rows_ref, sem = rest[-2:]
    tm = x_ref.shape[0]
    n_steps = pl.num_programs(0) * pl.num_programs(1)
    step_id = pl.program_id(0) * pl.num_programs(1) + pl.program_id(1)
    slot = step_id % 2

    def row_copies(slot_, step_, which):
        dst = (kb_hbm, vb_hbm)[which]
        return [pltpu.make_async_copy(rows_ref.at[slot_, which, :, h * DSA_HEAD_DIM:(h + 1) * DSA_HEAD_DIM],
                                      dst.at[pl.ds(step_ * tm, tm), h, :], sem.at[slot_, which])
                for h in range(n_heads)]

    def wait_slot(slot_, step_):
        for which in range(2):
            for cp in row_copies(slot_, step_, which):
                cp.wait()

    @pl.when(step_id >= 2)
    def _():
        wait_slot(slot, step_id - 2)

    hb = _modulated_norm(x_ref, scale_ref, shift_ref, lng_ref)
    dscale = DSA_HEAD_DIM ** -0.5 * LOG2E
    step = 512
    for c0 in range(0, wq_ref.shape[1], step):
        qb_ref[:, c0:c0 + step] = _mx(_dot(hb, wq_ref[:, c0:c0 + step]) * dscale)
    for which, (w_ref, b_ref) in enumerate(((wk_ref, kbb_ref), (wv_ref, vbb_ref))):
        for c0 in range(0, w_ref.shape[1], step):
            kv = _dot(hb, w_ref[:, c0:c0 + step])
            rows_ref[slot, which, :, c0:c0 + step] = kv
            b_ref[:, c0:c0 + step] = _mx(kv)
        for cp in row_copies(slot, step_id, which):
            cp.start()

    @pl.when(step_id == n_steps - 1)
    def _():
        wait_slot(slot, step_id)

        @pl.when(n_steps >= 2)
        def _():
            wait_slot(1 - slot, step_id - 1)
    for c0 in range(0, wqi_ref.shape[1], step):
        qi_ref[:, c0:c0 + step] = _mx(_dot(hb, wqi_ref[:, c0:c0 + step]))
    a = _dot(hb, wkw_ref[...])
    ki_ref[...] = a[:, :IDX_DIM]
    lane = lax.broadcasted_iota(jnp.int32, a.shape, 1)
    kz = jnp.where(lane < IDX_DIM, a, 0.0)
    kiab_ref[:, :LANES] = _mx(kz)
    kiab_ref[:, LANES:] = _mx(pltpu.roll(kz, IDX_DIM, 1))
    wi_ref[...] = a * (IDX_HEADS ** -0.5 * IDX_DIM ** -0.5)


def _row_specs(x3, mod_rows, tm):
    bv, tv, d = x3.shape
    x_spec = pl.BlockSpec((None, tm, d), lambda b, i: (b, i, 0))
    if mod_rows == 1:
        m_spec = pl.BlockSpec((None, 1, d), lambda b, i: (b, 0, 0))
    else:
        m_spec = pl.BlockSpec((None, tm, d), lambda b, i: (b, i, 0))
    return x_spec, m_spec


def _out2d(m, width, dtype, tm, nt):
    return (jax.ShapeDtypeStruct((m, width), dtype),
            pl.BlockSpec((tm, width), lambda b, i: (b * nt + i, 0)))


def _proj_a(x3, scale, shift, ln_gain, tab, ws_a, kv_gain, wkv, tm):
    bv, tv, d = x3.shape
    nt = tv // tm
    m = bv * tv
    n_heads = d // 256
    wa = n_heads * MLA_V
    emit_kv = wkv is not None
    x_spec, m_spec = _row_specs(x3, scale.shape[1], tm)
    in_specs = ([x_spec, m_spec, m_spec, _resident((1, d)), pl.BlockSpec((tm, LANES), lambda b, i: (i, 0))]
                + [_resident(w.shape) for w in ws_a] + [_resident((1, KV_LORA))])
    args = [x3, scale, shift, ln_gain.reshape(1, d), tab, *ws_a, kv_gain.reshape(1, KV_LORA)]
    outs = [_out2d(m, n_heads * HEAD_SLOT, MXU_DT, tm, nt),
            _out2d(m, KV_LORA, jnp.float32, tm, nt),
            _out2d(m, MLA_ROPE, jnp.float32, tm, nt),
            _out2d(m, wa, MXU_DT, tm, nt),
            _out2d(m, wa, MXU_DT, tm, nt)]
    if emit_kv:
        in_specs.append(_resident(wkv.shape))
        args.append(wkv)
        outs += [_out2d(m, n_heads * HEAD_SLOT, MXU_DT, tm, nt),
                 _out2d(m, wa, MXU_DT, tm, nt),
                 _out2d(m, wa, MXU_DT, tm, nt)]
    return pl.pallas_call(
        functools.partial(_proj_a_kernel, n_heads, emit_kv),
        grid=(bv, nt),
        in_specs=in_specs,
        out_specs=[o[1] for o in outs],
        out_shape=[o[0] for o in outs],
        compiler_params=_cparams(2),
        name="proj_a",
    )(*args)


def _proj_b(x3, scale, shift, ln_gain, ws_b, tm, emit_attn_buffer):
    bv, tv, d = x3.shape
    nt = tv // tm
    m = bv * tv
    n_heads = d // 256
    width_b = n_heads * DSA_HEAD_DIM
    x_spec, m_spec = _row_specs(x3, scale.shape[1], tm)
    cache_rows = (jax.ShapeDtypeStruct((m, n_heads, DSA_HEAD_DIM), jnp.float32), pl.BlockSpec(memory_space=pl.ANY))
    outs = [_out2d(m, width_b, MXU_DT, tm, nt),
            cache_rows,
            cache_rows,
            _out2d(m, width_b, MXU_DT, tm, nt),
            _out2d(m, width_b, MXU_DT, tm, nt),
            _out2d(m, IDX_HEADS * IDX_DIM, MXU_DT, tm, nt),
            _out2d(m, IDX_DIM, jnp.float32, tm, nt),
            _out2d(m, 2 * LANES, MXU_DT, tm, nt),
            _out2d(m, LANES, jnp.float32, tm, nt)]
    if emit_attn_buffer:
        outs.append(_out2d(m, width_b, MXU_DT, tm, nt))
    return pl.pallas_call(
        functools.partial(_proj_b_kernel, n_heads, emit_attn_buffer),
        grid=(bv, nt),
        in_specs=[x_spec, m_spec, m_spec, _resident((1, d))] + [_resident(w.shape) for w in ws_b],
        out_specs=[o[1] for o in outs],
        out_shape=[o[0] for o in outs],
        scratch_shapes=[pltpu.VMEM((2, 2, tm, width_b), jnp.float32), pltpu.SemaphoreType.DMA((2, 2))],
        compiler_params=_cparams(2),
        name="proj_b",
    )(x3, scale, shift, ln_gain.reshape(1, d), *ws_b)


def _softmax_pv(s, v):
    m = jnp.max(s, axis=-1, keepdims=True)
    p = jnp.exp2(s - m)
    l = jnp.sum(p, axis=-1, keepdims=True)
    return _dot(_mx(p), v) * (1.0 / l)


def _topk_mask(score, adm, topk, sc_ref, madd_ref):
    rows, n = score.shape
    if n <= topk:
        madd_ref[...] = jnp.where(adm, 0.0, NEG_BIG)
        return
    sc_ref[...] = jnp.where(adm, score, -jnp.inf)
    kf = float(topk)

    def count(pred):
        return jnp.sum(jnp.where(pred, 1.0, 0.0), axis=-1, keepdims=True)

    def key_to_float(key):
        return pltpu.bitcast(key ^ ((key >> 31) & 0x7FFFFFFF), jnp.float32)

    def bit_step(i, thr):
        inc = lax.shift_left(jnp.int32(1), 31 - i)
        cand = thr + inc
        ok = count(sc_ref[...] >= key_to_float(cand)) >= kf
        return jnp.where(ok, cand, thr)

    thr = lax.fori_loop(0, 32, bit_step, jnp.full((rows, 1), INT_MIN, jnp.int32))
    thr = key_to_float(jnp.maximum(thr, KEY_NEG_INF))
    ge = sc_ref[...] >= thr
    madd_ref[...] = jnp.where(ge, 0.0, NEG_BIG)

    @pl.when(jnp.max(count(ge)) > kf)
    def _():
        sc = sc_ref[...]
        gt = sc > thr
        need = kf - count(gt)
        eqf = jnp.where(sc == thr, 1.0, 0.0)
        col = lax.broadcasted_iota(jnp.int32, (rows, n), 1)
        nbits = max(1, int(n).bit_length())

        def col_step(i, bound):
            cand = bound + lax.shift_left(jnp.int32(1), nbits - 1 - i)
            taken = jnp.sum(jnp.where(col < cand, eqf, 0.0), axis=-1, keepdims=True)
            return jnp.where(taken <= need, cand, bound)

        bound = lax.fori_loop(0, nbits, col_step, jnp.zeros((rows, 1), jnp.int32))
        tie_madd = jnp.where(jnp.where(col < bound, eqf, 0.0) > 0.5, 0.0, NEG_BIG)
        madd_ref[...] = jnp.where(adm, jnp.where(gt, 0.0, tie_madd), NEG_BIG)


def _indexer_scores(qi_ref, wi, kia, kib):
    half = IDX_HEADS // 2
    score = None
    for j in range(half):
        qp = qi_ref[:, j * LANES:(j + 1) * LANES]
        da = jnp.maximum(_dot_nt(qp, kia), 0.0) * wi[:, IDX_DIM + j:IDX_DIM + j + 1]
        db = jnp.maximum(_dot_nt(qp, kib), 0.0) * wi[:, IDX_DIM + half + j:IDX_DIM + half + j + 1]
        score = da + db if score is None else score + da + db
    return score


def _indexer_scores_stacked(qi_ref, wi, dots_a, dots_b):
    half = IDX_HEADS // 2
    ts = qi_ref.shape[0]
    q_all = jnp.concatenate([qi_ref[:, j * LANES:(j + 1) * LANES] for j in range(half)], axis=0)
    wa = jnp.concatenate([wi[:, IDX_DIM + j:IDX_DIM + j + 1] for j in range(half)], axis=0)
    wb = jnp.concatenate([wi[:, IDX_DIM + half + j:IDX_DIM + half + j + 1] for j in range(half)], axis=0)
    part = jnp.maximum(dots_a(q_all), 0.0) * wa + jnp.maximum(dots_b(q_all), 0.0) * wb
    score = part[0:ts]
    for j in range(1, half):
        score = score + part[j * ts:(j + 1) * ts]
    return score


MLA_TQ = 512
MLA_PAIR_MAX_KEYS = 512
DSA_TQ = 256
DSA_STEP_MAX_KEYS = 2048


def _chunk_madd(row0, tq, n_keys):
    qc = (row0 + lax.broadcasted_iota(jnp.int32, (tq, n_keys), 0)) // CHUNK
    kc = lax.broadcasted_iota(jnp.int32, (tq, n_keys), 1) // CHUNK
    return kc <= qc


def _chained_tile_calls(n_tiles, make_call, out):
    for c in range(n_tiles):
        out = make_call(c, out)
    return out


def _mla_prompt_kernel(n_heads, row0, q_ref, k_ref, v_ref, prev_ref, o_ref):
    del prev_ref
    per, tq, _ = q_ref.shape
    n_keys = k_ref.shape[1]
    madd = jnp.where(_chunk_madd(row0, tq, n_keys), 0.0, NEG_BIG)
    for h in range(n_heads):
        c0 = h * HEAD_SLOT
        for u in range(per):
            s = _dot_nt(q_ref[u, :, c0:c0 + HEAD_SLOT], k_ref[u, :, c0:c0 + HEAD_SLOT]) + madd
            o_ref[u, :, h * MLA_V:(h + 1) * MLA_V] = _mx(_softmax_pv(s, v_ref[u, :, h * MLA_V:(h + 1) * MLA_V]))


def _mla_prompt(qcat, kcat, vup, out_init, b, t):
    tq = MLA_TQ
    n_heads = qcat.shape[1] // HEAD_SLOT
    nq = t // tq
    wa = n_heads * MLA_V
    q3 = qcat.reshape(b, t, qcat.shape[1])
    k3 = kcat.reshape(b, t, kcat.shape[1])
    v3 = vup.reshape(b, t, wa)
    keys_of = lambda bi: (bi, 0, 0)

    def make_call(c, out):
        n_keys = (c + 1) * tq
        per = 2 if (b % 2 == 0 and n_keys <= MLA_PAIR_MAX_KEYS) else 1
        rows_of = lambda bi: (bi, c, 0)
        return pl.pallas_call(
            functools.partial(_mla_prompt_kernel, n_heads, c * tq),
            grid=(b // per,),
            in_specs=[pl.BlockSpec((per, tq, qcat.shape[1]), rows_of),
                      pl.BlockSpec((per, n_keys, kcat.shape[1]), keys_of),
                      pl.BlockSpec((per, n_keys, wa), keys_of),
                      pl.BlockSpec(memory_space=pl.ANY)],
            out_specs=pl.BlockSpec((per, tq, wa), rows_of),
            out_shape=jax.ShapeDtypeStruct(out.shape, out.dtype),
            input_output_aliases={3: 0},
            compiler_params=_cparams(1),
            name="mla_prompt_%d" % c,
        )(q3, k3, v3, out)

    return _chained_tile_calls(nq, make_call, out_init.reshape(b, t, wa)).reshape(b * t, wa)


def _toeplitz_bias(tab_ref, h, start, n_rows, n_cols):
    shift0 = LANES - (n_rows - 1)
    from_left = (lax.broadcasted_iota(jnp.int32, (n_rows, LANES), 1)
                 < shift0 + lax.broadcasted_iota(jnp.int32, (n_rows, LANES), 0))
    pieces = []
    prev = None
    for n in range(n_cols // LANES + 1):
        seg = tab_ref[h:h + 1, start + n * LANES:start + (n + 1) * LANES]
        rot = pltpu.roll(jnp.broadcast_to(seg, (n_rows, LANES)), shift0 % LANES, 1, stride=1, stride_axis=0)
        if prev is not None:
            pieces.append(jnp.where(from_left, prev, rot))
        prev = rot
    return jnp.concatenate(pieces, axis=1)


def _dsa_prompt_kernel(n_heads, row0, t, topk, qb_ref, qi_ref, wi_ref, k_ref, v_ref, kiab_ref, tab_ref, prev_ref,
                       o_ref, key_ref, madd_ref, near_ref):
    del prev_ref
    per, tq, _ = qb_ref.shape
    n_keys = k_ref.shape[1]
    score = jnp.concatenate(
        [_indexer_scores(qi_ref.at[u], wi_ref[u], kiab_ref[u, :, :LANES], kiab_ref[u, :, LANES:])
         for u in range(per)], axis=0)
    adm = _chunk_madd(row0, tq, n_keys)
    _topk_mask(score, jnp.concatenate([adm] * per, axis=0), topk, key_ref, madd_ref)
    near0 = max(0, row0 - LANES)
    n_near = n_keys - near0

    @pl.when(pl.program_id(0) == 0)
    def _():
        for h in range(n_heads):
            near_ref[h] = jnp.concatenate(
                [_toeplitz_bias(tab_ref, h, near0 - (row0 + u * LANES) - LANES + t, LANES, n_near)
                 for u in range(tq // LANES)], axis=0)

    for h in range(n_heads):
        c0 = h * DSA_HEAD_DIM
        if near0 > 0:
            far = jnp.broadcast_to(tab_ref[h:h + 1, 0:1], (tq, near0))
            bias = jnp.concatenate([far, near_ref[h]], axis=1)
        else:
            bias = near_ref[h]
        for u in range(per):
            s = (_dot_nt(qb_ref[u, :, c0:c0 + DSA_HEAD_DIM], k_ref[u, :, c0:c0 + DSA_HEAD_DIM]) + bias
                 + madd_ref[u * tq:(u + 1) * tq, :])
            o_ref[u, :, c0:c0 + DSA_HEAD_DIM] = _mx(_softmax_pv(s, v_ref[u, :, c0:c0 + DSA_HEAD_DIM]))


def _dsa_prompt(qb, qi, wi, kbb, vbb, kiab, bias_tab, out_init, b, t, topk):
    tq = DSA_TQ
    assert MAX_DISTANCE <= LANES and tq % LANES == 0
    n_heads = qb.shape[1] // DSA_HEAD_DIM
    nq = t // tq
    wb = qb.shape[1]
    k3, v3, ki3 = kbb.reshape(b, t, wb), vbb.reshape(b, t, wb), kiab.reshape(b, t, 2 * LANES)
    qb3, qi3, wi3 = qb.reshape(b, t, wb), qi.reshape(b, t, qi.shape[1]), wi.reshape(b, t, LANES)
    keys_of = lambda bi: (bi, 0, 0)

    def make_call(c, out):
        n_keys = (c + 1) * tq
        per = max(p for p in (1, 2, 4) if p == 1 or (b % p == 0 and p * n_keys <= DSA_STEP_MAX_KEYS))
        rows_of = lambda bi: (bi, c, 0)
        return pl.pallas_call(
            functools.partial(_dsa_prompt_kernel, n_heads, c * tq, t, topk),
            grid=(b // per,),
            in_specs=[pl.BlockSpec((per, tq, wb), rows_of),
                      pl.BlockSpec((per, tq, qi.shape[1]), rows_of),
                      pl.BlockSpec((per, tq, LANES), rows_of),
                      pl.BlockSpec((per, n_keys, wb), keys_of),
                      pl.BlockSpec((per, n_keys, wb), keys_of),
                      pl.BlockSpec((per, n_keys, 2 * LANES), keys_of),
                      _resident(bias_tab.shape),
                      pl.BlockSpec(memory_space=pl.ANY)],
            out_specs=pl.BlockSpec((per, tq, wb), rows_of),
            out_shape=jax.ShapeDtypeStruct(out.shape, out.dtype),
            input_output_aliases={7: 0},
            scratch_shapes=[pltpu.VMEM((per * tq, n_keys), jnp.float32),
                            pltpu.VMEM((per * tq, n_keys), jnp.float32),
                            pltpu.VMEM((n_heads, tq, n_keys - max(0, c * tq - LANES)), jnp.float32)],
            compiler_params=_cparams(1),
            name="dsa_prompt_%d" % c,
        )(qb3, qi3, wi3, k3, v3, ki3, bias_tab, out)

    return _chained_tile_calls(nq, make_call, out_init.reshape(b, t, wb)).reshape(b * t, wb)


def _mla_sample_kernel(n_heads, past, ts, qcat_ref, cckv_ref, ckrt_ref, nckv_ref, nkr_ref, wuk_ref, wuv_ref,
                       o_ref, kall_ref, rt_ref, rnew_ref):
    n_keys = past + ts

    @pl.when(pl.program_id(0) == 0)
    def _():
        kall_ref[...] = jnp.zeros_like(kall_ref)
        rt_ref[...] = jnp.zeros_like(rt_ref)
        rnew_ref[...] = jnp.zeros_like(rnew_ref)

    for u in range(cckv_ref.shape[0]):
        rows = slice(u * ts, (u + 1) * ts)
        kall_ref[u, 0:past, :] = _mx(cckv_ref[u])
        kall_ref[u, past:n_keys, :] = _mx(nckv_ref[rows, :])
        rt_ref[u, 0:MLA_ROPE, :] = _mx(ckrt_ref[u])
        rnew_ref[u, 0:ts, 0:MLA_ROPE] = _mx(nkr_ref[rows, :])
        qlat, qrope = [], []
        for h in range(n_heads):
            c0 = h * HEAD_SLOT
            qlat.append(_mx(_dot(qcat_ref[rows, c0:c0 + LANES], wuk_ref[h])))
            qrope.append(qcat_ref[rows, c0 + LANES:c0 + HEAD_SLOT])
        qlat = jnp.concatenate(qlat, axis=0)
        qrope = jnp.concatenate(qrope, axis=0)
        s = _dot_nt(qlat, kall_ref[u]) + jnp.concatenate(
            [_dot(qrope, rt_ref[u]), _dot_nt(qrope, rnew_ref[u])], axis=1)
        col = lax.broadcasted_iota(jnp.int32, s.shape, 1)
        s = jnp.where(col < n_keys, s, NEG_BIG)
        olat = _mx(_softmax_pv(s, kall_ref[u]))
        for h in range(n_heads):
            o_ref[rows, h * MLA_V:(h + 1) * MLA_V] = _mx(_dot_nt(olat[h * ts:(h + 1) * ts], wuv_ref[h]))


def _mla_sample(qcat, cache_ckv, cache_kr_t, new_ckv, new_kr, wuk, wuv, ts):
    nb, past, c = cache_ckv.shape
    assert past % LANES == 0
    n_heads = qcat.shape[1] // HEAD_SLOT
    wa = n_heads * MLA_V
    s_pad = pl.cdiv(past + ts, LANES) * LANES
    per = max(p for p in (1, 2, 4) if nb % p == 0)
    row = lambda bi: (bi, 0)
    return pl.pallas_call(
        functools.partial(_mla_sample_kernel, n_heads, past, ts),
        grid=(nb // per,),
        in_specs=[pl.BlockSpec((per * ts, qcat.shape[1]), row),
                  pl.BlockSpec((per, past, c), lambda bi: (bi, 0, 0)),
                  pl.BlockSpec((per, MLA_ROPE, past), lambda bi: (bi, 0, 0)),
                  pl.BlockSpec((per * ts, c), row),
                  pl.BlockSpec((per * ts, MLA_ROPE), row),
                  _resident(wuk.shape), _resident(wuv.shape)],
        out_specs=pl.BlockSpec((per * ts, wa), row),
        out_shape=jax.ShapeDtypeStruct((nb * ts, wa), MXU_DT),
        scratch_shapes=[pltpu.VMEM((per, s_pad, c), MXU_DT), pltpu.VMEM((per, LANES, past), MXU_DT),
                        pltpu.VMEM((per, s_pad - past, LANES), MXU_DT)],
        compiler_params=_cparams(1),
        name="mla_sample",
    )(qcat, cache_ckv, cache_kr_t, new_ckv, new_kr, wuk, wuv)


def _dsa_select_kernel(past, ts, topk, qi_ref, wi_ref, ckit_ref, nkiab_ref, madd_ref,
                       kat_ref, kbt_ref, anew_ref, bnew_ref, key_ref):
    bi = pl.program_id(0)
    n_keys = past + ts
    rows, s_pad = madd_ref.shape

    @pl.when(bi == 0)
    def _():
        for ref in (kat_ref, kbt_ref, anew_ref, bnew_ref):
            ref[...] = jnp.zeros_like(ref)

    per = ckit_ref.shape[0]
    for u in range(per):
        r = slice(u * ts, (u + 1) * ts)
        ckit = _mx(ckit_ref[u])
        kat_ref[u, 0:IDX_DIM, :] = ckit
        kbt_ref[u, IDX_DIM:LANES, :] = ckit
        anew_ref[u, 0:ts, :] = nkiab_ref[r, :LANES]
        bnew_ref[u, 0:ts, :] = nkiab_ref[r, LANES:]
        madd_ref[pl.ds(pl.multiple_of((bi * per + u) * ts, ts), ts), :] = _indexer_scores_stacked(
            qi_ref.at[r], wi_ref[r, :],
            lambda q, u=u: jnp.concatenate([_dot(q, kat_ref[u]), _dot_nt(q, anew_ref[u])], axis=1),
            lambda q, u=u: jnp.concatenate([_dot(q, kbt_ref[u]), _dot_nt(q, bnew_ref[u])], axis=1))

    @pl.when(bi == pl.num_programs(0) - 1)
    def _():
        col = lax.broadcasted_iota(jnp.int32, (rows, s_pad), 1)
        _topk_mask(madd_ref[...], col < n_keys, topk, key_ref, madd_ref)


def _dsa_select(qi, wi, cache_ki_t, new_kiab, ts, topk):
    nb, _, past = cache_ki_t.shape
    assert past % LANES == 0
    s_pad = pl.cdiv(past + ts, LANES) * LANES
    per = max(p for p in (1, 2, 4) if nb % p == 0)
    row = lambda bi: (bi, 0)
    return pl.pallas_call(
        functools.partial(_dsa_select_kernel, past, ts, topk),
        grid=(nb // per,),
        in_specs=[pl.BlockSpec((per * ts, qi.shape[1]), row),
                  pl.BlockSpec((per * ts, LANES), row),
                  pl.BlockSpec((per, IDX_DIM, past), lambda bi: (bi, 0, 0)),
                  pl.BlockSpec((per * ts, 2 * LANES), row)],
        out_specs=pl.BlockSpec((nb * ts, s_pad), lambda bi: (0, 0)),
        out_shape=jax.ShapeDtypeStruct((nb * ts, s_pad), jnp.float32),
        scratch_shapes=[pltpu.VMEM((per, LANES, past), MXU_DT), pltpu.VMEM((per, LANES, past), MXU_DT),
                        pltpu.VMEM((per, s_pad - past, LANES), MXU_DT),
                        pltpu.VMEM((per, s_pad - past, LANES), MXU_DT),
                        pltpu.VMEM((nb * ts, s_pad), jnp.float32)],
        compiler_params=_cparams(1),
        name="dsa_select",
    )(qi, wi, cache_ki_t, new_kiab)


def _dsa_sample_kernel(n_heads, past, ts, qb_ref, madd_ref, ck_ref, cv_ref, nk_ref, nv_ref, tab_ref, expand_ref,
                       o_ref, kflat_ref, vflat_ref, bias_ref, biasw_ref):
    n_keys = past + ts
    wide = n_heads * LANES
    n_blocks = pl.cdiv(n_keys, LANES)
    widths = [min(wide, (n_keys - j * LANES) * n_heads) for j in range(n_blocks)]

    @pl.when(pl.program_id(0) == 0)
    def _():
        for h in range(n_heads):
            bias_ref[h * ts:(h + 1) * ts, :] = _toeplitz_bias(tab_ref, h, 0, ts, n_blocks * LANES)
        shape = (n_heads * ts, wide)
        same_head = (lax.broadcasted_iota(jnp.int32, shape, 0) // ts
                     == lax.broadcasted_iota(jnp.int32, shape, 1) % n_heads)
        for j in range(n_blocks):
            b = bias_ref[:, j * LANES:(j + 1) * LANES]
            hi = _mx(b)
            rest = b - hi.astype(jnp.float32)
            mid = _mx(rest)
            lo = _mx(rest - mid.astype(jnp.float32))
            piece = _dot(hi, expand_ref[...]) + _dot(mid, expand_ref[...]) + _dot(lo, expand_ref[...])
            biasw_ref[:, j * wide:j * wide + widths[j]] = jnp.where(same_head, piece, NEG_BIG)[:, :widths[j]]

    kflat_ref[0:past * n_heads, :] = _mx(ck_ref[...])
    kflat_ref[past * n_heads:, :] = nk_ref[...]
    vflat_ref[0:past * n_heads, :] = _mx(cv_ref[...])
    vflat_ref[past * n_heads:, :] = nv_ref[...]
    sel = _mx(jnp.where(madd_ref[...] == 0.0, 1.0, 0.0))
    pieces = [_dot(sel[:, j * LANES:(j + 1) * LANES], expand_ref[...])[:, :widths[j]] for j in range(n_blocks)]
    sel_wide = jnp.concatenate(pieces, axis=1)
    sel_wide = jnp.concatenate([sel_wide] * n_heads, axis=0)
    q_all = jnp.concatenate([qb_ref[:, h * DSA_HEAD_DIM:(h + 1) * DSA_HEAD_DIM] for h in range(n_heads)], axis=0)
    s = _dot_nt(q_all, kflat_ref[...]) + biasw_ref[...] + jnp.where(sel_wide > 0.5, 0.0, NEG_BIG)
    o = _mx(_softmax_pv(s, vflat_ref[...]))
    for h in range(n_heads):
        o_ref[:, h * DSA_HEAD_DIM:(h + 1) * DSA_HEAD_DIM] = o[h * ts:(h + 1) * ts]


def _dsa_sample(qb, madd, cache_k, cache_v, new_k, new_v, bias_tab, ts):
    nb, past, n_heads, _ = cache_k.shape
    wb = n_heads * DSA_HEAD_DIM
    n_keys = past + ts
    assert (n_keys % LANES * n_heads) % LANES == 0
    expand = _mx(jnp.repeat(jnp.eye(LANES, dtype=jnp.float32), n_heads, axis=1))
    row = lambda bi: (bi, 0)
    per_b = lambda bi: (bi, 0, 0)
    return pl.pallas_call(
        functools.partial(_dsa_sample_kernel, n_heads, past, ts),
        grid=(nb,),
        in_specs=[pl.BlockSpec((ts, wb), row),
                  pl.BlockSpec((ts, madd.shape[1]), row),
                  pl.BlockSpec((None, past * n_heads, DSA_HEAD_DIM), per_b),
                  pl.BlockSpec((None, past * n_heads, DSA_HEAD_DIM), per_b),
                  pl.BlockSpec((ts * n_heads, DSA_HEAD_DIM), row),
                  pl.BlockSpec((ts * n_heads, DSA_HEAD_DIM), row),
                  _resident(bias_tab.shape), _resident(expand.shape)],
        out_specs=pl.BlockSpec((ts, wb), row),
        out_shape=jax.ShapeDtypeStruct((nb * ts, wb), MXU_DT),
        scratch_shapes=[pltpu.VMEM((n_keys * n_heads, DSA_HEAD_DIM), MXU_DT),
                        pltpu.VMEM((n_keys * n_heads, DSA_HEAD_DIM), MXU_DT),
                        pltpu.VMEM((n_heads * ts, pl.cdiv(n_keys, LANES) * LANES), jnp.float32),
                        pltpu.VMEM((n_heads * ts, n_keys * n_heads), jnp.float32)],
        compiler_params=_cparams(1),
        name="dsa_sample",
    )(qb, madd, cache_k.reshape(nb, past * n_heads, DSA_HEAD_DIM), cache_v.reshape(nb, past * n_heads, DSA_HEAD_DIM),
      new_k.reshape(nb * ts * n_heads, DSA_HEAD_DIM), new_v.reshape(nb * ts * n_heads, DSA_HEAD_DIM),
      bias_tab, expand)


def _out_kernel(oa_ref, ob_ref, ga_ref, gb_ref, x_ref, gate_ref, gna_ref, gnb_ref, w_ref, fg_ref, y_ref):
    def gated(o_ref, g_ref, gain_ref):
        o = o_ref[...].astype(jnp.float32)
        on = o * lax.rsqrt(jnp.mean(o * o, axis=-1, keepdims=True) + EPS) * gain_ref[...]
        return _mx(on * g_ref[...].astype(jnp.float32))

    wa = oa_ref.shape[1]
    out = _dot(gated(oa_ref, ga_ref, gna_ref), w_ref[0:wa, :]) + _dot(gated(ob_ref, gb_ref, gnb_ref), w_ref[wa:, :])
    xn = x_ref[...] + gate_ref[...] * out
    y_ref[...] = xn * lax.rsqrt(jnp.mean(xn * xn, axis=-1, keepdims=True) + EPS) * fg_ref[...]


def _out_proj(oa, ob, ga, gb, x3, gate, gain_a, gain_b, w_out, final_gain, tm):
    bv, tv, d = x3.shape
    nt = tv // tm
    wa = oa.shape[1]
    x_spec, g_spec = _row_specs(x3, gate.shape[1], tm)
    row = lambda b, i: (b * nt + i, 0)
    return pl.pallas_call(
        _out_kernel,
        grid=(bv, nt),
        in_specs=[pl.BlockSpec((tm, wa), row), pl.BlockSpec((tm, wa), row),
                  pl.BlockSpec((tm, wa), row), pl.BlockSpec((tm, wa), row),
                  x_spec, g_spec, _resident((1, wa)), _resident((1, wa)),
                  _resident(w_out.shape), _resident((1, d))],
        out_specs=pl.BlockSpec((None, tm, d), lambda b, i: (b, i, 0)),
        out_shape=jax.ShapeDtypeStruct((bv, tv, d), jnp.float32),
        compiler_params=_cparams(2),
        name="out_proj",
    )(oa, ob, ga, gb, x3, gate, gain_a.reshape(1, wa), gain_b.reshape(1, wa), w_out, final_gain.reshape(1, d))


def _pack_kernel(n_heads, wt_ref, qa_ref, ckv_ref, kr_ref, ga_ref, gb_ref, qb_ref, kb_ref, vb_ref, qi_ref, kiw_ref):
    wa = n_heads * MLA_V
    half = MLA_ROPE // 2
    q_head = MLA_NOPE + MLA_ROPE
    step = 2 * LANES

    def panel(ref, off):
        for c in range(0, ref.shape[1], step):
            w = min(step, ref.shape[1] - c)
            ref[:, c:c + w] = _mx(wt_ref[off + c:off + c + w, :].T)

    def dup_rope(off):
        x1, x2 = wt_ref[off:off + half, :], wt_ref[off + half:off + 2 * half, :]
        return _mx(jnp.concatenate([x1, x2, x2, x1], axis=0).T)

    for h in range(n_heads):
        qa_ref[:, h * LANES:(h + 1) * LANES] = _mx(wt_ref[h * q_head:h * q_head + MLA_NOPE, :].T)
    for h in range(0, n_heads, 2):
        r0 = wt_ref[h * q_head + MLA_NOPE:(h + 1) * q_head, :]
        r1 = wt_ref[(h + 1) * q_head + MLA_NOPE:(h + 2) * q_head, :]
        dst = n_heads * MLA_NOPE + (h // 2) * LANES
        qa_ref[:, dst:dst + LANES] = _mx(jnp.concatenate([r0, r1], axis=0).T)
    off = n_heads * q_head
    panel(ckv_ref, off)
    off += KV_LORA
    kr_ref[...] = dup_rope(off)
    off += MLA_ROPE
    for ref in (ga_ref, qb_ref, kb_ref, vb_ref):
        panel(ref, off)
        off += wa
    pairs = IDX_HEADS // 2
    for j in range(pairs):
        lo = wt_ref[off + IDX_DIM * j:off + IDX_DIM * (j + 1), :]
        hi = wt_ref[off + IDX_DIM * (j + pairs):off + IDX_DIM * (j + pairs + 1), :]
        qi_ref[:, LANES * j:LANES * (j + 1)] = _mx(jnp.concatenate([lo, hi], axis=0).T)
    off += IDX_HEADS * IDX_DIM
    n_kw = IDX_DIM + IDX_HEADS
    kiw = jnp.concatenate([wt_ref[off:off + n_kw, :], jnp.zeros((LANES - n_kw, wt_ref.shape[1]), jnp.float32)], axis=0)
    kiw_ref[...] = _mx(kiw.T)
    off += n_kw
    panel(gb_ref, off)


def _pack_w_in(w_in, n_heads):
    wt = w_in.T
    n, d = wt.shape
    wa = n_heads * MLA_V
    assert n == n_heads * (MLA_NOPE + MLA_ROPE) + KV_LORA + MLA_ROPE + 5 * wa + IDX_HEADS * IDX_DIM + IDX_DIM + IDX_HEADS
    assert 2 * IDX_DIM == LANES and 2 * MLA_ROPE == LANES and MLA_NOPE == LANES
    slab = 256
    assert n_heads % 4 == 0
    widths = [n_heads * (MLA_NOPE + MLA_ROPE), KV_LORA, LANES, wa, wa, wa, wa, wa, IDX_HEADS * IDX_DIM, LANES]
    q_a, ckv, kr, g_a, g_b, q_b, k_b, v_b, q_i, kiw = pl.pallas_call(
        functools.partial(_pack_kernel, n_heads),
        grid=(d // slab,),
        in_specs=[pl.BlockSpec((n, slab), lambda i: (0, i))],
        out_specs=[pl.BlockSpec((slab, w), lambda i: (i, 0)) for w in widths],
        out_shape=[jax.ShapeDtypeStruct((d, w), MXU_DT) for w in widths],
        compiler_params=_cparams(1),
        name="pack_w_in",
    )(wt)
    return [q_a, ckv, kr, g_a, g_b], [q_b, k_b, v_b, q_i, kiw]


def _rope_table(pos):
    half = MLA_ROPE // 2
    freqs = jnp.power(ROPE_THETA, -jnp.arange(half, dtype=jnp.float32) / half)
    ang = pos.astype(jnp.float32)[:, None] * freqs
    cos, sin = jnp.cos(ang), jnp.sin(ang)
    return jnp.concatenate([cos, cos, -sin, sin], axis=1)


def _rel_bucket(rel):
    nb = N_BUCKETS // 2
    max_exact = nb // 2
    n = jnp.abs(rel)
    nf = jnp.maximum(n, 1).astype(jnp.float32)
    large = max_exact + (jnp.log(nf / max_exact) / math.log(MAX_DISTANCE / max_exact)
                         * (nb - max_exact)).astype(jnp.int32)
    large = jnp.minimum(large, nb - 1)
    return jnp.where(rel > 0, nb, 0) + jnp.where(n < max_exact, n, large)


def _bucket_bias(rel_bias, rel):
    return (rel_bias * LOG2E)[_rel_bucket(rel)]


def kernel(x_prompt, x_sample, cache_mla_ckv, cache_mla_krope, cache_dsa_k, cache_dsa_v, cache_idx_k,
           c_prompt, c_sample, w_ada, b_ada, ln_gain, w_in, mla_kv_gain, w_uk, w_uv, rel_bias,
           out_gain_a, out_gain_b, w_out, final_gain):
    assert w_ada.shape[0] == 1, "single-layer step"
    b, t, d = x_prompt.shape
    nb, ts, _ = x_sample.shape
    past = cache_mla_ckv.shape[2]
    n_heads = d // 256
    s_all = past + ts
    assert t % MLA_TQ == 0 and t % DSA_TQ == 0 and MLA_TQ % CHUNK == 0 and DSA_TQ % CHUNK == 0
    assert (s_all - 1) // CHUNK <= past // CHUNK

    mod = _ada_mod(jnp.concatenate([c_prompt, c_sample], axis=0), w_ada[0], b_ada[0])
    shift, scale, gate = mod[:, :d], mod[:, d:2 * d], mod[:, 2 * d:]
    mod_p = [v[:b].reshape(b, 1, d) for v in (scale, shift, gate)]
    mod_s = [jnp.broadcast_to(v[b:, None, :], (nb, ts, d)).reshape(1, nb * ts, d) for v in (scale, shift, gate)]

    w_a, w_b = _pack_w_in(w_in[0], n_heads)
    wkv = _mx(jnp.concatenate([w_uk[0].transpose(2, 0, 1).reshape(KV_LORA, n_heads * MLA_NOPE),
                               w_uv[0].transpose(2, 0, 1).reshape(KV_LORA, n_heads * MLA_V)], axis=1))
    w_o = _mx(w_out[0])
    pos_p = jnp.arange(t, dtype=jnp.int32)
    pos_s = past + jnp.arange(ts, dtype=jnp.int32)
    tab_p = _rope_table(pos_p)
    tab_s = jnp.tile(_rope_table(pos_s), (nb, 1))

    tm = PROJ_TM
    qcat, ckv_p, krope_p, ga, gb, kcat, vup, oa_buf = _proj_a(
        x_prompt, mod_p[0], mod_p[1], ln_gain[0], tab_p, w_a, mla_kv_gain[0], wkv, tm)
    qb, k_p, v_p, kbb, vbb, qi, idxk_p, kiab, wi, ob_buf = _proj_b(
        x_prompt, mod_p[0], mod_p[1], ln_gain[0], w_b, tm, True)
    o_a = _mla_prompt(qcat, kcat, vup, oa_buf, b, t)
    near = _bucket_bias(rel_bias, jnp.arange(-MAX_DISTANCE - 1, MAX_DISTANCE, dtype=jnp.int32))
    far = near[:1]
    bias_tab = jnp.concatenate([jnp.broadcast_to(far, (t - 1 - MAX_DISTANCE, n_heads)), near[1:],
                                jnp.broadcast_to(far, (t + 1 - MAX_DISTANCE, n_heads))], axis=0).T
    o_b = _dsa_prompt(qb, qi, wi, kbb, vbb, kiab, bias_tab, ob_buf, b, t, min(TOPK_MAX, t // 4))
    y_prompt = _out_proj(o_a, o_b, ga, gb, x_prompt, mod_p[2], out_gain_a[0], out_gain_b[0], w_o, final_gain, 2 * tm)

    xs3 = x_sample.reshape(1, nb * ts, d)
    tms = min(PROJ_TM, nb * ts)
    qcat_s, ckv_s, krope_s, ga_s, gb_s = _proj_a(
        xs3, mod_s[0], mod_s[1], ln_gain[0], tab_s, w_a, mla_kv_gain[0], None, tms)
    qb_s, k_s, v_s, kbb_s, vbb_s, qi_s, idxk_s, kiab_s, wi_s = _proj_b(
        xs3, mod_s[0], mod_s[1], ln_gain[0], w_b, tms, False)
    oa_s = _mla_sample(qcat_s, cache_mla_ckv[0], jnp.swapaxes(cache_mla_krope[0], 1, 2), ckv_s, krope_s,
                       _mx(w_uk[0]), _mx(w_uv[0]), ts)
    s_pad = pl.cdiv(s_all, LANES) * LANES
    tab_s = _bucket_bias(rel_bias, jnp.arange(s_pad + LANES, dtype=jnp.int32) - (s_all - 1)).T
    madd_s = _dsa_select(qi_s, wi_s, jnp.swapaxes(cache_idx_k[0], 1, 2), kiab_s, ts, min(TOPK_MAX, s_all // 4))
    ob_s = _dsa_sample(qb_s, madd_s, cache_dsa_k[0], cache_dsa_v[0], kbb_s, vbb_s, tab_s, ts)
    y_sample = _out_proj(oa_s, ob_s, ga_s, gb_s, xs3, mod_s[2], out_gain_a[0], out_gain_b[0], w_o, final_gain, tms)

    hd = (n_heads, DSA_HEAD_DIM)
    return (y_prompt, y_sample.reshape(nb, ts, d),
            ckv_p.reshape(1, b, t, KV_LORA), krope_p.reshape(1, b, t, MLA_ROPE),
            k_p.reshape(1, b, t, *hd), v_p.reshape(1, b, t, *hd), idxk_p.reshape(1, b, t, IDX_DIM),
            ckv_s.reshape(1, nb, ts, KV_LORA), krope_s.reshape(1, nb, ts, MLA_ROPE),
            k_s.reshape(1, nb, ts, *hd), v_s.reshape(1, nb, ts, *hd), idxk_s.reshape(1, nb, ts, IDX_DIM))
```

```python
import functools
import math

import jax
import jax.numpy as jnp
from jax import lax
from jax.experimental import pallas as pl
from jax.experimental.pallas import tpu as pltpu

MXU_DT = jnp.bfloat16

CHUNK = 64
MLA_NOPE = 128
MLA_ROPE = 64
MLA_V = 128
KV_LORA = 512
DSA_HEAD_DIM = 128
IDX_HEADS = 16
IDX_DIM = 64
TOPK_MAX = 256
N_BUCKETS = 32
MAX_DISTANCE = 128
ROPE_THETA = 10000.0
EPS = 1e-6

LANES = 128
HEAD_SLOT = 256
NEG_BIG = -1e30
LOG2E = math.log2(math.e)
INT_MIN = -2 ** 31
KEY_NEG_INF = INT_MIN + 0x7FFFFF
VMEM_LIMIT = 56 * 1024 * 1024
CACHE_RING = 3
PROJ_TM = 256


def _cparams(n_grid, vmem=VMEM_LIMIT):
    return pltpu.CompilerParams(dimension_semantics=("arbitrary",) * n_grid, vmem_limit_bytes=vmem)


def _mx(v):
    return v.astype(MXU_DT)


def _dot(a, b):
    return jnp.dot(a, b, preferred_element_type=jnp.float32)


def _dot_nt(a, b):
    return lax.dot_general(a, b, (((1,), (1,)), ((), ())), preferred_element_type=jnp.float32)


def _silu(v):
    return v * (1.0 / (1.0 + jnp.exp(-v)))


def _resident(shape):
    nd = len(shape)
    return pl.BlockSpec(shape, lambda *_: (0,) * nd, pipeline_mode=pl.Buffered(1))


def _ada_kernel(c_ref, w_ref, b_ref, o_ref):
    a = _mx(_silu(c_ref[...]))
    o_ref[...] = _dot(a, _mx(w_ref[...])) + b_ref[...]


def _ada_mod(c_all, w_ada, b_ada):
    m, d = c_all.shape
    n = w_ada.shape[1]
    tn = 1024
    return pl.pallas_call(
        _ada_kernel,
        grid=(n // tn,),
        in_specs=[pl.BlockSpec((m, d), lambda j: (0, 0)),
                  pl.BlockSpec((d, tn), lambda j: (0, j)),
                  pl.BlockSpec((1, tn), lambda j: (0, j))],
        out_specs=pl.BlockSpec((m, tn), lambda j: (0, j)),
        out_shape=jax.ShapeDtypeStruct((m, n), jnp.float32),
        compiler_params=_cparams(1),
        name="ada_mod",
    )(c_all, w_ada, b_ada.reshape(1, n))


def _modulated_norm(x_ref, scale_ref, shift_ref, lng_ref):
    x = x_ref[...]
    xn = x * lax.rsqrt(jnp.mean(x * x, axis=-1, keepdims=True) + EPS) * lng_ref[...]
    return _mx(xn * (1.0 + scale_ref[...]) + shift_ref[...])


def _rope128(a, tab):
    t = a * tab
    return t + pltpu.roll(t, 64, 1)


def _proj_a_kernel(n_heads, emit_kv, x_ref, scale_ref, shift_ref, lng_ref, tab_ref,
                   wq_ref, wc_ref, wr_ref, wga_ref, wgb_ref, kvg_ref, *rest):
    if emit_kv:
        wkv_ref, qcat_ref, ckv_ref, krope_ref, ga_ref, gb_ref, kcat_ref, vup_ref, attn_ref = rest
        attn_ref[...] = jnp.zeros_like(attn_ref)
    else:
        qcat_ref, ckv_ref, krope_ref, ga_ref, gb_ref = rest
    hb = _modulated_norm(x_ref, scale_ref, shift_ref, lng_ref)
    tab = tab_ref[...]
    qscale = (MLA_NOPE + MLA_ROPE) ** -0.5 * LOG2E
    for h in range(0, n_heads, 2):
        a = _dot(hb, wq_ref[:, h * LANES:(h + 2) * LANES])
        qcat_ref[:, h * HEAD_SLOT:h * HEAD_SLOT + LANES] = _mx(a[:, :LANES] * qscale)
        qcat_ref[:, (h + 1) * HEAD_SLOT:(h + 1) * HEAD_SLOT + LANES] = _mx(a[:, LANES:] * qscale)
    lane = lax.broadcasted_iota(jnp.int32, tab.shape, 1)
    tab_r = pltpu.roll(tab, MLA_ROPE, 1)
    cos4 = jnp.where(lane < MLA_ROPE, tab, tab_r)
    sin4 = jnp.where(lane < MLA_ROPE, tab_r, tab)
    is_x1 = lane % MLA_ROPE < MLA_ROPE // 2
    n_nope = n_heads * MLA_NOPE
    for h in range(0, n_heads, 4):
        a = _dot(hb, wq_ref[:, n_nope + (h // 2) * LANES:n_nope + (h // 2 + 2) * LANES])
        for u in range(2):
            ap = a[:, u * LANES:(u + 1) * LANES]
            swapped = jnp.where(is_x1, pltpu.roll(ap, LANES - MLA_ROPE // 2, 1), pltpu.roll(ap, MLA_ROPE // 2, 1))
            roped = (ap * cos4 + swapped * sin4) * qscale
            c0 = (h + 2 * u) * HEAD_SLOT
            qcat_ref[:, c0 + LANES:c0 + HEAD_SLOT] = _mx(roped)
            qcat_ref[:, c0 + HEAD_SLOT + LANES:c0 + 2 * HEAD_SLOT] = _mx(pltpu.roll(roped, MLA_ROPE, 1))
    c = _dot(hb, wc_ref[...])
    cn = c * lax.rsqrt(jnp.mean(c * c, axis=-1, keepdims=True) + EPS) * kvg_ref[...]
    ckv_ref[...] = cn
    r = _rope128(_dot(hb, wr_ref[...]), tab)
    krope_ref[...] = r[:, :MLA_ROPE]
    ga_ref[...] = _mx(_silu(_dot(hb, wga_ref[...])))
    gb_ref[...] = _mx(_silu(_dot(hb, wgb_ref[...])))
    if emit_kv:
        cb = _mx(cn)
        lane = lax.broadcasted_iota(jnp.int32, r.shape, 1)
        krz = _mx(jnp.where(lane < MLA_ROPE, r, 0.0))
        kn = _dot(cb, wkv_ref[:, :n_heads * MLA_NOPE])
        for h in range(n_heads):
            c0 = h * HEAD_SLOT
            kcat_ref[:, c0:c0 + LANES] = _mx(kn[:, h * MLA_NOPE:(h + 1) * MLA_NOPE])
            kcat_ref[:, c0 + LANES:c0 + HEAD_SLOT] = krz
        vup_ref[...] = _mx(_dot(cb, wkv_ref[:, n_heads * MLA_NOPE:]))


def _proj_b_kernel(n_heads, emit_attn, x_ref, scale_ref, shift_ref, lng_ref, wq_ref, wk_ref, wv_ref, wqi_ref,
                   wkw_ref, qb_ref, kb_hbm, vb_hbm, kbb_ref, vbb_ref, qi_ref, ki_ref, kiab_ref, wi_ref, *rest):
    if emit_attn:
        rest[0][...] = jnp.zeros_like(rest[0])
    rows_ref, sem = rest[-2:]
    tm = x_ref.shape[0]
    n_steps = pl.num_programs(0) * pl.num_programs(1)
    step_id = pl.program_id(0) * pl.num_programs(1) + pl.program_id(1)
    slot = step_id % 2

    def row_copies(slot_, step_, which):
        dst = (kb_hbm, vb_hbm)[which]
        return [pltpu.make_async_copy(rows_ref.at[slot_, which, :, h * DSA_HEAD_DIM:(h + 1) * DSA_HEAD_DIM],
                                      dst.at[pl.ds(step_ * tm, tm), h, :], sem.at[slot_, which])
                for h in range(n_heads)]

    def wait_slot(slot_, step_):
        for which in range(2):
            for cp in row_copies(slot_, step_, which):
                cp.wait()

    @pl.when(step_id >= 2)
    def _():
        wait_slot(slot, step_id - 2)

    hb = _modulated_norm(x_ref, scale_ref, shift_ref, lng_ref)
    dscale = DSA_HEAD_DIM ** -0.5 * LOG2E
    step = 512
    for c0 in range(0, wq_ref.shape[1], step):
        qb_ref[:, c0:c0 + step] = _mx(_dot(hb, wq_ref[:, c0:c0 + step]) * dscale)
    for which, (w_ref, b_ref) in enumerate(((wk_ref, kbb_ref), (wv_ref, vbb_ref))):
        for c0 in range(0, w_ref.shape[1], step):
            kv = _dot(hb, w_ref[:, c0:c0 + step])
            rows_ref[slot, which, :, c0:c0 + step] = kv
            b_ref[:, c0:c0 + step] = _mx(kv)
        for cp in row_copies(slot, step_id, which):
            cp.start()

    @pl.when(step_id == n_steps - 1)
    def _():
        wait_slot(slot, step_id)

        @pl.when(n_steps >= 2)
        def _():
            wait_slot(1 - slot, step_id - 1)
    for c0 in range(0, wqi_ref.shape[1], step):
        qi_ref[:, c0:c0 + step] = _mx(_dot(hb, wqi_ref[:, c0:c0 + step]))
    a = _dot(hb, wkw_ref[...])
    ki_ref[...] = a[:, :IDX_DIM]
    lane = lax.broadcasted_iota(jnp.int32, a.shape, 1)
    kz = jnp.where(lane < IDX_DIM, a, 0.0)
    kiab_ref[:, :LANES] = _mx(kz)
    kiab_ref[:, LANES:] = _mx(pltpu.roll(kz, IDX_DIM, 1))
    wi_ref[...] = a * (IDX_HEADS ** -0.5 * IDX_DIM ** -0.5)


def _row_specs(x3, mod_rows, tm):
    bv, tv, d = x3.shape
    x_spec = pl.BlockSpec((None, tm, d), lambda b, i: (b, i, 0))
    if mod_rows == 1:
        m_spec = pl.BlockSpec((None, 1, d), lambda b, i: (b, 0, 0))
    else:
        m_spec = pl.BlockSpec((None, tm, d), lambda b, i: (b, i, 0))
    return x_spec, m_spec


def _out2d(m, width, dtype, tm, nt):
    return (jax.ShapeDtypeStruct((m, width), dtype),
            pl.BlockSpec((tm, width), lambda b, i: (b * nt + i, 0)))


def _proj_a(x3, scale, shift, ln_gain, tab, ws_a, kv_gain, wkv, tm):
    bv, tv, d = x3.shape
    nt = tv // tm
    m = bv * tv
    n_heads = d // 256
    wa = n_heads * MLA_V
    emit_kv = wkv is not None
    x_spec, m_spec = _row_specs(x3, scale.shape[1], tm)
    in_specs = ([x_spec, m_spec, m_spec, _resident((1, d)), pl.BlockSpec((tm, LANES), lambda b, i: (i, 0))]
                + [_resident(w.shape) for w in ws_a] + [_resident((1, KV_LORA))])
    args = [x3, scale, shift, ln_gain.reshape(1, d), tab, *ws_a, kv_gain.reshape(1, KV_LORA)]
    outs = [_out2d(m, n_heads * HEAD_SLOT, MXU_DT, tm, nt),
            _out2d(m, KV_LORA, jnp.float32, tm, nt),
            _out2d(m, MLA_ROPE, jnp.float32, tm, nt),
            _out2d(m, wa, MXU_DT, tm, nt),
            _out2d(m, wa, MXU_DT, tm, nt)]
    if emit_kv:
        in_specs.append(_resident(wkv.shape))
        args.append(wkv)
        outs += [_out2d(m, n_heads * HEAD_SLOT, MXU_DT, tm, nt),
                 _out2d(m, wa, MXU_DT, tm, nt),
                 _out2d(m, wa, MXU_DT, tm, nt)]
    return pl.pallas_call(
        functools.partial(_proj_a_kernel, n_heads, emit_kv),
        grid=(bv, nt),
        in_specs=in_specs,
        out_specs=[o[1] for o in outs],
        out_shape=[o[0] for o in outs],
        compiler_params=_cparams(2),
        name="proj_a",
    )(*args)


def _proj_b(x3, scale, shift, ln_gain, ws_b, tm, emit_attn_buffer):
    bv, tv, d = x3.shape
    nt = tv // tm
    m = bv * tv
    n_heads = d // 256
    width_b = n_heads * DSA_HEAD_DIM
    x_spec, m_spec = _row_specs(x3, scale.shape[1], tm)
    cache_rows = (jax.ShapeDtypeStruct((m, n_heads, DSA_HEAD_DIM), jnp.float32), pl.BlockSpec(memory_space=pl.ANY))
    outs = [_out2d(m, width_b, MXU_DT, tm, nt),
            cache_rows,
            cache_rows,
            _out2d(m, width_b, MXU_DT, tm, nt),
            _out2d(m, width_b, MXU_DT, tm, nt),
            _out2d(m, IDX_HEADS * IDX_DIM, MXU_DT, tm, nt),
            _out2d(m, IDX_DIM, jnp.float32, tm, nt),
            _out2d(m, 2 * LANES, MXU_DT, tm, nt),
            _out2d(m, LANES, jnp.float32, tm, nt)]
    if emit_attn_buffer:
        outs.append(_out2d(m, width_b, MXU_DT, tm, nt))
    return pl.pallas_call(
        functools.partial(_proj_b_kernel, n_heads, emit_attn_buffer),
        grid=(bv, nt),
        in_specs=[x_spec, m_spec, m_spec, _resident((1, d))] + [_resident(w.shape) for w in ws_b],
        out_specs=[o[1] for o in outs],
        out_shape=[o[0] for o in outs],
        scratch_shapes=[pltpu.VMEM((2, 2, tm, width_b), jnp.float32), pltpu.SemaphoreType.DMA((2, 2))],
        compiler_params=_cparams(2),
        name="proj_b",
    )(x3, scale, shift, ln_gain.reshape(1, d), *ws_b)


def _softmax_pv(s, v):
    m = jnp.max(s, axis=-1, keepdims=True)
    p = jnp.exp2(s - m)
    l = jnp.sum(p, axis=-1, keepdims=True)
    return _dot(_mx(p), v) * (1.0 / l)


def _topk_mask(score, adm, topk, sc_ref, madd_ref):
    rows, n = score.shape
    if n <= topk:
        madd_ref[...] = jnp.where(adm, 0.0, NEG_BIG)
        return
    sc_ref[...] = jnp.where(adm, score, -jnp.inf)
    kf = float(topk)

    def count(pred):
        return jnp.sum(jnp.where(pred, 1.0, 0.0), axis=-1, keepdims=True)

    def key_to_float(key):
        return pltpu.bitcast(key ^ ((key >> 31) & 0x7FFFFFFF), jnp.float32)

    def bit_step(i, thr):
        inc = lax.shift_left(jnp.int32(1), 31 - i)
        cand = thr + inc
        ok = count(sc_ref[...] >= key_to_float(cand)) >= kf
        return jnp.where(ok, cand, thr)

    thr = lax.fori_loop(0, 32, bit_step, jnp.full((rows, 1), INT_MIN, jnp.int32))
    thr = key_to_float(jnp.maximum(thr, KEY_NEG_INF))
    ge = sc_ref[...] >= thr
    madd_ref[...] = jnp.where(ge, 0.0, NEG_BIG)

    @pl.when(jnp.max(count(ge)) > kf)
    def _():
        sc = sc_ref[...]
        gt = sc > thr
        need = kf - count(gt)
        eqf = jnp.where(sc == thr, 1.0, 0.0)
        col = lax.broadcasted_iota(jnp.int32, (rows, n), 1)
        nbits = max(1, int(n).bit_length())

        def col_step(i, bound):
            cand = bound + lax.shift_left(jnp.int32(1), nbits - 1 - i)
            taken = jnp.sum(jnp.where(col < cand, eqf, 0.0), axis=-1, keepdims=True)
            return jnp.where(taken <= need, cand, bound)

        bound = lax.fori_loop(0, nbits, col_step, jnp.zeros((rows, 1), jnp.int32))
        tie_madd = jnp.where(jnp.where(col < bound, eqf, 0.0) > 0.5, 0.0, NEG_BIG)
        madd_ref[...] = jnp.where(adm, jnp.where(gt, 0.0, tie_madd), NEG_BIG)


def _indexer_scores(qi_ref, wi, kia, kib):
    half = IDX_HEADS // 2
    score = None
    for j in range(half):
        qp = qi_ref[:, j * LANES:(j + 1) * LANES]
        da = jnp.maximum(_dot_nt(qp, kia), 0.0) * wi[:, IDX_DIM + j:IDX_DIM + j + 1]
        db = jnp.maximum(_dot_nt(qp, kib), 0.0) * wi[:, IDX_DIM + half + j:IDX_DIM + half + j + 1]
        score = da + db if score is None else score + da + db
    return score


def _indexer_scores_stacked(qi_ref, wi, dots_a, dots_b):
    half = IDX_HEADS // 2
    ts = qi_ref.shape[0]
    q_all = jnp.concatenate([qi_ref[:, j * LANES:(j + 1) * LANES] for j in range(half)], axis=0)
    wa = jnp.concatenate([wi[:, IDX_DIM + j:IDX_DIM + j + 1] for j in range(half)], axis=0)
    wb = jnp.concatenate([wi[:, IDX_DIM + half + j:IDX_DIM + half + j + 1] for j in range(half)], axis=0)
    part = jnp.maximum(dots_a(q_all), 0.0) * wa + jnp.maximum(dots_b(q_all), 0.0) * wb
    score = part[0:ts]
    for j in range(1, half):
        score = score + part[j * ts:(j + 1) * ts]
    return score


MLA_TQ = 512
DSA_TQ = 256
DSA_STEP_MAX_KEYS = 2048


def _chunk_madd(row0, tq, n_keys):
    qc = (row0 + lax.broadcasted_iota(jnp.int32, (tq, n_keys), 0)) // CHUNK
    kc = lax.broadcasted_iota(jnp.int32, (tq, n_keys), 1) // CHUNK
    return kc <= qc


def _chained_tile_calls(n_tiles, make_call, out):
    for c in range(n_tiles):
        out = make_call(c, out)
    return out


def _mla_prompt_kernel(n_heads, row0, q_ref, k_ref, v_ref, prev_ref, o_ref):
    del prev_ref
    tq, n_keys = q_ref.shape[0], k_ref.shape[0]
    madd = jnp.where(_chunk_madd(row0, tq, n_keys), 0.0, NEG_BIG)
    for h in range(n_heads):
        c0 = h * HEAD_SLOT
        s = _dot_nt(q_ref[:, c0:c0 + HEAD_SLOT], k_ref[:, c0:c0 + HEAD_SLOT]) + madd
        o_ref[:, h * MLA_V:(h + 1) * MLA_V] = _mx(_softmax_pv(s, v_ref[:, h * MLA_V:(h + 1) * MLA_V]))


def _mla_prompt(qcat, kcat, vup, out_init, b, t):
    tq = MLA_TQ
    n_heads = qcat.shape[1] // HEAD_SLOT
    nq = t // tq
    wa = n_heads * MLA_V
    k3 = kcat.reshape(b, t, kcat.shape[1])
    v3 = vup.reshape(b, t, wa)

    def make_call(c, out):
        n_keys = (c + 1) * tq
        row = lambda bi: (bi * nq + c, 0)
        return pl.pallas_call(
            functools.partial(_mla_prompt_kernel, n_heads, c * tq),
            grid=(b,),
            in_specs=[pl.BlockSpec((tq, qcat.shape[1]), row),
                      pl.BlockSpec((None, n_keys, kcat.shape[1]), lambda bi: (bi, 0, 0)),
                      pl.BlockSpec((None, n_keys, wa), lambda bi: (bi, 0, 0)),
                      pl.BlockSpec(memory_space=pl.ANY)],
            out_specs=pl.BlockSpec((tq, wa), row),
            out_shape=jax.ShapeDtypeStruct(out.shape, out.dtype),
            input_output_aliases={3: 0},
            compiler_params=_cparams(1),
            name="mla_prompt_%d" % c,
        )(qcat, k3, v3, out)

    return _chained_tile_calls(nq, make_call, out_init)


def _toeplitz_bias(tab_ref, h, start, n_rows, n_cols):
    shift0 = LANES - (n_rows - 1)
    from_left = (lax.broadcasted_iota(jnp.int32, (n_rows, LANES), 1)
                 < shift0 + lax.broadcasted_iota(jnp.int32, (n_rows, LANES), 0))
    pieces = []
    prev = None
    for n in range(n_cols // LANES + 1):
        seg = tab_ref[h:h + 1, start + n * LANES:start + (n + 1) * LANES]
        rot = pltpu.roll(jnp.broadcast_to(seg, (n_rows, LANES)), shift0 % LANES, 1, stride=1, stride_axis=0)
        if prev is not None:
            pieces.append(jnp.where(from_left, prev, rot))
        prev = rot
    return jnp.concatenate(pieces, axis=1)


def _dsa_prompt_kernel(n_heads, row0, t, topk, qb_ref, qi_ref, wi_ref, k_ref, v_ref, kiab_ref, tab_ref, prev_ref,
                       o_ref, key_ref, madd_ref, near_ref):
    del prev_ref
    per, tq, _ = qb_ref.shape
    n_keys = k_ref.shape[1]
    score = jnp.concatenate(
        [_indexer_scores(qi_ref.at[u], wi_ref[u], kiab_ref[u, :, :LANES], kiab_ref[u, :, LANES:])
         for u in range(per)], axis=0)
    adm = _chunk_madd(row0, tq, n_keys)
    _topk_mask(score, jnp.concatenate([adm] * per, axis=0), topk, key_ref, madd_ref)
    near0 = max(0, row0 - LANES)
    n_near = n_keys - near0

    @pl.when(pl.program_id(0) == 0)
    def _():
        for h in range(n_heads):
            near_ref[h] = jnp.concatenate(
                [_toeplitz_bias(tab_ref, h, near0 - (row0 + u * LANES) - LANES + t, LANES, n_near)
                 for u in range(tq // LANES)], axis=0)

    for h in range(n_heads):
        c0 = h * DSA_HEAD_DIM
        if near0 > 0:
            far = jnp.broadcast_to(tab_ref[h:h + 1, 0:1], (tq, near0))
            bias = jnp.concatenate([far, near_ref[h]], axis=1)
        else:
            bias = near_ref[h]
        for u in range(per):
            s = (_dot_nt(qb_ref[u, :, c0:c0 + DSA_HEAD_DIM], k_ref[u, :, c0:c0 + DSA_HEAD_DIM]) + bias
                 + madd_ref[u * tq:(u + 1) * tq, :])
            o_ref[u, :, c0:c0 + DSA_HEAD_DIM] = _mx(_softmax_pv(s, v_ref[u, :, c0:c0 + DSA_HEAD_DIM]))


def _dsa_prompt(qb, qi, wi, kbb, vbb, kiab, bias_tab, out_init, b, t, topk):
    tq = DSA_TQ
    assert MAX_DISTANCE <= LANES and tq % LANES == 0
    n_heads = qb.shape[1] // DSA_HEAD_DIM
    nq = t // tq
    wb = qb.shape[1]
    k3, v3, ki3 = kbb.reshape(b, t, wb), vbb.reshape(b, t, wb), kiab.reshape(b, t, 2 * LANES)
    qb3, qi3, wi3 = qb.reshape(b, t, wb), qi.reshape(b, t, qi.shape[1]), wi.reshape(b, t, LANES)
    keys_of = lambda bi: (bi, 0, 0)

    def make_call(c, out):
        n_keys = (c + 1) * tq
        per = max(p for p in (1, 2, 4) if p == 1 or (b % p == 0 and p * n_keys <= DSA_STEP_MAX_KEYS))
        rows_of = lambda bi: (bi, c, 0)
        return pl.pallas_call(
            functools.partial(_dsa_prompt_kernel, n_heads, c * tq, t, topk),
            grid=(b // per,),
            in_specs=[pl.BlockSpec((per, tq, wb), rows_of),
                      pl.BlockSpec((per, tq, qi.shape[1]), rows_of),
                      pl.BlockSpec((per, tq, LANES), rows_of),
                      pl.BlockSpec((per, n_keys, wb), keys_of),
                      pl.BlockSpec((per, n_keys, wb), keys_of),
                      pl.BlockSpec((per, n_keys, 2 * LANES), keys_of),
                      _resident(bias_tab.shape),
                      pl.BlockSpec(memory_space=pl.ANY)],
            out_specs=pl.BlockSpec((per, tq, wb), rows_of),
            out_shape=jax.ShapeDtypeStruct(out.shape, out.dtype),
            input_output_aliases={7: 0},
            scratch_shapes=[pltpu.VMEM((per * tq, n_keys), jnp.float32),
                            pltpu.VMEM((per * tq, n_keys), jnp.float32),
                            pltpu.VMEM((n_heads, tq, n_keys - max(0, c * tq - LANES)), jnp.float32)],
            compiler_params=_cparams(1),
            name="dsa_prompt_%d" % c,
        )(qb3, qi3, wi3, k3, v3, ki3, bias_tab, out)

    return _chained_tile_calls(nq, make_call, out_init.reshape(b, t, wb)).reshape(b * t, wb)


def _mla_sample_kernel(n_heads, past, ts, qcat_ref, cckv_ref, ckrt_ref, nckv_ref, nkr_ref, wuk_ref, wuv_ref,
                       o_ref, kall_ref, rt_ref, rnew_ref):
    n_keys = past + ts

    @pl.when(pl.program_id(0) == 0)
    def _():
        kall_ref[...] = jnp.zeros_like(kall_ref)
        rt_ref[...] = jnp.zeros_like(rt_ref)
        rnew_ref[...] = jnp.zeros_like(rnew_ref)

    for u in range(cckv_ref.shape[0]):
        rows = slice(u * ts, (u + 1) * ts)
        kall_ref[u, 0:past, :] = _mx(cckv_ref[u])
        kall_ref[u, past:n_keys, :] = _mx(nckv_ref[rows, :])
        rt_ref[u, 0:MLA_ROPE, :] = _mx(ckrt_ref[u])
        rnew_ref[u, 0:ts, 0:MLA_ROPE] = _mx(nkr_ref[rows, :])
        qlat, qrope = [], []
        for h in range(n_heads):
            c0 = h * HEAD_SLOT
            qlat.append(_mx(_dot(qcat_ref[rows, c0:c0 + LANES], wuk_ref[h])))
            qrope.append(qcat_ref[rows, c0 + LANES:c0 + HEAD_SLOT])
        qlat = jnp.concatenate(qlat, axis=0)
        qrope = jnp.concatenate(qrope, axis=0)
        s = _dot_nt(qlat, kall_ref[u]) + jnp.concatenate(
            [_dot(qrope, rt_ref[u]), _dot_nt(qrope, rnew_ref[u])], axis=1)
        col = lax.broadcasted_iota(jnp.int32, s.shape, 1)
        s = jnp.where(col < n_keys, s, NEG_BIG)
        olat = _mx(_softmax_pv(s, kall_ref[u]))
        for h in range(n_heads):
            o_ref[rows, h * MLA_V:(h + 1) * MLA_V] = _mx(_dot_nt(olat[h * ts:(h + 1) * ts], wuv_ref[h]))


def _mla_sample(qcat, cache_ckv, cache_kr_t, new_ckv, new_kr, wuk, wuv, ts):
    nb, past, c = cache_ckv.shape
    assert past % LANES == 0
    n_heads = qcat.shape[1] // HEAD_SLOT
    wa = n_heads * MLA_V
    s_pad = pl.cdiv(past + ts, LANES) * LANES
    per = max(p for p in (1, 2, 4) if nb % p == 0)
    row = lambda bi: (bi, 0)
    return pl.pallas_call(
        functools.partial(_mla_sample_kernel, n_heads, past, ts),
        grid=(nb // per,),
        in_specs=[pl.BlockSpec((per * ts, qcat.shape[1]), row),
                  pl.BlockSpec((per, past, c), lambda bi: (bi, 0, 0)),
                  pl.BlockSpec((per, MLA_ROPE, past), lambda bi: (bi, 0, 0)),
                  pl.BlockSpec((per * ts, c), row),
                  pl.BlockSpec((per * ts, MLA_ROPE), row),
                  _resident(wuk.shape), _resident(wuv.shape)],
        out_specs=pl.BlockSpec((per * ts, wa), row),
        out_shape=jax.ShapeDtypeStruct((nb * ts, wa), MXU_DT),
        scratch_shapes=[pltpu.VMEM((per, s_pad, c), MXU_DT), pltpu.VMEM((per, LANES, past), MXU_DT),
                        pltpu.VMEM((per, s_pad - past, LANES), MXU_DT)],
        compiler_params=_cparams(1),
        name="mla_sample",
    )(qcat, cache_ckv, cache_kr_t, new_ckv, new_kr, wuk, wuv)


def _dsa_select_kernel(past, ts, topk, qi_ref, wi_ref, ckit_ref, nkiab_ref, madd_ref,
                       kat_ref, kbt_ref, anew_ref, bnew_ref, key_ref):
    bi = pl.program_id(0)
    n_keys = past + ts
    rows, s_pad = madd_ref.shape

    @pl.when(bi == 0)
    def _():
        for ref in (kat_ref, kbt_ref, anew_ref, bnew_ref):
            ref[...] = jnp.zeros_like(ref)

    per = ckit_ref.shape[0]
    for u in range(per):
        r = slice(u * ts, (u + 1) * ts)
        ckit = _mx(ckit_ref[u])
        kat_ref[u, 0:IDX_DIM, :] = ckit
        kbt_ref[u, IDX_DIM:LANES, :] = ckit
        anew_ref[u, 0:ts, :] = nkiab_ref[r, :LANES]
        bnew_ref[u, 0:ts, :] = nkiab_ref[r, LANES:]
        madd_ref[pl.ds(pl.multiple_of((bi * per + u) * ts, ts), ts), :] = _indexer_scores_stacked(
            qi_ref.at[r], wi_ref[r, :],
            lambda q, u=u: jnp.concatenate([_dot(q, kat_ref[u]), _dot_nt(q, anew_ref[u])], axis=1),
            lambda q, u=u: jnp.concatenate([_dot(q, kbt_ref[u]), _dot_nt(q, bnew_ref[u])], axis=1))

    @pl.when(bi == pl.num_programs(0) - 1)
    def _():
        col = lax.broadcasted_iota(jnp.int32, (rows, s_pad), 1)
        _topk_mask(madd_ref[...], col < n_keys, topk, key_ref, madd_ref)


def _dsa_select(qi, wi, cache_ki_t, new_kiab, ts, topk):
    nb, _, past = cache_ki_t.shape
    assert past % LANES == 0
    s_pad = pl.cdiv(past + ts, LANES) * LANES
    per = max(p for p in (1, 2, 4) if nb % p == 0)
    row = lambda bi: (bi, 0)
    return pl.pallas_call(
        functools.partial(_dsa_select_kernel, past, ts, topk),
        grid=(nb // per,),
        in_specs=[pl.BlockSpec((per * ts, qi.shape[1]), row),
                  pl.BlockSpec((per * ts, LANES), row),
                  pl.BlockSpec((per, IDX_DIM, past), lambda bi: (bi, 0, 0)),
                  pl.BlockSpec((per * ts, 2 * LANES), row)],
        out_specs=pl.BlockSpec((nb * ts, s_pad), lambda bi: (0, 0)),
        out_shape=jax.ShapeDtypeStruct((nb * ts, s_pad), jnp.float32),
        scratch_shapes=[pltpu.VMEM((per, LANES, past), MXU_DT), pltpu.VMEM((per, LANES, past), MXU_DT),
                        pltpu.VMEM((per, s_pad - past, LANES), MXU_DT),
                        pltpu.VMEM((per, s_pad - past, LANES), MXU_DT),
                        pltpu.VMEM((nb * ts, s_pad), jnp.float32)],
        compiler_params=_cparams(1),
        name="dsa_select",
    )(qi, wi, cache_ki_t, new_kiab)


def _dsa_sample_kernel(n_heads, past, ts, nb, qb_ref, madd_ref, ck_hbm, cv_hbm, nk_ref, nv_ref, tab_ref, expand_ref,
                       o_ref, kflat_ref, vflat_ref, bias_ref, biasw_ref, cache_ref, sem):
    n_keys = past + ts
    wide = n_heads * LANES
    n_blocks = pl.cdiv(n_keys, LANES)
    widths = [min(wide, (n_keys - j * LANES) * n_heads) for j in range(n_blocks)]

    @pl.when(pl.program_id(0) == 0)
    def _():
        for h in range(n_heads):
            bias_ref[h * ts:(h + 1) * ts, :] = _toeplitz_bias(tab_ref, h, 0, ts, n_blocks * LANES)
        shape = (n_heads * ts, wide)
        same_head = (lax.broadcasted_iota(jnp.int32, shape, 0) // ts
                     == lax.broadcasted_iota(jnp.int32, shape, 1) % n_heads)
        for j in range(n_blocks):
            b = bias_ref[:, j * LANES:(j + 1) * LANES]
            hi = _mx(b)
            rest = b - hi.astype(jnp.float32)
            mid = _mx(rest)
            lo = _mx(rest - mid.astype(jnp.float32))
            piece = _dot(hi, expand_ref[...]) + _dot(mid, expand_ref[...]) + _dot(lo, expand_ref[...])
            biasw_ref[:, j * wide:j * wide + widths[j]] = jnp.where(same_head, piece, NEG_BIG)[:, :widths[j]]

    bi = pl.program_id(0)

    def fetch(batch, slot_):
        return [pltpu.make_async_copy(src.at[batch], cache_ref.at[slot_, which], sem.at[slot_, which])
                for which, src in enumerate((ck_hbm, cv_hbm))]

    @pl.when(bi == 0)
    def _():
        for ahead in range(min(CACHE_RING - 1, nb)):
            for cp in fetch(ahead, ahead):
                cp.start()

    nxt = bi + CACHE_RING - 1

    @pl.when(nxt < nb)
    def _():
        for cp in fetch(nxt, nxt % CACHE_RING):
            cp.start()

    slot = bi % CACHE_RING
    for cp in fetch(bi, slot):
        cp.wait()
    kflat_ref[0:past * n_heads, :] = _mx(cache_ref[slot, 0])
    kflat_ref[past * n_heads:, :] = nk_ref[...]
    vflat_ref[0:past * n_heads, :] = _mx(cache_ref[slot, 1])
    vflat_ref[past * n_heads:, :] = nv_ref[...]
    sel = _mx(jnp.where(madd_ref[...] == 0.0, 1.0, 0.0))
    pieces = [_dot(sel[:, j * LANES:(j + 1) * LANES], expand_ref[...])[:, :widths[j]] for j in range(n_blocks)]
    sel_wide = jnp.concatenate(pieces, axis=1)
    sel_wide = jnp.concatenate([sel_wide] * n_heads, axis=0)
    q_all = jnp.concatenate([qb_ref[:, h * DSA_HEAD_DIM:(h + 1) * DSA_HEAD_DIM] for h in range(n_heads)], axis=0)
    s = _dot_nt(q_all, kflat_ref[...]) + biasw_ref[...] + jnp.where(sel_wide > 0.5, 0.0, NEG_BIG)
    o = _mx(_softmax_pv(s, vflat_ref[...]))
    for h in range(n_heads):
        o_ref[:, h * DSA_HEAD_DIM:(h + 1) * DSA_HEAD_DIM] = o[h * ts:(h + 1) * ts]


def _dsa_sample(qb, madd, cache_k, cache_v, new_k, new_v, bias_tab, ts):
    nb, past, n_heads, _ = cache_k.shape
    wb = n_heads * DSA_HEAD_DIM
    n_keys = past + ts
    assert (n_keys % LANES * n_heads) % LANES == 0
    expand = _mx(jnp.repeat(jnp.eye(LANES, dtype=jnp.float32), n_heads, axis=1))
    row = lambda bi: (bi, 0)
    per_b = lambda bi: (bi, 0, 0)
    return pl.pallas_call(
        functools.partial(_dsa_sample_kernel, n_heads, past, ts, nb),
        grid=(nb,),
        in_specs=[pl.BlockSpec((ts, wb), row),
                  pl.BlockSpec((ts, madd.shape[1]), row),
                  pl.BlockSpec(memory_space=pl.ANY),
                  pl.BlockSpec(memory_space=pl.ANY),
                  pl.BlockSpec((ts * n_heads, DSA_HEAD_DIM), row),
                  pl.BlockSpec((ts * n_heads, DSA_HEAD_DIM), row),
                  _resident(bias_tab.shape), _resident(expand.shape)],
        out_specs=pl.BlockSpec((ts, wb), row),
        out_shape=jax.ShapeDtypeStruct((nb * ts, wb), MXU_DT),
        scratch_shapes=[pltpu.VMEM((n_keys * n_heads, DSA_HEAD_DIM), MXU_DT),
                        pltpu.VMEM((n_keys * n_heads, DSA_HEAD_DIM), MXU_DT),
                        pltpu.VMEM((n_heads * ts, pl.cdiv(n_keys, LANES) * LANES), jnp.float32),
                        pltpu.VMEM((n_heads * ts, n_keys * n_heads), jnp.float32),
                        pltpu.VMEM((CACHE_RING, 2, past * n_heads, DSA_HEAD_DIM), jnp.float32),
                        pltpu.SemaphoreType.DMA((CACHE_RING, 2))],
        compiler_params=_cparams(1),
        name="dsa_sample",
    )(qb, madd, cache_k.reshape(nb, past * n_heads, DSA_HEAD_DIM), cache_v.reshape(nb, past * n_heads, DSA_HEAD_DIM),
      new_k.reshape(nb * ts * n_heads, DSA_HEAD_DIM), new_v.reshape(nb * ts * n_heads, DSA_HEAD_DIM),
      bias_tab, expand)


def _out_kernel(oa_ref, ob_ref, ga_ref, gb_ref, x_ref, gate_ref, gna_ref, gnb_ref, w_ref, fg_ref, y_ref):
    def gated(o_ref, g_ref, gain_ref):
        o = o_ref[...].astype(jnp.float32)
        on = o * lax.rsqrt(jnp.mean(o * o, axis=-1, keepdims=True) + EPS) * gain_ref[...]
        return _mx(on * g_ref[...].astype(jnp.float32))

    wa = oa_ref.shape[1]
    out = _dot(gated(oa_ref, ga_ref, gna_ref), w_ref[0:wa, :]) + _dot(gated(ob_ref, gb_ref, gnb_ref), w_ref[wa:, :])
    xn = x_ref[...] + gate_ref[...] * out
    y_ref[...] = xn * lax.rsqrt(jnp.mean(xn * xn, axis=-1, keepdims=True) + EPS) * fg_ref[...]


def _out_proj(oa, ob, ga, gb, x3, gate, gain_a, gain_b, w_out, final_gain, tm):
    bv, tv, d = x3.shape
    nt = tv // tm
    wa = oa.shape[1]
    x_spec, g_spec = _row_specs(x3, gate.shape[1], tm)
    row = lambda b, i: (b * nt + i, 0)
    return pl.pallas_call(
        _out_kernel,
        grid=(bv, nt),
        in_specs=[pl.BlockSpec((tm, wa), row), pl.BlockSpec((tm, wa), row),
                  pl.BlockSpec((tm, wa), row), pl.BlockSpec((tm, wa), row),
                  x_spec, g_spec, _resident((1, wa)), _resident((1, wa)),
                  _resident(w_out.shape), _resident((1, d))],
        out_specs=pl.BlockSpec((None, tm, d), lambda b, i: (b, i, 0)),
        out_shape=jax.ShapeDtypeStruct((bv, tv, d), jnp.float32),
        compiler_params=_cparams(2),
        name="out_proj",
    )(oa, ob, ga, gb, x3, gate, gain_a.reshape(1, wa), gain_b.reshape(1, wa), w_out, final_gain.reshape(1, d))


def _pack_kernel(n_heads, wt_ref, qa_ref, ckv_ref, kr_ref, ga_ref, gb_ref, qb_ref, kb_ref, vb_ref, qi_ref, kiw_ref):
    wa = n_heads * MLA_V
    half = MLA_ROPE // 2
    q_head = MLA_NOPE + MLA_ROPE
    step = 2 * LANES

    def panel(ref, off):
        for c in range(0, ref.shape[1], step):
            w = min(step, ref.shape[1] - c)
            ref[:, c:c + w] = _mx(wt_ref[off + c:off + c + w, :].T)

    def dup_rope(off):
        x1, x2 = wt_ref[off:off + half, :], wt_ref[off + half:off + 2 * half, :]
        return _mx(jnp.concatenate([x1, x2, x2, x1], axis=0).T)

    for h in range(n_heads):
        qa_ref[:, h * LANES:(h + 1) * LANES] = _mx(wt_ref[h * q_head:h * q_head + MLA_NOPE, :].T)
    for h in range(0, n_heads, 2):
        r0 = wt_ref[h * q_head + MLA_NOPE:(h + 1) * q_head, :]
        r1 = wt_ref[(h + 1) * q_head + MLA_NOPE:(h + 2) * q_head, :]
        dst = n_heads * MLA_NOPE + (h // 2) * LANES
        qa_ref[:, dst:dst + LANES] = _mx(jnp.concatenate([r0, r1], axis=0).T)
    off = n_heads * q_head
    panel(ckv_ref, off)
    off += KV_LORA
    kr_ref[...] = dup_rope(off)
    off += MLA_ROPE
    for ref in (ga_ref, qb_ref, kb_ref, vb_ref):
        panel(ref, off)
        off += wa
    pairs = IDX_HEADS // 2
    for j in range(pairs):
        lo = wt_ref[off + IDX_DIM * j:off + IDX_DIM * (j + 1), :]
        hi = wt_ref[off + IDX_DIM * (j + pairs):off + IDX_DIM * (j + pairs + 1), :]
        qi_ref[:, LANES * j:LANES * (j + 1)] = _mx(jnp.concatenate([lo, hi], axis=0).T)
    off += IDX_HEADS * IDX_DIM
    n_kw = IDX_DIM + IDX_HEADS
    kiw = jnp.concatenate([wt_ref[off:off + n_kw, :], jnp.zeros((LANES - n_kw, wt_ref.shape[1]), jnp.float32)], axis=0)
    kiw_ref[...] = _mx(kiw.T)
    off += n_kw
    panel(gb_ref, off)


def _pack_w_in(w_in, n_heads):
    wt = w_in.T
    n, d = wt.shape
    wa = n_heads * MLA_V
    assert n == n_heads * (MLA_NOPE + MLA_ROPE) + KV_LORA + MLA_ROPE + 5 * wa + IDX_HEADS * IDX_DIM + IDX_DIM + IDX_HEADS
    assert 2 * IDX_DIM == LANES and 2 * MLA_ROPE == LANES and MLA_NOPE == LANES
    slab = 256
    assert n_heads % 4 == 0
    widths = [n_heads * (MLA_NOPE + MLA_ROPE), KV_LORA, LANES, wa, wa, wa, wa, wa, IDX_HEADS * IDX_DIM, LANES]
    q_a, ckv, kr, g_a, g_b, q_b, k_b, v_b, q_i, kiw = pl.pallas_call(
        functools.partial(_pack_kernel, n_heads),
        grid=(d // slab,),
        in_specs=[pl.BlockSpec((n, slab), lambda i: (0, i))],
        out_specs=[pl.BlockSpec((slab, w), lambda i: (i, 0)) for w in widths],
        out_shape=[jax.ShapeDtypeStruct((d, w), MXU_DT) for w in widths],
        compiler_params=_cparams(1),
        name="pack_w_in",
    )(wt)
    return [q_a, ckv, kr, g_a, g_b], [q_b, k_b, v_b, q_i, kiw]


def _rope_table(pos):
    half = MLA_ROPE // 2
    freqs = jnp.power(ROPE_THETA, -jnp.arange(half, dtype=jnp.float32) / half)
    ang = pos.astype(jnp.float32)[:, None] * freqs
    cos, sin = jnp.cos(ang), jnp.sin(ang)
    return jnp.concatenate([cos, cos, -sin, sin], axis=1)


def _rel_bucket(rel):
    nb = N_BUCKETS // 2
    max_exact = nb // 2
    n = jnp.abs(rel)
    nf = jnp.maximum(n, 1).astype(jnp.float32)
    large = max_exact + (jnp.log(nf / max_exact) / math.log(MAX_DISTANCE / max_exact)
                         * (nb - max_exact)).astype(jnp.int32)
    large = jnp.minimum(large, nb - 1)
    return jnp.where(rel > 0, nb, 0) + jnp.where(n < max_exact, n, large)


def _bucket_bias(rel_bias, rel):
    return (rel_bias * LOG2E)[_rel_bucket(rel)]


def kernel(x_prompt, x_sample, cache_mla_ckv, cache_mla_krope, cache_dsa_k, cache_dsa_v, cache_idx_k,
           c_prompt, c_sample, w_ada, b_ada, ln_gain, w_in, mla_kv_gain, w_uk, w_uv, rel_bias,
           out_gain_a, out_gain_b, w_out, final_gain):
    assert w_ada.shape[0] == 1, "single-layer step"
    b, t, d = x_prompt.shape
    nb, ts, _ = x_sample.shape
    past = cache_mla_ckv.shape[2]
    n_heads = d // 256
    s_all = past + ts
    assert t % MLA_TQ == 0 and t % DSA_TQ == 0 and MLA_TQ % CHUNK == 0 and DSA_TQ % CHUNK == 0
    assert (s_all - 1) // CHUNK <= past // CHUNK

    mod = _ada_mod(jnp.concatenate([c_prompt, c_sample], axis=0), w_ada[0], b_ada[0])
    shift, scale, gate = mod[:, :d], mod[:, d:2 * d], mod[:, 2 * d:]
    mod_p = [v[:b].reshape(b, 1, d) for v in (scale, shift, gate)]
    mod_s = [jnp.broadcast_to(v[b:, None, :], (nb, ts, d)).reshape(1, nb * ts, d) for v in (scale, shift, gate)]

    w_a, w_b = _pack_w_in(w_in[0], n_heads)
    wkv = _mx(jnp.concatenate([w_uk[0].transpose(2, 0, 1).reshape(KV_LORA, n_heads * MLA_NOPE),
                               w_uv[0].transpose(2, 0, 1).reshape(KV_LORA, n_heads * MLA_V)], axis=1))
    w_o = _mx(w_out[0])
    pos_p = jnp.arange(t, dtype=jnp.int32)
    pos_s = past + jnp.arange(ts, dtype=jnp.int32)
    tab_p = _rope_table(pos_p)
    tab_s = jnp.tile(_rope_table(pos_s), (nb, 1))

    tm = PROJ_TM
    qcat, ckv_p, krope_p, ga, gb, kcat, vup, oa_buf = _proj_a(
        x_prompt, mod_p[0], mod_p[1], ln_gain[0], tab_p, w_a, mla_kv_gain[0], wkv, tm)
    qb, k_p, v_p, kbb, vbb, qi, idxk_p, kiab, wi, ob_buf = _proj_b(
        x_prompt, mod_p[0], mod_p[1], ln_gain[0], w_b, tm, True)
    o_a = _mla_prompt(qcat, kcat, vup, oa_buf, b, t)
    near = _bucket_bias(rel_bias, jnp.arange(-MAX_DISTANCE - 1, MAX_DISTANCE, dtype=jnp.int32))
    far = near[:1]
    bias_tab = jnp.concatenate([jnp.broadcast_to(far, (t - 1 - MAX_DISTANCE, n_heads)), near[1:],
                                jnp.broadcast_to(far, (t + 1 - MAX_DISTANCE, n_heads))], axis=0).T
    o_b = _dsa_prompt(qb, qi, wi, kbb, vbb, kiab, bias_tab, ob_buf, b, t, min(TOPK_MAX, t // 4))
    y_prompt = _out_proj(o_a, o_b, ga, gb, x_prompt, mod_p[2], out_gain_a[0], out_gain_b[0], w_o, final_gain, 2 * tm)

    xs3 = x_sample.reshape(1, nb * ts, d)
    tms = min(PROJ_TM, nb * ts)
    qcat_s, ckv_s, krope_s, ga_s, gb_s = _proj_a(
        xs3, mod_s[0], mod_s[1], ln_gain[0], tab_s, w_a, mla_kv_gain[0], None, tms)
    qb_s, k_s, v_s, kbb_s, vbb_s, qi_s, idxk_s, kiab_s, wi_s = _proj_b(
        xs3, mod_s[0], mod_s[1], ln_gain[0], w_b, tms, False)
    oa_s = _mla_sample(qcat_s, cache_mla_ckv[0], jnp.swapaxes(cache_mla_krope[0], 1, 2), ckv_s, krope_s,
                       _mx(w_uk[0]), _mx(w_uv[0]), ts)
    s_pad = pl.cdiv(s_all, LANES) * LANES
    tab_s = _bucket_bias(rel_bias, jnp.arange(s_pad + LANES, dtype=jnp.int32) - (s_all - 1)).T
    madd_s = _dsa_select(qi_s, wi_s, jnp.swapaxes(cache_idx_k[0], 1, 2), kiab_s, ts, min(TOPK_MAX, s_all // 4))
    ob_s = _dsa_sample(qb_s, madd_s, cache_dsa_k[0], cache_dsa_v[0], kbb_s, vbb_s, tab_s, ts)
    y_sample = _out_proj(oa_s, ob_s, ga_s, gb_s, xs3, mod_s[2], out_gain_a[0], out_gain_b[0], w_o, final_gain, tms)

    hd = (n_heads, DSA_HEAD_DIM)
    return (y_prompt, y_sample.reshape(nb, ts, d),
            ckv_p.reshape(1, b, t, KV_LORA), krope_p.reshape(1, b, t, MLA_ROPE),
            k_p.reshape(1, b, t, *hd), v_p.reshape(1, b, t, *hd), idxk_p.reshape(1, b, t, IDX_DIM),
            ckv_s.reshape(1, nb, ts, KV_LORA), krope_s.reshape(1, nb, ts, MLA_ROPE),
            k_s.reshape(1, nb, ts, *hd), v_s.reshape(1, nb, ts, *hd), idxk_s.reshape(1, nb, ts, IDX_DIM))
```
